```python
import math
import jax, jax.numpy as jnp
from jax import lax
import numpy as np

D_MODEL = 1024
BATCH = 2
SEQ = 8192
DEPTH = 2

GRID_W = 64
CTX_LEN = 256
N_MIXERS = 4
GROUP_W = D_MODEL // N_MIXERS
HEAD = 64
N_HEADS = GROUP_W // HEAD
NORM_EPS = 1e-6
RW_DECAY_LORA = 64
RW_AAA_LORA = 64
RW_GATE_LORA = 128
RW_LN_EPS = 64e-5
RW_SIZES = [GROUP_W] * 3 + [RW_DECAY_LORA] * 2 + [RW_AAA_LORA] * 2 + [RW_GATE_LORA]
RW_COLS = int(sum(RW_SIZES))
RW_SPLITS = np.cumsum(RW_SIZES)[:-1].tolist()
DA_QK = HEAD // 2
DA_COLS = 3 * GROUP_W
ROPE_BASE = 10000.0
Q_BLOCK = 128
POOL_WINDOWS = (2, 4, 8, 16)
POOL_CH = GROUP_W // len(POOL_WINDOWS)
IN_COLS = RW_COLS + DA_COLS + 2 * GROUP_W
N_EXPERTS = 16
CAPACITY_FACTOR = 2
D_EXPERT = 2816

kernel_name = 'hybrid_parallel_rwkv7_diffattn_fnet_pool_ecmoe_dit'

F32 = jnp.float32


def rms_norm(x, gain, eps=NORM_EPS):
    xf = x.astype(F32)
    y = xf * lax.rsqrt(jnp.mean(xf * xf, axis=-1, keepdims=True) + eps)
    return (y * gain.astype(F32)).astype(x.dtype)


def modulate(x, gain, shift, scale):
    return rms_norm(x, gain) * (1 + scale) + shift


def axial_rope_tables(n_tokens):
    rows = n_tokens // GRID_W
    row = jnp.repeat(jnp.arange(rows), GRID_W).astype(F32)
    col = jnp.tile(jnp.arange(GRID_W), rows).astype(F32)
    n_freq = DA_QK // 4
    inv = ROPE_BASE ** (-jnp.arange(n_freq, dtype=F32) / n_freq)
    ar = row[:, None] * inv
    ac = col[:, None] * inv
    ang = jnp.concatenate([ar, ar, ac, ac], axis=-1)
    return jnp.cos(ang), jnp.sin(ang)


def apply_rope(x, cos, sin):
    xr = x.reshape(x.shape[:-1] + (2, 2, DA_QK // 4))
    rot = jnp.stack([-xr[..., 1, :], xr[..., 0, :]], axis=-2).reshape(x.shape)
    c = cos[None, :, None, None, :]
    s = sin[None, :, None, None, :]
    return (x.astype(F32) * c + rot.astype(F32) * s).astype(x.dtype)


def centred_shift(p, mu_prev, mu_next):
    prev = jnp.pad(p, ((0, 0), (1, 0), (0, 0)))[:, :-1]
    nxt = jnp.pad(p, ((0, 0), (0, 1), (0, 0)))[:, 1:]
    return p + mu_prev * (prev - p) + mu_next * (nxt - p)


def rwkv_streams(u, w0, w_up, a0, a_up, g_up, k_k, k_a):
    B, T, _ = u.shape
    r, k, v, wf, wb, af, ab, g = jnp.split(u, RW_SPLITS, axis=-1)
    hd = lambda t: t.reshape(B, T, N_HEADS, HEAD)
    kk = hd(k * k_k).astype(F32)
    kk = kk * lax.rsqrt(jnp.maximum(jnp.sum(kk * kk, axis=-1, keepdims=True), 1e-24))
    dirs = []
    for d, (wl, al) in enumerate(((wf, af), (wb, ab))):
        w_log = -jax.nn.softplus(-(w0[d] + jnp.tanh(wl) @ w_up[d]).astype(F32)) - 0.5
        decay = jnp.exp(-jnp.exp(w_log))
        a = jax.nn.sigmoid((a0[d] + al @ a_up[d]).astype(F32))
        k_d = k.astype(F32) * (1 + (a - 1) * k_a.astype(F32))
        dirs.append((hd(decay), kk * hd(a), hd(k_d)))
    gate = jax.nn.sigmoid(g) @ g_up
    return hd(r), hd(k), hd(v), kk, dirs, gate


def wkv_scan(S0, r, decay, kk, kka, k, v, reverse):
    tm = lambda t: jnp.moveaxis(t.astype(F32), 1, 0)

    def step(S, inp):
        r_t, w_t, kk_t, kka_t, k_t, v_t = inp
        sa = jnp.einsum('bhvk,bhk->bhv', S, kk_t)
        S = S * w_t[:, :, None, :] - sa[..., None] * kka_t[:, :, None, :] + v_t[..., None] * k_t[:, :, None, :]
        return S, jnp.einsum('bhvk,bhk->bhv', S, r_t)

    S, y = lax.scan(step, S0, tuple(tm(t) for t in (r, decay, kk, kka, k, v)), reverse=reverse)
    return S, jnp.moveaxis(y, 0, 1)


def rwkv_out(y, r, k, v, gate, r_k, ln_w, ln_b):
    B, T = y.shape[:2]
    mu = jnp.mean(y, axis=-1, keepdims=True)
    var = jnp.mean(jnp.square(y - mu), axis=-1, keepdims=True)
    yn = ((y - mu) * lax.rsqrt(var + RW_LN_EPS)).reshape(B, T, GROUP_W) * ln_w + ln_b
    bonus = (jnp.sum(r.astype(F32) * k * r_k, axis=-1, keepdims=True) * v).reshape(B, T, GROUP_W)
    return ((yn + bonus) * gate).astype(gate.dtype)


def rwkv_mixer(px, pc, mu_prev, mu_next, w0, w_up, a0, a_up, g_up, k_k, k_a, r_k, ln_w, ln_b, ctx_out):
    rx, kx, vx, kkx, dirx, gx = rwkv_streams(centred_shift(px, mu_prev, mu_next), w0, w_up, a0, a_up, g_up, k_k, k_a)
    rc, kc, vc, kkc, dirc, gc = rwkv_streams(centred_shift(pc, mu_prev, mu_next), w0, w_up, a0, a_up, g_up, k_k, k_a)
    S0 = jnp.zeros((px.shape[0], N_HEADS, HEAD, HEAD), F32)
    yx = jnp.zeros(rx.shape, F32)
    yc = jnp.zeros(rc.shape, F32)
    for d in range(2):
        rev = d == 1
        dec_c, kka_c, kd_c = dirc[d]
        S_ctx, yc_d = wkv_scan(S0, rc, dec_c, kkc, kka_c, kd_c, vc, rev)
        dec_x, kka_x, kd_x = dirx[d]
        _, yx_d = wkv_scan(S_ctx, rx, dec_x, kkx, kka_x, kd_x, vx, rev)
        yx = yx + yx_d
        yc = yc + yc_d
    out_x = rwkv_out(yx, rx, kx, vx, gx, r_k, ln_w, ln_b)
    out_c = rwkv_out(yc, rc, kc, vc, gc, r_k, ln_w, ln_b) if ctx_out else None
    return out_x, out_c


def split_qkv(p, q_gain, k_gain):
    B, T, _ = p.shape
    q, k, v = jnp.split(p, [GROUP_W, 2 * GROUP_W], axis=-1)
    q = rms_norm(q.reshape(B, T, N_HEADS, 2, DA_QK), q_gain)
    k = rms_norm(k.reshape(B, T, N_HEADS, 2, DA_QK), k_gain)
    return q, k, v.reshape(B, T, N_HEADS, HEAD)


def diff_weights(q, K, V, lam):
    s = jnp.einsum('bhmqd,bhmkd->bhmqk', q, K).astype(F32) * (DA_QK ** -0.5)
    p = jax.nn.softmax(s, axis=-1)
    a = p[:, :, 0] - lam * p[:, :, 1]
    return jnp.einsum('bhqk,bhkd->bhqd', a.astype(V.dtype), V)


def diff_attention(px, pc, q_gain, k_gain, lq1, lk1, lq2, lk2, sub_gain, cos, sin, lam_init, ctx_out):
    B, T, _ = px.shape
    qx, kx, vx = split_qkv(px, q_gain, k_gain)
    qc, kc, vc = split_qkv(pc, q_gain, k_gain)
    qx = apply_rope(qx, cos, sin)
    kx = apply_rope(kx, cos, sin)
    lam = (jnp.exp(jnp.sum(lq1.astype(F32) * lk1.astype(F32))) - jnp.exp(jnp.sum(lq2.astype(F32) * lk2.astype(F32)))
           + lam_init)
    kc_t = kc.transpose(0, 2, 3, 1, 4)
    vc_t = vc.transpose(0, 2, 1, 3)
    K = jnp.concatenate([kx.transpose(0, 2, 3, 1, 4), kc_t], axis=3)
    V = jnp.concatenate([vx.transpose(0, 2, 1, 3), vc_t], axis=2)
    nb = T // Q_BLOCK
    qb = qx.reshape(B, nb, Q_BLOCK, N_HEADS, 2, DA_QK).transpose(1, 0, 3, 4, 2, 5)
    ob = lax.map(lambda q: diff_weights(q, K, V, lam), qb)
    ox = ob.transpose(1, 0, 3, 2, 4).reshape(B, T, N_HEADS, HEAD)
    head_out = lambda o: (rms_norm(o, sub_gain) * (1.0 - lam_init)).reshape(o.shape[0], o.shape[1], GROUP_W)
    out_x = head_out(ox)
    if ctx_out:
        oc = diff_weights(qc.transpose(0, 2, 3, 1, 4), kc_t, vc_t, lam).transpose(0, 2, 1, 3)
        return out_x, head_out(oc)
    return out_x, None


def fourier_mix(u, w_f):
    B, T, _ = u.shape
    z = u.astype(F32).reshape(B, T, N_HEADS, HEAD)
    f = jnp.fft.fft2(z, axes=(1, 3), norm='ortho').real
    return f.reshape(B, T, GROUP_W).astype(u.dtype) @ w_f


def pool_mix(u, w_p, s_p):
    B, T, _ = u.shape
    uf = u.astype(F32)
    cs = jnp.concatenate([jnp.zeros((B, 1, GROUP_W), F32), jnp.cumsum(uf, axis=1)], axis=1)
    t = jnp.arange(T)
    outs = []
    for i, w in enumerate(POOL_WINDOWS):
        lo = jnp.clip(t - w // 2, 0, T)
        hi = jnp.clip(t + w // 2, 0, T)
        sl = slice(i * POOL_CH, (i + 1) * POOL_CH)
        mean = (cs[:, hi, sl] - cs[:, lo, sl]) / (hi - lo).astype(F32)[None, :, None]
        outs.append((mean - uf[..., sl]).astype(u.dtype) @ w_p[i])
    return jnp.concatenate(outs, axis=-1) * s_p


def token_mixing(hx, hc, w_in, w_out, rw, da, ft_w, pl_w, pl_scale, cos, sin, lam_init, ctx_out):
    px = hx @ w_in
    pc = hc @ w_in
    cut = [RW_COLS, RW_COLS + DA_COLS, RW_COLS + DA_COLS + GROUP_W]
    rwx, dax, ftx, plx = jnp.split(px, cut, axis=-1)
    rwc, dac, ftc, plc = jnp.split(pc, cut, axis=-1)
    ax, ac = rwkv_mixer(rwx, rwc, *rw, ctx_out=ctx_out)
    bx, bc = diff_attention(dax, dac, *da, cos, sin, lam_init, ctx_out)
    out_x = jnp.concatenate([ax, bx, fourier_mix(ftx, ft_w), pool_mix(plx, pl_w, pl_scale)], axis=-1) @ w_out
    if ctx_out:
        out_c = jnp.concatenate([ac, bc, fourier_mix(ftc, ft_w), pool_mix(plc, pl_w, pl_scale)], axis=-1) @ w_out
        return out_x, out_c
    return out_x, None


def expert_choice_ffn(h, router, w_gate, w_up, w_down):
    B, T, _ = h.shape
    cap = CAPACITY_FACTOR * T // N_EXPERTS
    aff = jax.nn.softmax((h @ router).astype(F32), axis=-1)
    g, idx = lax.top_k(jnp.swapaxes(aff, 1, 2), cap)
    bidx = jnp.arange(B)[:, None, None]
    xs = h[bidx, idx]
    hid = jax.nn.silu(jnp.einsum('becd,edf->becf', xs, w_gate)) * jnp.einsum('becd,edf->becf', xs, w_up)
    out = jnp.einsum('becf,efd->becd', hid, w_down) * g[..., None].astype(h.dtype)
    return jnp.zeros_like(h).at[bidx, idx].add(out)


def setup_inputs(seed: int = 0) -> dict:
    key = jax.random.key(seed)
    ks = iter(jax.random.split(key, 48))
    L, D = DEPTH, D_MODEL
    nrm = lambda shape, s: jax.random.normal(next(ks), shape, F32) * s
    uni = lambda shape, lo, hi: jax.random.uniform(next(ks), shape, F32, lo, hi)
    return {
        'x': nrm((BATCH, SEQ, D), 1.0),
        'c': nrm((BATCH, D), 1.0),
        'ctx': nrm((BATCH, CTX_LEN, D), 1.0),
        'c_ctx': nrm((D,), 1.0),
        'mod_w': nrm((L, D, 6 * D), 0.5 * D ** -0.5),
        'mod_b': nrm((L, 6 * D), 0.02),
        'norm1_w': 1.0 + nrm((L, D), 0.02),
        'norm2_w': 1.0 + nrm((L, D), 0.02),
        'w_in': nrm((L, D, IN_COLS), D ** -0.5),
        'rw_mu_prev': uni((L, RW_COLS), 0.0, 0.5),
        'rw_mu_next': uni((L, RW_COLS), 0.0, 0.5),
        'rw_w0': nrm((L, 2, GROUP_W), 0.5),
        'rw_w_up': nrm((L, 2, RW_DECAY_LORA, GROUP_W), 0.1 * RW_DECAY_LORA ** -0.5),
        'rw_a0': nrm((L, 2, GROUP_W), 0.5),
        'rw_a_up': nrm((L, 2, RW_AAA_LORA, GROUP_W), 0.5 * RW_AAA_LORA ** -0.5),
        'rw_g_up': nrm((L, RW_GATE_LORA, GROUP_W), RW_GATE_LORA ** -0.5),
        'rw_k_k': 0.85 + nrm((L, GROUP_W), 0.05),
        'rw_k_a': 1.0 + nrm((L, GROUP_W), 0.05),
        'rw_r_k': nrm((L, N_HEADS, HEAD), 0.1),
        'rw_ln_w': 1.0 + nrm((L, GROUP_W), 0.02),
        'rw_ln_b': nrm((L, GROUP_W), 0.02),
        'da_q_gain': 1.0 + nrm((L, DA_QK), 0.02),
        'da_k_gain': 1.0 + nrm((L, DA_QK), 0.02),
        'da_lq1': nrm((L, DA_QK), 0.1),
        'da_lk1': nrm((L, DA_QK), 0.1),
        'da_lq2': nrm((L, DA_QK), 0.1),
        'da_lk2': nrm((L, DA_QK), 0.1),
        'da_sub_gain': 1.0 + nrm((L, HEAD), 0.02),
        'ft_w': nrm((L, GROUP_W, GROUP_W), GROUP_W ** -0.5),
        'pl_w': nrm((L, len(POOL_WINDOWS), POOL_CH, POOL_CH), POOL_CH ** -0.5),
        'pl_scale': 1.0 + nrm((L, GROUP_W), 0.1),
        'w_out': nrm((L, D, D), D ** -0.5),
        'moe_router': nrm((L, D, N_EXPERTS), D ** -0.5),
        'moe_w_gate': nrm((L, N_EXPERTS, D, D_EXPERT), D ** -0.5),
        'moe_w_up': nrm((L, N_EXPERTS, D, D_EXPERT), D ** -0.5),
        'moe_w_down': nrm((L, N_EXPERTS, D_EXPERT, D), D_EXPERT ** -0.5),
    }


def reference(x, c, ctx, c_ctx, mod_w, mod_b, norm1_w, norm2_w, w_in, rw_mu_prev, rw_mu_next, rw_w0, rw_w_up,
              rw_a0, rw_a_up, rw_g_up, rw_k_k, rw_k_a, rw_r_k, rw_ln_w, rw_ln_b, da_q_gain, da_k_gain, da_lq1,
              da_lk1, da_lq2, da_lk2, da_sub_gain, ft_w, pl_w, pl_scale, w_out, moe_router, moe_w_gate, moe_w_up,
              moe_w_down):
    cos, sin = axial_rope_tables(x.shape[1])
    for l in range(DEPTH):
        ctx_out = l < DEPTH - 1
        lam_init = 0.8 - 0.6 * math.exp(-0.3 * l)
        mod_x = jax.nn.silu(c) @ mod_w[l] + mod_b[l]
        mod_c = jax.nn.silu(c_ctx) @ mod_w[l] + mod_b[l]
        sh1x, sc1x, g1x, sh2x, sc2x, g2x = [m[:, None, :] for m in jnp.split(mod_x, 6, axis=-1)]
        sh1c, sc1c, g1c, sh2c, sc2c, g2c = jnp.split(mod_c, 6, axis=-1)
        hx = modulate(x, norm1_w[l], sh1x, sc1x)
        hc = modulate(ctx, norm1_w[l], sh1c, sc1c)
        rw = (rw_mu_prev[l], rw_mu_next[l], rw_w0[l], rw_w_up[l], rw_a0[l], rw_a_up[l], rw_g_up[l],
              rw_k_k[l], rw_k_a[l], rw_r_k[l], rw_ln_w[l], rw_ln_b[l])
        da = (da_q_gain[l], da_k_gain[l], da_lq1[l], da_lk1[l], da_lq2[l], da_lk2[l], da_sub_gain[l])
        mx, mc = token_mixing(hx, hc, w_in[l], w_out[l], rw, da, ft_w[l], pl_w[l], pl_scale[l],
                              cos, sin, lam_init, ctx_out)
        x = x + g1x * mx
        x = x + g2x * expert_choice_ffn(modulate(x, norm2_w[l], sh2x, sc2x),
                                        moe_router[l], moe_w_gate[l], moe_w_up[l], moe_w_down[l])
        if ctx_out:
            ctx = ctx + g1c * mc
            ctx = ctx + g2c * expert_choice_ffn(modulate(ctx, norm2_w[l], sh2c, sc2c),
                                                moe_router[l], moe_w_gate[l], moe_w_up[l], moe_w_down[l])
    return x
```

```python
import functools
import math

import numpy as np
import jax
import jax.numpy as jnp
from jax import lax
from jax.experimental import pallas as pl
from jax.experimental.pallas import tpu as pltpu

F32 = jnp.float32
BF16 = jnp.bfloat16
HIGHEST = lax.Precision.HIGHEST

N_MIXERS = 4
HEAD = 64
NORM_EPS = 1e-6
RW_LN_EPS = 64e-5
GRID_W = 64
DA_QK = HEAD // 2
ROPE_BASE = 10000.0
POOL_WINDOWS = (2, 4, 8, 16)
N_EXPERTS = 16
CAPACITY_FACTOR = 2

LANES = 128
SUBLANES = 8
VMEM_LIMIT_BYTES = 56 * 1024 * 1024

RW_CHUNK = 64
RW_CHUNKS_PER_STEP = 8


def _cparams(*sem):
    return pltpu.CompilerParams(dimension_semantics=sem, vmem_limit_bytes=VMEM_LIMIT_BYTES)


def _dot(a, b):
    return jnp.dot(a.astype(BF16), b.astype(BF16), preferred_element_type=F32)


def _dot_f32(a, b):
    return jnp.dot(a, b, precision=HIGHEST, preferred_element_type=F32)


def _dot_nt(a, b, exact=False):
    dn = (((1,), (1,)), ((), ()))
    if exact:
        return lax.dot_general(a, b, dn, precision=HIGHEST, preferred_element_type=F32)
    return lax.dot_general(a.astype(BF16), b.astype(BF16), dn, preferred_element_type=F32)


def _dot_tn(a, b, exact=False):
    dn = (((0,), (0,)), ((), ()))
    if exact:
        return lax.dot_general(a, b, dn, precision=HIGHEST, preferred_element_type=F32)
    return lax.dot_general(a.astype(BF16), b.astype(BF16), dn, preferred_element_type=F32)


def _sigmoid(x):
    return 1.0 / (1.0 + jnp.exp(-x))


def _block_ones(n, blk, value=1.0):
    i = np.arange(n) // blk
    return jnp.asarray((i[:, None] == i[None, :]).astype(np.float32) * value)


def _const_spec(shape):
    nd = len(shape)
    return pl.BlockSpec(shape, lambda *_: (0,) * nd)


def _mod_kernel(c_ref, w_ref, b_ref, o_ref):
    c = c_ref[...]
    o_ref[...] = _dot_f32(c * _sigmoid(c), w_ref[...]) + b_ref[...]


def modulation_vectors(c_rows, mod_w, mod_b):
    depth, d, n = mod_w.shape
    tn = 1536
    return pl.pallas_call(
        _mod_kernel,
        grid=(depth, n // tn),
        in_specs=[
            pl.BlockSpec((SUBLANES, d), lambda l, j: (0, 0)),
            pl.BlockSpec((None, d, tn), lambda l, j: (l, 0, j)),
            pl.BlockSpec((None, 1, tn), lambda l, j: (l, 0, j)),
        ],
        out_specs=pl.BlockSpec((None, SUBLANES, tn), lambda l, j: (l, 0, j)),
        out_shape=jax.ShapeDtypeStruct((depth, SUBLANES, n), F32),
        compiler_params=_cparams("parallel", "parallel"),
    )(c_rows, mod_w, mod_b.reshape(depth, 1, n))


def _modulated_norm(x, gain, scale, shift):
    ms = jnp.mean(x * x, axis=-1, keepdims=True)
    return (x * lax.rsqrt(ms + NORM_EPS) * gain) * (1.0 + scale) + shift


def _inproj_kernel(splits, x_ref, gain_ref, sc_ref, sh_ref, w_ref, *o_refs):
    h = _modulated_norm(x_ref[...], gain_ref[...], sc_ref[...], sh_ref[...]).astype(BF16)
    for (lo, hi), o_ref in zip(splits, o_refs):
        o_ref[...] = jnp.dot(h, w_ref[:, lo:hi], preferred_element_type=F32)


def input_projection(x, gain, scale, shift, w_in_bf16, group_w):
    b, t, d = x.shape
    rw_cols = w_in_bf16.shape[1] - 3 * group_w - 2 * group_w
    cuts = [0, rw_cols, rw_cols + 3 * group_w, rw_cols + 4 * group_w, rw_cols + 5 * group_w]
    splits = tuple((cuts[i], cuts[i + 1]) for i in range(4))
    tm = min(512, t)
    row = pl.BlockSpec((None, 1, d), lambda bi, i: (bi, 0, 0))
    return pl.pallas_call(
        functools.partial(_inproj_kernel, splits),
        grid=(b, t // tm),
        in_specs=[
            pl.BlockSpec((None, tm, d), lambda bi, i: (bi, i, 0)),
            _const_spec((1, d)),
            row, row,
            _const_spec(w_in_bf16.shape),
        ],
        out_specs=[pl.BlockSpec((None, tm, hi - lo), lambda bi, i: (bi, i, 0)) for lo, hi in splits],
        out_shape=[jax.ShapeDtypeStruct((b, t, hi - lo), F32) for lo, hi in splits],
        compiler_params=_cparams("parallel", "parallel"),
    )(x, gain.reshape(1, d), scale, shift, w_in_bf16)


def _halo_specs(tm, t, width):
    nb8 = t // SUBLANES
    r8 = tm // SUBLANES
    return [
        pl.BlockSpec((None, tm, width), lambda b, i: (b, i, 0)),
        pl.BlockSpec((None, SUBLANES, width), lambda b, i: (b, jnp.maximum(i * r8 - 1, 0), 0)),
        pl.BlockSpec((None, SUBLANES, width), lambda b, i: (b, jnp.minimum((i + 1) * r8, nb8 - 1), 0)),
    ]


def _stage_with_halo(buf_ref, main_ref, prev_ref, next_ref):
    tm = main_ref.shape[0]
    i = pl.program_id(1)
    n = pl.num_programs(1)
    buf_ref[SUBLANES:SUBLANES + tm, :] = main_ref[...]
    buf_ref[0:SUBLANES, :] = jnp.where(i > 0, prev_ref[...], 0.0)
    buf_ref[SUBLANES + tm:2 * SUBLANES + tm, :] = jnp.where(i < n - 1, next_ref[...], 0.0)


def _rwkv_prep_kernel(gw, rw_ref, prev_ref, next_ref, mup_ref, mun_ref, kk_ref_, ka_ref, w0_ref, wup_ref,
                      a0_ref, aup_ref, gup_ref, hsum_ref,
                      r_o, k_o, v_o, kk_o, gate_o, lwf_o, kaf_o, kdf_o, lwb_o, kab_o, kdb_o, buf_ref):
    tm = rw_ref.shape[0]
    _stage_with_halo(buf_ref, rw_ref, prev_ref, next_ref)
    p = buf_ref[SUBLANES:SUBLANES + tm, :]
    prev = buf_ref[SUBLANES - 1:SUBLANES - 1 + tm, :]
    nxt = buf_ref[SUBLANES + 1:SUBLANES + 1 + tm, :]
    u = p + mup_ref[...] * (prev - p) + mun_ref[...] * (nxt - p)
    r = u[:, 0:gw]
    k = u[:, gw:2 * gw]
    v = u[:, 2 * gw:3 * gw]
    lora_w = u[:, 3 * gw:3 * gw + LANES]
    lora_a = u[:, 3 * gw + LANES:3 * gw + 2 * LANES]
    g = u[:, 3 * gw + 2 * LANES:3 * gw + 3 * LANES]
    kk = k * kk_ref_[...]
    ss = _dot_f32(kk * kk, hsum_ref[...])
    kk = kk * lax.rsqrt(jnp.maximum(ss, 1e-24))
    zw = _dot_f32(jnp.tanh(lora_w), wup_ref[...]) + w0_ref[...]
    za = _dot_f32(lora_a, aup_ref[...]) + a0_ref[...]
    logw = -_sigmoid(zw) * math.exp(-0.5)
    a = _sigmoid(za)
    r_o[...] = r
    k_o[...] = k
    v_o[...] = v
    kk_o[...] = kk
    gate_o[...] = _dot_f32(_sigmoid(g), gup_ref[...])
    ka = ka_ref[...]
    for d, (lw_o, kka_o, kd_o) in enumerate(((lwf_o, kaf_o, kdf_o), (lwb_o, kab_o, kdb_o))):
        a_d = a[:, d * gw:(d + 1) * gw]
        lw_o[...] = logw[:, d * gw:(d + 1) * gw]
        kka_o[...] = kk * a_d
        kd_o[...] = k * (1.0 + (a_d - 1.0) * ka)


def _blockdiag2(m):
    r, c = m.shape[1:]
    z = jnp.zeros((r, c), m.dtype)
    return jnp.concatenate([jnp.concatenate([m[0], z], 1), jnp.concatenate([z, m[1]], 1)], 0)


def rwkv_prepare(rw, params, gw):
    (mu_prev, mu_next, w0, w_up, a0, a_up, g_up, k_k, k_a, r_k, ln_w, ln_b) = params
    b, t, cols = rw.shape
    tm = min(512, t)
    row = lambda v: v.reshape(1, -1)
    small = [row(mu_prev), row(mu_next), row(k_k), row(k_a), row(w0), _blockdiag2(w_up), row(a0),
             _blockdiag2(a_up), g_up, _block_ones(gw, HEAD)]
    out = jax.ShapeDtypeStruct((b, t, gw), F32)
    return pl.pallas_call(
        functools.partial(_rwkv_prep_kernel, gw),
        grid=(b, t // tm),
        in_specs=_halo_specs(tm, t, cols) + [_const_spec(s.shape) for s in small],
        out_specs=[pl.BlockSpec((None, tm, gw), lambda bi, i: (bi, i, 0))] * 11,
        out_shape=[out] * 11,
        scratch_shapes=[pltpu.VMEM((tm + 2 * SUBLANES, cols), F32)],
        compiler_params=_cparams("parallel", "parallel"),
    )(rw, rw, rw, *small)


def _unit_triangular_inverse(n_strict):
    size = n_strict.shape[0]
    rows = lax.broadcasted_iota(jnp.int32, (size, size), 0)
    cols = lax.broadcasted_iota(jnp.int32, (size, size), 1)
    acc = jnp.where(rows == cols, 1.0, 0.0) + n_strict
    power = n_strict
    span = 2
    while span < size:
        power = _dot_f32(power, power)
        acc = acc + _dot_f32(acc, power)
        span *= 2
    return acc


def _chunk_pair(reverse, r, kd, v, ka, kk, logw):
    L = r.shape[0]
    rows = lax.broadcasted_iota(jnp.int32, (L, L), 0)
    cols = lax.broadcasted_iota(jnp.int32, (L, L), 1)
    if reverse:
        incl = cols >= rows
        strict = cols > rows
    else:
        incl = cols <= rows
        strict = cols < rows
    tri = jnp.where(incl, 1.0, 0.0)
    cum = _dot_f32(tri, logw)
    total = jnp.sum(logw, axis=0, keepdims=True)
    g_incl = jnp.exp(cum)
    g_excl = jnp.exp(cum - logw)
    g_inv = jnp.exp(-cum)
    g_tail = jnp.exp(total - cum)
    a = -ka
    bd = kk * g_excl
    rd = r * g_incl
    a_inv = a * g_inv
    k_inv = kd * g_inv
    a_tail = a * g_tail
    k_tail = kd * g_tail
    lane = lax.broadcasted_iota(jnp.int32, (1, LANES), 1)
    lhs = jnp.concatenate([bd, rd], axis=0)
    rhs = jnp.concatenate([a_inv, k_inv], axis=0)
    w1 = jnp.zeros((L, LANES), F32)
    pm = jnp.zeros((L, LANES), F32)
    y0 = jnp.zeros((L, LANES), F32)
    q = jnp.zeros((L, LANES), F32)
    for h in range(2):
        hm = (lane >= h * HEAD) & (lane < (h + 1) * HEAD)
        gram = _dot_nt(jnp.where(hm, lhs, 0.0), rhs)
        aab = jnp.where(strict, gram[:L, :L], 0.0)
        aak = jnp.where(strict, gram[:L, L:], 0.0)
        ara = jnp.where(incl, gram[L:, :L], 0.0)
        ark = jnp.where(incl, gram[L:, L:], 0.0)
        tinv = _unit_triangular_inverse(aab)
        w1_h = _dot_f32(tinv, _dot(aak, v))
        p_h = _dot_f32(tinv, bd)
        y0_h = _dot(ark, v) + _dot(ara, w1_h)
        q_h = rd + _dot(ara, p_h)
        w1 = jnp.where(hm, w1_h, w1)
        pm = jnp.where(hm, p_h, pm)
        y0 = jnp.where(hm, y0_h, y0)
        q = jnp.where(hm, q_h, q)
    r2 = lax.broadcasted_iota(jnp.int32, (LANES, LANES), 0)
    c2 = lax.broadcasted_iota(jnp.int32, (LANES, LANES), 1)
    same_head = (r2 // HEAD) == (c2 // HEAD)
    g0 = jnp.where(same_head, _dot_tn(k_tail, v) + _dot_tn(a_tail, w1), 0.0)
    m = jnp.where(same_head, _dot_tn(a_tail, pm), 0.0) + jnp.where(r2 == c2, jnp.exp(total), 0.0)
    return y0, q, m, g0


def _rwkv_chunk_kernel(r_ref, v_ref, kk_ref, lwf_ref, kaf_ref, kdf_ref, lwb_ref, kab_ref, kdb_ref,
                       y0f_o, qf_o, mf_o, g0f_o, y0b_o, qb_o, mb_o, g0b_o):
    for pair in range(2):
        sl = slice(pair * LANES, (pair + 1) * LANES)
        r = r_ref[:, sl]
        v = v_ref[:, sl]
        kk = kk_ref[:, sl]
        for reverse, lw_ref, ka_ref, kd_ref, y0_o, q_o, m_o, g0_o in (
                (False, lwf_ref, kaf_ref, kdf_ref, y0f_o, qf_o, mf_o, g0f_o),
                (True, lwb_ref, kab_ref, kdb_ref, y0b_o, qb_o, mb_o, g0b_o)):
            y0, q, m, g0 = _chunk_pair(reverse, r, kd_ref[:, sl], v, ka_ref[:, sl], kk, lw_ref[:, sl])
            y0_o[:, sl] = y0
            q_o[:, sl] = q
            m_o[pair] = m
            g0_o[pair] = g0


def rwkv_chunks(r, v, kk, lwf, kaf, kdf, lwb, kab, kdb):
    b, t, gw = r.shape
    L = RW_CHUNK
    nc = t // L
    tok = pl.BlockSpec((None, L, gw), lambda bi, i: (bi, i, 0))
    mat = pl.BlockSpec((None, None, 2, LANES, LANES), lambda bi, i: (bi, i, 0, 0, 0))
    tok_s = jax.ShapeDtypeStruct((b, t, gw), F32)
    mat_s = jax.ShapeDtypeStruct((b, nc, 2, LANES, LANES), F32)
    return pl.pallas_call(
        _rwkv_chunk_kernel,
        grid=(b, nc),
        in_specs=[tok] * 9,
        out_specs=[tok, tok, mat, mat] * 2,
        out_shape=[tok_s, tok_s, mat_s, mat_s] * 2,
        compiler_params=_cparams("parallel", "parallel"),
    )(r, v, kk, lwf, kaf, kdf, lwb, kab, kdb)


def _rwkv_scan_kernel(cps, h0_ref, y0f_ref, qf_ref, mf_ref, g0f_ref, y0b_ref, qb_ref, mb_ref, g0b_ref,
                      yf_o, yb_o, hfin_o, h_ref):
    L = RW_CHUNK
    j = pl.program_id(1)

    @pl.when(j == 0)
    def _():
        h_ref[...] = h0_ref[...]

    for step in range(cps):
        for d, (y0_ref, q_ref, m_ref, g0_ref, y_o) in enumerate(
                ((y0f_ref, qf_ref, mf_ref, g0f_ref, yf_o), (y0b_ref, qb_ref, mb_ref, g0b_ref, yb_o))):
            c = step if d == 0 else cps - 1 - step
            rows = slice(c * L, (c + 1) * L)
            for pair in range(2):
                sl = slice(pair * LANES, (pair + 1) * LANES)
                h = h_ref[d, pair]
                y_o[rows, sl] = y0_ref[rows, sl] + _dot_f32(q_ref[rows, sl], h)
                h_ref[d, pair] = _dot_f32(m_ref[c, pair], h) + g0_ref[c, pair]

    @pl.when(j == pl.num_programs(1) - 1)
    def _():
        hfin_o[...] = h_ref[...]


def rwkv_scan(h0, y0f, qf, mf, g0f, y0b, qb, mb, g0b):
    b, t, gw = y0f.shape
    L = RW_CHUNK
    nc = t // L
    cps = min(RW_CHUNKS_PER_STEP, nc)
    nb = nc // cps
    tm = cps * L
    tok_f = pl.BlockSpec((None, tm, gw), lambda bi, i: (bi, i, 0))
    tok_b = pl.BlockSpec((None, tm, gw), lambda bi, i: (bi, nb - 1 - i, 0))
    mat_f = pl.BlockSpec((None, cps, 2, LANES, LANES), lambda bi, i: (bi, i, 0, 0, 0))
    mat_b = pl.BlockSpec((None, cps, 2, LANES, LANES), lambda bi, i: (bi, nb - 1 - i, 0, 0, 0))
    st = pl.BlockSpec((None, 2, 2, LANES, LANES), lambda bi, i: (bi, 0, 0, 0, 0))
    tok_s = jax.ShapeDtypeStruct((b, t, gw), F32)
    return pl.pallas_call(
        functools.partial(_rwkv_scan_kernel, cps),
        grid=(b, nb),
        in_specs=[st, tok_f, tok_f, mat_f, mat_f, tok_b, tok_b, mat_b, mat_b],
        out_specs=[tok_f, tok_b, st],
        out_shape=[tok_s, tok_s, jax.ShapeDtypeStruct((b, 2, 2, LANES, LANES), F32)],
        scratch_shapes=[pltpu.VMEM((2, 2, LANES, LANES), F32)],
        compiler_params=_cparams("parallel", "arbitrary"),
    )(h0, y0f, qf, mf, g0f, y0b, qb, mb, g0b)


def _rwkv_out_kernel(yf_ref, yb_ref, r_ref, k_ref, v_ref, gate_ref, rk_ref, lnw_ref, lnb_ref, hmean_ref, o_ref):
    y = yf_ref[...] + yb_ref[...]
    hmean = hmean_ref[...]
    mu = _dot_f32(y, hmean)
    yc = y - mu
    var = _dot_f32(yc * yc, hmean)
    yn = yc * lax.rsqrt(var + RW_LN_EPS) * lnw_ref[...] + lnb_ref[...]
    bonus = _dot_f32(r_ref[...] * k_ref[...] * rk_ref[...], hmean) * float(HEAD) * v_ref[...]
    o_ref[...] = (yn + bonus) * gate_ref[...]


def rwkv_output(yf, yb, r, k, v, gate, r_k, ln_w, ln_b):
    b, t, gw = yf.shape
    tm = min(512, t)
    tok = pl.BlockSpec((None, tm, gw), lambda bi, i: (bi, i, 0))
    small = [r_k.reshape(1, gw), ln_w.reshape(1, gw), ln_b.reshape(1, gw), _block_ones(gw, HEAD, 1.0 / HEAD)]
    return pl.pallas_call(
        _rwkv_out_kernel,
        grid=(b, t // tm),
        in_specs=[tok] * 6 + [_const_spec(s.shape) for s in small],
        out_specs=tok,
        out_shape=jax.ShapeDtypeStruct((b, t, gw), F32),
        compiler_params=_cparams("parallel", "parallel"),
    )(yf, yb, r, k, v, gate, *small)


def rwkv_mixer(rwx, rwc, params, gw, ctx_out):
    r_k, ln_w, ln_b = params[9], params[10], params[11]
    sx = rwkv_prepare(rwx, params, gw)
    sc = rwkv_prepare(rwc, params, gw)
    (rx, kx, vx, kkx, gx), dx = sx[:5], sx[5:]
    (rc, kc, vc, kkc, gc), dc = sc[:5], sc[5:]
    cx = rwkv_chunks(rx, vx, kkx, *dx)
    cc = rwkv_chunks(rc, vc, kkc, *dc)
    b = rwx.shape[0]
    h0 = jnp.zeros((b, 2, 2, LANES, LANES), F32)
    ycf, ycb, h_ctx = rwkv_scan(h0, *cc)
    yxf, yxb, _ = rwkv_scan(h_ctx, *cx)
    out_x = rwkv_output(yxf, yxb, rx, kx, vx, gx, r_k, ln_w, ln_b)
    out_c = rwkv_output(ycf, ycb, rc, kc, vc, gc, r_k, ln_w, ln_b) if ctx_out else None
    return out_x, out_c


def _rope_tables(n_tokens, reps):
    rows = n_tokens // GRID_W
    row = np.repeat(np.arange(rows), GRID_W).astype(np.float64)
    col = np.tile(np.arange(GRID_W), rows).astype(np.float64)
    n_freq = DA_QK // 4
    inv = ROPE_BASE ** (-np.arange(n_freq, dtype=np.float64) / n_freq)
    ar = row[:, None] * inv
    ac = col[:, None] * inv
    ang = np.concatenate([ar, ar, ac, ac], axis=-1)
    cos = np.tile(np.cos(ang), (1, reps)).astype(np.float32)
    sin = np.tile(np.sin(ang), (1, reps)).astype(np.float32)
    return jnp.asarray(cos), jnp.asarray(sin)


def _attn_prep_kernel(gw, rope, da_ref, qg_ref, kg_ref, gmean_ref, *rest):
    if rope:
        cos_ref, sin_ref, q_o, k_o, v_o = rest
    else:
        q_o, k_o, v_o = rest
    da = da_ref[...]
    gmean = gmean_ref[...]
    lane = lax.broadcasted_iota(jnp.int32, (1, gw), 1)
    first_half = (lane % (DA_QK // 2)) < (DA_QK // 4)

    def norm_rope(x, gain):
        ms = _dot_f32(x * x, gmean)
        y = x * lax.rsqrt(ms + NORM_EPS) * gain
        if rope:
            quarter = DA_QK // 4
            rot = jnp.where(first_half, -pltpu.roll(y, gw - quarter, 1), pltpu.roll(y, quarter, 1))
            y = y * cos_ref[...] + rot * sin_ref[...]
        return y

    q_o[...] = (norm_rope(da[:, 0:gw], qg_ref[...]) * (DA_QK ** -0.5)).astype(q_o.dtype)
    k_o[...] = norm_rope(da[:, gw:2 * gw], kg_ref[...]).astype(k_o.dtype)
    v_o[...] = da[:, 2 * gw:3 * gw].astype(v_o.dtype)


def attention_prepare(da, q_gain, k_gain, gw, rope):
    b, t, cols = da.shape
    tm = min(512, t)
    reps = gw // DA_QK
    small = [jnp.tile(q_gain, reps).reshape(1, gw), jnp.tile(k_gain, reps).reshape(1, gw),
             _block_ones(gw, DA_QK, 1.0 / DA_QK)]
    args = [da] + small
    in_specs = [pl.BlockSpec((None, tm, cols), lambda bi, i: (bi, i, 0))] + [_const_spec(s.shape) for s in small]
    if rope:
        cos, sin = _rope_tables(t, reps)
        args += [cos, sin]
        in_specs += [pl.BlockSpec((tm, gw), lambda bi, i: (i, 0))] * 2
    tok = pl.BlockSpec((None, tm, gw), lambda bi, i: (bi, i, 0))
    return pl.pallas_call(
        functools.partial(_attn_prep_kernel, gw, rope),
        grid=(b, t // tm),
        in_specs=in_specs,
        out_specs=[tok] * 3,
        out_shape=[jax.ShapeDtypeStruct((b, t, gw), BF16)] * 3,
        compiler_params=_cparams("parallel", "parallel"),
    )(*args)


def _flash_kernel(lam_init, q_ref, kt_ref, v_ref, lq1_ref, lk1_ref, lq2_ref, lk2_ref, sg_ref, o_ref,
                  m_ref, l_ref, acc_ref):
    j = pl.program_id(3)

    @pl.when(j == 0)
    def _():
        m_ref[...] = jnp.full(m_ref.shape, -1e30, F32)
        l_ref[...] = jnp.zeros(l_ref.shape, F32)
        acc_ref[...] = jnp.zeros(acc_ref.shape, F32)

    v = v_ref[...]
    for m in range(2):
        s = jnp.dot(q_ref[m], kt_ref[m], preferred_element_type=F32)
        m_old = m_ref[m]
        m_new = jnp.maximum(m_old, jnp.max(s, axis=-1, keepdims=True))
        alpha = jnp.exp(m_old - m_new)
        p = jnp.exp(s - m_new)
        l_ref[m] = alpha * l_ref[m] + jnp.sum(p, axis=-1, keepdims=True)
        acc_ref[m] = alpha * acc_ref[m] + jnp.dot(p.astype(BF16), v, preferred_element_type=F32)
        m_ref[m] = m_new

    @pl.when(j == pl.num_programs(3) - 1)
    def _():
        lam = (jnp.exp(jnp.sum(lq1_ref[...] * lk1_ref[...], axis=-1, keepdims=True))
               - jnp.exp(jnp.sum(lq2_ref[...] * lk2_ref[...], axis=-1, keepdims=True)) + lam_init)
        o = acc_ref[0] / l_ref[0] - lam * (acc_ref[1] / l_ref[1])
        ms = jnp.mean(o * o, axis=-1, keepdims=True)
        o_ref[...] = o * lax.rsqrt(ms + NORM_EPS) * sg_ref[...] * (1.0 - lam_init)


def _largest_divisor(n, cap, multiple):
    best = None
    for d in range(multiple, cap + 1, multiple):
        if n % d == 0:
            best = d
    return best if best is not None else n


def diff_attention_core(q, kt, v, lam_params, sub_gain, lam_init):
    b, h, _, t, dk = q.shape
    s = kt.shape[-1]
    tq = min(512, t)
    tk = _largest_divisor(s, 1536, LANES)
    small = [p.reshape(1, -1) for p in lam_params] + [sub_gain.reshape(1, -1)]
    return pl.pallas_call(
        functools.partial(_flash_kernel, lam_init),
        grid=(b, h, t // tq, s // tk),
        in_specs=[
            pl.BlockSpec((None, None, 2, tq, dk), lambda bi, hi, i, j: (bi, hi, 0, i, 0)),
            pl.BlockSpec((None, None, 2, dk, tk), lambda bi, hi, i, j: (bi, hi, 0, 0, j)),
            pl.BlockSpec((None, None, tk, HEAD), lambda bi, hi, i, j: (bi, hi, j, 0)),
        ] + [_const_spec(x.shape) for x in small],
        out_specs=pl.BlockSpec((None, None, tq, HEAD), lambda bi, hi, i, j: (bi, hi, i, 0)),
        out_shape=jax.ShapeDtypeStruct((b, h, t, HEAD), F32),
        scratch_shapes=[pltpu.VMEM((2, tq, 1), F32), pltpu.VMEM((2, tq, 1), F32), pltpu.VMEM((2, tq, HEAD), F32)],
        compiler_params=_cparams("parallel", "parallel", "parallel", "arbitrary"),
    )(q, kt, v, *small)


def _split_heads(q, k, v):
    b, t, gw = q.shape
    h = gw // HEAD
    qh = q.reshape(b, t, h, 2, DA_QK).transpose(0, 2, 3, 1, 4)
    kth = k.reshape(b, t, h, 2, DA_QK).transpose(0, 2, 3, 4, 1)
    vh = v.reshape(b, t, h, HEAD).transpose(0, 2, 1, 3)
    return qh, kth, vh


def diff_attention(dax, dac, params, gw, lam_init, ctx_out):
    q_gain, k_gain, lq1, lk1, lq2, lk2, sub_gain = params
    qx, ktx, vx = _split_heads(*attention_prepare(dax, q_gain, k_gain, gw, rope=True))
    qc, ktc, vc = _split_heads(*attention_prepare(dac, q_gain, k_gain, gw, rope=False))
    kt = jnp.concatenate([ktx, ktc], axis=-1)
    v = jnp.concatenate([vx, vc], axis=2)
    lam_params = (lq1, lk1, lq2, lk2)
    merge = lambda o: o.transpose(0, 2, 1, 3).reshape(o.shape[0], o.shape[2], gw)
    out_x = merge(diff_attention_core(qx, kt, v, lam_params, sub_gain, lam_init))
    out_c = merge(diff_attention_core(qc, ktc, vc, lam_params, sub_gain, lam_init)) if ctx_out else None
    return out_x, out_c


FT_RADIX = 64


def _dft_cos_sin(n, scale=1.0):
    i = np.arange(n)
    ang = 2.0 * np.pi * ((i[:, None] * i[None, :]) % n) / n
    return np.cos(ang) * scale, np.sin(ang) * scale


def _channel_dft(gw, scale):
    c, s = _dft_cos_sin(HEAD, scale)
    eye = np.eye(gw // HEAD)
    return jnp.asarray(np.concatenate([np.kron(eye, c), np.kron(eye, s)], axis=0).astype(np.float32))


def _fnet_stage1_kernel(z_ref, gr_ref, gi_ref, or_ref, oi_ref):
    for j in range(z_ref.shape[1]):
        x = z_ref[:, j, :]
        or_ref[j] = _dot_f32(gr_ref[j], x)
        oi_ref[j] = _dot_f32(gi_ref[j], x)


def _fnet_stage2_kernel(gw, br_ref, bi_ref, rot_ref, chan_ref, wf_ref, o_ref):
    for g in range(br_ref.shape[1] // gw):
        sl = slice(g * gw, (g + 1) * gw)
        p = _dot_f32(rot_ref[...], jnp.concatenate([br_ref[:, sl], bi_ref[:, sl]], axis=0))
        n1 = p.shape[0] // 2
        f = _dot_f32(jnp.concatenate([p[:n1], p[n1:]], axis=1), chan_ref[...])
        o_ref[:, sl] = _dot_f32(f, wf_ref[...])


def fourier_mix_long(z, w_f):
    b, t, gw = z.shape
    n1 = FT_RADIX
    n2 = t // n1
    k2 = np.arange(n2)[None, :, None]
    n = np.arange(n1)[:, None, None] + n1 * np.arange(n2)[None, None, :]
    ang = 2.0 * np.pi * ((k2 * n) % t) / t
    g_r = jnp.asarray(np.cos(ang).astype(np.float32))
    g_i = jnp.asarray((-np.sin(ang)).astype(np.float32))
    j8 = SUBLANES
    br, bi = pl.pallas_call(
        _fnet_stage1_kernel,
        grid=(b, n1 // j8),
        in_specs=[
            pl.BlockSpec((None, n2, j8, gw), lambda bi_, i: (bi_, 0, i, 0)),
            pl.BlockSpec((j8, n2, n2), lambda bi_, i: (i, 0, 0)),
            pl.BlockSpec((j8, n2, n2), lambda bi_, i: (i, 0, 0)),
        ],
        out_specs=[pl.BlockSpec((None, j8, n2, gw), lambda bi_, i: (bi_, i, 0, 0))] * 2,
        out_shape=[jax.ShapeDtypeStruct((b, n1, n2, gw), F32)] * 2,
        compiler_params=_cparams("parallel", "parallel"),
    )(z.reshape(b, n2, n1, gw), g_r, g_i)
    c64, s64 = _dft_cos_sin(n1)
    rot = jnp.asarray(np.block([[c64, s64], [-s64, c64]]).astype(np.float32))
    chan = _channel_dft(gw, 1.0 / math.sqrt(t * HEAD))
    cols = n2 * gw
    tc = min(2048, cols)
    out = pl.pallas_call(
        functools.partial(_fnet_stage2_kernel, gw),
        grid=(b, cols // tc),
        in_specs=[
            pl.BlockSpec((None, n1, tc), lambda bi_, i: (bi_, 0, i)),
            pl.BlockSpec((None, n1, tc), lambda bi_, i: (bi_, 0, i)),
            _const_spec(rot.shape), _const_spec(chan.shape), _const_spec(w_f.shape),
        ],
        out_specs=pl.BlockSpec((None, n1, tc), lambda bi_, i: (bi_, 0, i)),
        out_shape=jax.ShapeDtypeStruct((b, n1, cols), F32),
        compiler_params=_cparams("parallel", "parallel"),
    )(br.reshape(b, n1, cols), bi.reshape(b, n1, cols), rot, chan, w_f)
    return out.reshape(b, t, gw)


def _fnet_dense_kernel(z_ref, ct_ref, st_ref, chan_ref, wf_ref, o_ref):
    z = z_ref[...]
    pr = _dot_f32(ct_ref[...], z)
    pi = -_dot_f32(st_ref[...], z)
    f = _dot_f32(jnp.concatenate([pr, pi], axis=1), chan_ref[...])
    o_ref[...] = _dot_f32(f, wf_ref[...])


def fourier_mix_short(z, w_f):
    b, t, gw = z.shape
    ct, st = _dft_cos_sin(t)
    ct = jnp.asarray(ct.astype(np.float32))
    st = jnp.asarray(st.astype(np.float32))
    chan = _channel_dft(gw, 1.0 / math.sqrt(t * HEAD))
    tok = pl.BlockSpec((None, t, gw), lambda bi: (bi, 0, 0))
    return pl.pallas_call(
        _fnet_dense_kernel,
        grid=(b,),
        in_specs=[tok, _const_spec(ct.shape), _const_spec(st.shape), _const_spec(chan.shape), _const_spec(w_f.shape)],
        out_specs=tok,
        out_shape=jax.ShapeDtypeStruct((b, t, gw), F32),
        compiler_params=_cparams("parallel"),
    )(z, ct, st, chan, w_f)


def fourier_mix(z, w_f):
    t = z.shape[1]
    if t % (FT_RADIX * SUBLANES) == 0 and t // FT_RADIX >= LANES:
        return fourier_mix_long(z, w_f)
    return fourier_mix_short(z, w_f)


def _pool_kernel(t_total, u_ref, prev_ref, next_ref, w_ref, s_ref, o_ref, buf_ref):
    tm, gw = u_ref.shape
    _stage_with_halo(buf_ref, u_ref, prev_ref, next_ref)
    at = lambda off: buf_ref[SUBLANES + off:SUBLANES + off + tm, :]
    u = at(0)
    t = pl.program_id(1) * tm + lax.broadcasted_iota(jnp.int32, (tm, 1), 0)
    lane = lax.broadcasted_iota(jnp.int32, (1, gw), 1)
    group = lane // (gw // len(POOL_WINDOWS))
    mean = jnp.zeros((tm, gw), F32)
    run = jnp.zeros((tm, gw), F32)
    half_prev = 0
    for i, w in enumerate(POOL_WINDOWS):
        half = w // 2
        for off in range(half_prev, half):
            run = run + at(-off - 1) + at(off)
        half_prev = half
        cnt = (jnp.minimum(t + half, t_total) - jnp.maximum(t - half, 0)).astype(F32)
        mean = jnp.where(group == i, run / cnt, mean)
    o_ref[...] = _dot_f32(mean - u, w_ref[...]) * s_ref[...]


def pool_mix(u, w_p, s_p):
    b, t, gw = u.shape
    tm = min(512, t)
    nw, ch = w_p.shape[0], w_p.shape[1]
    w_bd = jnp.zeros((gw, gw), F32)
    for i in range(nw):
        w_bd = w_bd.at[i * ch:(i + 1) * ch, i * ch:(i + 1) * ch].set(w_p[i])
    return pl.pallas_call(
        functools.partial(_pool_kernel, t),
        grid=(b, t // tm),
        in_specs=_halo_specs(tm, t, gw) + [_const_spec((gw, gw)), _const_spec((1, gw))],
        out_specs=pl.BlockSpec((None, tm, gw), lambda bi, i: (bi, i, 0)),
        out_shape=jax.ShapeDtypeStruct((b, t, gw), F32),
        scratch_shapes=[pltpu.VMEM((tm + 2 * SUBLANES, gw), F32)],
        compiler_params=_cparams("parallel", "parallel"),
    )(u, u, u, w_bd, s_p.reshape(1, gw))


def _outproj_kernel(gw, x_ref, g_ref, a_ref, b_ref, f_ref, p_ref, w_ref, o_ref):
    acc = None
    for i, m_ref in enumerate((a_ref, b_ref, f_ref, p_ref)):
        part = jnp.dot(m_ref[...].astype(BF16), w_ref[i * gw:(i + 1) * gw, :], preferred_element_type=F32)
        acc = part if acc is None else acc + part
    o_ref[...] = x_ref[...] + g_ref[...] * acc


def output_projection(x, gate, mixers, w_out_bf16):
    b, t, d = x.shape
    gw = mixers[0].shape[-1]
    tm = min(512, t)
    tok = pl.BlockSpec((None, tm, gw), lambda bi, i: (bi, i, 0))
    xs = pl.BlockSpec((None, tm, d), lambda bi, i: (bi, i, 0))
    return pl.pallas_call(
        functools.partial(_outproj_kernel, gw),
        grid=(b, t // tm),
        in_specs=[xs, pl.BlockSpec((None, 1, d), lambda bi, i: (bi, 0, 0))] + [tok] * 4 + [_const_spec(w_out_bf16.shape)],
        out_specs=xs,
        out_shape=jax.ShapeDtypeStruct((b, t, d), F32),
        compiler_params=_cparams("parallel", "parallel"),
    )(x, gate, *mixers, w_out_bf16)


def _router_kernel(n_exp, x_ref, gain_ref, sc_ref, sh_ref, wr_ref, h_o, aff_o, afft_o):
    h = _modulated_norm(x_ref[...], gain_ref[...], sc_ref[...], sh_ref[...])
    h_o[...] = h
    logits = _dot_f32(h, wr_ref[...])
    lane = lax.broadcasted_iota(jnp.int32, logits.shape, 1)
    logits = jnp.where(lane < n_exp, logits, -1e30)
    e = jnp.exp(logits - jnp.max(logits, axis=-1, keepdims=True))
    aff = e / jnp.sum(e, axis=-1, keepdims=True)
    aff_o[...] = aff
    afft_o[...] = jnp.transpose(aff)[:n_exp, :]


def router(x, gain, scale, shift, w_router):
    b, t, d = x.shape
    n_exp = w_router.shape[1]
    tm = min(512, t)
    wr = jnp.zeros((d, LANES), F32).at[:, :n_exp].set(w_router)
    row = pl.BlockSpec((None, 1, d), lambda bi, i: (bi, 0, 0))
    return pl.pallas_call(
        functools.partial(_router_kernel, n_exp),
        grid=(b, t // tm),
        in_specs=[pl.BlockSpec((None, tm, d), lambda bi, i: (bi, i, 0)), _const_spec((1, d)), row, row,
                  _const_spec(wr.shape)],
        out_specs=[pl.BlockSpec((None, tm, d), lambda bi, i: (bi, i, 0)),
                   pl.BlockSpec((None, tm, LANES), lambda bi, i: (bi, i, 0)),
                   pl.BlockSpec((None, n_exp, tm), lambda bi, i: (bi, 0, i))],
        out_shape=[jax.ShapeDtypeStruct((b, t, d), F32), jax.ShapeDtypeStruct((b, t, LANES), F32),
                   jax.ShapeDtypeStruct((b, n_exp, t), F32)],
        compiler_params=_cparams("parallel", "parallel"),
    )(x, gain.reshape(1, d), scale, shift, wr)


TOPK_EXPONENT_STEPS = 7
TOPK_MANTISSA_STEPS = 44


def _row_cumsum(x_ref, o_ref, upper_ref):
    rows, t = x_ref.shape
    carry = jnp.zeros((rows, 1), F32)
    for g in range(t // LANES):
        sl = slice(g * LANES, (g + 1) * LANES)
        local = jnp.dot(x_ref[:, sl].astype(BF16), upper_ref[...], preferred_element_type=F32) + carry
        o_ref[:, sl] = local
        carry = local[:, LANES - 1:LANES]


def _topk_kernel(cap, aff_ref, upper_ref, idx_o, sel_ref, cs_ref):
    aff = aff_ref[...]
    rows, t = aff.shape
    capf = float(cap)
    count_ge = lambda thr: jnp.sum(jnp.where(aff >= thr, 1.0, 0.0), axis=-1, keepdims=True)
    hi = jnp.full((rows, 1), 2.0, F32)
    for step in reversed(range(TOPK_EXPONENT_STEPS)):
        cand = hi * (2.0 ** -(2 ** step))
        hi = jnp.where(count_ge(cand) < capf, cand, hi)
    lo = hi * 0.5
    lo = jnp.where(count_ge(lo) >= capf, lo, 0.0)

    def bisect(_, carry):
        lo, hi = carry
        mid = 0.5 * (lo + hi)
        enough = count_ge(mid) >= capf
        return jnp.where(enough, mid, lo), jnp.where(enough, hi, mid)

    lo, hi = lax.fori_loop(0, TOPK_MANTISSA_STEPS, bisect, (lo, hi))
    above = aff >= hi
    need = capf - count_ge(hi)
    sel_ref[...] = jnp.where((aff >= lo) & jnp.logical_not(above), 1.0, 0.0)
    _row_cumsum(sel_ref, cs_ref, upper_ref)
    tied_in = (sel_ref[...] > 0.5) & (cs_ref[...] <= need)
    sel_ref[...] = jnp.where(above | tied_in, 1.0, 0.0)
    _row_cumsum(sel_ref, cs_ref, upper_ref)

    ones = jnp.ones((SUBLANES, t), BF16)

    def compact(r, _):
        cs_row = cs_ref[pl.ds(r, 1), :]
        for cb in range(0, cap, LANES):
            n = min(LANES, cap - cb)
            slot = (lax.broadcasted_iota(jnp.int32, (n, 1), 0) + cb).astype(F32)
            reached = jnp.where(cs_row <= slot, 1.0, 0.0).astype(BF16)
            cnt = _dot_nt(ones, reached)
            idx_o[r, :, cb:cb + n] = cnt[0:1].astype(jnp.int32)
        return 0

    lax.fori_loop(0, rows, compact, 0)


def expert_choice_topk(aff_t, cap):
    b, n_exp, t = aff_t.shape
    rows = b * n_exp
    upper = jnp.asarray(np.triu(np.ones((LANES, LANES), np.float32))).astype(BF16)
    idx = pl.pallas_call(
        functools.partial(_topk_kernel, cap),
        grid=(1,),
        in_specs=[_const_spec((rows, t)), _const_spec(upper.shape)],
        out_specs=_const_spec((rows, 1, cap)),
        out_shape=jax.ShapeDtypeStruct((rows, 1, cap), jnp.int32),
        scratch_shapes=[pltpu.VMEM((rows, t), F32), pltpu.VMEM((rows, t), F32)],
        compiler_params=_cparams("arbitrary"),
    )(aff_t.reshape(rows, t), upper)
    return idx.reshape(rows * cap)


GATHER_UNROLL = 8


def _gather_kernel(cap, idx_ref, h_ref, aff_ref, xs_o, g_o):
    base = (pl.program_id(0) * pl.num_programs(2) + pl.program_id(2)) * cap

    def body(i, _):
        for u in range(GATHER_UNROLL):
            c = i * GATHER_UNROLL + u
            row = idx_ref[base + c]
            xs_o[pl.ds(c, 1), :] = h_ref[pl.ds(row, 1), :]
            g_o[pl.ds(c, 1), :] = aff_ref[pl.ds(row, 1), :]
        return 0

    lax.fori_loop(0, cap // GATHER_UNROLL, body, 0)


def gather_tokens(idx, h, aff, n_exp, cap):
    b, t, d = h.shape
    dh = d // 2
    return pl.pallas_call(
        functools.partial(_gather_kernel, cap),
        grid_spec=pltpu.PrefetchScalarGridSpec(
            num_scalar_prefetch=1,
            grid=(b, 2, n_exp),
            in_specs=[pl.BlockSpec((None, t, dh), lambda bi, hf, e, idx_: (bi, 0, hf)),
                      pl.BlockSpec((None, t, LANES), lambda bi, hf, e, idx_: (bi, 0, 0))],
            out_specs=[pl.BlockSpec((None, None, cap, dh), lambda bi, hf, e, idx_: (bi, e, 0, hf)),
                       pl.BlockSpec((None, None, None, cap, LANES), lambda bi, hf, e, idx_: (bi, e, hf, 0, 0))],
        ),
        out_shape=[jax.ShapeDtypeStruct((b, n_exp, cap, d), F32),
                   jax.ShapeDtypeStruct((b, n_exp, 2, cap, LANES), F32)],
        compiler_params=_cparams("parallel", "parallel", "arbitrary"),
    )(idx, h, aff)


def _expert_ffn_kernel(xs_ref, g_ref, wg_ref, wu_ref, wd_ref, o_ref, xb_ref, acc_ref):
    f = pl.program_id(2)

    @pl.when(f == 0)
    def _():
        xb_ref[...] = xs_ref[...].astype(BF16)
        acc_ref[...] = jnp.zeros(acc_ref.shape, F32)

    xb = xb_ref[...]
    gate = jnp.dot(xb, wg_ref[...].astype(BF16), preferred_element_type=F32)
    up = jnp.dot(xb, wu_ref[...].astype(BF16), preferred_element_type=F32)
    hid = gate * _sigmoid(gate) * up
    acc_ref[...] += jnp.dot(hid.astype(BF16), wd_ref[...].astype(BF16), preferred_element_type=F32)

    @pl.when(f == pl.num_programs(2) - 1)
    def _():
        e = pl.program_id(0)
        lane = lax.broadcasted_iota(jnp.int32, g_ref.shape, 1)
        g = jnp.sum(jnp.where(lane == e, g_ref[...], 0.0), axis=-1, keepdims=True)
        o_ref[...] = acc_ref[...] * g


def expert_ffn(xs, g_rows, w_gate, w_up, w_down):
    b, n_exp, cap, d = xs.shape
    f_dim = w_gate.shape[-1]
    tf = _largest_divisor(f_dim, 256, LANES)
    return pl.pallas_call(
        _expert_ffn_kernel,
        grid=(n_exp, b, f_dim // tf),
        in_specs=[
            pl.BlockSpec((None, None, cap, d), lambda e, bi, f: (bi, e, 0, 0)),
            pl.BlockSpec((None, None, None, cap, LANES), lambda e, bi, f: (bi, e, 0, 0, 0)),
            pl.BlockSpec((None, d, tf), lambda e, bi, f: (e, 0, f)),
            pl.BlockSpec((None, d, tf), lambda e, bi, f: (e, 0, f)),
            pl.BlockSpec((None, tf, d), lambda e, bi, f: (e, f, 0)),
        ],
        out_specs=pl.BlockSpec((None, None, cap, d), lambda e, bi, f: (bi, e, 0, 0)),
        out_shape=jax.ShapeDtypeStruct((b, n_exp, cap, d), F32),
        scratch_shapes=[pltpu.VMEM((cap, d), BF16), pltpu.VMEM((cap, d), F32)],
        compiler_params=_cparams("parallel", "parallel", "arbitrary"),
    )(xs, g_rows, w_gate, w_up, w_down)


def _scatter_kernel(cap, idx_ref, eo_ref, x_ref, g_ref, o_ref):
    e = pl.program_id(2)
    base = (pl.program_id(0) * pl.num_programs(2) + e) * cap

    @pl.when(e == 0)
    def _():
        o_ref[...] = jnp.zeros(o_ref.shape, F32)

    def body(i, _):
        for u in range(GATHER_UNROLL):
            c = i * GATHER_UNROLL + u
            row = idx_ref[base + c]
            o_ref[pl.ds(row, 1), :] = o_ref[pl.ds(row, 1), :] + eo_ref[pl.ds(c, 1), :]
        return 0

    lax.fori_loop(0, cap // GATHER_UNROLL, body, 0)

    @pl.when(e == pl.num_programs(2) - 1)
    def _():
        o_ref[...] = x_ref[...] + g_ref[...] * o_ref[...]


def scatter_residual(idx, expert_out, x, gate, cap):
    b, n_exp, _, d = expert_out.shape
    t = x.shape[1]
    dq = d // 4
    return pl.pallas_call(
        functools.partial(_scatter_kernel, cap),
        grid_spec=pltpu.PrefetchScalarGridSpec(
            num_scalar_prefetch=1,
            grid=(b, 4, n_exp),
            in_specs=[pl.BlockSpec((None, None, cap, dq), lambda bi, q, e, idx_: (bi, e, 0, q)),
                      pl.BlockSpec((None, t, dq), lambda bi, q, e, idx_: (bi, 0, q)),
                      pl.BlockSpec((None, 1, dq), lambda bi, q, e, idx_: (bi, 0, q))],
            out_specs=pl.BlockSpec((None, t, dq), lambda bi, q, e, idx_: (bi, 0, q)),
        ),
        out_shape=jax.ShapeDtypeStruct((b, t, d), F32),
        compiler_params=_cparams("parallel", "parallel", "arbitrary"),
    )(idx, expert_out, x, gate)


def moe_residual(x, gain, scale, shift, gate, w_router, w_gate, w_up, w_down):
    t = x.shape[1]
    n_exp = w_router.shape[1]
    cap = CAPACITY_FACTOR * t // n_exp
    h, aff, aff_t = router(x, gain, scale, shift, w_router)
    idx = expert_choice_topk(aff_t, cap)
    xs, g_rows = gather_tokens(idx, h, aff, n_exp, cap)
    eo = expert_ffn(xs, g_rows, w_gate, w_up, w_down)
    return scatter_residual(idx, eo, x, gate, cap)


def kernel(x, c, ctx, c_ctx, mod_w, mod_b, norm1_w, norm2_w, w_in, rw_mu_prev, rw_mu_next, rw_w0, rw_w_up,
           rw_a0, rw_a_up, rw_g_up, rw_k_k, rw_k_a, rw_r_k, rw_ln_w, rw_ln_b, da_q_gain, da_k_gain, da_lq1,
           da_lk1, da_lq2, da_lk2, da_sub_gain, ft_w, pl_w, pl_scale, w_out, moe_router, moe_w_gate, moe_w_up,
           moe_w_down):
    depth, d = norm1_w.shape
    batch = x.shape[0]
    gw = d // N_MIXERS
    c_rows = jnp.zeros((SUBLANES, d), F32).at[:batch].set(c).at[batch].set(c_ctx)
    mod = modulation_vectors(c_rows, mod_w, mod_b)
    w_in_b = w_in.astype(BF16)
    w_out_b = w_out.astype(BF16)
    for l in range(depth):
        ctx_out = l < depth - 1
        lam_init = 0.8 - 0.6 * math.exp(-0.3 * l)
        mx = mod[l, :batch].reshape(batch, 6, 1, d)
        mc = jnp.broadcast_to(mod[l, batch].reshape(1, 6, 1, d), (batch, 6, 1, d))
        rw = (rw_mu_prev[l], rw_mu_next[l], rw_w0[l], rw_w_up[l], rw_a0[l], rw_a_up[l], rw_g_up[l],
              rw_k_k[l], rw_k_a[l], rw_r_k[l], rw_ln_w[l], rw_ln_b[l])
        da = (da_q_gain[l], da_k_gain[l], da_lq1[l], da_lk1[l], da_lq2[l], da_lk2[l], da_sub_gain[l])
        rwx, dax, ftx, plx = input_projection(x, norm1_w[l], mx[:, 1], mx[:, 0], w_in_b[l], gw)
        rwc, dac, ftc, plc = input_projection(ctx, norm1_w[l], mc[:, 1], mc[:, 0], w_in_b[l], gw)
        ax, ac = rwkv_mixer(rwx, rwc, rw, gw, ctx_out)
        bx, bc = diff_attention(dax, dac, da, gw, lam_init, ctx_out)
        fx = fourier_mix(ftx, ft_w[l])
        px = pool_mix(plx, pl_w[l], pl_scale[l])
        x = output_projection(x, mx[:, 2], (ax, bx, fx, px), w_out_b[l])
        x = moe_residual(x, norm2_w[l], mx[:, 4], mx[:, 3], mx[:, 5],
                         moe_router[l], moe_w_gate[l], moe_w_up[l], moe_w_down[l])
        if ctx_out:
            fc = fourier_mix(ftc, ft_w[l])
            pc = pool_mix(plc, pl_w[l], pl_scale[l])
            ctx = output_projection(ctx, mc[:, 2], (ac, bc, fc, pc), w_out_b[l])
            ctx = moe_residual(ctx, norm2_w[l], mc[:, 4], mc[:, 3], mc[:, 5],
                               moe_router[l], moe_w_gate[l], moe_w_up[l], moe_w_down[l])
    return x
```

```python
import functools
import math

import numpy as np
import jax
import jax.numpy as jnp
from jax import lax
from jax.experimental import pallas as pl
from jax.experimental.pallas import tpu as pltpu

F32 = jnp.float32
BF16 = jnp.bfloat16
HIGHEST = lax.Precision.HIGHEST

N_MIXERS = 4
HEAD = 64
NORM_EPS = 1e-6
RW_LN_EPS = 64e-5
GRID_W = 64
DA_QK = HEAD // 2
ROPE_BASE = 10000.0
POOL_WINDOWS = (2, 4, 8, 16)
N_EXPERTS = 16
CAPACITY_FACTOR = 2

LANES = 128
SUBLANES = 8
VMEM_LIMIT_BYTES = 56 * 1024 * 1024

RW_CHUNK = 64
RW_CHUNKS_PER_STEP = 8


def _cparams(*sem):
    return pltpu.CompilerParams(dimension_semantics=sem, vmem_limit_bytes=VMEM_LIMIT_BYTES)


def _dot(a, b):
    return jnp.dot(a.astype(BF16), b.astype(BF16), preferred_element_type=F32)


def _dot_f32(a, b):
    return jnp.dot(a, b, precision=HIGHEST, preferred_element_type=F32)


def _dot_tri(a, b):
    return _dot(a, b)


def _dot_nt(a, b, exact=False):
    dn = (((1,), (1,)), ((), ()))
    if exact:
        return lax.dot_general(a, b, dn, precision=HIGHEST, preferred_element_type=F32)
    return lax.dot_general(a.astype(BF16), b.astype(BF16), dn, preferred_element_type=F32)


def _dot_tn(a, b, exact=False):
    dn = (((0,), (0,)), ((), ()))
    if exact:
        return lax.dot_general(a, b, dn, precision=HIGHEST, preferred_element_type=F32)
    return lax.dot_general(a.astype(BF16), b.astype(BF16), dn, preferred_element_type=F32)


def _sigmoid(x):
    return 1.0 / (1.0 + jnp.exp(-x))


def _block_ones(n, blk, value=1.0):
    i = np.arange(n) // blk
    return jnp.asarray((i[:, None] == i[None, :]).astype(np.float32) * value)


def _const_spec(shape):
    nd = len(shape)
    return pl.BlockSpec(shape, lambda *_: (0,) * nd)


def _mod_kernel(c_ref, w_ref, b_ref, o_ref):
    c = c_ref[...]
    o_ref[...] = _dot_f32(c * _sigmoid(c), w_ref[...]) + b_ref[...]


def modulation_vectors(c_rows, mod_w, mod_b):
    depth, d, n = mod_w.shape
    tn = 1536
    return pl.pallas_call(
        _mod_kernel,
        grid=(depth, n // tn),
        in_specs=[
            pl.BlockSpec((SUBLANES, d), lambda l, j: (0, 0)),
            pl.BlockSpec((None, d, tn), lambda l, j: (l, 0, j)),
            pl.BlockSpec((None, 1, tn), lambda l, j: (l, 0, j)),
        ],
        out_specs=pl.BlockSpec((None, SUBLANES, tn), lambda l, j: (l, 0, j)),
        out_shape=jax.ShapeDtypeStruct((depth, SUBLANES, n), F32),
        compiler_params=_cparams("parallel", "parallel"),
    )(c_rows, mod_w, mod_b.reshape(depth, 1, n))


def _modulated_norm(x, gain, scale, shift):
    ms = jnp.mean(x * x, axis=-1, keepdims=True)
    return (x * lax.rsqrt(ms + NORM_EPS) * gain) * (1.0 + scale) + shift


def _inproj_kernel(splits, x_ref, gain_ref, sc_ref, sh_ref, w_ref, *o_refs):
    h = _modulated_norm(x_ref[...], gain_ref[...], sc_ref[...], sh_ref[...]).astype(BF16)
    for (lo, hi), o_ref in zip(splits, o_refs):
        o_ref[...] = jnp.dot(h, w_ref[:, lo:hi], preferred_element_type=F32)


def input_projection(x, gain, scale, shift, w_in_bf16, group_w):
    b, t, d = x.shape
    rw_cols = w_in_bf16.shape[1] - 3 * group_w - 2 * group_w
    cuts = [0, rw_cols, rw_cols + 3 * group_w, rw_cols + 4 * group_w, rw_cols + 5 * group_w]
    splits = tuple((cuts[i], cuts[i + 1]) for i in range(4))
    tm = min(512, t)
    row = pl.BlockSpec((None, 1, d), lambda bi, i: (bi, 0, 0))
    return pl.pallas_call(
        functools.partial(_inproj_kernel, splits),
        grid=(b, t // tm),
        in_specs=[
            pl.BlockSpec((None, tm, d), lambda bi, i: (bi, i, 0)),
            _const_spec((1, d)),
            row, row,
            _const_spec(w_in_bf16.shape),
        ],
        out_specs=[pl.BlockSpec((None, tm, hi - lo), lambda bi, i: (bi, i, 0)) for lo, hi in splits],
        out_shape=[jax.ShapeDtypeStruct((b, t, hi - lo), F32) for lo, hi in splits],
        compiler_params=_cparams("parallel", "parallel"),
    )(x, gain.reshape(1, d), scale, shift, w_in_bf16)


def _halo_specs(tm, t, width):
    nb8 = t // SUBLANES
    r8 = tm // SUBLANES
    return [
        pl.BlockSpec((None, tm, width), lambda b, i: (b, i, 0)),
        pl.BlockSpec((None, SUBLANES, width), lambda b, i: (b, jnp.maximum(i * r8 - 1, 0), 0)),
        pl.BlockSpec((None, SUBLANES, width), lambda b, i: (b, jnp.minimum((i + 1) * r8, nb8 - 1), 0)),
    ]


def _stage_with_halo(buf_ref, main_ref, prev_ref, next_ref):
    tm = main_ref.shape[0]
    i = pl.program_id(1)
    n = pl.num_programs(1)
    buf_ref[SUBLANES:SUBLANES + tm, :] = main_ref[...]
    buf_ref[0:SUBLANES, :] = jnp.where(i > 0, prev_ref[...], 0.0)
    buf_ref[SUBLANES + tm:2 * SUBLANES + tm, :] = jnp.where(i < n - 1, next_ref[...], 0.0)


def _rwkv_prep_kernel(gw, rw_ref, prev_ref, next_ref, mup_ref, mun_ref, kk_ref_, ka_ref, w0_ref, wup_ref,
                      a0_ref, aup_ref, gup_ref, hsum_ref,
                      r_o, k_o, v_o, kk_o, gate_o, lwf_o, kaf_o, kdf_o, lwb_o, kab_o, kdb_o, buf_ref):
    tm = rw_ref.shape[0]
    _stage_with_halo(buf_ref, rw_ref, prev_ref, next_ref)
    p = buf_ref[SUBLANES:SUBLANES + tm, :]
    prev = buf_ref[SUBLANES - 1:SUBLANES - 1 + tm, :]
    nxt = buf_ref[SUBLANES + 1:SUBLANES + 1 + tm, :]
    u = p + mup_ref[...] * (prev - p) + mun_ref[...] * (nxt - p)
    r = u[:, 0:gw]
    k = u[:, gw:2 * gw]
    v = u[:, 2 * gw:3 * gw]
    lora_w = u[:, 3 * gw:3 * gw + LANES]
    lora_a = u[:, 3 * gw + LANES:3 * gw + 2 * LANES]
    g = u[:, 3 * gw + 2 * LANES:3 * gw + 3 * LANES]
    kk = k * kk_ref_[...]
    ss = _dot_f32(kk * kk, hsum_ref[...])
    kk = kk * lax.rsqrt(jnp.maximum(ss, 1e-24))
    zw = _dot_f32(jnp.tanh(lora_w), wup_ref[...]) + w0_ref[...]
    za = _dot_f32(lora_a, aup_ref[...]) + a0_ref[...]
    logw = -_sigmoid(zw) * math.exp(-0.5)
    a = _sigmoid(za)
    r_o[...] = r
    k_o[...] = k
    v_o[...] = v
    kk_o[...] = kk
    gate_o[...] = _dot_f32(_sigmoid(g), gup_ref[...])
    ka = ka_ref[...]
    for d, (lw_o, kka_o, kd_o) in enumerate(((lwf_o, kaf_o, kdf_o), (lwb_o, kab_o, kdb_o))):
        a_d = a[:, d * gw:(d + 1) * gw]
        lw_o[...] = logw[:, d * gw:(d + 1) * gw]
        kka_o[...] = kk * a_d
        kd_o[...] = k * (1.0 + (a_d - 1.0) * ka)


def _blockdiag2(m):
    r, c = m.shape[1:]
    z = jnp.zeros((r, c), m.dtype)
    return jnp.concatenate([jnp.concatenate([m[0], z], 1), jnp.concatenate([z, m[1]], 1)], 0)


def rwkv_prepare(rw, params, gw):
    (mu_prev, mu_next, w0, w_up, a0, a_up, g_up, k_k, k_a, r_k, ln_w, ln_b) = params
    b, t, cols = rw.shape
    tm = min(512, t)
    row = lambda v: v.reshape(1, -1)
    small = [row(mu_prev), row(mu_next), row(k_k), row(k_a), row(w0), _blockdiag2(w_up), row(a0),
             _blockdiag2(a_up), g_up, _block_ones(gw, HEAD)]
    out = jax.ShapeDtypeStruct((b, t, gw), F32)
    return pl.pallas_call(
        functools.partial(_rwkv_prep_kernel, gw),
        grid=(b, t // tm),
        in_specs=_halo_specs(tm, t, cols) + [_const_spec(s.shape) for s in small],
        out_specs=[pl.BlockSpec((None, tm, gw), lambda bi, i: (bi, i, 0))] * 11,
        out_shape=[out] * 11,
        scratch_shapes=[pltpu.VMEM((tm + 2 * SUBLANES, cols), F32)],
        compiler_params=_cparams("parallel", "parallel"),
    )(rw, rw, rw, *small)


RW_BUILD_CHUNKS_PER_STEP = 2


def _rwkv_chunk_kernel(ncs, r_ref, v_ref, kk_ref, lwf_ref, kaf_ref, kdf_ref, lwb_ref, kab_ref, kdb_ref,
                       y0f_o, qf_o, mf_o, g0f_o, y0b_o, qb_o, mb_o, g0b_o):
    L = RW_CHUNK
    rows = lax.broadcasted_iota(jnp.int32, (L, L), 0)
    cols = lax.broadcasted_iota(jnp.int32, (L, L), 1)
    eye = jnp.where(rows == cols, 1.0, 0.0)
    rows2 = lax.broadcasted_iota(jnp.int32, (L, 2 * L), 0)
    cols2 = lax.broadcasted_iota(jnp.int32, (L, 2 * L), 1) % L
    lane = lax.broadcasted_iota(jnp.int32, (1, LANES), 1)
    head_masks = [(lane >= h * HEAD) & (lane < (h + 1) * HEAD) for h in range(2)]
    r2 = lax.broadcasted_iota(jnp.int32, (LANES, LANES), 0)
    c2 = lax.broadcasted_iota(jnp.int32, (LANES, LANES), 1)
    same_head = (r2 // HEAD) == (c2 // HEAD)
    diag = r2 == c2
    directions = ((False, lwf_ref, kaf_ref, kdf_ref, y0f_o, qf_o, mf_o, g0f_o),
                  (True, lwb_ref, kab_ref, kdb_ref, y0b_o, qb_o, mb_o, g0b_o))

    probs = []
    for c in range(ncs):
        rs = slice(c * L, (c + 1) * L)
        for pair in range(2):
            sl = slice(pair * LANES, (pair + 1) * LANES)
            r = r_ref[rs, sl]
            v = v_ref[rs, sl]
            kk = kk_ref[rs, sl]
            for reverse, lw_ref, ka_ref, kd_ref, y0_o, q_o, m_o, g0_o in directions:
                incl = (cols >= rows) if reverse else (cols <= rows)
                strict = (cols > rows) if reverse else (cols < rows)
                incl2 = (cols2 >= rows2) if reverse else (cols2 <= rows2)
                logw = lw_ref[rs, sl]
                kd = kd_ref[rs, sl]
                a = -ka_ref[rs, sl]
                cum = _dot_f32(jnp.where(incl, 1.0, 0.0), logw)
                total = jnp.sum(logw, axis=0, keepdims=True)
                g_inv = jnp.exp(-cum)
                g_tail = jnp.exp(total - cum)
                bd = kk * jnp.exp(cum - logw)
                rd = r * jnp.exp(cum)
                probs.append(dict(
                    incl2=incl2, strict=strict, v=v, bd=bd, rd=rd, total=total,
                    lhs=jnp.concatenate([bd, rd], axis=0), rhs=jnp.concatenate([a * g_inv, kd * g_inv], axis=0),
                    tails=jnp.concatenate([kd * g_tail, a * g_tail], axis=0),
                    outs=(y0_o, q_o, m_o, g0_o), rs=rs, sl=sl, c=c, pair=pair))

    heads = []
    for p in probs:
        for h in range(2):
            gram = _dot_nt(jnp.where(head_masks[h], p["lhs"], 0.0), p["rhs"])
            heads.append(dict(
                p=p, h=h,
                nil=jnp.where(p["strict"], gram[:L, :L], 0.0),
                aak=jnp.where(p["strict"], gram[:L, L:], 0.0),
                ara_ark=jnp.where(p["incl2"], gram[L:, :], 0.0)))

    for hd in heads:
        hd["acc"] = eye + hd["nil"]
        hd["pow"] = hd["nil"]
    span = 2
    while span < L:
        for hd in heads:
            hd["pow"] = _dot_tri(hd["pow"], hd["pow"])
        for hd in heads:
            hd["acc"] = hd["acc"] + _dot_tri(hd["acc"], hd["pow"])
        span *= 2

    for hd in heads:
        hd["aakv"] = _dot(hd["aak"], hd["p"]["v"])
    for hd in heads:
        wp = _dot_tri(hd["acc"], jnp.concatenate([hd["aakv"], hd["p"]["bd"]], axis=1))
        hd["w1"], hd["pm"] = wp[:, :LANES], wp[:, LANES:]
    for hd in heads:
        rhs = jnp.concatenate([jnp.concatenate([hd["w1"], hd["pm"]], axis=1),
                               jnp.concatenate([hd["p"]["v"], jnp.zeros((L, LANES), F32)], axis=1)], axis=0)
        yq = _dot(hd["ara_ark"], rhs)
        hd["y0"], hd["q"] = yq[:, :LANES], yq[:, LANES:] + hd["p"]["rd"]

    for i, p in enumerate(probs):
        h0, h1 = heads[2 * i], heads[2 * i + 1]
        pick = lambda key: jnp.where(head_masks[0], h0[key], h1[key])
        y0_o, q_o, m_o, g0_o = p["outs"]
        y0_o[p["rs"], p["sl"]] = pick("y0")
        q_o[p["rs"], p["sl"]] = pick("q")
        vw = jnp.concatenate([p["v"], pick("w1")], axis=0)
        g0_o[p["c"], p["pair"]] = jnp.where(same_head, _dot_tn(p["tails"], vw), 0.0)
        m_o[p["c"], p["pair"]] = (jnp.where(same_head, _dot_tn(p["tails"][L:], pick("pm")), 0.0)
                                  + jnp.where(diag, jnp.exp(p["total"]), 0.0))


def rwkv_chunks(r, v, kk, lwf, kaf, kdf, lwb, kab, kdb):
    b, t, gw = r.shape
    L = RW_CHUNK
    nc = t // L
    ncs = RW_BUILD_CHUNKS_PER_STEP if nc % RW_BUILD_CHUNKS_PER_STEP == 0 else 1
    tok = pl.BlockSpec((None, ncs * L, gw), lambda bi, i: (bi, i, 0))
    mat = pl.BlockSpec((None, ncs, 2, LANES, LANES), lambda bi, i: (bi, i, 0, 0, 0))
    tok_s = jax.ShapeDtypeStruct((b, t, gw), F32)
    mat_s = jax.ShapeDtypeStruct((b, nc, 2, LANES, LANES), F32)
    return pl.pallas_call(
        functools.partial(_rwkv_chunk_kernel, ncs),
        grid=(b, nc // ncs),
        in_specs=[tok] * 9,
        out_specs=[tok, tok, mat, mat] * 2,
        out_shape=[tok_s, tok_s, mat_s, mat_s] * 2,
        compiler_params=_cparams("parallel", "parallel"),
    )(r, v, kk, lwf, kaf, kdf, lwb, kab, kdb)


def _rwkv_scan_kernel(cps, h0_ref, y0f_ref, qf_ref, mf_ref, g0f_ref, y0b_ref, qb_ref, mb_ref, g0b_ref,
                      yf_o, yb_o, hfin_o, h_ref):
    L = RW_CHUNK
    j = pl.program_id(1)

    @pl.when(j == 0)
    def _():
        h_ref[...] = h0_ref[...]

    for step in range(cps):
        for d, (y0_ref, q_ref, m_ref, g0_ref, y_o) in enumerate(
                ((y0f_ref, qf_ref, mf_ref, g0f_ref, yf_o), (y0b_ref, qb_ref, mb_ref, g0b_ref, yb_o))):
            c = step if d == 0 else cps - 1 - step
            rows = slice(c * L, (c + 1) * L)
            for pair in range(2):
                sl = slice(pair * LANES, (pair + 1) * LANES)
                h = h_ref[d, pair]
                y_o[rows, sl] = y0_ref[rows, sl] + _dot_f32(q_ref[rows, sl], h)
                h_ref[d, pair] = _dot_f32(m_ref[c, pair], h) + g0_ref[c, pair]

    @pl.when(j == pl.num_programs(1) - 1)
    def _():
        hfin_o[...] = h_ref[...]


def rwkv_scan(h0, y0f, qf, mf, g0f, y0b, qb, mb, g0b):
    b, t, gw = y0f.shape
    L = RW_CHUNK
    nc = t // L
    cps = min(RW_CHUNKS_PER_STEP, nc)
    nb = nc // cps
    tm = cps * L
    tok_f = pl.BlockSpec((None, tm, gw), lambda bi, i: (bi, i, 0))
    tok_b = pl.BlockSpec((None, tm, gw), lambda bi, i: (bi, nb - 1 - i, 0))
    mat_f = pl.BlockSpec((None, cps, 2, LANES, LANES), lambda bi, i: (bi, i, 0, 0, 0))
    mat_b = pl.BlockSpec((None, cps, 2, LANES, LANES), lambda bi, i: (bi, nb - 1 - i, 0, 0, 0))
    st = pl.BlockSpec((None, 2, 2, LANES, LANES), lambda bi, i: (bi, 0, 0, 0, 0))
    tok_s = jax.ShapeDtypeStruct((b, t, gw), F32)
    return pl.pallas_call(
        functools.partial(_rwkv_scan_kernel, cps),
        grid=(b, nb),
        in_specs=[st, tok_f, tok_f, mat_f, mat_f, tok_b, tok_b, mat_b, mat_b],
        out_specs=[tok_f, tok_b, st],
        out_shape=[tok_s, tok_s, jax.ShapeDtypeStruct((b, 2, 2, LANES, LANES), F32)],
        scratch_shapes=[pltpu.VMEM((2, 2, LANES, LANES), F32)],
        compiler_params=_cparams("parallel", "arbitrary"),
    )(h0, y0f, qf, mf, g0f, y0b, qb, mb, g0b)


def _rwkv_out_kernel(yf_ref, yb_ref, r_ref, k_ref, v_ref, gate_ref, rk_ref, lnw_ref, lnb_ref, hmean_ref, o_ref):
    y = yf_ref[...] + yb_ref[...]
    hmean = hmean_ref[...]
    mu = _dot_f32(y, hmean)
    yc = y - mu
    var = _dot_f32(yc * yc, hmean)
    yn = yc * lax.rsqrt(var + RW_LN_EPS) * lnw_ref[...] + lnb_ref[...]
    bonus = _dot_f32(r_ref[...] * k_ref[...] * rk_ref[...], hmean) * float(HEAD) * v_ref[...]
    o_ref[...] = (yn + bonus) * gate_ref[...]


def rwkv_output(yf, yb, r, k, v, gate, r_k, ln_w, ln_b):
    b, t, gw = yf.shape
    tm = min(512, t)
    tok = pl.BlockSpec((None, tm, gw), lambda bi, i: (bi, i, 0))
    small = [r_k.reshape(1, gw), ln_w.reshape(1, gw), ln_b.reshape(1, gw), _block_ones(gw, HEAD, 1.0 / HEAD)]
    return pl.pallas_call(
        _rwkv_out_kernel,
        grid=(b, t // tm),
        in_specs=[tok] * 6 + [_const_spec(s.shape) for s in small],
        out_specs=tok,
        out_shape=jax.ShapeDtypeStruct((b, t, gw), F32),
        compiler_params=_cparams("parallel", "parallel"),
    )(yf, yb, r, k, v, gate, *small)


def rwkv_mixer(rwx, rwc, params, gw, ctx_out):
    r_k, ln_w, ln_b = params[9], params[10], params[11]
    sx = rwkv_prepare(rwx, params, gw)
    sc = rwkv_prepare(rwc, params, gw)
    (rx, kx, vx, kkx, gx), dx = sx[:5], sx[5:]
    (rc, kc, vc, kkc, gc), dc = sc[:5], sc[5:]
    cx = rwkv_chunks(rx, vx, kkx, *dx)
    cc = rwkv_chunks(rc, vc, kkc, *dc)
    b = rwx.shape[0]
    h0 = jnp.zeros((b, 2, 2, LANES, LANES), F32)
    ycf, ycb, h_ctx = rwkv_scan(h0, *cc)
    yxf, yxb, _ = rwkv_scan(h_ctx, *cx)
    out_x = rwkv_output(yxf, yxb, rx, kx, vx, gx, r_k, ln_w, ln_b)
    out_c = rwkv_output(ycf, ycb, rc, kc, vc, gc, r_k, ln_w, ln_b) if ctx_out else None
    return out_x, out_c


def _rope_tables(n_tokens, reps):
    rows = n_tokens // GRID_W
    row = np.repeat(np.arange(rows), GRID_W).astype(np.float64)
    col = np.tile(np.arange(GRID_W), rows).astype(np.float64)
    n_freq = DA_QK // 4
    inv = ROPE_BASE ** (-np.arange(n_freq, dtype=np.float64) / n_freq)
    ar = row[:, None] * inv
    ac = col[:, None] * inv
    ang = np.concatenate([ar, ar, ac, ac], axis=-1)
    cos = np.tile(np.cos(ang), (1, reps)).astype(np.float32)
    sin = np.tile(np.sin(ang), (1, reps)).astype(np.float32)
    return jnp.asarray(cos), jnp.asarray(sin)


def _attn_prep_kernel(gw, rope, da_ref, qg_ref, kg_ref, gmean_ref, *rest):
    if rope:
        cos_ref, sin_ref, q_o, k_o, v_o = rest
    else:
        q_o, k_o, v_o = rest
    da = da_ref[...]
    gmean = gmean_ref[...]
    lane = lax.broadcasted_iota(jnp.int32, (1, gw), 1)
    first_half = (lane % (DA_QK // 2)) < (DA_QK // 4)

    def norm_rope(x, gain):
        ms = _dot_f32(x * x, gmean)
        y = x * lax.rsqrt(ms + NORM_EPS) * gain
        if rope:
            quarter = DA_QK // 4
            rot = jnp.where(first_half, -pltpu.roll(y, gw - quarter, 1), pltpu.roll(y, quarter, 1))
            y = y * cos_ref[...] + rot * sin_ref[...]
        return y

    q_o[...] = (norm_rope(da[:, 0:gw], qg_ref[...]) * (DA_QK ** -0.5 * LOG2_E)).astype(q_o.dtype)
    k_o[...] = norm_rope(da[:, gw:2 * gw], kg_ref[...]).astype(k_o.dtype)
    v_o[...] = da[:, 2 * gw:3 * gw].astype(v_o.dtype)


def attention_prepare(da, q_gain, k_gain, gw, rope):
    b, t, cols = da.shape
    tm = min(512, t)
    reps = gw // DA_QK
    small = [jnp.tile(q_gain, reps).reshape(1, gw), jnp.tile(k_gain, reps).reshape(1, gw),
             _block_ones(gw, DA_QK, 1.0 / DA_QK)]
    args = [da] + small
    in_specs = [pl.BlockSpec((None, tm, cols), lambda bi, i: (bi, i, 0))] + [_const_spec(s.shape) for s in small]
    if rope:
        cos, sin = _rope_tables(t, reps)
        args += [cos, sin]
        in_specs += [pl.BlockSpec((tm, gw), lambda bi, i: (i, 0))] * 2
    tok = pl.BlockSpec((None, tm, gw), lambda bi, i: (bi, i, 0))
    return pl.pallas_call(
        functools.partial(_attn_prep_kernel, gw, rope),
        grid=(b, t // tm),
        in_specs=in_specs,
        out_specs=[tok] * 3,
        out_shape=[jax.ShapeDtypeStruct((b, t, gw), BF16)] * 3,
        compiler_params=_cparams("parallel", "parallel"),
    )(*args)


LOG2_E = 1.4426950408889634
FLASH_ROW_TILE = 16


def _flash_kernel(lam_init, q_ref, kt_ref, v_ref, lq1_ref, lk1_ref, lq2_ref, lk2_ref, sg_ref, o_ref,
                  m_ref, l_ref, acc_ref, alpha_ref, p_ref):
    j = pl.program_id(3)

    @pl.when(j == 0)
    def _():
        m_ref[...] = jnp.full(m_ref.shape, -1e30, F32)
        l_ref[...] = jnp.zeros(l_ref.shape, F32)
        acc_ref[...] = jnp.zeros(acc_ref.shape, F32)

    tq = q_ref.shape[1]

    def row_tile(i, _):
        rs = pl.ds(pl.multiple_of(i * FLASH_ROW_TILE, FLASH_ROW_TILE), FLASH_ROW_TILE)
        for m in range(2):
            s = jnp.dot(q_ref[m, rs, :], kt_ref[m], preferred_element_type=F32)
            m_old = m_ref[m, rs, :]
            m_new = jnp.maximum(m_old, jnp.max(s, axis=-1, keepdims=True))
            alpha = jnp.exp2(m_old - m_new)
            p = jnp.exp2(s - m_new)
            l_ref[m, rs, :] = alpha * l_ref[m, rs, :] + jnp.sum(p, axis=-1, keepdims=True)
            m_ref[m, rs, :] = m_new
            alpha_ref[m, rs, :] = alpha
            p_ref[m, rs, :] = p.astype(BF16)
        return 0

    lax.fori_loop(0, tq // FLASH_ROW_TILE, row_tile, 0, unroll=2)
    v = v_ref[...]
    for m in range(2):
        acc_ref[m] = alpha_ref[m] * acc_ref[m] + jnp.dot(p_ref[m], v, preferred_element_type=F32)

    @pl.when(j == pl.num_programs(3) - 1)
    def _():
        lam = (jnp.exp(jnp.sum(lq1_ref[...] * lk1_ref[...], axis=-1, keepdims=True))
               - jnp.exp(jnp.sum(lq2_ref[...] * lk2_ref[...], axis=-1, keepdims=True)) + lam_init)
        o = acc_ref[0] / l_ref[0] - lam * (acc_ref[1] / l_ref[1])
        ms = jnp.mean(o * o, axis=-1, keepdims=True)
        o_ref[...] = o * lax.rsqrt(ms + NORM_EPS) * sg_ref[...] * (1.0 - lam_init)


def _largest_divisor(n, cap, multiple):
    best = None
    for d in range(multiple, cap + 1, multiple):
        if n % d == 0:
            best = d
    return best if best is not None else n


def diff_attention_core(q, kt, v, lam_params, sub_gain, lam_init):
    b, h, _, t, dk = q.shape
    s = kt.shape[-1]
    tq = min(512, t)
    tk = _largest_divisor(s, 1536, LANES)
    small = [p.reshape(1, -1) for p in lam_params] + [sub_gain.reshape(1, -1)]
    return pl.pallas_call(
        functools.partial(_flash_kernel, lam_init),
        grid=(b, h, t // tq, s // tk),
        in_specs=[
            pl.BlockSpec((None, None, 2, tq, dk), lambda bi, hi, i, j: (bi, hi, 0, i, 0)),
            pl.BlockSpec((None, None, 2, dk, tk), lambda bi, hi, i, j: (bi, hi, 0, 0, j)),
            pl.BlockSpec((None, None, tk, HEAD), lambda bi, hi, i, j: (bi, hi, j, 0)),
        ] + [_const_spec(x.shape) for x in small],
        out_specs=pl.BlockSpec((None, None, tq, HEAD), lambda bi, hi, i, j: (bi, hi, i, 0)),
        out_shape=jax.ShapeDtypeStruct((b, h, t, HEAD), F32),
        scratch_shapes=[pltpu.VMEM((2, tq, 1), F32), pltpu.VMEM((2, tq, 1), F32), pltpu.VMEM((2, tq, HEAD), F32),
                        pltpu.VMEM((2, tq, 1), F32), pltpu.VMEM((2, tq, tk), BF16)],
        compiler_params=_cparams("parallel", "parallel", "parallel", "arbitrary"),
    )(q, kt, v, *small)


def _split_heads(q, k, v):
    b, t, gw = q.shape
    h = gw // HEAD
    qh = q.reshape(b, t, h, 2, DA_QK).transpose(0, 2, 3, 1, 4)
    kth = k.reshape(b, t, h, 2, DA_QK).transpose(0, 2, 3, 4, 1)
    vh = v.reshape(b, t, h, HEAD).transpose(0, 2, 1, 3)
    return qh, kth, vh


def diff_attention(dax, dac, params, gw, lam_init, ctx_out):
    q_gain, k_gain, lq1, lk1, lq2, lk2, sub_gain = params
    qx, ktx, vx = _split_heads(*attention_prepare(dax, q_gain, k_gain, gw, rope=True))
    qc, ktc, vc = _split_heads(*attention_prepare(dac, q_gain, k_gain, gw, rope=False))
    kt = jnp.concatenate([ktx, ktc], axis=-1)
    v = jnp.concatenate([vx, vc], axis=2)
    lam_params = (lq1, lk1, lq2, lk2)
    merge = lambda o: o.transpose(0, 2, 1, 3).reshape(o.shape[0], o.shape[2], gw)
    out_x = merge(diff_attention_core(qx, kt, v, lam_params, sub_gain, lam_init))
    out_c = merge(diff_attention_core(qc, ktc, vc, lam_params, sub_gain, lam_init)) if ctx_out else None
    return out_x, out_c


FT_RADIX = 64


def _dft_cos_sin(n, scale=1.0):
    i = np.arange(n)
    ang = 2.0 * np.pi * ((i[:, None] * i[None, :]) % n) / n
    return np.cos(ang) * scale, np.sin(ang) * scale


def _channel_dft(gw, scale):
    c, s = _dft_cos_sin(HEAD, scale)
    eye = np.eye(gw // HEAD)
    return jnp.asarray(np.concatenate([np.kron(eye, c), np.kron(eye, s)], axis=0).astype(np.float32))


def _fnet_stage1_kernel(z_ref, gr_ref, gi_ref, or_ref, oi_ref):
    for j in range(z_ref.shape[1]):
        x = z_ref[:, j, :]
        or_ref[j] = _dot_f32(gr_ref[j], x)
        oi_ref[j] = _dot_f32(gi_ref[j], x)


def _fnet_stage2_kernel(gw, br_ref, bi_ref, rot_ref, chan_ref, wf_ref, o_ref):
    for g in range(br_ref.shape[1] // gw):
        sl = slice(g * gw, (g + 1) * gw)
        p = _dot_f32(rot_ref[...], jnp.concatenate([br_ref[:, sl], bi_ref[:, sl]], axis=0))
        n1 = p.shape[0] // 2
        f = _dot_f32(jnp.concatenate([p[:n1], p[n1:]], axis=1), chan_ref[...])
        o_ref[:, sl] = _dot_f32(f, wf_ref[...])


def fourier_mix_long(z, w_f):
    b, t, gw = z.shape
    n1 = FT_RADIX
    n2 = t // n1
    k2 = np.arange(n2)[None, :, None]
    n = np.arange(n1)[:, None, None] + n1 * np.arange(n2)[None, None, :]
    ang = 2.0 * np.pi * ((k2 * n) % t) / t
    g_r = jnp.asarray(np.cos(ang).astype(np.float32))
    g_i = jnp.asarray((-np.sin(ang)).astype(np.float32))
    j8 = SUBLANES
    br, bi = pl.pallas_call(
        _fnet_stage1_kernel,
        grid=(b, n1 // j8),
        in_specs=[
            pl.BlockSpec((None, n2, j8, gw), lambda bi_, i: (bi_, 0, i, 0)),
            pl.BlockSpec((j8, n2, n2), lambda bi_, i: (i, 0, 0)),
            pl.BlockSpec((j8, n2, n2), lambda bi_, i: (i, 0, 0)),
        ],
        out_specs=[pl.BlockSpec((None, j8, n2, gw), lambda bi_, i: (bi_, i, 0, 0))] * 2,
        out_shape=[jax.ShapeDtypeStruct((b, n1, n2, gw), F32)] * 2,
        compiler_params=_cparams("parallel", "parallel"),
    )(z.reshape(b, n2, n1, gw), g_r, g_i)
    c64, s64 = _dft_cos_sin(n1)
    rot = jnp.asarray(np.block([[c64, s64], [-s64, c64]]).astype(np.float32))
    chan = _channel_dft(gw, 1.0 / math.sqrt(t * HEAD))
    cols = n2 * gw
    tc = min(2048, cols)
    out = pl.pallas_call(
        functools.partial(_fnet_stage2_kernel, gw),
        grid=(b, cols // tc),
        in_specs=[
            pl.BlockSpec((None, n1, tc), lambda bi_, i: (bi_, 0, i)),
            pl.BlockSpec((None, n1, tc), lambda bi_, i: (bi_, 0, i)),
            _const_spec(rot.shape), _const_spec(chan.shape), _const_spec(w_f.shape),
        ],
        out_specs=pl.BlockSpec((None, n1, tc), lambda bi_, i: (bi_, 0, i)),
        out_shape=jax.ShapeDtypeStruct((b, n1, cols), F32),
        compiler_params=_cparams("parallel", "parallel"),
    )(br.reshape(b, n1, cols), bi.reshape(b, n1, cols), rot, chan, w_f)
    return out.reshape(b, t, gw)


def _fnet_dense_kernel(z_ref, ct_ref, st_ref, chan_ref, wf_ref, o_ref):
    z = z_ref[...]
    pr = _dot_f32(ct_ref[...], z)
    pi = -_dot_f32(st_ref[...], z)
    f = _dot_f32(jnp.concatenate([pr, pi], axis=1), chan_ref[...])
    o_ref[...] = _dot_f32(f, wf_ref[...])


def fourier_mix_short(z, w_f):
    b, t, gw = z.shape
    ct, st = _dft_cos_sin(t)
    ct = jnp.asarray(ct.astype(np.float32))
    st = jnp.asarray(st.astype(np.float32))
    chan = _channel_dft(gw, 1.0 / math.sqrt(t * HEAD))
    tok = pl.BlockSpec((None, t, gw), lambda bi: (bi, 0, 0))
    return pl.pallas_call(
        _fnet_dense_kernel,
        grid=(b,),
        in_specs=[tok, _const_spec(ct.shape), _const_spec(st.shape), _const_spec(chan.shape), _const_spec(w_f.shape)],
        out_specs=tok,
        out_shape=jax.ShapeDtypeStruct((b, t, gw), F32),
        compiler_params=_cparams("parallel"),
    )(z, ct, st, chan, w_f)


def fourier_mix(z, w_f):
    t = z.shape[1]
    if t % (FT_RADIX * SUBLANES) == 0 and t // FT_RADIX >= LANES:
        return fourier_mix_long(z, w_f)
    return fourier_mix_short(z, w_f)


def _pool_kernel(t_total, u_ref, prev_ref, next_ref, w_ref, s_ref, o_ref, buf_ref):
    tm, gw = u_ref.shape
    _stage_with_halo(buf_ref, u_ref, prev_ref, next_ref)
    at = lambda off: buf_ref[SUBLANES + off:SUBLANES + off + tm, :]
    u = at(0)
    t = pl.program_id(1) * tm + lax.broadcasted_iota(jnp.int32, (tm, 1), 0)
    lane = lax.broadcasted_iota(jnp.int32, (1, gw), 1)
    group = lane // (gw // len(POOL_WINDOWS))
    mean = jnp.zeros((tm, gw), F32)
    run = jnp.zeros((tm, gw), F32)
    half_prev = 0
    for i, w in enumerate(POOL_WINDOWS):
        half = w // 2
        for off in range(half_prev, half):
            run = run + at(-off - 1) + at(off)
        half_prev = half
        cnt = (jnp.minimum(t + half, t_total) - jnp.maximum(t - half, 0)).astype(F32)
        mean = jnp.where(group == i, run / cnt, mean)
    o_ref[...] = _dot_f32(mean - u, w_ref[...]) * s_ref[...]


def pool_mix(u, w_p, s_p):
    b, t, gw = u.shape
    tm = min(512, t)
    nw, ch = w_p.shape[0], w_p.shape[1]
    w_bd = jnp.zeros((gw, gw), F32)
    for i in range(nw):
        w_bd = w_bd.at[i * ch:(i + 1) * ch, i * ch:(i + 1) * ch].set(w_p[i])
    return pl.pallas_call(
        functools.partial(_pool_kernel, t),
        grid=(b, t // tm),
        in_specs=_halo_specs(tm, t, gw) + [_const_spec((gw, gw)), _const_spec((1, gw))],
        out_specs=pl.BlockSpec((None, tm, gw), lambda bi, i: (bi, i, 0)),
        out_shape=jax.ShapeDtypeStruct((b, t, gw), F32),
        scratch_shapes=[pltpu.VMEM((tm + 2 * SUBLANES, gw), F32)],
        compiler_params=_cparams("parallel", "parallel"),
    )(u, u, u, w_bd, s_p.reshape(1, gw))


def _outproj_kernel(gw, x_ref, g_ref, a_ref, b_ref, f_ref, p_ref, w_ref, o_ref):
    acc = None
    for i, m_ref in enumerate((a_ref, b_ref, f_ref, p_ref)):
        part = jnp.dot(m_ref[...].astype(BF16), w_ref[i * gw:(i + 1) * gw, :], preferred_element_type=F32)
        acc = part if acc is None else acc + part
    o_ref[...] = x_ref[...] + g_ref[...] * acc


def output_projection(x, gate, mixers, w_out_bf16):
    b, t, d = x.shape
    gw = mixers[0].shape[-1]
    tm = min(512, t)
    tok = pl.BlockSpec((None, tm, gw), lambda bi, i: (bi, i, 0))
    xs = pl.BlockSpec((None, tm, d), lambda bi, i: (bi, i, 0))
    return pl.pallas_call(
        functools.partial(_outproj_kernel, gw),
        grid=(b, t // tm),
        in_specs=[xs, pl.BlockSpec((None, 1, d), lambda bi, i: (bi, 0, 0))] + [tok] * 4 + [_const_spec(w_out_bf16.shape)],
        out_specs=xs,
        out_shape=jax.ShapeDtypeStruct((b, t, d), F32),
        compiler_params=_cparams("parallel", "parallel"),
    )(x, gate, *mixers, w_out_bf16)


def _router_kernel(n_exp, x_ref, gain_ref, sc_ref, sh_ref, wr_ref, h_o, aff_o, afft_o):
    h = _modulated_norm(x_ref[...], gain_ref[...], sc_ref[...], sh_ref[...])
    h_o[...] = h
    logits = _dot_f32(h, wr_ref[...])
    lane = lax.broadcasted_iota(jnp.int32, logits.shape, 1)
    logits = jnp.where(lane < n_exp, logits, -1e30)
    e = jnp.exp(logits - jnp.max(logits, axis=-1, keepdims=True))
    aff = e / jnp.sum(e, axis=-1, keepdims=True)
    aff_o[...] = aff
    afft_o[...] = jnp.transpose(aff)[:n_exp, :]


def router(x, gain, scale, shift, w_router):
    b, t, d = x.shape
    n_exp = w_router.shape[1]
    tm = min(512, t)
    wr = jnp.zeros((d, LANES), F32).at[:, :n_exp].set(w_router)
    row = pl.BlockSpec((None, 1, d), lambda bi, i: (bi, 0, 0))
    return pl.pallas_call(
        functools.partial(_router_kernel, n_exp),
        grid=(b, t // tm),
        in_specs=[pl.BlockSpec((None, tm, d), lambda bi, i: (bi, i, 0)), _const_spec((1, d)), row, row,
                  _const_spec(wr.shape)],
        out_specs=[pl.BlockSpec((None, tm, d), lambda bi, i: (bi, i, 0)),
                   pl.BlockSpec((None, tm, LANES), lambda bi, i: (bi, i, 0)),
                   pl.BlockSpec((None, n_exp, tm), lambda bi, i: (bi, 0, i))],
        out_shape=[jax.ShapeDtypeStruct((b, t, d), F32), jax.ShapeDtypeStruct((b, t, LANES), F32),
                   jax.ShapeDtypeStruct((b, n_exp, t), F32)],
        compiler_params=_cparams("parallel", "parallel"),
    )(x, gain.reshape(1, d), scale, shift, wr)


TOPK_EXPONENT_STEPS = 7
TOPK_MANTISSA_STEPS = 44


def _row_cumsum(x_ref, o_ref, upper_ref):
    rows, t = x_ref.shape
    carry = jnp.zeros((rows, 1), F32)
    for g in range(t // LANES):
        sl = slice(g * LANES, (g + 1) * LANES)
        local = jnp.dot(x_ref[:, sl].astype(BF16), upper_ref[...], preferred_element_type=F32) + carry
        o_ref[:, sl] = local
        carry = local[:, LANES - 1:LANES]


def _topk_kernel(cap, aff_ref, upper_ref, idx_o, sel_ref, cs_ref):
    aff = aff_ref[...]
    rows, t = aff.shape
    capf = float(cap)
    count_ge = lambda thr: jnp.sum(jnp.where(aff >= thr, 1.0, 0.0), axis=-1, keepdims=True)
    hi = jnp.full((rows, 1), 2.0, F32)
    for step in reversed(range(TOPK_EXPONENT_STEPS)):
        cand = hi * (2.0 ** -(2 ** step))
        hi = jnp.where(count_ge(cand) < capf, cand, hi)
    lo = hi * 0.5
    lo = jnp.where(count_ge(lo) >= capf, lo, 0.0)

    def bisect(_, carry):
        lo, hi = carry
        mid = 0.5 * (lo + hi)
        enough = count_ge(mid) >= capf
        return jnp.where(enough, mid, lo), jnp.where(enough, hi, mid)

    lo, hi = lax.fori_loop(0, TOPK_MANTISSA_STEPS, bisect, (lo, hi))
    above = aff >= hi
    need = capf - count_ge(hi)
    sel_ref[...] = jnp.where((aff >= lo) & jnp.logical_not(above), 1.0, 0.0)
    _row_cumsum(sel_ref, cs_ref, upper_ref)
    tied_in = (sel_ref[...] > 0.5) & (cs_ref[...] <= need)
    sel_ref[...] = jnp.where(above | tied_in, 1.0, 0.0)
    _row_cumsum(sel_ref, cs_ref, upper_ref)

    ones = jnp.ones((SUBLANES, t), BF16)

    def compact(r, _):
        cs_row = cs_ref[pl.ds(r, 1), :]
        for cb in range(0, cap, LANES):
            n = min(LANES, cap - cb)
            slot = (lax.broadcasted_iota(jnp.int32, (n, 1), 0) + cb).astype(F32)
            reached = jnp.where(cs_row <= slot, 1.0, 0.0).astype(BF16)
            cnt = _dot_nt(ones, reached)
            idx_o[r, :, cb:cb + n] = cnt[0:1].astype(jnp.int32)
        return 0

    lax.fori_loop(0, rows, compact, 0)


def expert_choice_topk(aff_t, cap):
    b, n_exp, t = aff_t.shape
    rows = b * n_exp
    upper = jnp.asarray(np.triu(np.ones((LANES, LANES), np.float32))).astype(BF16)
    idx = pl.pallas_call(
        functools.partial(_topk_kernel, cap),
        grid=(1,),
        in_specs=[_const_spec((rows, t)), _const_spec(upper.shape)],
        out_specs=_const_spec((rows, 1, cap)),
        out_shape=jax.ShapeDtypeStruct((rows, 1, cap), jnp.int32),
        scratch_shapes=[pltpu.VMEM((rows, t), F32), pltpu.VMEM((rows, t), F32)],
        compiler_params=_cparams("arbitrary"),
    )(aff_t.reshape(rows, t), upper)
    return idx.reshape(rows * cap)


GATHER_UNROLL = 8


def _gather_kernel(cap, idx_ref, h_ref, aff_ref, xs_o, g_o, buf_ref):
    base = (pl.program_id(0) * pl.num_programs(2) + pl.program_id(2)) * cap

    def body(i, _):
        for u in range(GATHER_UNROLL):
            c = i * GATHER_UNROLL + u
            row = idx_ref[base + c]
            buf_ref[pl.ds(c, 1), :] = h_ref[pl.ds(row, 1), :]
            g_o[pl.ds(c, 1), :] = aff_ref[pl.ds(row, 1), :]
        return 0

    lax.fori_loop(0, cap // GATHER_UNROLL, body, 0)
    xs_o[...] = buf_ref[...].astype(BF16)


def gather_tokens(idx, h, aff, n_exp, cap):
    b, t, d = h.shape
    dh = d // 2
    return pl.pallas_call(
        functools.partial(_gather_kernel, cap),
        grid_spec=pltpu.PrefetchScalarGridSpec(
            num_scalar_prefetch=1,
            grid=(b, 2, n_exp),
            in_specs=[pl.BlockSpec((None, t, dh), lambda bi, hf, e, idx_: (bi, 0, hf)),
                      pl.BlockSpec((None, t, LANES), lambda bi, hf, e, idx_: (bi, 0, 0))],
            out_specs=[pl.BlockSpec((None, None, cap, dh), lambda bi, hf, e, idx_: (bi, e, 0, hf)),
                       pl.BlockSpec((None, None, None, cap, LANES), lambda bi, hf, e, idx_: (bi, e, hf, 0, 0))],
            scratch_shapes=[pltpu.VMEM((cap, dh), F32)],
        ),
        out_shape=[jax.ShapeDtypeStruct((b, n_exp, cap, d), BF16),
                   jax.ShapeDtypeStruct((b, n_exp, 2, cap, LANES), F32)],
        compiler_params=_cparams("parallel", "parallel", "arbitrary"),
    )(idx, h, aff)


def _expert_ffn_kernel(n_groups, *refs):
    xs_refs = refs[0:2 * n_groups:2]
    g_refs = refs[1:2 * n_groups:2]
    wg_ref, wu_ref, wd_ref = refs[2 * n_groups:2 * n_groups + 3]
    o_refs = refs[2 * n_groups + 3:3 * n_groups + 3]
    acc_refs = refs[3 * n_groups + 3:]
    f = pl.program_id(1)
    wg = wg_ref[...].astype(BF16)
    wu = wu_ref[...].astype(BF16)
    wd = wd_ref[...].astype(BF16)
    for xs_ref, acc_ref in zip(xs_refs, acc_refs):
        for bi in range(xs_ref.shape[0]):
            xb = xs_ref[bi]
            gate = jnp.dot(xb, wg, preferred_element_type=F32)
            up = jnp.dot(xb, wu, preferred_element_type=F32)
            hid = (gate * _sigmoid(gate) * up).astype(BF16)
            part = jnp.dot(hid, wd, preferred_element_type=F32)

            @pl.when(f == 0)
            def _():
                acc_ref[bi] = part

            @pl.when(f > 0)
            def _():
                acc_ref[bi] += part

    @pl.when(f == pl.num_programs(1) - 1)
    def _():
        e = pl.program_id(0)
        for g_ref, acc_ref, o_ref in zip(g_refs, acc_refs, o_refs):
            lane = lax.broadcasted_iota(jnp.int32, g_ref.shape, 2)
            g = jnp.sum(jnp.where(lane == e, g_ref[...], 0.0), axis=-1, keepdims=True)
            o_ref[...] = acc_ref[...] * g


def expert_ffn(groups, layer, w_gate, w_up, w_down):
    _, n_exp, d, f_dim = w_gate.shape
    tf = _largest_divisor(f_dim, 256, LANES)
    args, in_specs, out_specs, out_shapes, scratch = [], [], [], [], []
    for xs, g_rows in groups:
        b, _, cap, _ = xs.shape
        args += [xs, g_rows]
        in_specs += [pl.BlockSpec((b, None, cap, d), lambda e, f: (0, e, 0, 0)),
                     pl.BlockSpec((b, None, None, cap, LANES), lambda e, f: (0, e, 0, 0, 0))]
        out_specs.append(pl.BlockSpec((None, b, cap, d), lambda e, f: (e, 0, 0, 0)))
        out_shapes.append(jax.ShapeDtypeStruct((n_exp, b, cap, d), F32))
        scratch.append(pltpu.VMEM((b, cap, d), F32))
    in_specs += [pl.BlockSpec((None, None, d, tf), lambda e, f: (layer, e, 0, f)),
                 pl.BlockSpec((None, None, d, tf), lambda e, f: (layer, e, 0, f)),
                 pl.BlockSpec((None, None, tf, d), lambda e, f: (layer, e, f, 0))]
    return pl.pallas_call(
        functools.partial(_expert_ffn_kernel, len(groups)),
        grid=(n_exp, f_dim // tf),
        in_specs=in_specs,
        out_specs=out_specs,
        out_shape=out_shapes,
        scratch_shapes=scratch,
        compiler_params=_cparams("parallel", "arbitrary"),
    )(*args, w_gate, w_up, w_down)


def _scatter_kernel(cap, idx_ref, eo_ref, x_ref, g_ref, o_ref):
    e = pl.program_id(2)
    base = (pl.program_id(0) * pl.num_programs(2) + e) * cap

    @pl.when(e == 0)
    def _():
        o_ref[...] = jnp.zeros(o_ref.shape, F32)

    def body(i, _):
        for u in range(GATHER_UNROLL):
            c = i * GATHER_UNROLL + u
            row = idx_ref[base + c]
            o_ref[pl.ds(row, 1), :] = o_ref[pl.ds(row, 1), :] + eo_ref[pl.ds(c, 1), :]
        return 0

    lax.fori_loop(0, cap // GATHER_UNROLL, body, 0)

    @pl.when(e == pl.num_programs(2) - 1)
    def _():
        o_ref[...] = x_ref[...] + g_ref[...] * o_ref[...]


def scatter_residual(idx, expert_out, x, gate, cap):
    n_exp, b, _, d = expert_out.shape
    t = x.shape[1]
    dq = d // 4
    return pl.pallas_call(
        functools.partial(_scatter_kernel, cap),
        grid_spec=pltpu.PrefetchScalarGridSpec(
            num_scalar_prefetch=1,
            grid=(b, 4, n_exp),
            in_specs=[pl.BlockSpec((None, None, cap, dq), lambda bi, q, e, idx_: (e, bi, 0, q)),
                      pl.BlockSpec((None, t, dq), lambda bi, q, e, idx_: (bi, 0, q)),
                      pl.BlockSpec((None, 1, dq), lambda bi, q, e, idx_: (bi, 0, q))],
            out_specs=pl.BlockSpec((None, t, dq), lambda bi, q, e, idx_: (bi, 0, q)),
        ),
        out_shape=jax.ShapeDtypeStruct((b, t, d), F32),
        compiler_params=_cparams("parallel", "parallel", "arbitrary"),
    )(idx, expert_out, x, gate)


def moe_residual(streams, gain, layer, w_router, w_gate, w_up, w_down):
    n_exp = w_router.shape[1]
    routed = []
    for x, scale, shift, _ in streams:
        cap = CAPACITY_FACTOR * x.shape[1] // n_exp
        h, aff, aff_t = router(x, gain, scale, shift, w_router)
        idx = expert_choice_topk(aff_t, cap)
        routed.append((idx, cap, gather_tokens(idx, h, aff, n_exp, cap)))
    outs = expert_ffn([g for _, _, g in routed], layer, w_gate, w_up, w_down)
    return [scatter_residual(idx, eo, x, gate, cap)
            for (idx, cap, _), eo, (x, _, _, gate) in zip(routed, outs, streams)]


def kernel(x, c, ctx, c_ctx, mod_w, mod_b, norm1_w, norm2_w, w_in, rw_mu_prev, rw_mu_next, rw_w0, rw_w_up,
           rw_a0, rw_a_up, rw_g_up, rw_k_k, rw_k_a, rw_r_k, rw_ln_w, rw_ln_b, da_q_gain, da_k_gain, da_lq1,
           da_lk1, da_lq2, da_lk2, da_sub_gain, ft_w, pl_w, pl_scale, w_out, moe_router, moe_w_gate, moe_w_up,
           moe_w_down):
    depth, d = norm1_w.shape
    batch = x.shape[0]
    gw = d // N_MIXERS
    c_rows = jnp.zeros((SUBLANES, d), F32).at[:batch].set(c).at[batch].set(c_ctx)
    mod = modulation_vectors(c_rows, mod_w, mod_b)
    w_in_b = w_in.astype(BF16)
    w_out_b = w_out.astype(BF16)
    for l in range(depth):
        ctx_out = l < depth - 1
        lam_init = 0.8 - 0.6 * math.exp(-0.3 * l)
        mx = mod[l, :batch].reshape(batch, 6, 1, d)
        mc = jnp.broadcast_to(mod[l, batch].reshape(1, 6, 1, d), (batch, 6, 1, d))
        rw = (rw_mu_prev[l], rw_mu_next[l], rw_w0[l], rw_w_up[l], rw_a0[l], rw_a_up[l], rw_g_up[l],
              rw_k_k[l], rw_k_a[l], rw_r_k[l], rw_ln_w[l], rw_ln_b[l])
        da = (da_q_gain[l], da_k_gain[l], da_lq1[l], da_lk1[l], da_lq2[l], da_lk2[l], da_sub_gain[l])
        rwx, dax, ftx, plx = input_projection(x, norm1_w[l], mx[:, 1], mx[:, 0], w_in_b[l], gw)
        rwc, dac, ftc, plc = input_projection(ctx, norm1_w[l], mc[:, 1], mc[:, 0], w_in_b[l], gw)
        ax, ac = rwkv_mixer(rwx, rwc, rw, gw, ctx_out)
        bx, bc = diff_attention(dax, dac, da, gw, lam_init, ctx_out)
        fx = fourier_mix(ftx, ft_w[l])
        px = pool_mix(plx, pl_w[l], pl_scale[l])
        x = output_projection(x, mx[:, 2], (ax, bx, fx, px), w_out_b[l])
        streams = [(x, mx[:, 4], mx[:, 3], mx[:, 5])]
        if ctx_out:
            fc = fourier_mix(ftc, ft_w[l])
            pc = pool_mix(plc, pl_w[l], pl_scale[l])
            ctx = output_projection(ctx, mc[:, 2], (ac, bc, fc, pc), w_out_b[l])
            streams.append((ctx, mc[:, 4], mc[:, 3], mc[:, 5]))
        outs = moe_residual(streams, norm2_w[l], l, moe_router[l], moe_w_gate, moe_w_up, moe_w_down)
        x = outs[0]
        if ctx_out:
            ctx = outs[1]
    return x
```

```python
import functools
import math

import numpy as np
import jax
import jax.numpy as jnp
from jax import lax
from jax.experimental import pallas as pl
from jax.experimental.pallas import tpu as pltpu

F32 = jnp.float32
BF16 = jnp.bfloat16
HIGHEST = lax.Precision.HIGHEST

N_MIXERS = 4
HEAD = 64
NORM_EPS = 1e-6
RW_LN_EPS = 64e-5
GRID_W = 64
DA_QK = HEAD // 2
ROPE_BASE = 10000.0
POOL_WINDOWS = (2, 4, 8, 16)
N_EXPERTS = 16
CAPACITY_FACTOR = 2

LANES = 128
SUBLANES = 8
VMEM_LIMIT_BYTES = 56 * 1024 * 1024

RW_CHUNK = 64
RW_CHUNKS_PER_STEP = 8


def _cparams(*sem):
    return pltpu.CompilerParams(dimension_semantics=sem, vmem_limit_bytes=VMEM_LIMIT_BYTES)


def _dot(a, b):
    return jnp.dot(a.astype(BF16), b.astype(BF16), preferred_element_type=F32)


def _dot_f32(a, b):
    return jnp.dot(a, b, precision=HIGHEST, preferred_element_type=F32)


def _dot_tri(a, b):
    return _dot(a, b)


def _dot_nt(a, b, exact=False):
    dn = (((1,), (1,)), ((), ()))
    if exact:
        return lax.dot_general(a, b, dn, precision=HIGHEST, preferred_element_type=F32)
    return lax.dot_general(a.astype(BF16), b.astype(BF16), dn, preferred_element_type=F32)


def _dot_tn(a, b, exact=False):
    dn = (((0,), (0,)), ((), ()))
    if exact:
        return lax.dot_general(a, b, dn, precision=HIGHEST, preferred_element_type=F32)
    return lax.dot_general(a.astype(BF16), b.astype(BF16), dn, preferred_element_type=F32)


def _sigmoid(x):
    return 1.0 / (1.0 + jnp.exp(-x))


def _block_ones(n, blk, value=1.0):
    i = np.arange(n) // blk
    return jnp.asarray((i[:, None] == i[None, :]).astype(np.float32) * value)


def _const_spec(shape):
    nd = len(shape)
    return pl.BlockSpec(shape, lambda *_: (0,) * nd)


def _mod_kernel(c_ref, w_ref, b_ref, o_ref):
    c = c_ref[...]
    o_ref[...] = _dot_f32(c * _sigmoid(c), w_ref[...]) + b_ref[...]


def modulation_vectors(c_rows, mod_w, mod_b):
    depth, d, n = mod_w.shape
    tn = 1536
    return pl.pallas_call(
        _mod_kernel,
        grid=(depth, n // tn),
        in_specs=[
            pl.BlockSpec((SUBLANES, d), lambda l, j: (0, 0)),
            pl.BlockSpec((None, d, tn), lambda l, j: (l, 0, j)),
            pl.BlockSpec((None, 1, tn), lambda l, j: (l, 0, j)),
        ],
        out_specs=pl.BlockSpec((None, SUBLANES, tn), lambda l, j: (l, 0, j)),
        out_shape=jax.ShapeDtypeStruct((depth, SUBLANES, n), F32),
        compiler_params=_cparams("parallel", "parallel"),
    )(c_rows, mod_w, mod_b.reshape(depth, 1, n))


def _modulated_norm(x, gain, scale, shift):
    ms = jnp.mean(x * x, axis=-1, keepdims=True)
    return (x * lax.rsqrt(ms + NORM_EPS) * gain) * (1.0 + scale) + shift


def _inproj_kernel(splits, x_ref, gain_ref, sc_ref, sh_ref, w_ref, *o_refs):
    h = _modulated_norm(x_ref[...], gain_ref[...], sc_ref[...], sh_ref[...]).astype(BF16)
    for (lo, hi), o_ref in zip(splits, o_refs):
        o_ref[...] = jnp.dot(h, w_ref[:, lo:hi], preferred_element_type=F32)


def input_projection(x, gain, scale, shift, w_in_bf16, group_w):
    b, t, d = x.shape
    rw_cols = w_in_bf16.shape[1] - 3 * group_w - 2 * group_w
    cuts = [0, rw_cols, rw_cols + 3 * group_w, rw_cols + 4 * group_w, rw_cols + 5 * group_w]
    splits = tuple((cuts[i], cuts[i + 1]) for i in range(4))
    tm = min(512, t)
    row = pl.BlockSpec((None, 1, d), lambda bi, i: (bi, 0, 0))
    return pl.pallas_call(
        functools.partial(_inproj_kernel, splits),
        grid=(b, t // tm),
        in_specs=[
            pl.BlockSpec((None, tm, d), lambda bi, i: (bi, i, 0)),
            _const_spec((1, d)),
            row, row,
            _const_spec(w_in_bf16.shape),
        ],
        out_specs=[pl.BlockSpec((None, tm, hi - lo), lambda bi, i: (bi, i, 0)) for lo, hi in splits],
        out_shape=[jax.ShapeDtypeStruct((b, t, hi - lo), F32) for lo, hi in splits],
        compiler_params=_cparams("parallel", "parallel"),
    )(x, gain.reshape(1, d), scale, shift, w_in_bf16)


def _halo_specs(tm, t, width):
    nb8 = t // SUBLANES
    r8 = tm // SUBLANES
    return [
        pl.BlockSpec((None, tm, width), lambda b, i: (b, i, 0)),
        pl.BlockSpec((None, SUBLANES, width), lambda b, i: (b, jnp.maximum(i * r8 - 1, 0), 0)),
        pl.BlockSpec((None, SUBLANES, width), lambda b, i: (b, jnp.minimum((i + 1) * r8, nb8 - 1), 0)),
    ]


def _stage_with_halo(buf_ref, main_ref, prev_ref, next_ref):
    tm = main_ref.shape[0]
    i = pl.program_id(1)
    n = pl.num_programs(1)
    buf_ref[SUBLANES:SUBLANES + tm, :] = main_ref[...]
    buf_ref[0:SUBLANES, :] = jnp.where(i > 0, prev_ref[...], 0.0)
    buf_ref[SUBLANES + tm:2 * SUBLANES + tm, :] = jnp.where(i < n - 1, next_ref[...], 0.0)


def _rwkv_prep_kernel(gw, rw_ref, prev_ref, next_ref, mup_ref, mun_ref, kk_ref_, ka_ref, w0_ref, wup_ref,
                      a0_ref, aup_ref, gup_ref, hsum_ref,
                      r_o, k_o, v_o, kk_o, gate_o, lwf_o, kaf_o, kdf_o, lwb_o, kab_o, kdb_o, buf_ref):
    tm = rw_ref.shape[0]
    _stage_with_halo(buf_ref, rw_ref, prev_ref, next_ref)
    p = buf_ref[SUBLANES:SUBLANES + tm, :]
    prev = buf_ref[SUBLANES - 1:SUBLANES - 1 + tm, :]
    nxt = buf_ref[SUBLANES + 1:SUBLANES + 1 + tm, :]
    u = p + mup_ref[...] * (prev - p) + mun_ref[...] * (nxt - p)
    r = u[:, 0:gw]
    k = u[:, gw:2 * gw]
    v = u[:, 2 * gw:3 * gw]
    lora_w = u[:, 3 * gw:3 * gw + LANES]
    lora_a = u[:, 3 * gw + LANES:3 * gw + 2 * LANES]
    g = u[:, 3 * gw + 2 * LANES:3 * gw + 3 * LANES]
    kk = k * kk_ref_[...]
    ss = _dot_f32(kk * kk, hsum_ref[...])
    kk = kk * lax.rsqrt(jnp.maximum(ss, 1e-24))
    zw = _dot_f32(jnp.tanh(lora_w), wup_ref[...]) + w0_ref[...]
    za = _dot_f32(lora_a, aup_ref[...]) + a0_ref[...]
    logw = -_sigmoid(zw) * math.exp(-0.5)
    a = _sigmoid(za)
    r_o[...] = r
    k_o[...] = k
    v_o[...] = v
    kk_o[...] = kk
    gate_o[...] = _dot_f32(_sigmoid(g), gup_ref[...])
    ka = ka_ref[...]
    for d, (lw_o, kka_o, kd_o) in enumerate(((lwf_o, kaf_o, kdf_o), (lwb_o, kab_o, kdb_o))):
        a_d = a[:, d * gw:(d + 1) * gw]
        lw_o[...] = logw[:, d * gw:(d + 1) * gw]
        kka_o[...] = kk * a_d
        kd_o[...] = k * (1.0 + (a_d - 1.0) * ka)


def _blockdiag2(m):
    r, c = m.shape[1:]
    z = jnp.zeros((r, c), m.dtype)
    return jnp.concatenate([jnp.concatenate([m[0], z], 1), jnp.concatenate([z, m[1]], 1)], 0)


def rwkv_prepare(rw, params, gw):
    (mu_prev, mu_next, w0, w_up, a0, a_up, g_up, k_k, k_a, r_k, ln_w, ln_b) = params
    b, t, cols = rw.shape
    tm = min(512, t)
    row = lambda v: v.reshape(1, -1)
    small = [row(mu_prev), row(mu_next), row(k_k), row(k_a), row(w0), _blockdiag2(w_up), row(a0),
             _blockdiag2(a_up), g_up, _block_ones(gw, HEAD)]
    out = jax.ShapeDtypeStruct((b, t, gw), F32)
    return pl.pallas_call(
        functools.partial(_rwkv_prep_kernel, gw),
        grid=(b, t // tm),
        in_specs=_halo_specs(tm, t, cols) + [_const_spec(s.shape) for s in small],
        out_specs=[pl.BlockSpec((None, tm, gw), lambda bi, i: (bi, i, 0))] * 11,
        out_shape=[out] * 11,
        scratch_shapes=[pltpu.VMEM((tm + 2 * SUBLANES, cols), F32)],
        compiler_params=_cparams("parallel", "parallel"),
    )(rw, rw, rw, *small)


RW_BUILD_CHUNKS_PER_STEP = 2


def _rwkv_chunk_kernel(ncs, r_ref, v_ref, kk_ref, lwf_ref, kaf_ref, kdf_ref, lwb_ref, kab_ref, kdb_ref,
                       y0f_o, qf_o, mf_o, g0f_o, y0b_o, qb_o, mb_o, g0b_o):
    L = RW_CHUNK
    rows = lax.broadcasted_iota(jnp.int32, (L, L), 0)
    cols = lax.broadcasted_iota(jnp.int32, (L, L), 1)
    eye = jnp.where(rows == cols, 1.0, 0.0)
    rows2 = lax.broadcasted_iota(jnp.int32, (L, 2 * L), 0)
    cols2 = lax.broadcasted_iota(jnp.int32, (L, 2 * L), 1) % L
    lane = lax.broadcasted_iota(jnp.int32, (1, LANES), 1)
    head_masks = [(lane >= h * HEAD) & (lane < (h + 1) * HEAD) for h in range(2)]
    r2 = lax.broadcasted_iota(jnp.int32, (LANES, LANES), 0)
    c2 = lax.broadcasted_iota(jnp.int32, (LANES, LANES), 1)
    same_head = (r2 // HEAD) == (c2 // HEAD)
    diag = r2 == c2
    directions = ((False, lwf_ref, kaf_ref, kdf_ref, y0f_o, qf_o, mf_o, g0f_o),
                  (True, lwb_ref, kab_ref, kdb_ref, y0b_o, qb_o, mb_o, g0b_o))

    probs = []
    for c in range(ncs):
        rs = slice(c * L, (c + 1) * L)
        for pair in range(2):
            sl = slice(pair * LANES, (pair + 1) * LANES)
            r = r_ref[rs, sl]
            v = v_ref[rs, sl]
            kk = kk_ref[rs, sl]
            for reverse, lw_ref, ka_ref, kd_ref, y0_o, q_o, m_o, g0_o in directions:
                incl = (cols >= rows) if reverse else (cols <= rows)
                strict = (cols > rows) if reverse else (cols < rows)
                incl2 = (cols2 >= rows2) if reverse else (cols2 <= rows2)
                logw = lw_ref[rs, sl]
                kd = kd_ref[rs, sl]
                a = -ka_ref[rs, sl]
                cum = _dot_f32(jnp.where(incl, 1.0, 0.0), logw)
                total = jnp.sum(logw, axis=0, keepdims=True)
                g_inv = jnp.exp(-cum)
                g_tail = jnp.exp(total - cum)
                bd = kk * jnp.exp(cum - logw)
                rd = r * jnp.exp(cum)
                probs.append(dict(
                    incl2=incl2, strict=strict, v=v, bd=bd, rd=rd, total=total,
                    lhs=jnp.concatenate([bd, rd], axis=0), rhs=jnp.concatenate([a * g_inv, kd * g_inv], axis=0),
                    tails=jnp.concatenate([kd * g_tail, a * g_tail], axis=0),
                    outs=(y0_o, q_o, m_o, g0_o), rs=rs, sl=sl, c=c, pair=pair))

    heads = []
    for p in probs:
        for h in range(2):
            gram = _dot_nt(jnp.where(head_masks[h], p["lhs"], 0.0), p["rhs"])
            heads.append(dict(
                p=p, h=h,
                nil=jnp.where(p["strict"], gram[:L, :L], 0.0),
                aak=jnp.where(p["strict"], gram[:L, L:], 0.0),
                ara_ark=jnp.where(p["incl2"], gram[L:, :], 0.0)))

    for hd in heads:
        hd["acc"] = eye + hd["nil"]
        hd["pow"] = hd["nil"]
    span = 2
    while span < L:
        for hd in heads:
            hd["pow"] = _dot_tri(hd["pow"], hd["pow"])
        for hd in heads:
            hd["acc"] = hd["acc"] + _dot_tri(hd["acc"], hd["pow"])
        span *= 2

    for hd in heads:
        hd["aakv"] = _dot(hd["aak"], hd["p"]["v"])
    for hd in heads:
        wp = _dot_tri(hd["acc"], jnp.concatenate([hd["aakv"], hd["p"]["bd"]], axis=1))
        hd["w1"], hd["pm"] = wp[:, :LANES], wp[:, LANES:]
    for hd in heads:
        rhs = jnp.concatenate([jnp.concatenate([hd["w1"], hd["pm"]], axis=1),
                               jnp.concatenate([hd["p"]["v"], jnp.zeros((L, LANES), F32)], axis=1)], axis=0)
        yq = _dot(hd["ara_ark"], rhs)
        hd["y0"], hd["q"] = yq[:, :LANES], yq[:, LANES:] + hd["p"]["rd"]

    for i, p in enumerate(probs):
        h0, h1 = heads[2 * i], heads[2 * i + 1]
        pick = lambda key: jnp.where(head_masks[0], h0[key], h1[key])
        y0_o, q_o, m_o, g0_o = p["outs"]
        y0_o[p["rs"], p["sl"]] = pick("y0")
        q_o[p["rs"], p["sl"]] = pick("q")
        vw = jnp.concatenate([p["v"], pick("w1")], axis=0)
        g0_o[p["c"], p["pair"]] = jnp.where(same_head, _dot_tn(p["tails"], vw), 0.0)
        m_o[p["c"], p["pair"]] = (jnp.where(same_head, _dot_tn(p["tails"][L:], pick("pm")), 0.0)
                                  + jnp.where(diag, jnp.exp(p["total"]), 0.0))


def rwkv_chunks(r, v, kk, lwf, kaf, kdf, lwb, kab, kdb):
    b, t, gw = r.shape
    L = RW_CHUNK
    nc = t // L
    ncs = RW_BUILD_CHUNKS_PER_STEP if nc % RW_BUILD_CHUNKS_PER_STEP == 0 else 1
    tok = pl.BlockSpec((None, ncs * L, gw), lambda bi, i: (bi, i, 0))
    mat = pl.BlockSpec((None, ncs, 2, LANES, LANES), lambda bi, i: (bi, i, 0, 0, 0))
    tok_s = jax.ShapeDtypeStruct((b, t, gw), F32)
    mat_s = jax.ShapeDtypeStruct((b, nc, 2, LANES, LANES), F32)
    return pl.pallas_call(
        functools.partial(_rwkv_chunk_kernel, ncs),
        grid=(b, nc // ncs),
        in_specs=[tok] * 9,
        out_specs=[tok, tok, mat, mat] * 2,
        out_shape=[tok_s, tok_s, mat_s, mat_s] * 2,
        compiler_params=_cparams("parallel", "parallel"),
    )(r, v, kk, lwf, kaf, kdf, lwb, kab, kdb)


def _rwkv_scan_kernel(cps, h0_ref, y0f_ref, qf_ref, mf_ref, g0f_ref, y0b_ref, qb_ref, mb_ref, g0b_ref,
                      yf_o, yb_o, hfin_o, h_ref):
    L = RW_CHUNK
    j = pl.program_id(1)

    @pl.when(j == 0)
    def _():
        h_ref[...] = h0_ref[...]

    for step in range(cps):
        for d, (y0_ref, q_ref, m_ref, g0_ref, y_o) in enumerate(
                ((y0f_ref, qf_ref, mf_ref, g0f_ref, yf_o), (y0b_ref, qb_ref, mb_ref, g0b_ref, yb_o))):
            c = step if d == 0 else cps - 1 - step
            rows = slice(c * L, (c + 1) * L)
            for pair in range(2):
                sl = slice(pair * LANES, (pair + 1) * LANES)
                h = h_ref[d, pair]
                y_o[rows, sl] = y0_ref[rows, sl] + _dot_f32(q_ref[rows, sl], h)
                h_ref[d, pair] = _dot_f32(m_ref[c, pair], h) + g0_ref[c, pair]

    @pl.when(j == pl.num_programs(1) - 1)
    def _():
        hfin_o[...] = h_ref[...]


def rwkv_scan(h0, y0f, qf, mf, g0f, y0b, qb, mb, g0b):
    b, t, gw = y0f.shape
    L = RW_CHUNK
    nc = t // L
    cps = min(RW_CHUNKS_PER_STEP, nc)
    nb = nc // cps
    tm = cps * L
    tok_f = pl.BlockSpec((None, tm, gw), lambda bi, i: (bi, i, 0))
    tok_b = pl.BlockSpec((None, tm, gw), lambda bi, i: (bi, nb - 1 - i, 0))
    mat_f = pl.BlockSpec((None, cps, 2, LANES, LANES), lambda bi, i: (bi, i, 0, 0, 0))
    mat_b = pl.BlockSpec((None, cps, 2, LANES, LANES), lambda bi, i: (bi, nb - 1 - i, 0, 0, 0))
    st = pl.BlockSpec((None, 2, 2, LANES, LANES), lambda bi, i: (bi, 0, 0, 0, 0))
    tok_s = jax.ShapeDtypeStruct((b, t, gw), F32)
    return pl.pallas_call(
        functools.partial(_rwkv_scan_kernel, cps),
        grid=(b, nb),
        in_specs=[st, tok_f, tok_f, mat_f, mat_f, tok_b, tok_b, mat_b, mat_b],
        out_specs=[tok_f, tok_b, st],
        out_shape=[tok_s, tok_s, jax.ShapeDtypeStruct((b, 2, 2, LANES, LANES), F32)],
        scratch_shapes=[pltpu.VMEM((2, 2, LANES, LANES), F32)],
        compiler_params=_cparams("parallel", "arbitrary"),
    )(h0, y0f, qf, mf, g0f, y0b, qb, mb, g0b)


def _rwkv_out_kernel(yf_ref, yb_ref, r_ref, k_ref, v_ref, gate_ref, rk_ref, lnw_ref, lnb_ref, hmean_ref, o_ref):
    y = yf_ref[...] + yb_ref[...]
    hmean = hmean_ref[...]
    mu = _dot_f32(y, hmean)
    yc = y - mu
    var = _dot_f32(yc * yc, hmean)
    yn = yc * lax.rsqrt(var + RW_LN_EPS) * lnw_ref[...] + lnb_ref[...]
    bonus = _dot_f32(r_ref[...] * k_ref[...] * rk_ref[...], hmean) * float(HEAD) * v_ref[...]
    o_ref[...] = (yn + bonus) * gate_ref[...]


def rwkv_output(yf, yb, r, k, v, gate, r_k, ln_w, ln_b):
    b, t, gw = yf.shape
    tm = min(512, t)
    tok = pl.BlockSpec((None, tm, gw), lambda bi, i: (bi, i, 0))
    small = [r_k.reshape(1, gw), ln_w.reshape(1, gw), ln_b.reshape(1, gw), _block_ones(gw, HEAD, 1.0 / HEAD)]
    return pl.pallas_call(
        _rwkv_out_kernel,
        grid=(b, t // tm),
        in_specs=[tok] * 6 + [_const_spec(s.shape) for s in small],
        out_specs=tok,
        out_shape=jax.ShapeDtypeStruct((b, t, gw), F32),
        compiler_params=_cparams("parallel", "parallel"),
    )(yf, yb, r, k, v, gate, *small)


def rwkv_mixer(rwx, rwc, params, gw, ctx_out):
    r_k, ln_w, ln_b = params[9], params[10], params[11]
    sx = rwkv_prepare(rwx, params, gw)
    sc = rwkv_prepare(rwc, params, gw)
    (rx, kx, vx, kkx, gx), dx = sx[:5], sx[5:]
    (rc, kc, vc, kkc, gc), dc = sc[:5], sc[5:]
    cx = rwkv_chunks(rx, vx, kkx, *dx)
    cc = rwkv_chunks(rc, vc, kkc, *dc)
    b = rwx.shape[0]
    h0 = jnp.zeros((b, 2, 2, LANES, LANES), F32)
    ycf, ycb, h_ctx = rwkv_scan(h0, *cc)
    yxf, yxb, _ = rwkv_scan(h_ctx, *cx)
    out_x = rwkv_output(yxf, yxb, rx, kx, vx, gx, r_k, ln_w, ln_b)
    out_c = rwkv_output(ycf, ycb, rc, kc, vc, gc, r_k, ln_w, ln_b) if ctx_out else None
    return out_x, out_c


def _rope_tables(n_tokens, reps):
    rows = n_tokens // GRID_W
    row = np.repeat(np.arange(rows), GRID_W).astype(np.float64)
    col = np.tile(np.arange(GRID_W), rows).astype(np.float64)
    n_freq = DA_QK // 4
    inv = ROPE_BASE ** (-np.arange(n_freq, dtype=np.float64) / n_freq)
    ar = row[:, None] * inv
    ac = col[:, None] * inv
    ang = np.concatenate([ar, ar, ac, ac], axis=-1)
    cos = np.tile(np.cos(ang), (1, reps)).astype(np.float32)
    sin = np.tile(np.sin(ang), (1, reps)).astype(np.float32)
    return jnp.asarray(cos), jnp.asarray(sin)


def _attn_prep_kernel(gw, rope, da_ref, qg_ref, kg_ref, gmean_ref, *rest):
    if rope:
        cos_ref, sin_ref, q_o, k_o, v_o = rest
    else:
        q_o, k_o, v_o = rest
    da = da_ref[...]
    gmean = gmean_ref[...]
    lane = lax.broadcasted_iota(jnp.int32, (1, gw), 1)
    first_half = (lane % (DA_QK // 2)) < (DA_QK // 4)

    def norm_rope(x, gain):
        ms = _dot_f32(x * x, gmean)
        y = x * lax.rsqrt(ms + NORM_EPS) * gain
        if rope:
            quarter = DA_QK // 4
            rot = jnp.where(first_half, -pltpu.roll(y, gw - quarter, 1), pltpu.roll(y, quarter, 1))
            y = y * cos_ref[...] + rot * sin_ref[...]
        return y

    q_o[...] = (norm_rope(da[:, 0:gw], qg_ref[...]) * (DA_QK ** -0.5 * LOG2_E)).astype(q_o.dtype)
    k_o[...] = norm_rope(da[:, gw:2 * gw], kg_ref[...]).astype(k_o.dtype)
    v_o[...] = da[:, 2 * gw:3 * gw].astype(v_o.dtype)


def attention_prepare(da, q_gain, k_gain, gw, rope):
    b, t, cols = da.shape
    tm = min(512, t)
    reps = gw // DA_QK
    small = [jnp.tile(q_gain, reps).reshape(1, gw), jnp.tile(k_gain, reps).reshape(1, gw),
             _block_ones(gw, DA_QK, 1.0 / DA_QK)]
    args = [da] + small
    in_specs = [pl.BlockSpec((None, tm, cols), lambda bi, i: (bi, i, 0))] + [_const_spec(s.shape) for s in small]
    if rope:
        cos, sin = _rope_tables(t, reps)
        args += [cos, sin]
        in_specs += [pl.BlockSpec((tm, gw), lambda bi, i: (i, 0))] * 2
    tok = pl.BlockSpec((None, tm, gw), lambda bi, i: (bi, i, 0))
    return pl.pallas_call(
        functools.partial(_attn_prep_kernel, gw, rope),
        grid=(b, t // tm),
        in_specs=in_specs,
        out_specs=[tok] * 3,
        out_shape=[jax.ShapeDtypeStruct((b, t, gw), BF16)] * 3,
        compiler_params=_cparams("parallel", "parallel"),
    )(*args)


LOG2_E = 1.4426950408889634
FLASH_ROW_TILE = 16


def _flash_kernel(lam_init, q_ref, kt_ref, v_ref, lq1_ref, lk1_ref, lq2_ref, lk2_ref, sg_ref, o_ref,
                  m_ref, acc_ref, alpha_ref, s_ref, p_ref):
    j = pl.program_id(3)

    @pl.when(j == 0)
    def _():
        m_ref[...] = jnp.full(m_ref.shape, -1e30, F32)
        acc_ref[...] = jnp.zeros(acc_ref.shape, F32)

    tq = q_ref.shape[1]
    for m in range(2):
        s_ref[m] = jnp.dot(q_ref[m], kt_ref[m], preferred_element_type=F32)

    def row_tile(i, _):
        rs = pl.ds(pl.multiple_of(i * FLASH_ROW_TILE, FLASH_ROW_TILE), FLASH_ROW_TILE)
        for m in range(2):
            s = s_ref[m, rs, :]
            m_old = m_ref[m, rs, :]
            m_new = jnp.maximum(m_old, jnp.max(s, axis=-1, keepdims=True))
            alpha_ref[m, rs, :] = jnp.exp2(m_old - m_new)
            m_ref[m, rs, :] = m_new
            p_ref[m, rs, :] = jnp.exp2(s - m_new).astype(BF16)
        return 0

    lax.fori_loop(0, tq // FLASH_ROW_TILE, row_tile, 0, unroll=4)
    v = v_ref[...]
    for m in range(2):
        acc_ref[m] = alpha_ref[m] * acc_ref[m] + jnp.dot(p_ref[m], v, preferred_element_type=F32)

    @pl.when(j == pl.num_programs(3) - 1)
    def _():
        lam = (jnp.exp(jnp.sum(lq1_ref[...] * lk1_ref[...], axis=-1, keepdims=True))
               - jnp.exp(jnp.sum(lq2_ref[...] * lk2_ref[...], axis=-1, keepdims=True)) + lam_init)
        a0 = acc_ref[0]
        a1 = acc_ref[1]
        o = a0[:, :HEAD] / a0[:, HEAD:HEAD + 1] - lam * (a1[:, :HEAD] / a1[:, HEAD:HEAD + 1])
        ms = jnp.mean(o * o, axis=-1, keepdims=True)
        o_ref[...] = o * lax.rsqrt(ms + NORM_EPS) * sg_ref[...] * (1.0 - lam_init)


def _largest_divisor(n, cap, multiple):
    best = None
    for d in range(multiple, cap + 1, multiple):
        if n % d == 0:
            best = d
    return best if best is not None else n


def diff_attention_core(q, kt, v, lam_params, sub_gain, lam_init):
    b, h, _, t, dk = q.shape
    s = kt.shape[-1]
    tq = min(512, t)
    tk = _largest_divisor(s, 1536, LANES)
    small = [p.reshape(1, -1) for p in lam_params] + [sub_gain.reshape(1, -1)]
    return pl.pallas_call(
        functools.partial(_flash_kernel, lam_init),
        grid=(b, h, t // tq, s // tk),
        in_specs=[
            pl.BlockSpec((None, None, 2, tq, dk), lambda bi, hi, i, j: (bi, hi, 0, i, 0)),
            pl.BlockSpec((None, None, 2, dk, tk), lambda bi, hi, i, j: (bi, hi, 0, 0, j)),
            pl.BlockSpec((None, None, tk, LANES), lambda bi, hi, i, j: (bi, hi, j, 0)),
        ] + [_const_spec(x.shape) for x in small],
        out_specs=pl.BlockSpec((None, None, tq, HEAD), lambda bi, hi, i, j: (bi, hi, i, 0)),
        out_shape=jax.ShapeDtypeStruct((b, h, t, HEAD), F32),
        scratch_shapes=[pltpu.VMEM((2, tq, 1), F32), pltpu.VMEM((2, tq, LANES), F32), pltpu.VMEM((2, tq, 1), F32),
                        pltpu.VMEM((2, tq, tk), F32), pltpu.VMEM((2, tq, tk), BF16)],
        compiler_params=_cparams("parallel", "parallel", "parallel", "arbitrary"),
    )(q, kt, v, *small)


def _split_heads(q, k, v):
    b, t, gw = q.shape
    h = gw // HEAD
    qh = q.reshape(b, t, h, 2, DA_QK).transpose(0, 2, 3, 1, 4)
    kth = k.reshape(b, t, h, 2, DA_QK).transpose(0, 2, 3, 4, 1)
    vh = v.reshape(b, t, h, HEAD).transpose(0, 2, 1, 3)
    ones_col = (lax.broadcasted_iota(jnp.int32, (b, h, t, LANES - HEAD), 3) == 0).astype(v.dtype)
    return qh, kth, jnp.concatenate([vh, ones_col], axis=-1)


def diff_attention(dax, dac, params, gw, lam_init, ctx_out):
    q_gain, k_gain, lq1, lk1, lq2, lk2, sub_gain = params
    qx, ktx, vx = _split_heads(*attention_prepare(dax, q_gain, k_gain, gw, rope=True))
    qc, ktc, vc = _split_heads(*attention_prepare(dac, q_gain, k_gain, gw, rope=False))
    kt = jnp.concatenate([ktx, ktc], axis=-1)
    v = jnp.concatenate([vx, vc], axis=2)
    lam_params = (lq1, lk1, lq2, lk2)
    merge = lambda o: o.transpose(0, 2, 1, 3).reshape(o.shape[0], o.shape[2], gw)
    out_x = merge(diff_attention_core(qx, kt, v, lam_params, sub_gain, lam_init))
    out_c = merge(diff_attention_core(qc, ktc, vc, lam_params, sub_gain, lam_init)) if ctx_out else None
    return out_x, out_c


FT_RADIX = 64


def _dft_cos_sin(n, scale=1.0):
    i = np.arange(n)
    ang = 2.0 * np.pi * ((i[:, None] * i[None, :]) % n) / n
    return np.cos(ang) * scale, np.sin(ang) * scale


def _channel_dft(gw, scale):
    c, s = _dft_cos_sin(HEAD, scale)
    eye = np.eye(gw // HEAD)
    return jnp.asarray(np.concatenate([np.kron(eye, c), np.kron(eye, s)], axis=0).astype(np.float32))


def _fnet_stage1_kernel(z_ref, gr_ref, gi_ref, or_ref, oi_ref):
    for j in range(z_ref.shape[1]):
        x = z_ref[:, j, :]
        or_ref[j] = _dot_f32(gr_ref[j], x)
        oi_ref[j] = _dot_f32(gi_ref[j], x)


def _matmul_f32_kernel(a_ref, b_ref, o_ref):
    o_ref[...] = _dot_f32(a_ref[...], b_ref[...])


def _channel_dft_times(chan, w_f):
    return pl.pallas_call(
        _matmul_f32_kernel,
        out_shape=jax.ShapeDtypeStruct((chan.shape[0], w_f.shape[1]), F32),
    )(chan, w_f)


def _fnet_stage2_kernel(gw, br_ref, bi_ref, rot_ref, chanw_ref, o_ref):
    n1 = br_ref.shape[0]
    p = _dot_f32(rot_ref[...], jnp.concatenate([br_ref[...], bi_ref[...]], axis=0))
    groups = br_ref.shape[1] // gw
    rows = jnp.concatenate(
        [jnp.concatenate([p[:n1, g * gw:(g + 1) * gw], p[n1:, g * gw:(g + 1) * gw]], axis=1) for g in range(groups)],
        axis=0)
    out = _dot_f32(rows, chanw_ref[...])
    for g in range(groups):
        o_ref[:, g * gw:(g + 1) * gw] = out[g * n1:(g + 1) * n1]


def fourier_mix_long(z, w_f):
    b, t, gw = z.shape
    n1 = FT_RADIX
    n2 = t // n1
    k2 = np.arange(n2)[None, :, None]
    n = np.arange(n1)[:, None, None] + n1 * np.arange(n2)[None, None, :]
    ang = 2.0 * np.pi * ((k2 * n) % t) / t
    g_r = jnp.asarray(np.cos(ang).astype(np.float32))
    g_i = jnp.asarray((-np.sin(ang)).astype(np.float32))
    j8 = SUBLANES
    br, bi = pl.pallas_call(
        _fnet_stage1_kernel,
        grid=(b, n1 // j8),
        in_specs=[
            pl.BlockSpec((None, n2, j8, gw), lambda bi_, i: (bi_, 0, i, 0)),
            pl.BlockSpec((j8, n2, n2), lambda bi_, i: (i, 0, 0)),
            pl.BlockSpec((j8, n2, n2), lambda bi_, i: (i, 0, 0)),
        ],
        out_specs=[pl.BlockSpec((None, j8, n2, gw), lambda bi_, i: (bi_, i, 0, 0))] * 2,
        out_shape=[jax.ShapeDtypeStruct((b, n1, n2, gw), F32)] * 2,
        compiler_params=_cparams("parallel", "parallel"),
    )(z.reshape(b, n2, n1, gw), g_r, g_i)
    c64, s64 = _dft_cos_sin(n1)
    rot = jnp.asarray(np.block([[c64, s64], [-s64, c64]]).astype(np.float32))
    chanw = _channel_dft_times(_channel_dft(gw, 1.0 / math.sqrt(t * HEAD)), w_f)
    cols = n2 * gw
    tc = min(2048, cols)
    out = pl.pallas_call(
        functools.partial(_fnet_stage2_kernel, gw),
        grid=(b, cols // tc),
        in_specs=[
            pl.BlockSpec((None, n1, tc), lambda bi_, i: (bi_, 0, i)),
            pl.BlockSpec((None, n1, tc), lambda bi_, i: (bi_, 0, i)),
            _const_spec(rot.shape), _const_spec(chanw.shape),
        ],
        out_specs=pl.BlockSpec((None, n1, tc), lambda bi_, i: (bi_, 0, i)),
        out_shape=jax.ShapeDtypeStruct((b, n1, cols), F32),
        compiler_params=_cparams("parallel", "parallel"),
    )(br.reshape(b, n1, cols), bi.reshape(b, n1, cols), rot, chanw)
    return out.reshape(b, t, gw)


def _fnet_dense_kernel(z_ref, ct_ref, st_ref, chanw_ref, o_ref):
    z = z_ref[...]
    pr = _dot_f32(ct_ref[...], z)
    pi = -_dot_f32(st_ref[...], z)
    o_ref[...] = _dot_f32(jnp.concatenate([pr, pi], axis=1), chanw_ref[...])


def fourier_mix_short(z, w_f):
    b, t, gw = z.shape
    ct, st = _dft_cos_sin(t)
    ct = jnp.asarray(ct.astype(np.float32))
    st = jnp.asarray(st.astype(np.float32))
    chanw = _channel_dft_times(_channel_dft(gw, 1.0 / math.sqrt(t * HEAD)), w_f)
    tok = pl.BlockSpec((None, t, gw), lambda bi: (bi, 0, 0))
    return pl.pallas_call(
        _fnet_dense_kernel,
        grid=(b,),
        in_specs=[tok, _const_spec(ct.shape), _const_spec(st.shape), _const_spec(chanw.shape)],
        out_specs=tok,
        out_shape=jax.ShapeDtypeStruct((b, t, gw), F32),
        compiler_params=_cparams("parallel"),
    )(z, ct, st, chanw)


def fourier_mix(z, w_f):
    t = z.shape[1]
    if t % (FT_RADIX * SUBLANES) == 0 and t // FT_RADIX >= LANES:
        return fourier_mix_long(z, w_f)
    return fourier_mix_short(z, w_f)


def _pool_kernel(t_total, u_ref, prev_ref, next_ref, w_ref, s_ref, o_ref, buf_ref):
    tm, gw = u_ref.shape
    _stage_with_halo(buf_ref, u_ref, prev_ref, next_ref)
    at = lambda off: buf_ref[SUBLANES + off:SUBLANES + off + tm, :]
    u = at(0)
    t = pl.program_id(1) * tm + lax.broadcasted_iota(jnp.int32, (tm, 1), 0)
    lane = lax.broadcasted_iota(jnp.int32, (1, gw), 1)
    group = lane // (gw // len(POOL_WINDOWS))
    mean = jnp.zeros((tm, gw), F32)
    run = jnp.zeros((tm, gw), F32)
    half_prev = 0
    for i, w in enumerate(POOL_WINDOWS):
        half = w // 2
        for off in range(half_prev, half):
            run = run + at(-off - 1) + at(off)
        half_prev = half
        cnt = (jnp.minimum(t + half, t_total) - jnp.maximum(t - half, 0)).astype(F32)
        mean = jnp.where(group == i, run / cnt, mean)
    o_ref[...] = _dot_f32(mean - u, w_ref[...]) * s_ref[...]


def pool_mix(u, w_p, s_p):
    b, t, gw = u.shape
    tm = min(512, t)
    nw, ch = w_p.shape[0], w_p.shape[1]
    w_bd = jnp.zeros((gw, gw), F32)
    for i in range(nw):
        w_bd = w_bd.at[i * ch:(i + 1) * ch, i * ch:(i + 1) * ch].set(w_p[i])
    return pl.pallas_call(
        functools.partial(_pool_kernel, t),
        grid=(b, t // tm),
        in_specs=_halo_specs(tm, t, gw) + [_const_spec((gw, gw)), _const_spec((1, gw))],
        out_specs=pl.BlockSpec((None, tm, gw), lambda bi, i: (bi, i, 0)),
        out_shape=jax.ShapeDtypeStruct((b, t, gw), F32),
        scratch_shapes=[pltpu.VMEM((tm + 2 * SUBLANES, gw), F32)],
        compiler_params=_cparams("parallel", "parallel"),
    )(u, u, u, w_bd, s_p.reshape(1, gw))


def _outproj_kernel(gw, x_ref, g_ref, a_ref, b_ref, f_ref, p_ref, w_ref, o_ref):
    acc = None
    for i, m_ref in enumerate((a_ref, b_ref, f_ref, p_ref)):
        part = jnp.dot(m_ref[...].astype(BF16), w_ref[i * gw:(i + 1) * gw, :], preferred_element_type=F32)
        acc = part if acc is None else acc + part
    o_ref[...] = x_ref[...] + g_ref[...] * acc


def output_projection(x, gate, mixers, w_out_bf16):
    b, t, d = x.shape
    gw = mixers[0].shape[-1]
    tm = min(512, t)
    tok = pl.BlockSpec((None, tm, gw), lambda bi, i: (bi, i, 0))
    xs = pl.BlockSpec((None, tm, d), lambda bi, i: (bi, i, 0))
    return pl.pallas_call(
        functools.partial(_outproj_kernel, gw),
        grid=(b, t // tm),
        in_specs=[xs, pl.BlockSpec((None, 1, d), lambda bi, i: (bi, 0, 0))] + [tok] * 4 + [_const_spec(w_out_bf16.shape)],
        out_specs=xs,
        out_shape=jax.ShapeDtypeStruct((b, t, d), F32),
        compiler_params=_cparams("parallel", "parallel"),
    )(x, gate, *mixers, w_out_bf16)


def _router_kernel(n_exp, x_ref, gain_ref, sc_ref, sh_ref, wr_ref, h_o, aff_o, afft_o):
    h = _modulated_norm(x_ref[...], gain_ref[...], sc_ref[...], sh_ref[...])
    h_o[...] = h
    logits = _dot_f32(h, wr_ref[...])
    lane = lax.broadcasted_iota(jnp.int32, logits.shape, 1)
    logits = jnp.where(lane < n_exp, logits, -1e30)
    e = jnp.exp(logits - jnp.max(logits, axis=-1, keepdims=True))
    aff = e / jnp.sum(e, axis=-1, keepdims=True)
    aff_o[...] = aff
    afft_o[...] = jnp.transpose(aff)[:n_exp, :]


def router(x, gain, scale, shift, w_router):
    b, t, d = x.shape
    n_exp = w_router.shape[1]
    tm = min(512, t)
    wr = jnp.zeros((d, LANES), F32).at[:, :n_exp].set(w_router)
    row = pl.BlockSpec((None, 1, d), lambda bi, i: (bi, 0, 0))
    return pl.pallas_call(
        functools.partial(_router_kernel, n_exp),
        grid=(b, t // tm),
        in_specs=[pl.BlockSpec((None, tm, d), lambda bi, i: (bi, i, 0)), _const_spec((1, d)), row, row,
                  _const_spec(wr.shape)],
        out_specs=[pl.BlockSpec((None, tm, d), lambda bi, i: (bi, i, 0)),
                   pl.BlockSpec((None, tm, LANES), lambda bi, i: (bi, i, 0)),
                   pl.BlockSpec((None, n_exp, tm), lambda bi, i: (bi, 0, i))],
        out_shape=[jax.ShapeDtypeStruct((b, t, d), F32), jax.ShapeDtypeStruct((b, t, LANES), F32),
                   jax.ShapeDtypeStruct((b, n_exp, t), F32)],
        compiler_params=_cparams("parallel", "parallel"),
    )(x, gain.reshape(1, d), scale, shift, wr)


TOPK_EXPONENT_STEPS = 7
TOPK_MANTISSA_STEPS = 44


def _row_cumsum(x_ref, o_ref, upper_ref):
    rows, t = x_ref.shape
    carry = jnp.zeros((rows, 1), F32)
    for g in range(t // LANES):
        sl = slice(g * LANES, (g + 1) * LANES)
        local = jnp.dot(x_ref[:, sl].astype(BF16), upper_ref[...], preferred_element_type=F32) + carry
        o_ref[:, sl] = local
        carry = local[:, LANES - 1:LANES]


def _topk_kernel(cap, aff_ref, upper_ref, idx_o, sel_ref, cs_ref):
    aff = aff_ref[...]
    rows, t = aff.shape
    capf = float(cap)
    count_ge = lambda thr: jnp.sum(jnp.where(aff >= thr, 1.0, 0.0), axis=-1, keepdims=True)
    hi = jnp.full((rows, 1), 2.0, F32)
    for step in reversed(range(TOPK_EXPONENT_STEPS)):
        cand = hi * (2.0 ** -(2 ** step))
        hi = jnp.where(count_ge(cand) < capf, cand, hi)
    lo = hi * 0.5
    lo = jnp.where(count_ge(lo) >= capf, lo, 0.0)

    def bisect(_, carry):
        lo, hi = carry
        mid = 0.5 * (lo + hi)
        enough = count_ge(mid) >= capf
        return jnp.where(enough, mid, lo), jnp.where(enough, hi, mid)

    lo, hi = lax.fori_loop(0, TOPK_MANTISSA_STEPS, bisect, (lo, hi))
    above = aff >= hi
    need = capf - count_ge(hi)
    sel_ref[...] = jnp.where((aff >= lo) & jnp.logical_not(above), 1.0, 0.0)
    _row_cumsum(sel_ref, cs_ref, upper_ref)
    tied_in = (sel_ref[...] > 0.5) & (cs_ref[...] <= need)
    sel_ref[...] = jnp.where(above | tied_in, 1.0, 0.0)
    _row_cumsum(sel_ref, cs_ref, upper_ref)

    ones = jnp.ones((SUBLANES, t), BF16)

    def compact(r, _):
        cs_row = cs_ref[pl.ds(r, 1), :]
        for cb in range(0, cap, LANES):
            n = min(LANES, cap - cb)
            slot = (lax.broadcasted_iota(jnp.int32, (n, 1), 0) + cb).astype(F32)
            reached = jnp.where(cs_row <= slot, 1.0, 0.0).astype(BF16)
            cnt = _dot_nt(ones, reached)
            idx_o[r, :, cb:cb + n] = cnt[0:1].astype(jnp.int32)
        return 0

    lax.fori_loop(0, rows, compact, 0)


def expert_choice_topk(aff_t, cap):
    b, n_exp, t = aff_t.shape
    rows = b * n_exp
    upper = jnp.asarray(np.triu(np.ones((LANES, LANES), np.float32))).astype(BF16)
    idx = pl.pallas_call(
        functools.partial(_topk_kernel, cap),
        grid=(1,),
        in_specs=[_const_spec((rows, t)), _const_spec(upper.shape)],
        out_specs=_const_spec((rows, 1, cap)),
        out_shape=jax.ShapeDtypeStruct((rows, 1, cap), jnp.int32),
        scratch_shapes=[pltpu.VMEM((rows, t), F32), pltpu.VMEM((rows, t), F32)],
        compiler_params=_cparams("arbitrary"),
    )(aff_t.reshape(rows, t), upper)
    return idx.reshape(rows * cap)


GATHER_UNROLL = 8
SCATTER_BATCH = 16


def _gather_kernel(cap, idx_ref, h_ref, aff_ref, xs_o, g_o, buf_ref):
    base = (pl.program_id(0) * pl.num_programs(2) + pl.program_id(2)) * cap

    def body(i, _):
        for u in range(GATHER_UNROLL):
            c = i * GATHER_UNROLL + u
            row = idx_ref[base + c]
            buf_ref[pl.ds(c, 1), :] = h_ref[pl.ds(row, 1), :]
            g_o[pl.ds(c, 1), :] = aff_ref[pl.ds(row, 1), :]
        return 0

    lax.fori_loop(0, cap // GATHER_UNROLL, body, 0)
    xs_o[...] = buf_ref[...].astype(BF16)


def gather_tokens(idx, h, aff, n_exp, cap):
    b, t, d = h.shape
    dh = d // 2
    return pl.pallas_call(
        functools.partial(_gather_kernel, cap),
        grid_spec=pltpu.PrefetchScalarGridSpec(
            num_scalar_prefetch=1,
            grid=(b, 2, n_exp),
            in_specs=[pl.BlockSpec((None, t, dh), lambda bi, hf, e, idx_: (bi, 0, hf)),
                      pl.BlockSpec((None, t, LANES), lambda bi, hf, e, idx_: (bi, 0, 0))],
            out_specs=[pl.BlockSpec((None, None, cap, dh), lambda bi, hf, e, idx_: (bi, e, 0, hf)),
                       pl.BlockSpec((None, None, None, cap, LANES), lambda bi, hf, e, idx_: (bi, e, hf, 0, 0))],
            scratch_shapes=[pltpu.VMEM((cap, dh), F32)],
        ),
        out_shape=[jax.ShapeDtypeStruct((b, n_exp, cap, d), BF16),
                   jax.ShapeDtypeStruct((b, n_exp, 2, cap, LANES), F32)],
        compiler_params=_cparams("parallel", "parallel", "arbitrary"),
    )(idx, h, aff)


def _expert_ffn_kernel(n_groups, *refs):
    xs_refs = refs[0:2 * n_groups:2]
    g_refs = refs[1:2 * n_groups:2]
    wg_ref, wu_ref, wd_ref = refs[2 * n_groups:2 * n_groups + 3]
    o_refs = refs[2 * n_groups + 3:3 * n_groups + 3]
    acc_refs = refs[3 * n_groups + 3:]
    f = pl.program_id(1)
    wg = wg_ref[...].astype(BF16)
    wu = wu_ref[...].astype(BF16)
    wd = wd_ref[...].astype(BF16)
    for xs_ref, acc_ref in zip(xs_refs, acc_refs):
        for bi in range(xs_ref.shape[0]):
            xb = xs_ref[bi]
            gate = jnp.dot(xb, wg, preferred_element_type=F32)
            up = jnp.dot(xb, wu, preferred_element_type=F32)
            hid = (gate * _sigmoid(gate) * up).astype(BF16)
            part = jnp.dot(hid, wd, preferred_element_type=F32)

            @pl.when(f == 0)
            def _():
                acc_ref[bi] = part

            @pl.when(f > 0)
            def _():
                acc_ref[bi] += part

    @pl.when(f == pl.num_programs(1) - 1)
    def _():
        e = pl.program_id(0)
        for g_ref, acc_ref, o_ref in zip(g_refs, acc_refs, o_refs):
            lane = lax.broadcasted_iota(jnp.int32, g_ref.shape, 2)
            g = jnp.sum(jnp.where(lane == e, g_ref[...], 0.0), axis=-1, keepdims=True)
            o_ref[...] = acc_ref[...] * g


def expert_ffn(groups, layer, w_gate, w_up, w_down):
    _, n_exp, d, f_dim = w_gate.shape
    tf = _largest_divisor(f_dim, 256, LANES)
    args, in_specs, out_specs, out_shapes, scratch = [], [], [], [], []
    for xs, g_rows in groups:
        b, _, cap, _ = xs.shape
        args += [xs, g_rows]
        in_specs += [pl.BlockSpec((b, None, cap, d), lambda e, f: (0, e, 0, 0)),
                     pl.BlockSpec((b, None, None, cap, LANES), lambda e, f: (0, e, 0, 0, 0))]
        out_specs.append(pl.BlockSpec((None, b, cap, d), lambda e, f: (e, 0, 0, 0)))
        out_shapes.append(jax.ShapeDtypeStruct((n_exp, b, cap, d), F32))
        scratch.append(pltpu.VMEM((b, cap, d), F32))
    in_specs += [pl.BlockSpec((None, None, d, tf), lambda e, f: (layer, e, 0, f)),
                 pl.BlockSpec((None, None, d, tf), lambda e, f: (layer, e, 0, f)),
                 pl.BlockSpec((None, None, tf, d), lambda e, f: (layer, e, f, 0))]
    return pl.pallas_call(
        functools.partial(_expert_ffn_kernel, len(groups)),
        grid=(n_exp, f_dim // tf),
        in_specs=in_specs,
        out_specs=out_specs,
        out_shape=out_shapes,
        scratch_shapes=scratch,
        compiler_params=_cparams("parallel", "arbitrary"),
    )(*args, w_gate, w_up, w_down)


def _scatter_kernel(cap, idx_ref, eo_ref, x_ref, g_ref, o_ref):
    e = pl.program_id(2)
    base = (pl.program_id(0) * pl.num_programs(2) + e) * cap

    @pl.when(e == 0)
    def _():
        o_ref[...] = jnp.zeros(o_ref.shape, F32)

    def body(i, _):
        c0 = i * SCATTER_BATCH
        rows = [idx_ref[base + c0 + u] for u in range(SCATTER_BATCH)]
        sums = [o_ref[pl.ds(rows[u], 1), :] + eo_ref[pl.ds(c0 + u, 1), :] for u in range(SCATTER_BATCH)]
        for u in range(SCATTER_BATCH):
            o_ref[pl.ds(rows[u], 1), :] = sums[u]
        return 0

    lax.fori_loop(0, cap // SCATTER_BATCH, body, 0)

    @pl.when(e == pl.num_programs(2) - 1)
    def _():
        o_ref[...] = x_ref[...] + g_ref[...] * o_ref[...]


def scatter_residual(idx, expert_out, x, gate, cap):
    n_exp, b, _, d = expert_out.shape
    t = x.shape[1]
    dq = d // 4
    return pl.pallas_call(
        functools.partial(_scatter_kernel, cap),
        grid_spec=pltpu.PrefetchScalarGridSpec(
            num_scalar_prefetch=1,
            grid=(b, 4, n_exp),
            in_specs=[pl.BlockSpec((None, None, cap, dq), lambda bi, q, e, idx_: (e, bi, 0, q)),
                      pl.BlockSpec((None, t, dq), lambda bi, q, e, idx_: (bi, 0, q)),
                      pl.BlockSpec((None, 1, dq), lambda bi, q, e, idx_: (bi, 0, q))],
            out_specs=pl.BlockSpec((None, t, dq), lambda bi, q, e, idx_: (bi, 0, q)),
        ),
        out_shape=jax.ShapeDtypeStruct((b, t, d), F32),
        compiler_params=_cparams("parallel", "parallel", "arbitrary"),
    )(idx, expert_out, x, gate)


def moe_residual(streams, gain, layer, w_router, w_gate, w_up, w_down):
    n_exp = w_router.shape[1]
    routed = []
    for x, scale, shift, _ in streams:
        cap = CAPACITY_FACTOR * x.shape[1] // n_exp
        h, aff, aff_t = router(x, gain, scale, shift, w_router)
        idx = expert_choice_topk(aff_t, cap)
        routed.append((idx, cap, gather_tokens(idx, h, aff, n_exp, cap)))
    outs = expert_ffn([g for _, _, g in routed], layer, w_gate, w_up, w_down)
    return [scatter_residual(idx, eo, x, gate, cap)
            for (idx, cap, _), eo, (x, _, _, gate) in zip(routed, outs, streams)]


def kernel(x, c, ctx, c_ctx, mod_w, mod_b, norm1_w, norm2_w, w_in, rw_mu_prev, rw_mu_next, rw_w0, rw_w_up,
           rw_a0, rw_a_up, rw_g_up, rw_k_k, rw_k_a, rw_r_k, rw_ln_w, rw_ln_b, da_q_gain, da_k_gain, da_lq1,
           da_lk1, da_lq2, da_lk2, da_sub_gain, ft_w, pl_w, pl_scale, w_out, moe_router, moe_w_gate, moe_w_up,
           moe_w_down):
    depth, d = norm1_w.shape
    batch = x.shape[0]
    gw = d // N_MIXERS
    c_rows = jnp.zeros((SUBLANES, d), F32).at[:batch].set(c).at[batch].set(c_ctx)
    mod = modulation_vectors(c_rows, mod_w, mod_b)
    w_in_b = w_in.astype(BF16)
    w_out_b = w_out.astype(BF16)
    for l in range(depth):
        ctx_out = l < depth - 1
        lam_init = 0.8 - 0.6 * math.exp(-0.3 * l)
        mx = mod[l, :batch].reshape(batch, 6, 1, d)
        mc = jnp.broadcast_to(mod[l, batch].reshape(1, 6, 1, d), (batch, 6, 1, d))
        rw = (rw_mu_prev[l], rw_mu_next[l], rw_w0[l], rw_w_up[l], rw_a0[l], rw_a_up[l], rw_g_up[l],
              rw_k_k[l], rw_k_a[l], rw_r_k[l], rw_ln_w[l], rw_ln_b[l])
        da = (da_q_gain[l], da_k_gain[l], da_lq1[l], da_lk1[l], da_lq2[l], da_lk2[l], da_sub_gain[l])
        rwx, dax, ftx, plx = input_projection(x, norm1_w[l], mx[:, 1], mx[:, 0], w_in_b[l], gw)
        rwc, dac, ftc, plc = input_projection(ctx, norm1_w[l], mc[:, 1], mc[:, 0], w_in_b[l], gw)
        ax, ac = rwkv_mixer(rwx, rwc, rw, gw, ctx_out)
        bx, bc = diff_attention(dax, dac, da, gw, lam_init, ctx_out)
        fx = fourier_mix(ftx, ft_w[l])
        px = pool_mix(plx, pl_w[l], pl_scale[l])
        x = output_projection(x, mx[:, 2], (ax, bx, fx, px), w_out_b[l])
        streams = [(x, mx[:, 4], mx[:, 3], mx[:, 5])]
        if ctx_out:
            fc = fourier_mix(ftc, ft_w[l])
            pc = pool_mix(plc, pl_w[l], pl_scale[l])
            ctx = output_projection(ctx, mc[:, 2], (ac, bc, fc, pc), w_out_b[l])
            streams.append((ctx, mc[:, 4], mc[:, 3], mc[:, 5]))
        outs = moe_residual(streams, norm2_w[l], l, moe_router[l], moe_w_gate, moe_w_up, moe_w_down)
        x = outs[0]
        if ctx_out:
            ctx = outs[1]
    return x
```

```python
import functools
import math

import numpy as np
import jax
import jax.numpy as jnp
from jax import lax
from jax.experimental import pallas as pl
from jax.experimental.pallas import tpu as pltpu

F32 = jnp.float32
BF16 = jnp.bfloat16
HIGHEST = lax.Precision.HIGHEST

N_MIXERS = 4
HEAD = 64
NORM_EPS = 1e-6
RW_LN_EPS = 64e-5
GRID_W = 64
DA_QK = HEAD // 2
ROPE_BASE = 10000.0
POOL_WINDOWS = (2, 4, 8, 16)
N_EXPERTS = 16
CAPACITY_FACTOR = 2

LANES = 128
SUBLANES = 8
VMEM_LIMIT_BYTES = 56 * 1024 * 1024

RW_CHUNK = 64
RW_CHUNKS_PER_STEP = 8


def _cparams(*sem):
    return pltpu.CompilerParams(dimension_semantics=sem, vmem_limit_bytes=VMEM_LIMIT_BYTES)


def _dot(a, b):
    return jnp.dot(a.astype(BF16), b.astype(BF16), preferred_element_type=F32)


def _dot_f32(a, b):
    return jnp.dot(a, b, precision=HIGHEST, preferred_element_type=F32)


def _dot_tri(a, b):
    return _dot(a, b)


def _dot_nt(a, b, exact=False):
    dn = (((1,), (1,)), ((), ()))
    if exact:
        return lax.dot_general(a, b, dn, precision=HIGHEST, preferred_element_type=F32)
    return lax.dot_general(a.astype(BF16), b.astype(BF16), dn, preferred_element_type=F32)


def _dot_tn(a, b, exact=False):
    dn = (((0,), (0,)), ((), ()))
    if exact:
        return lax.dot_general(a, b, dn, precision=HIGHEST, preferred_element_type=F32)
    return lax.dot_general(a.astype(BF16), b.astype(BF16), dn, preferred_element_type=F32)


def _sigmoid(x):
    return 1.0 / (1.0 + jnp.exp(-x))


def _block_ones(n, blk, value=1.0):
    i = np.arange(n) // blk
    return jnp.asarray((i[:, None] == i[None, :]).astype(np.float32) * value)


def _const_spec(shape):
    nd = len(shape)
    return pl.BlockSpec(shape, lambda *_: (0,) * nd)


def _mod_kernel(c_ref, w_ref, b_ref, o_ref):
    c = c_ref[...]
    o_ref[...] = _dot_f32(c * _sigmoid(c), w_ref[...]) + b_ref[...]


def modulation_vectors(c_rows, mod_w, mod_b):
    depth, d, n = mod_w.shape
    tn = 1536
    return pl.pallas_call(
        _mod_kernel,
        grid=(depth, n // tn),
        in_specs=[
            pl.BlockSpec((SUBLANES, d), lambda l, j: (0, 0)),
            pl.BlockSpec((None, d, tn), lambda l, j: (l, 0, j)),
            pl.BlockSpec((None, 1, tn), lambda l, j: (l, 0, j)),
        ],
        out_specs=pl.BlockSpec((None, SUBLANES, tn), lambda l, j: (l, 0, j)),
        out_shape=jax.ShapeDtypeStruct((depth, SUBLANES, n), F32),
        compiler_params=_cparams("parallel", "parallel"),
    )(c_rows, mod_w, mod_b.reshape(depth, 1, n))


def _modulated_norm(x, gain, scale, shift):
    ms = jnp.mean(x * x, axis=-1, keepdims=True)
    return (x * lax.rsqrt(ms + NORM_EPS) * gain) * (1.0 + scale) + shift


def _inproj_kernel(splits, x_ref, gain_ref, sc_ref, sh_ref, w_ref, *o_refs):
    h = _modulated_norm(x_ref[...], gain_ref[...], sc_ref[...], sh_ref[...]).astype(BF16)
    for (lo, hi), o_ref in zip(splits, o_refs):
        o_ref[...] = jnp.dot(h, w_ref[:, lo:hi], preferred_element_type=F32)


def input_projection(x, gain, scale, shift, w_in_bf16, group_w):
    b, t, d = x.shape
    rw_cols = w_in_bf16.shape[1] - 3 * group_w - 2 * group_w
    cuts = [0, rw_cols, rw_cols + 3 * group_w, rw_cols + 4 * group_w, rw_cols + 5 * group_w]
    splits = tuple((cuts[i], cuts[i + 1]) for i in range(4))
    tm = min(512, t)
    row = pl.BlockSpec((None, 1, d), lambda bi, i: (bi, 0, 0))
    return pl.pallas_call(
        functools.partial(_inproj_kernel, splits),
        grid=(b, t // tm),
        in_specs=[
            pl.BlockSpec((None, tm, d), lambda bi, i: (bi, i, 0)),
            _const_spec((1, d)),
            row, row,
            _const_spec(w_in_bf16.shape),
        ],
        out_specs=[pl.BlockSpec((None, tm, hi - lo), lambda bi, i: (bi, i, 0)) for lo, hi in splits],
        out_shape=[jax.ShapeDtypeStruct((b, t, hi - lo), F32) for lo, hi in splits],
        compiler_params=_cparams("parallel", "parallel"),
    )(x, gain.reshape(1, d), scale, shift, w_in_bf16)


def _halo_specs(tm, t, width):
    nb8 = t // SUBLANES
    r8 = tm // SUBLANES
    return [
        pl.BlockSpec((None, tm, width), lambda b, i: (b, i, 0)),
        pl.BlockSpec((None, SUBLANES, width), lambda b, i: (b, jnp.maximum(i * r8 - 1, 0), 0)),
        pl.BlockSpec((None, SUBLANES, width), lambda b, i: (b, jnp.minimum((i + 1) * r8, nb8 - 1), 0)),
    ]


def _stage_with_halo(buf_ref, main_ref, prev_ref, next_ref):
    tm = main_ref.shape[0]
    i = pl.program_id(1)
    n = pl.num_programs(1)
    buf_ref[SUBLANES:SUBLANES + tm, :] = main_ref[...]
    buf_ref[0:SUBLANES, :] = jnp.where(i > 0, prev_ref[...], 0.0)
    buf_ref[SUBLANES + tm:2 * SUBLANES + tm, :] = jnp.where(i < n - 1, next_ref[...], 0.0)


def _rwkv_prep_kernel(gw, rw_ref, prev_ref, next_ref, mup_ref, mun_ref, kk_ref_, ka_ref, w0_ref, wup_ref,
                      a0_ref, aup_ref, gup_ref, hsum_ref,
                      r_o, k_o, v_o, kk_o, gate_o, lwf_o, kaf_o, kdf_o, lwb_o, kab_o, kdb_o, buf_ref):
    tm = rw_ref.shape[0]
    _stage_with_halo(buf_ref, rw_ref, prev_ref, next_ref)
    p = buf_ref[SUBLANES:SUBLANES + tm, :]
    prev = buf_ref[SUBLANES - 1:SUBLANES - 1 + tm, :]
    nxt = buf_ref[SUBLANES + 1:SUBLANES + 1 + tm, :]
    u = p + mup_ref[...] * (prev - p) + mun_ref[...] * (nxt - p)
    r = u[:, 0:gw]
    k = u[:, gw:2 * gw]
    v = u[:, 2 * gw:3 * gw]
    lora_w = u[:, 3 * gw:3 * gw + LANES]
    lora_a = u[:, 3 * gw + LANES:3 * gw + 2 * LANES]
    g = u[:, 3 * gw + 2 * LANES:3 * gw + 3 * LANES]
    kk = k * kk_ref_[...]
    ss = _dot_f32(kk * kk, hsum_ref[...])
    kk = kk * lax.rsqrt(jnp.maximum(ss, 1e-24))
    zw = _dot_f32(jnp.tanh(lora_w), wup_ref[...]) + w0_ref[...]
    za = _dot_f32(lora_a, aup_ref[...]) + a0_ref[...]
    logw = -_sigmoid(zw) * math.exp(-0.5)
    a = _sigmoid(za)
    r_o[...] = r
    k_o[...] = k
    v_o[...] = v
    kk_o[...] = kk
    gate_o[...] = _dot_f32(_sigmoid(g), gup_ref[...])
    ka = ka_ref[...]
    for d, (lw_o, kka_o, kd_o) in enumerate(((lwf_o, kaf_o, kdf_o), (lwb_o, kab_o, kdb_o))):
        a_d = a[:, d * gw:(d + 1) * gw]
        lw_o[...] = logw[:, d * gw:(d + 1) * gw]
        kka_o[...] = kk * a_d
        kd_o[...] = k * (1.0 + (a_d - 1.0) * ka)


def _blockdiag2(m):
    r, c = m.shape[1:]
    z = jnp.zeros((r, c), m.dtype)
    return jnp.concatenate([jnp.concatenate([m[0], z], 1), jnp.concatenate([z, m[1]], 1)], 0)


def rwkv_prepare(rw, params, gw):
    (mu_prev, mu_next, w0, w_up, a0, a_up, g_up, k_k, k_a, r_k, ln_w, ln_b) = params
    b, t, cols = rw.shape
    tm = min(512, t)
    row = lambda v: v.reshape(1, -1)
    small = [row(mu_prev), row(mu_next), row(k_k), row(k_a), row(w0), _blockdiag2(w_up), row(a0),
             _blockdiag2(a_up), g_up, _block_ones(gw, HEAD)]
    out = jax.ShapeDtypeStruct((b, t, gw), F32)
    return pl.pallas_call(
        functools.partial(_rwkv_prep_kernel, gw),
        grid=(b, t // tm),
        in_specs=_halo_specs(tm, t, cols) + [_const_spec(s.shape) for s in small],
        out_specs=[pl.BlockSpec((None, tm, gw), lambda bi, i: (bi, i, 0))] * 11,
        out_shape=[out] * 11,
        scratch_shapes=[pltpu.VMEM((tm + 2 * SUBLANES, cols), F32)],
        compiler_params=_cparams("parallel", "parallel"),
    )(rw, rw, rw, *small)


RW_BUILD_CHUNKS_PER_STEP = 2


def _rwkv_chunk_kernel(ncs, r_ref, v_ref, kk_ref, lwf_ref, kaf_ref, kdf_ref, lwb_ref, kab_ref, kdb_ref,
                       y0f_o, qf_o, mf_o, g0f_o, y0b_o, qb_o, mb_o, g0b_o):
    L = RW_CHUNK
    rows = lax.broadcasted_iota(jnp.int32, (L, L), 0)
    cols = lax.broadcasted_iota(jnp.int32, (L, L), 1)
    eye = jnp.where(rows == cols, 1.0, 0.0)
    rows2 = lax.broadcasted_iota(jnp.int32, (L, 2 * L), 0)
    cols2 = lax.broadcasted_iota(jnp.int32, (L, 2 * L), 1) % L
    lane = lax.broadcasted_iota(jnp.int32, (1, LANES), 1)
    head_masks = [(lane >= h * HEAD) & (lane < (h + 1) * HEAD) for h in range(2)]
    r2 = lax.broadcasted_iota(jnp.int32, (LANES, LANES), 0)
    c2 = lax.broadcasted_iota(jnp.int32, (LANES, LANES), 1)
    same_head = (r2 // HEAD) == (c2 // HEAD)
    diag = r2 == c2
    directions = ((False, lwf_ref, kaf_ref, kdf_ref, y0f_o, qf_o, mf_o, g0f_o),
                  (True, lwb_ref, kab_ref, kdb_ref, y0b_o, qb_o, mb_o, g0b_o))

    probs = []
    for c in range(ncs):
        rs = slice(c * L, (c + 1) * L)
        for pair in range(2):
            sl = slice(pair * LANES, (pair + 1) * LANES)
            r = r_ref[rs, sl]
            v = v_ref[rs, sl]
            kk = kk_ref[rs, sl]
            for reverse, lw_ref, ka_ref, kd_ref, y0_o, q_o, m_o, g0_o in directions:
                incl = (cols >= rows) if reverse else (cols <= rows)
                strict = (cols > rows) if reverse else (cols < rows)
                incl2 = (cols2 >= rows2) if reverse else (cols2 <= rows2)
                logw = lw_ref[rs, sl]
                kd = kd_ref[rs, sl]
                a = -ka_ref[rs, sl]
                cum = _dot_f32(jnp.where(incl, 1.0, 0.0), logw)
                total = jnp.sum(logw, axis=0, keepdims=True)
                g_inv = jnp.exp(-cum)
                g_tail = jnp.exp(total - cum)
                bd = kk * jnp.exp(cum - logw)
                rd = r * jnp.exp(cum)
                probs.append(dict(
                    incl2=incl2, strict=strict, v=v, bd=bd, rd=rd, total=total,
                    lhs=jnp.concatenate([bd, rd], axis=0), rhs=jnp.concatenate([a * g_inv, kd * g_inv], axis=0),
                    tails=jnp.concatenate([kd * g_tail, a * g_tail], axis=0),
                    outs=(y0_o, q_o, m_o, g0_o), rs=rs, sl=sl, c=c, pair=pair))

    heads = []
    for p in probs:
        for h in range(2):
            gram = _dot_nt(jnp.where(head_masks[h], p["lhs"], 0.0), p["rhs"])
            heads.append(dict(
                p=p, h=h,
                nil=jnp.where(p["strict"], gram[:L, :L], 0.0),
                aak=jnp.where(p["strict"], gram[:L, L:], 0.0),
                ara_ark=jnp.where(p["incl2"], gram[L:, :], 0.0)))

    for hd in heads:
        hd["acc"] = eye + hd["nil"]
        hd["pow"] = hd["nil"]
    span = 2
    while span < L:
        for hd in heads:
            hd["pow"] = _dot_tri(hd["pow"], hd["pow"])
        for hd in heads:
            hd["acc"] = hd["acc"] + _dot_tri(hd["acc"], hd["pow"])
        span *= 2

    for hd in heads:
        hd["aakv"] = _dot(hd["aak"], hd["p"]["v"])
    for hd in heads:
        wp = _dot_tri(hd["acc"], jnp.concatenate([hd["aakv"], hd["p"]["bd"]], axis=1))
        hd["w1"], hd["pm"] = wp[:, :LANES], wp[:, LANES:]
    for hd in heads:
        rhs = jnp.concatenate([jnp.concatenate([hd["w1"], hd["pm"]], axis=1),
                               jnp.concatenate([hd["p"]["v"], jnp.zeros((L, LANES), F32)], axis=1)], axis=0)
        yq = _dot(hd["ara_ark"], rhs)
        hd["y0"], hd["q"] = yq[:, :LANES], yq[:, LANES:] + hd["p"]["rd"]

    for i, p in enumerate(probs):
        h0, h1 = heads[2 * i], heads[2 * i + 1]
        pick = lambda key: jnp.where(head_masks[0], h0[key], h1[key])
        y0_o, q_o, m_o, g0_o = p["outs"]
        y0_o[p["rs"], p["sl"]] = pick("y0")
        q_o[p["rs"], p["sl"]] = pick("q")
        vw = jnp.concatenate([p["v"], pick("w1")], axis=0)
        g0_o[p["c"], p["pair"]] = jnp.where(same_head, _dot_tn(p["tails"], vw), 0.0)
        m_o[p["c"], p["pair"]] = (jnp.where(same_head, _dot_tn(p["tails"][L:], pick("pm")), 0.0)
                                  + jnp.where(diag, jnp.exp(p["total"]), 0.0))


def rwkv_chunks(r, v, kk, lwf, kaf, kdf, lwb, kab, kdb):
    b, t, gw = r.shape
    L = RW_CHUNK
    nc = t // L
    ncs = RW_BUILD_CHUNKS_PER_STEP if nc % RW_BUILD_CHUNKS_PER_STEP == 0 else 1
    tok = pl.BlockSpec((None, ncs * L, gw), lambda bi, i: (bi, i, 0))
    mat = pl.BlockSpec((None, ncs, 2, LANES, LANES), lambda bi, i: (bi, i, 0, 0, 0))
    tok_s = jax.ShapeDtypeStruct((b, t, gw), F32)
    mat_s = jax.ShapeDtypeStruct((b, nc, 2, LANES, LANES), F32)
    return pl.pallas_call(
        functools.partial(_rwkv_chunk_kernel, ncs),
        grid=(b, nc // ncs),
        in_specs=[tok] * 9,
        out_specs=[tok, tok, mat, mat] * 2,
        out_shape=[tok_s, tok_s, mat_s, mat_s] * 2,
        compiler_params=_cparams("parallel", "parallel"),
    )(r, v, kk, lwf, kaf, kdf, lwb, kab, kdb)


def _rwkv_scan_kernel(cps, h0_ref, y0f_ref, qf_ref, mf_ref, g0f_ref, y0b_ref, qb_ref, mb_ref, g0b_ref,
                      yf_o, yb_o, hfin_o, h_ref):
    L = RW_CHUNK
    j = pl.program_id(1)

    @pl.when(j == 0)
    def _():
        h_ref[...] = h0_ref[...]

    for step in range(cps):
        for d, (y0_ref, q_ref, m_ref, g0_ref, y_o) in enumerate(
                ((y0f_ref, qf_ref, mf_ref, g0f_ref, yf_o), (y0b_ref, qb_ref, mb_ref, g0b_ref, yb_o))):
            c = step if d == 0 else cps - 1 - step
            rows = slice(c * L, (c + 1) * L)
            for pair in range(2):
                sl = slice(pair * LANES, (pair + 1) * LANES)
                h = h_ref[d, pair]
                y_o[rows, sl] = y0_ref[rows, sl] + _dot_f32(q_ref[rows, sl], h)
                h_ref[d, pair] = _dot_f32(m_ref[c, pair], h) + g0_ref[c, pair]

    @pl.when(j == pl.num_programs(1) - 1)
    def _():
        hfin_o[...] = h_ref[...]


def rwkv_scan(h0, y0f, qf, mf, g0f, y0b, qb, mb, g0b):
    b, t, gw = y0f.shape
    L = RW_CHUNK
    nc = t // L
    cps = min(RW_CHUNKS_PER_STEP, nc)
    nb = nc // cps
    tm = cps * L
    tok_f = pl.BlockSpec((None, tm, gw), lambda bi, i: (bi, i, 0))
    tok_b = pl.BlockSpec((None, tm, gw), lambda bi, i: (bi, nb - 1 - i, 0))
    mat_f = pl.BlockSpec((None, cps, 2, LANES, LANES), lambda bi, i: (bi, i, 0, 0, 0))
    mat_b = pl.BlockSpec((None, cps, 2, LANES, LANES), lambda bi, i: (bi, nb - 1 - i, 0, 0, 0))
    st = pl.BlockSpec((None, 2, 2, LANES, LANES), lambda bi, i: (bi, 0, 0, 0, 0))
    tok_s = jax.ShapeDtypeStruct((b, t, gw), F32)
    return pl.pallas_call(
        functools.partial(_rwkv_scan_kernel, cps),
        grid=(b, nb),
        in_specs=[st, tok_f, tok_f, mat_f, mat_f, tok_b, tok_b, mat_b, mat_b],
        out_specs=[tok_f, tok_b, st],
        out_shape=[tok_s, tok_s, jax.ShapeDtypeStruct((b, 2, 2, LANES, LANES), F32)],
        scratch_shapes=[pltpu.VMEM((2, 2, LANES, LANES), F32)],
        compiler_params=_cparams("parallel", "arbitrary"),
    )(h0, y0f, qf, mf, g0f, y0b, qb, mb, g0b)


def _rwkv_out_kernel(yf_ref, yb_ref, r_ref, k_ref, v_ref, gate_ref, rk_ref, lnw_ref, lnb_ref, hmean_ref, o_ref):
    y = yf_ref[...] + yb_ref[...]
    hmean = hmean_ref[...]
    mu = _dot_f32(y, hmean)
    yc = y - mu
    var = _dot_f32(yc * yc, hmean)
    yn = yc * lax.rsqrt(var + RW_LN_EPS) * lnw_ref[...] + lnb_ref[...]
    bonus = _dot_f32(r_ref[...] * k_ref[...] * rk_ref[...], hmean) * float(HEAD) * v_ref[...]
    o_ref[...] = (yn + bonus) * gate_ref[...]


def rwkv_output(yf, yb, r, k, v, gate, r_k, ln_w, ln_b):
    b, t, gw = yf.shape
    tm = min(512, t)
    tok = pl.BlockSpec((None, tm, gw), lambda bi, i: (bi, i, 0))
    small = [r_k.reshape(1, gw), ln_w.reshape(1, gw), ln_b.reshape(1, gw), _block_ones(gw, HEAD, 1.0 / HEAD)]
    return pl.pallas_call(
        _rwkv_out_kernel,
        grid=(b, t // tm),
        in_specs=[tok] * 6 + [_const_spec(s.shape) for s in small],
        out_specs=tok,
        out_shape=jax.ShapeDtypeStruct((b, t, gw), F32),
        compiler_params=_cparams("parallel", "parallel"),
    )(yf, yb, r, k, v, gate, *small)


def rwkv_mixer(rwx, rwc, params, gw, ctx_out):
    r_k, ln_w, ln_b = params[9], params[10], params[11]
    sx = rwkv_prepare(rwx, params, gw)
    sc = rwkv_prepare(rwc, params, gw)
    (rx, kx, vx, kkx, gx), dx = sx[:5], sx[5:]
    (rc, kc, vc, kkc, gc), dc = sc[:5], sc[5:]
    cx = rwkv_chunks(rx, vx, kkx, *dx)
    cc = rwkv_chunks(rc, vc, kkc, *dc)
    b = rwx.shape[0]
    h0 = jnp.zeros((b, 2, 2, LANES, LANES), F32)
    ycf, ycb, h_ctx = rwkv_scan(h0, *cc)
    yxf, yxb, _ = rwkv_scan(h_ctx, *cx)
    out_x = rwkv_output(yxf, yxb, rx, kx, vx, gx, r_k, ln_w, ln_b)
    out_c = rwkv_output(ycf, ycb, rc, kc, vc, gc, r_k, ln_w, ln_b) if ctx_out else None
    return out_x, out_c


def _rope_tables(n_tokens, reps):
    rows = n_tokens // GRID_W
    row = np.repeat(np.arange(rows), GRID_W).astype(np.float64)
    col = np.tile(np.arange(GRID_W), rows).astype(np.float64)
    n_freq = DA_QK // 4
    inv = ROPE_BASE ** (-np.arange(n_freq, dtype=np.float64) / n_freq)
    ar = row[:, None] * inv
    ac = col[:, None] * inv
    ang = np.concatenate([ar, ar, ac, ac], axis=-1)
    cos = np.tile(np.cos(ang), (1, reps)).astype(np.float32)
    sin = np.tile(np.sin(ang), (1, reps)).astype(np.float32)
    return jnp.asarray(cos), jnp.asarray(sin)


def _attn_prep_kernel(gw, rope, da_ref, qg_ref, kg_ref, gmean_ref, *rest):
    if rope:
        cos_ref, sin_ref, q_o, k_o, v_o = rest
    else:
        q_o, k_o, v_o = rest
    da = da_ref[...]
    gmean = gmean_ref[...]
    lane = lax.broadcasted_iota(jnp.int32, (1, gw), 1)
    first_half = (lane % (DA_QK // 2)) < (DA_QK // 4)

    def norm_rope(x, gain):
        ms = _dot_f32(x * x, gmean)
        y = x * lax.rsqrt(ms + NORM_EPS) * gain
        if rope:
            quarter = DA_QK // 4
            rot = jnp.where(first_half, -pltpu.roll(y, gw - quarter, 1), pltpu.roll(y, quarter, 1))
            y = y * cos_ref[...] + rot * sin_ref[...]
        return y

    q_o[...] = (norm_rope(da[:, 0:gw], qg_ref[...]) * (DA_QK ** -0.5 * LOG2_E)).astype(q_o.dtype)
    k_o[...] = norm_rope(da[:, gw:2 * gw], kg_ref[...]).astype(k_o.dtype)
    v_o[...] = da[:, 2 * gw:3 * gw].astype(v_o.dtype)


def attention_prepare(da, q_gain, k_gain, gw, rope):
    b, t, cols = da.shape
    tm = min(512, t)
    reps = gw // DA_QK
    small = [jnp.tile(q_gain, reps).reshape(1, gw), jnp.tile(k_gain, reps).reshape(1, gw),
             _block_ones(gw, DA_QK, 1.0 / DA_QK)]
    args = [da] + small
    in_specs = [pl.BlockSpec((None, tm, cols), lambda bi, i: (bi, i, 0))] + [_const_spec(s.shape) for s in small]
    if rope:
        cos, sin = _rope_tables(t, reps)
        args += [cos, sin]
        in_specs += [pl.BlockSpec((tm, gw), lambda bi, i: (i, 0))] * 2
    tok = pl.BlockSpec((None, tm, gw), lambda bi, i: (bi, i, 0))
    return pl.pallas_call(
        functools.partial(_attn_prep_kernel, gw, rope),
        grid=(b, t // tm),
        in_specs=in_specs,
        out_specs=[tok] * 3,
        out_shape=[jax.ShapeDtypeStruct((b, t, gw), BF16)] * 3,
        compiler_params=_cparams("parallel", "parallel"),
    )(*args)


LOG2_E = 1.4426950408889634
FLASH_MAX_KV_BLOCK = 2816


def _flash_kernel(lam_init, bounded_ref, bound_ref, qt_ref, k_ref, vt_ref, lq1_ref, lk1_ref, lq2_ref, lk2_ref,
                  sg_ref, o_ref, m_ref, acc_ref):
    j = pl.program_id(3)

    @pl.when(j == 0)
    def _():
        m_ref[...] = jnp.full(m_ref.shape, -1e30, F32)
        acc_ref[...] = jnp.zeros(acc_ref.shape, F32)

    vt = vt_ref[...]
    bounded = bounded_ref[0] == 1

    @pl.when(bounded)
    def _():
        shift = bound_ref[0]
        for m in range(2):
            s = jnp.dot(k_ref[m], qt_ref[m], preferred_element_type=F32)
            p = jnp.exp2(s - shift).astype(BF16)
            acc_ref[m] += jnp.dot(vt, p, preferred_element_type=F32)

    @pl.when(jnp.logical_not(bounded))
    def _():
        for m in range(2):
            s = jnp.dot(k_ref[m], qt_ref[m], preferred_element_type=F32)
            m_old = m_ref[m]
            m_new = jnp.maximum(m_old, jnp.max(s, axis=0, keepdims=True))
            p = jnp.exp2(s - m_new).astype(BF16)
            acc_ref[m] = jnp.exp2(m_old - m_new) * acc_ref[m] + jnp.dot(vt, p, preferred_element_type=F32)
            m_ref[m] = m_new

    @pl.when(j == pl.num_programs(3) - 1)
    def _():
        lam = (jnp.exp(jnp.sum(lq1_ref[...] * lk1_ref[...], axis=-1, keepdims=True))
               - jnp.exp(jnp.sum(lq2_ref[...] * lk2_ref[...], axis=-1, keepdims=True)) + lam_init)
        a0 = acc_ref[0]
        a1 = acc_ref[1]
        o = a0[:HEAD] / a0[HEAD:HEAD + 1] - lam * (a1[:HEAD] / a1[HEAD:HEAD + 1])
        ms = jnp.mean(o * o, axis=0, keepdims=True)
        o_ref[...] = o * lax.rsqrt(ms + NORM_EPS) * sg_ref[...] * (1.0 - lam_init)


def _largest_divisor(n, cap, multiple):
    best = None
    for d in range(multiple, cap + 1, multiple):
        if n % d == 0:
            best = d
    return best if best is not None else n


FLASH_SAFE_SCORE_BOUND = 60.0


def _score_bound(q_gain, k_gain):
    bound = (1.02 * DA_QK * DA_QK ** -0.5 * LOG2_E) * jnp.max(jnp.abs(q_gain)) * jnp.max(jnp.abs(k_gain))
    return (bound <= FLASH_SAFE_SCORE_BOUND).astype(jnp.int32).reshape(1), bound.astype(F32).reshape(1)


def diff_attention_core(qt, k, vt, gains, lam_params, sub_gain, lam_init):
    b, h, _, dk, t = qt.shape
    s = k.shape[3]
    tq = min(512, t)
    tk = _largest_divisor(s, FLASH_MAX_KV_BLOCK, 2 * LANES)
    small = [p.reshape(1, -1) for p in lam_params] + [sub_gain.reshape(-1, 1)]
    bounded, bound = _score_bound(*gains)
    return pl.pallas_call(
        functools.partial(_flash_kernel, lam_init),
        grid_spec=pltpu.PrefetchScalarGridSpec(
            num_scalar_prefetch=2,
            grid=(b, h, t // tq, s // tk),
            in_specs=[
                pl.BlockSpec((None, None, 2, dk, tq), lambda bi, hi, i, j, *_: (bi, hi, 0, 0, i)),
                pl.BlockSpec((None, None, 2, tk, dk), lambda bi, hi, i, j, *_: (bi, hi, 0, j, 0)),
                pl.BlockSpec((None, None, LANES, tk), lambda bi, hi, i, j, *_: (bi, hi, 0, j)),
            ] + [pl.BlockSpec(x.shape, lambda bi, hi, i, j, *_: (0, 0)) for x in small],
            out_specs=pl.BlockSpec((None, None, HEAD, tq), lambda bi, hi, i, j, *_: (bi, hi, 0, i)),
            scratch_shapes=[pltpu.VMEM((2, 1, tq), F32), pltpu.VMEM((2, LANES, tq), F32)],
        ),
        out_shape=jax.ShapeDtypeStruct((b, h, HEAD, t), F32),
        compiler_params=_cparams("parallel", "parallel", "parallel", "arbitrary"),
    )(bounded, bound, qt, k, vt, *small)


def _split_heads(q, k, v):
    b, t, gw = q.shape
    h = gw // HEAD
    qth = q.reshape(b, t, h, 2, DA_QK).transpose(0, 2, 3, 4, 1)
    kh = k.reshape(b, t, h, 2, DA_QK).transpose(0, 2, 3, 1, 4)
    vth = v.reshape(b, t, h, HEAD).transpose(0, 2, 3, 1)
    ones_row = (lax.broadcasted_iota(jnp.int32, (b, h, LANES - HEAD, t), 2) == 0).astype(v.dtype)
    return qth, kh, jnp.concatenate([vth, ones_row], axis=2)


def diff_attention(dax, dac, params, gw, lam_init, ctx_out):
    q_gain, k_gain, lq1, lk1, lq2, lk2, sub_gain = params
    qtx, kx, vtx = _split_heads(*attention_prepare(dax, q_gain, k_gain, gw, rope=True))
    qtc, kc, vtc = _split_heads(*attention_prepare(dac, q_gain, k_gain, gw, rope=False))
    k = jnp.concatenate([kx, kc], axis=3)
    vt = jnp.concatenate([vtx, vtc], axis=3)
    lam_params = (lq1, lk1, lq2, lk2)
    merge = lambda o: o.transpose(0, 3, 1, 2).reshape(o.shape[0], o.shape[3], gw)
    gains = (q_gain, k_gain)
    out_x = merge(diff_attention_core(qtx, k, vt, gains, lam_params, sub_gain, lam_init))
    out_c = merge(diff_attention_core(qtc, kc, vtc, gains, lam_params, sub_gain, lam_init)) if ctx_out else None
    return out_x, out_c


FT_RADIX = 64


def _dft_cos_sin(n, scale=1.0):
    i = np.arange(n)
    ang = 2.0 * np.pi * ((i[:, None] * i[None, :]) % n) / n
    return np.cos(ang) * scale, np.sin(ang) * scale


def _channel_dft(gw, scale):
    c, s = _dft_cos_sin(HEAD, scale)
    eye = np.eye(gw // HEAD)
    return jnp.asarray(np.concatenate([np.kron(eye, c), np.kron(eye, s)], axis=0).astype(np.float32))


def _fnet_stage1_kernel(z_ref, gr_ref, gi_ref, or_ref, oi_ref):
    for j in range(z_ref.shape[1]):
        x = z_ref[:, j, :]
        or_ref[j] = _dot_f32(gr_ref[j], x)
        oi_ref[j] = _dot_f32(gi_ref[j], x)


def _matmul_f32_kernel(a_ref, b_ref, o_ref):
    o_ref[...] = _dot_f32(a_ref[...], b_ref[...])


def _channel_dft_times(chan, w_f):
    return pl.pallas_call(
        _matmul_f32_kernel,
        out_shape=jax.ShapeDtypeStruct((chan.shape[0], w_f.shape[1]), F32),
    )(chan, w_f)


def _fnet_stage2_kernel(gw, br_ref, bi_ref, rot_ref, chanw_ref, o_ref):
    n1 = br_ref.shape[0]
    p = _dot_f32(rot_ref[...], jnp.concatenate([br_ref[...], bi_ref[...]], axis=0))
    groups = br_ref.shape[1] // gw
    rows = jnp.concatenate(
        [jnp.concatenate([p[:n1, g * gw:(g + 1) * gw], p[n1:, g * gw:(g + 1) * gw]], axis=1) for g in range(groups)],
        axis=0)
    out = _dot_f32(rows, chanw_ref[...])
    for g in range(groups):
        o_ref[:, g * gw:(g + 1) * gw] = out[g * n1:(g + 1) * n1]


def fourier_mix_long(z, w_f):
    b, t, gw = z.shape
    n1 = FT_RADIX
    n2 = t // n1
    k2 = np.arange(n2)[None, :, None]
    n = np.arange(n1)[:, None, None] + n1 * np.arange(n2)[None, None, :]
    ang = 2.0 * np.pi * ((k2 * n) % t) / t
    g_r = jnp.asarray(np.cos(ang).astype(np.float32))
    g_i = jnp.asarray((-np.sin(ang)).astype(np.float32))
    j8 = SUBLANES
    br, bi = pl.pallas_call(
        _fnet_stage1_kernel,
        grid=(b, n1 // j8),
        in_specs=[
            pl.BlockSpec((None, n2, j8, gw), lambda bi_, i: (bi_, 0, i, 0)),
            pl.BlockSpec((j8, n2, n2), lambda bi_, i: (i, 0, 0)),
            pl.BlockSpec((j8, n2, n2), lambda bi_, i: (i, 0, 0)),
        ],
        out_specs=[pl.BlockSpec((None, j8, n2, gw), lambda bi_, i: (bi_, i, 0, 0))] * 2,
        out_shape=[jax.ShapeDtypeStruct((b, n1, n2, gw), F32)] * 2,
        compiler_params=_cparams("parallel", "parallel"),
    )(z.reshape(b, n2, n1, gw), g_r, g_i)
    c64, s64 = _dft_cos_sin(n1)
    rot = jnp.asarray(np.block([[c64, s64], [-s64, c64]]).astype(np.float32))
    chanw = _channel_dft_times(_channel_dft(gw, 1.0 / math.sqrt(t * HEAD)), w_f)
    cols = n2 * gw
    tc = min(2048, cols)
    out = pl.pallas_call(
        functools.partial(_fnet_stage2_kernel, gw),
        grid=(b, cols // tc),
        in_specs=[
            pl.BlockSpec((None, n1, tc), lambda bi_, i: (bi_, 0, i)),
            pl.BlockSpec((None, n1, tc), lambda bi_, i: (bi_, 0, i)),
            _const_spec(rot.shape), _const_spec(chanw.shape),
        ],
        out_specs=pl.BlockSpec((None, n1, tc), lambda bi_, i: (bi_, 0, i)),
        out_shape=jax.ShapeDtypeStruct((b, n1, cols), F32),
        compiler_params=_cparams("parallel", "parallel"),
    )(br.reshape(b, n1, cols), bi.reshape(b, n1, cols), rot, chanw)
    return out.reshape(b, t, gw)


def _fnet_dense_kernel(z_ref, ct_ref, st_ref, chanw_ref, o_ref):
    z = z_ref[...]
    pr = _dot_f32(ct_ref[...], z)
    pi = -_dot_f32(st_ref[...], z)
    o_ref[...] = _dot_f32(jnp.concatenate([pr, pi], axis=1), chanw_ref[...])


def fourier_mix_short(z, w_f):
    b, t, gw = z.shape
    ct, st = _dft_cos_sin(t)
    ct = jnp.asarray(ct.astype(np.float32))
    st = jnp.asarray(st.astype(np.float32))
    chanw = _channel_dft_times(_channel_dft(gw, 1.0 / math.sqrt(t * HEAD)), w_f)
    tok = pl.BlockSpec((None, t, gw), lambda bi: (bi, 0, 0))
    return pl.pallas_call(
        _fnet_dense_kernel,
        grid=(b,),
        in_specs=[tok, _const_spec(ct.shape), _const_spec(st.shape), _const_spec(chanw.shape)],
        out_specs=tok,
        out_shape=jax.ShapeDtypeStruct((b, t, gw), F32),
        compiler_params=_cparams("parallel"),
    )(z, ct, st, chanw)


def fourier_mix(z, w_f):
    t = z.shape[1]
    if t % (FT_RADIX * SUBLANES) == 0 and t // FT_RADIX >= LANES:
        return fourier_mix_long(z, w_f)
    return fourier_mix_short(z, w_f)


def _pool_kernel(t_total, u_ref, prev_ref, next_ref, w_ref, s_ref, o_ref, buf_ref):
    tm, gw = u_ref.shape
    _stage_with_halo(buf_ref, u_ref, prev_ref, next_ref)
    at = lambda off: buf_ref[SUBLANES + off:SUBLANES + off + tm, :]
    u = at(0)
    t = pl.program_id(1) * tm + lax.broadcasted_iota(jnp.int32, (tm, 1), 0)
    lane = lax.broadcasted_iota(jnp.int32, (1, gw), 1)
    group = lane // (gw // len(POOL_WINDOWS))
    mean = jnp.zeros((tm, gw), F32)
    run = jnp.zeros((tm, gw), F32)
    half_prev = 0
    for i, w in enumerate(POOL_WINDOWS):
        half = w // 2
        for off in range(half_prev, half):
            run = run + at(-off - 1) + at(off)
        half_prev = half
        cnt = (jnp.minimum(t + half, t_total) - jnp.maximum(t - half, 0)).astype(F32)
        mean = jnp.where(group == i, run / cnt, mean)
    o_ref[...] = _dot_f32(mean - u, w_ref[...]) * s_ref[...]


def pool_mix(u, w_p, s_p):
    b, t, gw = u.shape
    tm = min(512, t)
    nw, ch = w_p.shape[0], w_p.shape[1]
    w_bd = jnp.zeros((gw, gw), F32)
    for i in range(nw):
        w_bd = w_bd.at[i * ch:(i + 1) * ch, i * ch:(i + 1) * ch].set(w_p[i])
    return pl.pallas_call(
        functools.partial(_pool_kernel, t),
        grid=(b, t // tm),
        in_specs=_halo_specs(tm, t, gw) + [_const_spec((gw, gw)), _const_spec((1, gw))],
        out_specs=pl.BlockSpec((None, tm, gw), lambda bi, i: (bi, i, 0)),
        out_shape=jax.ShapeDtypeStruct((b, t, gw), F32),
        scratch_shapes=[pltpu.VMEM((tm + 2 * SUBLANES, gw), F32)],
        compiler_params=_cparams("parallel", "parallel"),
    )(u, u, u, w_bd, s_p.reshape(1, gw))


def _outproj_kernel(gw, x_ref, g_ref, a_ref, b_ref, f_ref, p_ref, w_ref, o_ref):
    acc = None
    for i, m_ref in enumerate((a_ref, b_ref, f_ref, p_ref)):
        part = jnp.dot(m_ref[...].astype(BF16), w_ref[i * gw:(i + 1) * gw, :], preferred_element_type=F32)
        acc = part if acc is None else acc + part
    o_ref[...] = x_ref[...] + g_ref[...] * acc


def output_projection(x, gate, mixers, w_out_bf16):
    b, t, d = x.shape
    gw = mixers[0].shape[-1]
    tm = min(512, t)
    tok = pl.BlockSpec((None, tm, gw), lambda bi, i: (bi, i, 0))
    xs = pl.BlockSpec((None, tm, d), lambda bi, i: (bi, i, 0))
    return pl.pallas_call(
        functools.partial(_outproj_kernel, gw),
        grid=(b, t // tm),
        in_specs=[xs, pl.BlockSpec((None, 1, d), lambda bi, i: (bi, 0, 0))] + [tok] * 4 + [_const_spec(w_out_bf16.shape)],
        out_specs=xs,
        out_shape=jax.ShapeDtypeStruct((b, t, d), F32),
        compiler_params=_cparams("parallel", "parallel"),
    )(x, gate, *mixers, w_out_bf16)


def _router_kernel(n_exp, x_ref, gain_ref, sc_ref, sh_ref, wr_ref, h_o, aff_o, afft_o):
    h = _modulated_norm(x_ref[...], gain_ref[...], sc_ref[...], sh_ref[...])
    h_o[...] = h
    logits = _dot_f32(h, wr_ref[...])
    lane = lax.broadcasted_iota(jnp.int32, logits.shape, 1)
    logits = jnp.where(lane < n_exp, logits, -1e30)
    e = jnp.exp(logits - jnp.max(logits, axis=-1, keepdims=True))
    aff = e / jnp.sum(e, axis=-1, keepdims=True)
    aff_o[...] = aff
    afft_o[...] = jnp.transpose(aff)[:n_exp, :]


def router(x, gain, scale, shift, w_router):
    b, t, d = x.shape
    n_exp = w_router.shape[1]
    tm = min(512, t)
    wr = jnp.zeros((d, LANES), F32).at[:, :n_exp].set(w_router)
    row = pl.BlockSpec((None, 1, d), lambda bi, i: (bi, 0, 0))
    return pl.pallas_call(
        functools.partial(_router_kernel, n_exp),
        grid=(b, t // tm),
        in_specs=[pl.BlockSpec((None, tm, d), lambda bi, i: (bi, i, 0)), _const_spec((1, d)), row, row,
                  _const_spec(wr.shape)],
        out_specs=[pl.BlockSpec((None, tm, d), lambda bi, i: (bi, i, 0)),
                   pl.BlockSpec((None, tm, LANES), lambda bi, i: (bi, i, 0)),
                   pl.BlockSpec((None, n_exp, tm), lambda bi, i: (bi, 0, i))],
        out_shape=[jax.ShapeDtypeStruct((b, t, d), F32), jax.ShapeDtypeStruct((b, t, LANES), F32),
                   jax.ShapeDtypeStruct((b, n_exp, t), F32)],
        compiler_params=_cparams("parallel", "parallel"),
    )(x, gain.reshape(1, d), scale, shift, wr)


TOPK_EXPONENT_STEPS = 7
TOPK_MANTISSA_STEPS = 44


def _row_cumsum(x_ref, o_ref, upper_ref):
    rows, t = x_ref.shape
    carry = jnp.zeros((rows, 1), F32)
    for g in range(t // LANES):
        sl = slice(g * LANES, (g + 1) * LANES)
        local = jnp.dot(x_ref[:, sl].astype(BF16), upper_ref[...], preferred_element_type=F32) + carry
        o_ref[:, sl] = local
        carry = local[:, LANES - 1:LANES]


def _topk_kernel(cap, aff_ref, upper_ref, idx_o, sel_ref, cs_ref):
    aff = aff_ref[...]
    rows, t = aff.shape
    capf = float(cap)
    count_ge = lambda thr: jnp.sum(jnp.where(aff >= thr, 1.0, 0.0), axis=-1, keepdims=True)
    hi = jnp.full((rows, 1), 2.0, F32)
    for step in reversed(range(TOPK_EXPONENT_STEPS)):
        cand = hi * (2.0 ** -(2 ** step))
        hi = jnp.where(count_ge(cand) < capf, cand, hi)
    lo = hi * 0.5
    lo = jnp.where(count_ge(lo) >= capf, lo, 0.0)

    def bisect(_, carry):
        lo, hi = carry
        mid = 0.5 * (lo + hi)
        enough = count_ge(mid) >= capf
        return jnp.where(enough, mid, lo), jnp.where(enough, hi, mid)

    lo, hi = lax.fori_loop(0, TOPK_MANTISSA_STEPS, bisect, (lo, hi))
    above = aff >= hi
    need = capf - count_ge(hi)
    sel_ref[...] = jnp.where((aff >= lo) & jnp.logical_not(above), 1.0, 0.0)
    _row_cumsum(sel_ref, cs_ref, upper_ref)
    tied_in = (sel_ref[...] > 0.5) & (cs_ref[...] <= need)
    sel_ref[...] = jnp.where(above | tied_in, 1.0, 0.0)
    _row_cumsum(sel_ref, cs_ref, upper_ref)

    ones = jnp.ones((SUBLANES, t), BF16)

    def compact(r, _):
        cs_row = cs_ref[pl.ds(r, 1), :]
        for cb in range(0, cap, LANES):
            n = min(LANES, cap - cb)
            slot = (lax.broadcasted_iota(jnp.int32, (n, 1), 0) + cb).astype(F32)
            reached = jnp.where(cs_row <= slot, 1.0, 0.0).astype(BF16)
            cnt = _dot_nt(ones, reached)
            idx_o[r, :, cb:cb + n] = cnt[0:1].astype(jnp.int32)
        return 0

    lax.fori_loop(0, rows, compact, 0)


def expert_choice_topk(aff_t, cap):
    b, n_exp, t = aff_t.shape
    rows = b * n_exp
    upper = jnp.asarray(np.triu(np.ones((LANES, LANES), np.float32))).astype(BF16)
    idx = pl.pallas_call(
        functools.partial(_topk_kernel, cap),
        grid=(1,),
        in_specs=[_const_spec((rows, t)), _const_spec(upper.shape)],
        out_specs=_const_spec((rows, 1, cap)),
        out_shape=jax.ShapeDtypeStruct((rows, 1, cap), jnp.int32),
        scratch_shapes=[pltpu.VMEM((rows, t), F32), pltpu.VMEM((rows, t), F32)],
        compiler_params=_cparams("arbitrary"),
    )(aff_t.reshape(rows, t), upper)
    return idx.reshape(rows * cap)


GATHER_UNROLL = 8
SCATTER_BATCH = 16


def _gather_kernel(cap, idx_ref, h_ref, aff_ref, xs_o, g_o, buf_ref):
    base = (pl.program_id(0) * pl.num_programs(2) + pl.program_id(2)) * cap

    def body(i, _):
        for u in range(GATHER_UNROLL):
            c = i * GATHER_UNROLL + u
            row = idx_ref[base + c]
            buf_ref[pl.ds(c, 1), :] = h_ref[pl.ds(row, 1), :]
            g_o[pl.ds(c, 1), :] = aff_ref[pl.ds(row, 1), :]
        return 0

    lax.fori_loop(0, cap // GATHER_UNROLL, body, 0)
    xs_o[...] = buf_ref[...].astype(BF16)


def gather_tokens(idx, h, aff, n_exp, cap):
    b, t, d = h.shape
    dh = d // 2
    return pl.pallas_call(
        functools.partial(_gather_kernel, cap),
        grid_spec=pltpu.PrefetchScalarGridSpec(
            num_scalar_prefetch=1,
            grid=(b, 2, n_exp),
            in_specs=[pl.BlockSpec((None, t, dh), lambda bi, hf, e, idx_: (bi, 0, hf)),
                      pl.BlockSpec((None, t, LANES), lambda bi, hf, e, idx_: (bi, 0, 0))],
            out_specs=[pl.BlockSpec((None, None, cap, dh), lambda bi, hf, e, idx_: (bi, e, 0, hf)),
                       pl.BlockSpec((None, None, None, cap, LANES), lambda bi, hf, e, idx_: (bi, e, hf, 0, 0))],
            scratch_shapes=[pltpu.VMEM((cap, dh), F32)],
        ),
        out_shape=[jax.ShapeDtypeStruct((b, n_exp, cap, d), BF16),
                   jax.ShapeDtypeStruct((b, n_exp, 2, cap, LANES), F32)],
        compiler_params=_cparams("parallel", "parallel", "arbitrary"),
    )(idx, h, aff)


def _expert_ffn_kernel(n_groups, *refs):
    xs_refs = refs[0:2 * n_groups:2]
    g_refs = refs[1:2 * n_groups:2]
    wg_ref, wu_ref, wd_ref = refs[2 * n_groups:2 * n_groups + 3]
    o_refs = refs[2 * n_groups + 3:3 * n_groups + 3]
    acc_refs = refs[3 * n_groups + 3:]
    f = pl.program_id(1)
    wg = wg_ref[...].astype(BF16)
    wu = wu_ref[...].astype(BF16)
    wd = wd_ref[...].astype(BF16)
    for xs_ref, acc_ref in zip(xs_refs, acc_refs):
        for bi in range(xs_ref.shape[0]):
            xb = xs_ref[bi]
            gate = jnp.dot(xb, wg, preferred_element_type=F32)
            up = jnp.dot(xb, wu, preferred_element_type=F32)
            hid = (gate * _sigmoid(gate) * up).astype(BF16)
            part = jnp.dot(hid, wd, preferred_element_type=F32)

            @pl.when(f == 0)
            def _():
                acc_ref[bi] = part

            @pl.when(f > 0)
            def _():
                acc_ref[bi] += part

    @pl.when(f == pl.num_programs(1) - 1)
    def _():
        e = pl.program_id(0)
        for g_ref, acc_ref, o_ref in zip(g_refs, acc_refs, o_refs):
            lane = lax.broadcasted_iota(jnp.int32, g_ref.shape, 2)
            g = jnp.sum(jnp.where(lane == e, g_ref[...], 0.0), axis=-1, keepdims=True)
            o_ref[...] = acc_ref[...] * g


def expert_ffn(groups, layer, w_gate, w_up, w_down):
    _, n_exp, d, f_dim = w_gate.shape
    tf = _largest_divisor(f_dim, 256, LANES)
    args, in_specs, out_specs, out_shapes, scratch = [], [], [], [], []
    for xs, g_rows in groups:
        b, _, cap, _ = xs.shape
        args += [xs, g_rows]
        in_specs += [pl.BlockSpec((b, None, cap, d), lambda e, f: (0, e, 0, 0)),
                     pl.BlockSpec((b, None, None, cap, LANES), lambda e, f: (0, e, 0, 0, 0))]
        out_specs.append(pl.BlockSpec((None, b, cap, d), lambda e, f: (e, 0, 0, 0)))
        out_shapes.append(jax.ShapeDtypeStruct((n_exp, b, cap, d), F32))
        scratch.append(pltpu.VMEM((b, cap, d), F32))
    in_specs += [pl.BlockSpec((None, None, d, tf), lambda e, f: (layer, e, 0, f)),
                 pl.BlockSpec((None, None, d, tf), lambda e, f: (layer, e, 0, f)),
                 pl.BlockSpec((None, None, tf, d), lambda e, f: (layer, e, f, 0))]
    return pl.pallas_call(
        functools.partial(_expert_ffn_kernel, len(groups)),
        grid=(n_exp, f_dim // tf),
        in_specs=in_specs,
        out_specs=out_specs,
        out_shape=out_shapes,
        scratch_shapes=scratch,
        compiler_params=_cparams("parallel", "arbitrary"),
    )(*args, w_gate, w_up, w_down)


def _scatter_kernel(cap, idx_ref, eo_ref, x_ref, g_ref, o_ref):
    e = pl.program_id(2)
    base = (pl.program_id(0) * pl.num_programs(2) + e) * cap

    @pl.when(e == 0)
    def _():
        o_ref[...] = jnp.zeros(o_ref.shape, F32)

    def body(i, _):
        c0 = i * SCATTER_BATCH
        rows = [idx_ref[base + c0 + u] for u in range(SCATTER_BATCH)]
        sums = [o_ref[pl.ds(rows[u], 1), :] + eo_ref[pl.ds(c0 + u, 1), :] for u in range(SCATTER_BATCH)]
        for u in range(SCATTER_BATCH):
            o_ref[pl.ds(rows[u], 1), :] = sums[u]
        return 0

    lax.fori_loop(0, cap // SCATTER_BATCH, body, 0)

    @pl.when(e == pl.num_programs(2) - 1)
    def _():
        o_ref[...] = x_ref[...] + g_ref[...] * o_ref[...]


def scatter_residual(idx, expert_out, x, gate, cap):
    n_exp, b, _, d = expert_out.shape
    t = x.shape[1]
    dq = d // 4
    return pl.pallas_call(
        functools.partial(_scatter_kernel, cap),
        grid_spec=pltpu.PrefetchScalarGridSpec(
            num_scalar_prefetch=1,
            grid=(b, 4, n_exp),
            in_specs=[pl.BlockSpec((None, None, cap, dq), lambda bi, q, e, idx_: (e, bi, 0, q)),
                      pl.BlockSpec((None, t, dq), lambda bi, q, e, idx_: (bi, 0, q)),
                      pl.BlockSpec((None, 1, dq), lambda bi, q, e, idx_: (bi, 0, q))],
            out_specs=pl.BlockSpec((None, t, dq), lambda bi, q, e, idx_: (bi, 0, q)),
        ),
        out_shape=jax.ShapeDtypeStruct((b, t, d), F32),
        compiler_params=_cparams("parallel", "parallel", "arbitrary"),
    )(idx, expert_out, x, gate)


def moe_residual(streams, gain, layer, w_router, w_gate, w_up, w_down):
    n_exp = w_router.shape[1]
    routed = []
    for x, scale, shift, _ in streams:
        cap = CAPACITY_FACTOR * x.shape[1] // n_exp
        h, aff, aff_t = router(x, gain, scale, shift, w_router)
        idx = expert_choice_topk(aff_t, cap)
        routed.append((idx, cap, gather_tokens(idx, h, aff, n_exp, cap)))
    outs = expert_ffn([g for _, _, g in routed], layer, w_gate, w_up, w_down)
    return [scatter_residual(idx, eo, x, gate, cap)
            for (idx, cap, _), eo, (x, _, _, gate) in zip(routed, outs, streams)]


def kernel(x, c, ctx, c_ctx, mod_w, mod_b, norm1_w, norm2_w, w_in, rw_mu_prev, rw_mu_next, rw_w0, rw_w_up,
           rw_a0, rw_a_up, rw_g_up, rw_k_k, rw_k_a, rw_r_k, rw_ln_w, rw_ln_b, da_q_gain, da_k_gain, da_lq1,
           da_lk1, da_lq2, da_lk2, da_sub_gain, ft_w, pl_w, pl_scale, w_out, moe_router, moe_w_gate, moe_w_up,
           moe_w_down):
    depth, d = norm1_w.shape
    batch = x.shape[0]
    gw = d // N_MIXERS
    c_rows = jnp.zeros((SUBLANES, d), F32).at[:batch].set(c).at[batch].set(c_ctx)
    mod = modulation_vectors(c_rows, mod_w, mod_b)
    w_in_b = w_in.astype(BF16)
    w_out_b = w_out.astype(BF16)
    for l in range(depth):
        ctx_out = l < depth - 1
        lam_init = 0.8 - 0.6 * math.exp(-0.3 * l)
        mx = mod[l, :batch].reshape(batch, 6, 1, d)
        mc = jnp.broadcast_to(mod[l, batch].reshape(1, 6, 1, d), (batch, 6, 1, d))
        rw = (rw_mu_prev[l], rw_mu_next[l], rw_w0[l], rw_w_up[l], rw_a0[l], rw_a_up[l], rw_g_up[l],
              rw_k_k[l], rw_k_a[l], rw_r_k[l], rw_ln_w[l], rw_ln_b[l])
        da = (da_q_gain[l], da_k_gain[l], da_lq1[l], da_lk1[l], da_lq2[l], da_lk2[l], da_sub_gain[l])
        rwx, dax, ftx, plx = input_projection(x, norm1_w[l], mx[:, 1], mx[:, 0], w_in_b[l], gw)
        rwc, dac, ftc, plc = input_projection(ctx, norm1_w[l], mc[:, 1], mc[:, 0], w_in_b[l], gw)
        ax, ac = rwkv_mixer(rwx, rwc, rw, gw, ctx_out)
        bx, bc = diff_attention(dax, dac, da, gw, lam_init, ctx_out)
        fx = fourier_mix(ftx, ft_w[l])
        px = pool_mix(plx, pl_w[l], pl_scale[l])
        x = output_projection(x, mx[:, 2], (ax, bx, fx, px), w_out_b[l])
        streams = [(x, mx[:, 4], mx[:, 3], mx[:, 5])]
        if ctx_out:
            fc = fourier_mix(ftc, ft_w[l])
            pc = pool_mix(plc, pl_w[l], pl_scale[l])
            ctx = output_projection(ctx, mc[:, 2], (ac, bc, fc, pc), w_out_b[l])
            streams.append((ctx, mc[:, 4], mc[:, 3], mc[:, 5]))
        outs = moe_residual(streams, norm2_w[l], l, moe_router[l], moe_w_gate, moe_w_up, moe_w_down)
        x = outs[0]
        if ctx_out:
            ctx = outs[1]
    return x
```

```python
import functools
import math

import numpy as np
import jax
import jax.numpy as jnp
from jax import lax
from jax.experimental import pallas as pl
from jax.experimental.pallas import tpu as pltpu

F32 = jnp.float32
BF16 = jnp.bfloat16
HIGHEST = lax.Precision.HIGHEST

N_MIXERS = 4
HEAD = 64
NORM_EPS = 1e-6
RW_LN_EPS = 64e-5
GRID_W = 64
DA_QK = HEAD // 2
ROPE_BASE = 10000.0
POOL_WINDOWS = (2, 4, 8, 16)
N_EXPERTS = 16
CAPACITY_FACTOR = 2

LANES = 128
SUBLANES = 8
VMEM_LIMIT_BYTES = 56 * 1024 * 1024

RW_CHUNK = 64
RW_CHUNKS_PER_STEP = 8


def _cparams(*sem):
    return pltpu.CompilerParams(dimension_semantics=sem, vmem_limit_bytes=VMEM_LIMIT_BYTES)


def _dot(a, b):
    return jnp.dot(a.astype(BF16), b.astype(BF16), preferred_element_type=F32)


def _dot_f32(a, b):
    return jnp.dot(a, b, precision=HIGHEST, preferred_element_type=F32)


def _dot_tri(a, b):
    return _dot(a, b)


def _dot_nt(a, b, exact=False):
    dn = (((1,), (1,)), ((), ()))
    if exact:
        return lax.dot_general(a, b, dn, precision=HIGHEST, preferred_element_type=F32)
    return lax.dot_general(a.astype(BF16), b.astype(BF16), dn, preferred_element_type=F32)


def _dot_tn(a, b, exact=False):
    dn = (((0,), (0,)), ((), ()))
    if exact:
        return lax.dot_general(a, b, dn, precision=HIGHEST, preferred_element_type=F32)
    return lax.dot_general(a.astype(BF16), b.astype(BF16), dn, preferred_element_type=F32)


def _sigmoid(x):
    return 1.0 / (1.0 + jnp.exp(-x))


def _block_ones(n, blk, value=1.0):
    i = np.arange(n) // blk
    return jnp.asarray((i[:, None] == i[None, :]).astype(np.float32) * value)


def _const_spec(shape):
    nd = len(shape)
    return pl.BlockSpec(shape, lambda *_: (0,) * nd)


def _mod_kernel(c_ref, w_ref, b_ref, o_ref):
    c = c_ref[...]
    o_ref[...] = _dot_f32(c * _sigmoid(c), w_ref[...]) + b_ref[...]


def modulation_vectors(c_rows, mod_w, mod_b):
    depth, d, n = mod_w.shape
    tn = 1536
    return pl.pallas_call(
        _mod_kernel,
        grid=(depth, n // tn),
        in_specs=[
            pl.BlockSpec((SUBLANES, d), lambda l, j: (0, 0)),
            pl.BlockSpec((None, d, tn), lambda l, j: (l, 0, j)),
            pl.BlockSpec((None, 1, tn), lambda l, j: (l, 0, j)),
        ],
        out_specs=pl.BlockSpec((None, SUBLANES, tn), lambda l, j: (l, 0, j)),
        out_shape=jax.ShapeDtypeStruct((depth, SUBLANES, n), F32),
        compiler_params=_cparams("parallel", "parallel"),
    )(c_rows, mod_w, mod_b.reshape(depth, 1, n))


def _modulated_norm(x, gain, scale, shift):
    ms = jnp.mean(x * x, axis=-1, keepdims=True)
    return (x * lax.rsqrt(ms + NORM_EPS) * gain) * (1.0 + scale) + shift


def _inproj_kernel(splits, x_ref, gain_ref, sc_ref, sh_ref, w_ref, *o_refs):
    h = _modulated_norm(x_ref[...], gain_ref[...], sc_ref[...], sh_ref[...]).astype(BF16)
    for (lo, hi), o_ref in zip(splits, o_refs):
        o_ref[...] = jnp.dot(h, w_ref[:, lo:hi], preferred_element_type=F32)


def input_projection(x, gain, scale, shift, w_in_bf16, group_w):
    b, t, d = x.shape
    rw_cols = w_in_bf16.shape[1] - 3 * group_w - 2 * group_w
    cuts = [0, rw_cols, rw_cols + 3 * group_w, rw_cols + 4 * group_w, rw_cols + 5 * group_w]
    splits = tuple((cuts[i], cuts[i + 1]) for i in range(4))
    tm = min(512, t)
    row = pl.BlockSpec((None, 1, d), lambda bi, i: (bi, 0, 0))
    return pl.pallas_call(
        functools.partial(_inproj_kernel, splits),
        grid=(b, t // tm),
        in_specs=[
            pl.BlockSpec((None, tm, d), lambda bi, i: (bi, i, 0)),
            _const_spec((1, d)),
            row, row,
            _const_spec(w_in_bf16.shape),
        ],
        out_specs=[pl.BlockSpec((None, tm, hi - lo), lambda bi, i: (bi, i, 0)) for lo, hi in splits],
        out_shape=[jax.ShapeDtypeStruct((b, t, hi - lo), F32) for lo, hi in splits],
        compiler_params=_cparams("parallel", "parallel"),
    )(x, gain.reshape(1, d), scale, shift, w_in_bf16)


def _halo_specs(tm, t, width):
    nb8 = t // SUBLANES
    r8 = tm // SUBLANES
    return [
        pl.BlockSpec((None, tm, width), lambda b, i: (b, i, 0)),
        pl.BlockSpec((None, SUBLANES, width), lambda b, i: (b, jnp.maximum(i * r8 - 1, 0), 0)),
        pl.BlockSpec((None, SUBLANES, width), lambda b, i: (b, jnp.minimum((i + 1) * r8, nb8 - 1), 0)),
    ]


def _stage_with_halo(buf_ref, main_ref, prev_ref, next_ref):
    tm = main_ref.shape[0]
    i = pl.program_id(1)
    n = pl.num_programs(1)
    buf_ref[SUBLANES:SUBLANES + tm, :] = main_ref[...]
    buf_ref[0:SUBLANES, :] = jnp.where(i > 0, prev_ref[...], 0.0)
    buf_ref[SUBLANES + tm:2 * SUBLANES + tm, :] = jnp.where(i < n - 1, next_ref[...], 0.0)


def _rwkv_prep_kernel(gw, rw_ref, prev_ref, next_ref, mup_ref, mun_ref, kk_ref_, ka_ref, w0_ref, wup_ref,
                      a0_ref, aup_ref, gup_ref, hsum_ref,
                      r_o, k_o, v_o, kk_o, gate_o, lwf_o, kaf_o, kdf_o, lwb_o, kab_o, kdb_o, buf_ref):
    tm = rw_ref.shape[0]
    _stage_with_halo(buf_ref, rw_ref, prev_ref, next_ref)
    p = buf_ref[SUBLANES:SUBLANES + tm, :]
    prev = buf_ref[SUBLANES - 1:SUBLANES - 1 + tm, :]
    nxt = buf_ref[SUBLANES + 1:SUBLANES + 1 + tm, :]
    u = p + mup_ref[...] * (prev - p) + mun_ref[...] * (nxt - p)
    r = u[:, 0:gw]
    k = u[:, gw:2 * gw]
    v = u[:, 2 * gw:3 * gw]
    lora_w = u[:, 3 * gw:3 * gw + LANES]
    lora_a = u[:, 3 * gw + LANES:3 * gw + 2 * LANES]
    g = u[:, 3 * gw + 2 * LANES:3 * gw + 3 * LANES]
    kk = k * kk_ref_[...]
    ss = _dot_f32(kk * kk, hsum_ref[...])
    kk = kk * lax.rsqrt(jnp.maximum(ss, 1e-24))
    zw = _dot_f32(jnp.tanh(lora_w), wup_ref[...]) + w0_ref[...]
    za = _dot_f32(lora_a, aup_ref[...]) + a0_ref[...]
    logw = -_sigmoid(zw) * math.exp(-0.5)
    a = _sigmoid(za)
    r_o[...] = r
    k_o[...] = k
    v_o[...] = v
    kk_o[...] = kk
    gate_o[...] = _dot_f32(_sigmoid(g), gup_ref[...])
    ka = ka_ref[...]
    for d, (lw_o, kka_o, kd_o) in enumerate(((lwf_o, kaf_o, kdf_o), (lwb_o, kab_o, kdb_o))):
        a_d = a[:, d * gw:(d + 1) * gw]
        lw_o[...] = logw[:, d * gw:(d + 1) * gw]
        kka_o[...] = kk * a_d
        kd_o[...] = k * (1.0 + (a_d - 1.0) * ka)


def _blockdiag2(m):
    r, c = m.shape[1:]
    z = jnp.zeros((r, c), m.dtype)
    return jnp.concatenate([jnp.concatenate([m[0], z], 1), jnp.concatenate([z, m[1]], 1)], 0)


def rwkv_prepare(rw, params, gw):
    (mu_prev, mu_next, w0, w_up, a0, a_up, g_up, k_k, k_a, r_k, ln_w, ln_b) = params
    b, t, cols = rw.shape
    tm = min(512, t)
    row = lambda v: v.reshape(1, -1)
    small = [row(mu_prev), row(mu_next), row(k_k), row(k_a), row(w0), _blockdiag2(w_up), row(a0),
             _blockdiag2(a_up), g_up, _block_ones(gw, HEAD)]
    out = jax.ShapeDtypeStruct((b, t, gw), F32)
    return pl.pallas_call(
        functools.partial(_rwkv_prep_kernel, gw),
        grid=(b, t // tm),
        in_specs=_halo_specs(tm, t, cols) + [_const_spec(s.shape) for s in small],
        out_specs=[pl.BlockSpec((None, tm, gw), lambda bi, i: (bi, i, 0))] * 11,
        out_shape=[out] * 11,
        scratch_shapes=[pltpu.VMEM((tm + 2 * SUBLANES, cols), F32)],
        compiler_params=_cparams("parallel", "parallel"),
    )(rw, rw, rw, *small)


RW_BUILD_CHUNKS_PER_STEP = 2


def _rwkv_chunk_kernel(ncs, r_ref, v_ref, kk_ref, lwf_ref, kaf_ref, kdf_ref, lwb_ref, kab_ref, kdb_ref,
                       y0f_o, qf_o, mf_o, g0f_o, y0b_o, qb_o, mb_o, g0b_o):
    L = RW_CHUNK
    rows = lax.broadcasted_iota(jnp.int32, (L, L), 0)
    cols = lax.broadcasted_iota(jnp.int32, (L, L), 1)
    eye = jnp.where(rows == cols, 1.0, 0.0)
    rows2 = lax.broadcasted_iota(jnp.int32, (L, 2 * L), 0)
    cols2 = lax.broadcasted_iota(jnp.int32, (L, 2 * L), 1) % L
    lane = lax.broadcasted_iota(jnp.int32, (1, LANES), 1)
    head_masks = [(lane >= h * HEAD) & (lane < (h + 1) * HEAD) for h in range(2)]
    r2 = lax.broadcasted_iota(jnp.int32, (LANES, LANES), 0)
    c2 = lax.broadcasted_iota(jnp.int32, (LANES, LANES), 1)
    same_head = (r2 // HEAD) == (c2 // HEAD)
    diag = r2 == c2
    directions = ((False, lwf_ref, kaf_ref, kdf_ref, y0f_o, qf_o, mf_o, g0f_o),
                  (True, lwb_ref, kab_ref, kdb_ref, y0b_o, qb_o, mb_o, g0b_o))

    probs = []
    for c in range(ncs):
        rs = slice(c * L, (c + 1) * L)
        for pair in range(2):
            sl = slice(pair * LANES, (pair + 1) * LANES)
            r = r_ref[rs, sl]
            v = v_ref[rs, sl]
            kk = kk_ref[rs, sl]
            for reverse, lw_ref, ka_ref, kd_ref, y0_o, q_o, m_o, g0_o in directions:
                incl = (cols >= rows) if reverse else (cols <= rows)
                strict = (cols > rows) if reverse else (cols < rows)
                incl2 = (cols2 >= rows2) if reverse else (cols2 <= rows2)
                logw = lw_ref[rs, sl]
                kd = kd_ref[rs, sl]
                a = -ka_ref[rs, sl]
                cum = _dot_f32(jnp.where(incl, 1.0, 0.0), logw)
                total = jnp.sum(logw, axis=0, keepdims=True)
                g_inv = jnp.exp(-cum)
                g_tail = jnp.exp(total - cum)
                bd = kk * jnp.exp(cum - logw)
                rd = r * jnp.exp(cum)
                probs.append(dict(
                    incl2=incl2, strict=strict, v=v, bd=bd, rd=rd, total=total,
                    lhs=jnp.concatenate([bd, rd], axis=0), rhs=jnp.concatenate([a * g_inv, kd * g_inv], axis=0),
                    tails=jnp.concatenate([kd * g_tail, a * g_tail], axis=0),
                    outs=(y0_o, q_o, m_o, g0_o), rs=rs, sl=sl, c=c, pair=pair))

    heads = []
    for p in probs:
        for h in range(2):
            gram = _dot_nt(jnp.where(head_masks[h], p["lhs"], 0.0), p["rhs"])
            heads.append(dict(
                p=p, h=h,
                nil=jnp.where(p["strict"], gram[:L, :L], 0.0),
                aak=jnp.where(p["strict"], gram[:L, L:], 0.0),
                ara_ark=jnp.where(p["incl2"], gram[L:, :], 0.0)))

    for hd in heads:
        hd["acc"] = eye + hd["nil"]
        hd["pow"] = hd["nil"]
    span = 2
    while span < L:
        for hd in heads:
            hd["pow"] = _dot_tri(hd["pow"], hd["pow"])
        for hd in heads:
            hd["acc"] = hd["acc"] + _dot_tri(hd["acc"], hd["pow"])
        span *= 2

    for hd in heads:
        hd["aakv"] = _dot(hd["aak"], hd["p"]["v"])
    for hd in heads:
        wp = _dot_tri(hd["acc"], jnp.concatenate([hd["aakv"], hd["p"]["bd"]], axis=1))
        hd["w1"], hd["pm"] = wp[:, :LANES], wp[:, LANES:]
    for hd in heads:
        rhs = jnp.concatenate([jnp.concatenate([hd["w1"], hd["pm"]], axis=1),
                               jnp.concatenate([hd["p"]["v"], jnp.zeros((L, LANES), F32)], axis=1)], axis=0)
        yq = _dot(hd["ara_ark"], rhs)
        hd["y0"], hd["q"] = yq[:, :LANES], yq[:, LANES:] + hd["p"]["rd"]

    for i, p in enumerate(probs):
        h0, h1 = heads[2 * i], heads[2 * i + 1]
        pick = lambda key: jnp.where(head_masks[0], h0[key], h1[key])
        y0_o, q_o, m_o, g0_o = p["outs"]
        y0_o[p["rs"], p["sl"]] = pick("y0")
        q_o[p["rs"], p["sl"]] = pick("q")
        vw = jnp.concatenate([p["v"], pick("w1")], axis=0)
        g0_o[p["c"], p["pair"]] = jnp.where(same_head, _dot_tn(p["tails"], vw), 0.0)
        m_o[p["c"], p["pair"]] = (jnp.where(same_head, _dot_tn(p["tails"][L:], pick("pm")), 0.0)
                                  + jnp.where(diag, jnp.exp(p["total"]), 0.0))


def rwkv_chunks(r, v, kk, lwf, kaf, kdf, lwb, kab, kdb):
    b, t, gw = r.shape
    L = RW_CHUNK
    nc = t // L
    ncs = RW_BUILD_CHUNKS_PER_STEP if nc % RW_BUILD_CHUNKS_PER_STEP == 0 else 1
    tok = pl.BlockSpec((None, ncs * L, gw), lambda bi, i: (bi, i, 0))
    mat = pl.BlockSpec((None, ncs, 2, LANES, LANES), lambda bi, i: (bi, i, 0, 0, 0))
    tok_s = jax.ShapeDtypeStruct((b, t, gw), F32)
    mat_s = jax.ShapeDtypeStruct((b, nc, 2, LANES, LANES), F32)
    return pl.pallas_call(
        functools.partial(_rwkv_chunk_kernel, ncs),
        grid=(b, nc // ncs),
        in_specs=[tok] * 9,
        out_specs=[tok, tok, mat, mat] * 2,
        out_shape=[tok_s, tok_s, mat_s, mat_s] * 2,
        compiler_params=_cparams("parallel", "parallel"),
    )(r, v, kk, lwf, kaf, kdf, lwb, kab, kdb)


def _rwkv_scan_kernel(cps, h0_ref, y0f_ref, qf_ref, mf_ref, g0f_ref, y0b_ref, qb_ref, mb_ref, g0b_ref,
                      yf_o, yb_o, hfin_o, h_ref):
    L = RW_CHUNK
    j = pl.program_id(1)

    @pl.when(j == 0)
    def _():
        h_ref[...] = h0_ref[...]

    for step in range(cps):
        for d, (y0_ref, q_ref, m_ref, g0_ref, y_o) in enumerate(
                ((y0f_ref, qf_ref, mf_ref, g0f_ref, yf_o), (y0b_ref, qb_ref, mb_ref, g0b_ref, yb_o))):
            c = step if d == 0 else cps - 1 - step
            rows = slice(c * L, (c + 1) * L)
            for pair in range(2):
                sl = slice(pair * LANES, (pair + 1) * LANES)
                h = h_ref[d, pair]
                y_o[rows, sl] = y0_ref[rows, sl] + _dot_f32(q_ref[rows, sl], h)
                h_ref[d, pair] = _dot_f32(m_ref[c, pair], h) + g0_ref[c, pair]

    @pl.when(j == pl.num_programs(1) - 1)
    def _():
        hfin_o[...] = h_ref[...]


def rwkv_scan(h0, y0f, qf, mf, g0f, y0b, qb, mb, g0b):
    b, t, gw = y0f.shape
    L = RW_CHUNK
    nc = t // L
    cps = min(RW_CHUNKS_PER_STEP, nc)
    nb = nc // cps
    tm = cps * L
    tok_f = pl.BlockSpec((None, tm, gw), lambda bi, i: (bi, i, 0))
    tok_b = pl.BlockSpec((None, tm, gw), lambda bi, i: (bi, nb - 1 - i, 0))
    mat_f = pl.BlockSpec((None, cps, 2, LANES, LANES), lambda bi, i: (bi, i, 0, 0, 0))
    mat_b = pl.BlockSpec((None, cps, 2, LANES, LANES), lambda bi, i: (bi, nb - 1 - i, 0, 0, 0))
    st = pl.BlockSpec((None, 2, 2, LANES, LANES), lambda bi, i: (bi, 0, 0, 0, 0))
    tok_s = jax.ShapeDtypeStruct((b, t, gw), F32)
    return pl.pallas_call(
        functools.partial(_rwkv_scan_kernel, cps),
        grid=(b, nb),
        in_specs=[st, tok_f, tok_f, mat_f, mat_f, tok_b, tok_b, mat_b, mat_b],
        out_specs=[tok_f, tok_b, st],
        out_shape=[tok_s, tok_s, jax.ShapeDtypeStruct((b, 2, 2, LANES, LANES), F32)],
        scratch_shapes=[pltpu.VMEM((2, 2, LANES, LANES), F32)],
        compiler_params=_cparams("parallel", "arbitrary"),
    )(h0, y0f, qf, mf, g0f, y0b, qb, mb, g0b)


def _rwkv_out_kernel(yf_ref, yb_ref, r_ref, k_ref, v_ref, gate_ref, rk_ref, lnw_ref, lnb_ref, hmean_ref, o_ref):
    y = yf_ref[...] + yb_ref[...]
    hmean = hmean_ref[...]
    mu = _dot_f32(y, hmean)
    yc = y - mu
    var = _dot_f32(yc * yc, hmean)
    yn = yc * lax.rsqrt(var + RW_LN_EPS) * lnw_ref[...] + lnb_ref[...]
    bonus = _dot_f32(r_ref[...] * k_ref[...] * rk_ref[...], hmean) * float(HEAD) * v_ref[...]
    o_ref[...] = (yn + bonus) * gate_ref[...]


def rwkv_output(yf, yb, r, k, v, gate, r_k, ln_w, ln_b):
    b, t, gw = yf.shape
    tm = min(512, t)
    tok = pl.BlockSpec((None, tm, gw), lambda bi, i: (bi, i, 0))
    small = [r_k.reshape(1, gw), ln_w.reshape(1, gw), ln_b.reshape(1, gw), _block_ones(gw, HEAD, 1.0 / HEAD)]
    return pl.pallas_call(
        _rwkv_out_kernel,
        grid=(b, t // tm),
        in_specs=[tok] * 6 + [_const_spec(s.shape) for s in small],
        out_specs=tok,
        out_shape=jax.ShapeDtypeStruct((b, t, gw), F32),
        compiler_params=_cparams("parallel", "parallel"),
    )(yf, yb, r, k, v, gate, *small)


def rwkv_mixer(rwx, rwc, params, gw, ctx_out):
    r_k, ln_w, ln_b = params[9], params[10], params[11]
    sx = rwkv_prepare(rwx, params, gw)
    sc = rwkv_prepare(rwc, params, gw)
    (rx, kx, vx, kkx, gx), dx = sx[:5], sx[5:]
    (rc, kc, vc, kkc, gc), dc = sc[:5], sc[5:]
    cx = rwkv_chunks(rx, vx, kkx, *dx)
    cc = rwkv_chunks(rc, vc, kkc, *dc)
    b = rwx.shape[0]
    h0 = jnp.zeros((b, 2, 2, LANES, LANES), F32)
    ycf, ycb, h_ctx = rwkv_scan(h0, *cc)
    yxf, yxb, _ = rwkv_scan(h_ctx, *cx)
    out_x = rwkv_output(yxf, yxb, rx, kx, vx, gx, r_k, ln_w, ln_b)
    out_c = rwkv_output(ycf, ycb, rc, kc, vc, gc, r_k, ln_w, ln_b) if ctx_out else None
    return out_x, out_c


def _rope_tables(n_tokens, reps):
    rows = n_tokens // GRID_W
    row = np.repeat(np.arange(rows), GRID_W).astype(np.float64)
    col = np.tile(np.arange(GRID_W), rows).astype(np.float64)
    n_freq = DA_QK // 4
    inv = ROPE_BASE ** (-np.arange(n_freq, dtype=np.float64) / n_freq)
    ar = row[:, None] * inv
    ac = col[:, None] * inv
    ang = np.concatenate([ar, ar, ac, ac], axis=-1)
    cos = np.tile(np.cos(ang), (1, reps)).astype(np.float32)
    sin = np.tile(np.sin(ang), (1, reps)).astype(np.float32)
    return jnp.asarray(cos), jnp.asarray(sin)


def _attn_prep_kernel(gw, rope, da_ref, qg_ref, kg_ref, gmean_ref, *rest):
    if rope:
        cos_ref, sin_ref, q_o, k_o, v_o = rest
    else:
        q_o, k_o, v_o = rest
    da = da_ref[...]
    gmean = gmean_ref[...]
    lane = lax.broadcasted_iota(jnp.int32, (1, gw), 1)
    first_half = (lane % (DA_QK // 2)) < (DA_QK // 4)

    def norm_rope(x, gain):
        ms = _dot_f32(x * x, gmean)
        y = x * lax.rsqrt(ms + NORM_EPS) * gain
        if rope:
            quarter = DA_QK // 4
            rot = jnp.where(first_half, -pltpu.roll(y, gw - quarter, 1), pltpu.roll(y, quarter, 1))
            y = y * cos_ref[...] + rot * sin_ref[...]
        return y

    q_o[...] = (norm_rope(da[:, 0:gw], qg_ref[...]) * (DA_QK ** -0.5 * LOG2_E)).astype(q_o.dtype)
    k_o[...] = norm_rope(da[:, gw:2 * gw], kg_ref[...]).astype(k_o.dtype)
    v_o[...] = da[:, 2 * gw:3 * gw].astype(v_o.dtype)


def attention_prepare(da, q_gain, k_gain, gw, rope):
    b, t, cols = da.shape
    tm = min(512, t)
    reps = gw // DA_QK
    small = [jnp.tile(q_gain, reps).reshape(1, gw), jnp.tile(k_gain, reps).reshape(1, gw),
             _block_ones(gw, DA_QK, 1.0 / DA_QK)]
    args = [da] + small
    in_specs = [pl.BlockSpec((None, tm, cols), lambda bi, i: (bi, i, 0))] + [_const_spec(s.shape) for s in small]
    if rope:
        cos, sin = _rope_tables(t, reps)
        args += [cos, sin]
        in_specs += [pl.BlockSpec((tm, gw), lambda bi, i: (i, 0))] * 2
    tok = pl.BlockSpec((None, tm, gw), lambda bi, i: (bi, i, 0))
    return pl.pallas_call(
        functools.partial(_attn_prep_kernel, gw, rope),
        grid=(b, t // tm),
        in_specs=in_specs,
        out_specs=[tok] * 3,
        out_shape=[jax.ShapeDtypeStruct((b, t, gw), BF16)] * 3,
        compiler_params=_cparams("parallel", "parallel"),
    )(*args)


LOG2_E = 1.4426950408889634
FLASH_MAX_KV_BLOCK = 2816


def _flash_kernel(lam_init, bounded_ref, bound_ref, qt_ref, k_ref, vt_ref, lq1_ref, lk1_ref, lq2_ref, lk2_ref,
                  sg_ref, o_ref, m_ref, acc_ref):
    j = pl.program_id(3)

    @pl.when(j == 0)
    def _():
        m_ref[...] = jnp.full(m_ref.shape, -1e30, F32)
        acc_ref[...] = jnp.zeros(acc_ref.shape, F32)

    vt = vt_ref[...]
    bounded = bounded_ref[0] == 1

    @pl.when(bounded)
    def _():
        shift = bound_ref[0]
        for m in range(2):
            s = jnp.dot(k_ref[m], qt_ref[m], preferred_element_type=F32)
            p = jnp.exp2(s - shift).astype(BF16)
            acc_ref[m] += jnp.dot(vt, p, preferred_element_type=F32)

    @pl.when(jnp.logical_not(bounded))
    def _():
        for m in range(2):
            s = jnp.dot(k_ref[m], qt_ref[m], preferred_element_type=F32)
            m_old = m_ref[m]
            m_new = jnp.maximum(m_old, jnp.max(s, axis=0, keepdims=True))
            p = jnp.exp2(s - m_new).astype(BF16)
            acc_ref[m] = jnp.exp2(m_old - m_new) * acc_ref[m] + jnp.dot(vt, p, preferred_element_type=F32)
            m_ref[m] = m_new

    @pl.when(j == pl.num_programs(3) - 1)
    def _():
        lam = (jnp.exp(jnp.sum(lq1_ref[...] * lk1_ref[...], axis=-1, keepdims=True))
               - jnp.exp(jnp.sum(lq2_ref[...] * lk2_ref[...], axis=-1, keepdims=True)) + lam_init)
        a0 = acc_ref[0]
        a1 = acc_ref[1]
        o = a0[:HEAD] / a0[HEAD:HEAD + 1] - lam * (a1[:HEAD] / a1[HEAD:HEAD + 1])
        ms = jnp.mean(o * o, axis=0, keepdims=True)
        o_ref[...] = o * lax.rsqrt(ms + NORM_EPS) * sg_ref[...] * (1.0 - lam_init)


def _largest_divisor(n, cap, multiple):
    best = None
    for d in range(multiple, cap + 1, multiple):
        if n % d == 0:
            best = d
    return best if best is not None else n


FLASH_SAFE_SCORE_BOUND = 60.0


def _score_bound(q_gain, k_gain):
    bound = (1.02 * DA_QK * DA_QK ** -0.5 * LOG2_E) * jnp.max(jnp.abs(q_gain)) * jnp.max(jnp.abs(k_gain))
    return (bound <= FLASH_SAFE_SCORE_BOUND).astype(jnp.int32).reshape(1), bound.astype(F32).reshape(1)


def diff_attention_core(qt, k, vt, gains, lam_params, sub_gain, lam_init):
    b, h, _, dk, t = qt.shape
    s = k.shape[3]
    tq = min(512, t)
    tk = _largest_divisor(s, FLASH_MAX_KV_BLOCK, 2 * LANES)
    small = [p.reshape(1, -1) for p in lam_params] + [sub_gain.reshape(-1, 1)]
    bounded, bound = _score_bound(*gains)
    return pl.pallas_call(
        functools.partial(_flash_kernel, lam_init),
        grid_spec=pltpu.PrefetchScalarGridSpec(
            num_scalar_prefetch=2,
            grid=(b, h, t // tq, s // tk),
            in_specs=[
                pl.BlockSpec((None, None, 2, dk, tq), lambda bi, hi, i, j, *_: (bi, hi, 0, 0, i)),
                pl.BlockSpec((None, None, 2, tk, dk), lambda bi, hi, i, j, *_: (bi, hi, 0, j, 0)),
                pl.BlockSpec((None, None, LANES, tk), lambda bi, hi, i, j, *_: (bi, hi, 0, j)),
            ] + [pl.BlockSpec(x.shape, lambda bi, hi, i, j, *_: (0, 0)) for x in small],
            out_specs=pl.BlockSpec((None, None, HEAD, tq), lambda bi, hi, i, j, *_: (bi, hi, 0, i)),
            scratch_shapes=[pltpu.VMEM((2, 1, tq), F32), pltpu.VMEM((2, LANES, tq), F32)],
        ),
        out_shape=jax.ShapeDtypeStruct((b, h, HEAD, t), F32),
        compiler_params=_cparams("parallel", "parallel", "parallel", "arbitrary"),
    )(bounded, bound, qt, k, vt, *small)


def _split_heads(q, k, v):
    b, t, gw = q.shape
    h = gw // HEAD
    qth = q.reshape(b, t, h, 2, DA_QK).transpose(0, 2, 3, 4, 1)
    kh = k.reshape(b, t, h, 2, DA_QK).transpose(0, 2, 3, 1, 4)
    vth = v.reshape(b, t, h, HEAD).transpose(0, 2, 3, 1)
    ones_row = (lax.broadcasted_iota(jnp.int32, (b, h, LANES - HEAD, t), 2) == 0).astype(v.dtype)
    return qth, kh, jnp.concatenate([vth, ones_row], axis=2)


def diff_attention(dax, dac, params, gw, lam_init, ctx_out):
    q_gain, k_gain, lq1, lk1, lq2, lk2, sub_gain = params
    qtx, kx, vtx = _split_heads(*attention_prepare(dax, q_gain, k_gain, gw, rope=True))
    qtc, kc, vtc = _split_heads(*attention_prepare(dac, q_gain, k_gain, gw, rope=False))
    k = jnp.concatenate([kx, kc], axis=3)
    vt = jnp.concatenate([vtx, vtc], axis=3)
    lam_params = (lq1, lk1, lq2, lk2)
    merge = lambda o: o.transpose(0, 3, 1, 2).reshape(o.shape[0], o.shape[3], gw)
    gains = (q_gain, k_gain)
    out_x = merge(diff_attention_core(qtx, k, vt, gains, lam_params, sub_gain, lam_init))
    out_c = merge(diff_attention_core(qtc, kc, vtc, gains, lam_params, sub_gain, lam_init)) if ctx_out else None
    return out_x, out_c


FT_RADIX = 64


def _dft_cos_sin(n, scale=1.0):
    i = np.arange(n)
    ang = 2.0 * np.pi * ((i[:, None] * i[None, :]) % n) / n
    return np.cos(ang) * scale, np.sin(ang) * scale


def _channel_dft(gw, scale):
    c, s = _dft_cos_sin(HEAD, scale)
    eye = np.eye(gw // HEAD)
    return jnp.asarray(np.concatenate([np.kron(eye, c), np.kron(eye, s)], axis=0).astype(np.float32))


def _fnet_stage1_kernel(z_ref, gr_ref, gi_ref, or_ref, oi_ref):
    for j in range(z_ref.shape[1]):
        x = z_ref[:, j, :]
        or_ref[j] = _dot_f32(gr_ref[j], x)
        oi_ref[j] = _dot_f32(gi_ref[j], x)


def _matmul_f32_kernel(a_ref, b_ref, o_ref):
    o_ref[...] = _dot_f32(a_ref[...], b_ref[...])


def _channel_dft_times(chan, w_f):
    return pl.pallas_call(
        _matmul_f32_kernel,
        out_shape=jax.ShapeDtypeStruct((chan.shape[0], w_f.shape[1]), F32),
    )(chan, w_f)


def _fnet_stage2_kernel(gw, br_ref, bi_ref, rot_ref, chanw_ref, o_ref):
    n1 = br_ref.shape[0]
    p = _dot_f32(rot_ref[...], jnp.concatenate([br_ref[...], bi_ref[...]], axis=0))
    groups = br_ref.shape[1] // gw
    rows = jnp.concatenate(
        [jnp.concatenate([p[:n1, g * gw:(g + 1) * gw], p[n1:, g * gw:(g + 1) * gw]], axis=1) for g in range(groups)],
        axis=0)
    out = _dot_f32(rows, chanw_ref[...])
    for g in range(groups):
        o_ref[:, g * gw:(g + 1) * gw] = out[g * n1:(g + 1) * n1]


def fourier_mix_long(z, w_f):
    b, t, gw = z.shape
    n1 = FT_RADIX
    n2 = t // n1
    k2 = np.arange(n2)[None, :, None]
    n = np.arange(n1)[:, None, None] + n1 * np.arange(n2)[None, None, :]
    ang = 2.0 * np.pi * ((k2 * n) % t) / t
    g_r = jnp.asarray(np.cos(ang).astype(np.float32))
    g_i = jnp.asarray((-np.sin(ang)).astype(np.float32))
    j8 = SUBLANES
    br, bi = pl.pallas_call(
        _fnet_stage1_kernel,
        grid=(b, n1 // j8),
        in_specs=[
            pl.BlockSpec((None, n2, j8, gw), lambda bi_, i: (bi_, 0, i, 0)),
            pl.BlockSpec((j8, n2, n2), lambda bi_, i: (i, 0, 0)),
            pl.BlockSpec((j8, n2, n2), lambda bi_, i: (i, 0, 0)),
        ],
        out_specs=[pl.BlockSpec((None, j8, n2, gw), lambda bi_, i: (bi_, i, 0, 0))] * 2,
        out_shape=[jax.ShapeDtypeStruct((b, n1, n2, gw), F32)] * 2,
        compiler_params=_cparams("parallel", "parallel"),
    )(z.reshape(b, n2, n1, gw), g_r, g_i)
    c64, s64 = _dft_cos_sin(n1)
    rot = jnp.asarray(np.block([[c64, s64], [-s64, c64]]).astype(np.float32))
    chanw = _channel_dft_times(_channel_dft(gw, 1.0 / math.sqrt(t * HEAD)), w_f)
    cols = n2 * gw
    tc = min(2048, cols)
    out = pl.pallas_call(
        functools.partial(_fnet_stage2_kernel, gw),
        grid=(b, cols // tc),
        in_specs=[
            pl.BlockSpec((None, n1, tc), lambda bi_, i: (bi_, 0, i)),
            pl.BlockSpec((None, n1, tc), lambda bi_, i: (bi_, 0, i)),
            _const_spec(rot.shape), _const_spec(chanw.shape),
        ],
        out_specs=pl.BlockSpec((None, n1, tc), lambda bi_, i: (bi_, 0, i)),
        out_shape=jax.ShapeDtypeStruct((b, n1, cols), F32),
        compiler_params=_cparams("parallel", "parallel"),
    )(br.reshape(b, n1, cols), bi.reshape(b, n1, cols), rot, chanw)
    return out.reshape(b, t, gw)


def _fnet_dense_kernel(z_ref, ct_ref, st_ref, chanw_ref, o_ref):
    z = z_ref[...]
    pr = _dot_f32(ct_ref[...], z)
    pi = -_dot_f32(st_ref[...], z)
    o_ref[...] = _dot_f32(jnp.concatenate([pr, pi], axis=1), chanw_ref[...])


def fourier_mix_short(z, w_f):
    b, t, gw = z.shape
    ct, st = _dft_cos_sin(t)
    ct = jnp.asarray(ct.astype(np.float32))
    st = jnp.asarray(st.astype(np.float32))
    chanw = _channel_dft_times(_channel_dft(gw, 1.0 / math.sqrt(t * HEAD)), w_f)
    tok = pl.BlockSpec((None, t, gw), lambda bi: (bi, 0, 0))
    return pl.pallas_call(
        _fnet_dense_kernel,
        grid=(b,),
        in_specs=[tok, _const_spec(ct.shape), _const_spec(st.shape), _const_spec(chanw.shape)],
        out_specs=tok,
        out_shape=jax.ShapeDtypeStruct((b, t, gw), F32),
        compiler_params=_cparams("parallel"),
    )(z, ct, st, chanw)


def fourier_mix(z, w_f):
    t = z.shape[1]
    if t % (FT_RADIX * SUBLANES) == 0 and t // FT_RADIX >= LANES:
        return fourier_mix_long(z, w_f)
    return fourier_mix_short(z, w_f)


def _pool_kernel(t_total, u_ref, prev_ref, next_ref, w_ref, s_ref, o_ref, buf_ref):
    tm, gw = u_ref.shape
    _stage_with_halo(buf_ref, u_ref, prev_ref, next_ref)
    at = lambda off: buf_ref[SUBLANES + off:SUBLANES + off + tm, :]
    u = at(0)
    t = pl.program_id(1) * tm + lax.broadcasted_iota(jnp.int32, (tm, 1), 0)
    lane = lax.broadcasted_iota(jnp.int32, (1, gw), 1)
    group = lane // (gw // len(POOL_WINDOWS))
    mean = jnp.zeros((tm, gw), F32)
    run = jnp.zeros((tm, gw), F32)
    half_prev = 0
    for i, w in enumerate(POOL_WINDOWS):
        half = w // 2
        for off in range(half_prev, half):
            run = run + at(-off - 1) + at(off)
        half_prev = half
        cnt = (jnp.minimum(t + half, t_total) - jnp.maximum(t - half, 0)).astype(F32)
        mean = jnp.where(group == i, run / cnt, mean)
    o_ref[...] = _dot_f32(mean - u, w_ref[...]) * s_ref[...]


def pool_mix(u, w_p, s_p):
    b, t, gw = u.shape
    tm = min(512, t)
    nw, ch = w_p.shape[0], w_p.shape[1]
    w_bd = jnp.zeros((gw, gw), F32)
    for i in range(nw):
        w_bd = w_bd.at[i * ch:(i + 1) * ch, i * ch:(i + 1) * ch].set(w_p[i])
    return pl.pallas_call(
        functools.partial(_pool_kernel, t),
        grid=(b, t // tm),
        in_specs=_halo_specs(tm, t, gw) + [_const_spec((gw, gw)), _const_spec((1, gw))],
        out_specs=pl.BlockSpec((None, tm, gw), lambda bi, i: (bi, i, 0)),
        out_shape=jax.ShapeDtypeStruct((b, t, gw), F32),
        scratch_shapes=[pltpu.VMEM((tm + 2 * SUBLANES, gw), F32)],
        compiler_params=_cparams("parallel", "parallel"),
    )(u, u, u, w_bd, s_p.reshape(1, gw))


def _outproj_kernel(gw, x_ref, g_ref, a_ref, b_ref, f_ref, p_ref, w_ref, o_ref):
    acc = None
    for i, m_ref in enumerate((a_ref, b_ref, f_ref, p_ref)):
        part = jnp.dot(m_ref[...].astype(BF16), w_ref[i * gw:(i + 1) * gw, :], preferred_element_type=F32)
        acc = part if acc is None else acc + part
    o_ref[...] = x_ref[...] + g_ref[...] * acc


def output_projection(x, gate, mixers, w_out_bf16):
    b, t, d = x.shape
    gw = mixers[0].shape[-1]
    tm = min(512, t)
    tok = pl.BlockSpec((None, tm, gw), lambda bi, i: (bi, i, 0))
    xs = pl.BlockSpec((None, tm, d), lambda bi, i: (bi, i, 0))
    return pl.pallas_call(
        functools.partial(_outproj_kernel, gw),
        grid=(b, t // tm),
        in_specs=[xs, pl.BlockSpec((None, 1, d), lambda bi, i: (bi, 0, 0))] + [tok] * 4 + [_const_spec(w_out_bf16.shape)],
        out_specs=xs,
        out_shape=jax.ShapeDtypeStruct((b, t, d), F32),
        compiler_params=_cparams("parallel", "parallel"),
    )(x, gate, *mixers, w_out_bf16)


def _router_kernel(n_exp, x_ref, gain_ref, sc_ref, sh_ref, wr_ref, h_o, aff_o, afft_o):
    h = _modulated_norm(x_ref[...], gain_ref[...], sc_ref[...], sh_ref[...])
    h_o[...] = h
    logits = _dot_f32(h, wr_ref[...])
    lane = lax.broadcasted_iota(jnp.int32, logits.shape, 1)
    logits = jnp.where(lane < n_exp, logits, -1e30)
    e = jnp.exp(logits - jnp.max(logits, axis=-1, keepdims=True))
    aff = e / jnp.sum(e, axis=-1, keepdims=True)
    aff_o[...] = aff
    afft_o[...] = jnp.transpose(aff)[:n_exp, :]


def router(x, gain, scale, shift, w_router):
    b, t, d = x.shape
    n_exp = w_router.shape[1]
    tm = min(512, t)
    wr = jnp.zeros((d, LANES), F32).at[:, :n_exp].set(w_router)
    row = pl.BlockSpec((None, 1, d), lambda bi, i: (bi, 0, 0))
    return pl.pallas_call(
        functools.partial(_router_kernel, n_exp),
        grid=(b, t // tm),
        in_specs=[pl.BlockSpec((None, tm, d), lambda bi, i: (bi, i, 0)), _const_spec((1, d)), row, row,
                  _const_spec(wr.shape)],
        out_specs=[pl.BlockSpec((None, tm, d), lambda bi, i: (bi, i, 0)),
                   pl.BlockSpec((None, tm, LANES), lambda bi, i: (bi, i, 0)),
                   pl.BlockSpec((None, n_exp, tm), lambda bi, i: (bi, 0, i))],
        out_shape=[jax.ShapeDtypeStruct((b, t, d), F32), jax.ShapeDtypeStruct((b, t, LANES), F32),
                   jax.ShapeDtypeStruct((b, n_exp, t), F32)],
        compiler_params=_cparams("parallel", "parallel"),
    )(x, gain.reshape(1, d), scale, shift, wr)


TOPK_EXPONENT_STEPS = 7
TOPK_MANTISSA_STEPS = 44


def _row_cumsum(x_ref, o_ref, upper_ref):
    rows, t = x_ref.shape
    carry = jnp.zeros((rows, 1), F32)
    for g in range(t // LANES):
        sl = slice(g * LANES, (g + 1) * LANES)
        local = jnp.dot(x_ref[:, sl].astype(BF16), upper_ref[...], preferred_element_type=F32) + carry
        o_ref[:, sl] = local
        carry = local[:, LANES - 1:LANES]


def _topk_kernel(cap, aff_ref, upper_ref, idx_o, sel_ref, cs_ref, local_ref, begin_ref, end_ref):
    aff = aff_ref[...]
    rows, t = aff.shape
    capf = float(cap)
    count_ge = lambda thr: jnp.sum(jnp.where(aff >= thr, 1.0, 0.0), axis=-1, keepdims=True)
    hi = jnp.full((rows, 1), 2.0, F32)
    for step in reversed(range(TOPK_EXPONENT_STEPS)):
        cand = hi * (2.0 ** -(2 ** step))
        hi = jnp.where(count_ge(cand) < capf, cand, hi)
    lo = hi * 0.5
    lo = jnp.where(count_ge(lo) >= capf, lo, 0.0)

    def bisect(_, carry):
        lo, hi = carry
        mid = 0.5 * (lo + hi)
        enough = count_ge(mid) >= capf
        return jnp.where(enough, mid, lo), jnp.where(enough, hi, mid)

    lo, hi = lax.fori_loop(0, TOPK_MANTISSA_STEPS, bisect, (lo, hi))
    above = aff >= hi
    need = capf - count_ge(hi)
    sel_ref[...] = jnp.where((aff >= lo) & jnp.logical_not(above), 1.0, 0.0)
    _row_cumsum(sel_ref, cs_ref, upper_ref)
    tied_in = (sel_ref[...] > 0.5) & (cs_ref[...] <= need)
    sel_ref[...] = jnp.where(above | tied_in, 1.0, 0.0)

    groups = t // LANES
    carry = jnp.zeros((rows, 1), F32)
    for g in range(groups):
        local = jnp.dot(sel_ref[:, g * LANES:(g + 1) * LANES].astype(BF16), upper_ref[...],
                        preferred_element_type=F32)
        local_ref[:, g, :] = local
        begin_ref[:, g:g + 1] = carry
        carry = carry + local[:, LANES - 1:LANES]
        end_ref[:, g:g + 1] = carry

    ones_g = jnp.ones((SUBLANES, groups), BF16)
    ones_l = jnp.ones((SUBLANES, LANES), BF16)

    def compact(r, _):
        begin = begin_ref[pl.ds(r, 1), :]
        end = end_ref[pl.ds(r, 1), :]
        local = local_ref[r].astype(BF16)
        for cb in range(0, cap, LANES):
            n = min(LANES, cap - cb)
            slot = (lax.broadcasted_iota(jnp.int32, (n, 1), 0) + cb).astype(F32)
            before = jnp.where(end <= slot, 1.0, 0.0)
            mine = jnp.where((begin <= slot) & (slot < end), 1.0, 0.0)
            rank = slot - jnp.sum(mine * begin, axis=-1, keepdims=True)
            counts = jnp.dot(mine.astype(BF16), local, preferred_element_type=F32)
            reached = jnp.where(counts <= rank, 1.0, 0.0).astype(BF16)
            idx_row = LANES * _dot_nt(ones_g, before) + _dot_nt(ones_l, reached)
            idx_o[r, :, cb:cb + n] = idx_row[0:1].astype(jnp.int32)
        return 0

    lax.fori_loop(0, rows, compact, 0)


def expert_choice_topk(aff_t, cap):
    b, n_exp, t = aff_t.shape
    rows = b * n_exp
    upper = jnp.asarray(np.triu(np.ones((LANES, LANES), np.float32))).astype(BF16)
    idx = pl.pallas_call(
        functools.partial(_topk_kernel, cap),
        grid=(1,),
        in_specs=[_const_spec((rows, t)), _const_spec(upper.shape)],
        out_specs=_const_spec((rows, 1, cap)),
        out_shape=jax.ShapeDtypeStruct((rows, 1, cap), jnp.int32),
        scratch_shapes=[pltpu.VMEM((rows, t), F32), pltpu.VMEM((rows, t), F32),
                        pltpu.VMEM((rows, t // LANES, LANES), F32),
                        pltpu.VMEM((rows, t // LANES), F32), pltpu.VMEM((rows, t // LANES), F32)],
        compiler_params=_cparams("arbitrary"),
    )(aff_t.reshape(rows, t), upper)
    return idx.reshape(rows * cap)


GATHER_UNROLL = 8
SCATTER_BATCH = 4


def _gather_kernel(cap, idx_ref, h_hbm, aff_ref, xs_o, g_o, h_ref, buf_ref, sem):
    b = pl.program_id(0)
    e = pl.program_id(1)
    base = (b * pl.num_programs(1) + e) * cap

    @pl.when(e == 0)
    def _():
        load = pltpu.make_async_copy(h_hbm.at[b], h_ref, sem)
        load.start()
        load.wait()

    def body(i, _):
        for u in range(GATHER_UNROLL):
            c = i * GATHER_UNROLL + u
            row = idx_ref[base + c]
            buf_ref[pl.ds(c, 1), :] = h_ref[pl.ds(row, 1), :]
            g_o[pl.ds(c, 1), :] = aff_ref[pl.ds(row, 1), :]
        return 0

    lax.fori_loop(0, cap // GATHER_UNROLL, body, 0)
    xs_o[...] = buf_ref[...].astype(BF16)


def gather_tokens(idx, h, aff, n_exp, cap):
    b, t, d = h.shape
    return pl.pallas_call(
        functools.partial(_gather_kernel, cap),
        grid_spec=pltpu.PrefetchScalarGridSpec(
            num_scalar_prefetch=1,
            grid=(b, n_exp),
            in_specs=[pl.BlockSpec(memory_space=pl.ANY),
                      pl.BlockSpec((None, t, LANES), lambda bi, e, idx_: (bi, 0, 0))],
            out_specs=[pl.BlockSpec((None, None, cap, d), lambda bi, e, idx_: (bi, e, 0, 0)),
                       pl.BlockSpec((None, None, cap, LANES), lambda bi, e, idx_: (bi, e, 0, 0))],
            scratch_shapes=[pltpu.VMEM((t, d), F32), pltpu.VMEM((cap, d), F32), pltpu.SemaphoreType.DMA(())],
        ),
        out_shape=[jax.ShapeDtypeStruct((b, n_exp, cap, d), BF16),
                   jax.ShapeDtypeStruct((b, n_exp, cap, LANES), F32)],
        compiler_params=_cparams("arbitrary", "arbitrary"),
    )(idx, h, aff)


def _expert_ffn_kernel(n_groups, *refs):
    xs_refs = refs[0:3 * n_groups:3]
    g_refs = refs[1:3 * n_groups:3]
    gate_refs = refs[2:3 * n_groups:3]
    wg_ref, wu_ref, wd_ref = refs[3 * n_groups:3 * n_groups + 3]
    o_refs = refs[3 * n_groups + 3:4 * n_groups + 3]
    acc_refs = refs[4 * n_groups + 3:]
    f = pl.program_id(1)
    wg = wg_ref[...].astype(BF16)
    wu = wu_ref[...].astype(BF16)
    wd = wd_ref[...].astype(BF16)
    for xs_ref, acc_ref in zip(xs_refs, acc_refs):
        for bi in range(xs_ref.shape[0]):
            xb = xs_ref[bi]
            gate = jnp.dot(xb, wg, preferred_element_type=F32)
            up = jnp.dot(xb, wu, preferred_element_type=F32)
            hid = (gate * _sigmoid(gate) * up).astype(BF16)
            part = jnp.dot(hid, wd, preferred_element_type=F32)

            @pl.when(f == 0)
            def _():
                acc_ref[bi] = part

            @pl.when(f > 0)
            def _():
                acc_ref[bi] += part

    @pl.when(f == pl.num_programs(1) - 1)
    def _():
        e = pl.program_id(0)
        for g_ref, gate_ref, acc_ref, o_ref in zip(g_refs, gate_refs, acc_refs, o_refs):
            lane = lax.broadcasted_iota(jnp.int32, g_ref.shape, 2)
            g = jnp.sum(jnp.where(lane == e, g_ref[...], 0.0), axis=-1, keepdims=True)
            o_ref[...] = acc_ref[...] * g * gate_ref[...]


def expert_ffn(groups, layer, w_gate, w_up, w_down):
    _, n_exp, d, f_dim = w_gate.shape
    tf = _largest_divisor(f_dim, 256, LANES)
    args, in_specs, out_specs, out_shapes, scratch = [], [], [], [], []
    for xs, g_rows, gate in groups:
        b, _, cap, _ = xs.shape
        args += [xs, g_rows, gate]
        in_specs += [pl.BlockSpec((b, None, cap, d), lambda e, f: (0, e, 0, 0)),
                     pl.BlockSpec((b, None, cap, LANES), lambda e, f: (0, e, 0, 0)),
                     pl.BlockSpec((b, 1, d), lambda e, f: (0, 0, 0))]
        out_specs.append(pl.BlockSpec((None, b, cap, d), lambda e, f: (e, 0, 0, 0)))
        out_shapes.append(jax.ShapeDtypeStruct((n_exp, b, cap, d), F32))
        scratch.append(pltpu.VMEM((b, cap, d), F32))
    in_specs += [pl.BlockSpec((None, None, d, tf), lambda e, f: (layer, e, 0, f)),
                 pl.BlockSpec((None, None, d, tf), lambda e, f: (layer, e, 0, f)),
                 pl.BlockSpec((None, None, tf, d), lambda e, f: (layer, e, f, 0))]
    return pl.pallas_call(
        functools.partial(_expert_ffn_kernel, len(groups)),
        grid=(n_exp, f_dim // tf),
        in_specs=in_specs,
        out_specs=out_specs,
        out_shape=out_shapes,
        scratch_shapes=scratch,
        compiler_params=_cparams("parallel", "arbitrary"),
    )(*args, w_gate, w_up, w_down)


def _scatter_kernel(cap, idx_ref, eo_ref, x_hbm, o_hbm, acc_ref, sem):
    b = pl.program_id(0)
    e = pl.program_id(1)
    base = (b * pl.num_programs(1) + e) * cap

    @pl.when(e == 0)
    def _():
        load = pltpu.make_async_copy(x_hbm.at[b], acc_ref, sem)
        load.start()
        load.wait()

    def body(i, _):
        c0 = i * SCATTER_BATCH
        rows = [idx_ref[base + c0 + u] for u in range(SCATTER_BATCH)]
        sums = [acc_ref[pl.ds(rows[u], 1), :] + eo_ref[pl.ds(c0 + u, 1), :] for u in range(SCATTER_BATCH)]
        for u in range(SCATTER_BATCH):
            acc_ref[pl.ds(rows[u], 1), :] = sums[u]
        return 0

    lax.fori_loop(0, cap // SCATTER_BATCH, body, 0)

    @pl.when(e == pl.num_programs(1) - 1)
    def _():
        store = pltpu.make_async_copy(acc_ref, o_hbm.at[b], sem)
        store.start()
        store.wait()


def scatter_residual(idx, expert_out, x, cap):
    n_exp, b, _, d = expert_out.shape
    t = x.shape[1]
    return pl.pallas_call(
        functools.partial(_scatter_kernel, cap),
        grid_spec=pltpu.PrefetchScalarGridSpec(
            num_scalar_prefetch=1,
            grid=(b, n_exp),
            in_specs=[pl.BlockSpec((None, None, cap, d), lambda bi, e, idx_: (e, bi, 0, 0)),
                      pl.BlockSpec(memory_space=pl.ANY)],
            out_specs=pl.BlockSpec(memory_space=pl.ANY),
            scratch_shapes=[pltpu.VMEM((t, d), F32), pltpu.SemaphoreType.DMA(())],
        ),
        out_shape=jax.ShapeDtypeStruct((b, t, d), F32),
        compiler_params=_cparams("arbitrary", "arbitrary"),
    )(idx, expert_out, x)


def moe_residual(streams, gain, layer, w_router, w_gate, w_up, w_down):
    n_exp = w_router.shape[1]
    routed = []
    for x, scale, shift, gate in streams:
        cap = CAPACITY_FACTOR * x.shape[1] // n_exp
        h, aff, aff_t = router(x, gain, scale, shift, w_router)
        idx = expert_choice_topk(aff_t, cap)
        xs, g_rows = gather_tokens(idx, h, aff, n_exp, cap)
        routed.append((idx, cap, (xs, g_rows, gate)))
    outs = expert_ffn([g for _, _, g in routed], layer, w_gate, w_up, w_down)
    return [scatter_residual(idx, eo, x, cap) for (idx, cap, _), eo, (x, _, _, _) in zip(routed, outs, streams)]


def kernel(x, c, ctx, c_ctx, mod_w, mod_b, norm1_w, norm2_w, w_in, rw_mu_prev, rw_mu_next, rw_w0, rw_w_up,
           rw_a0, rw_a_up, rw_g_up, rw_k_k, rw_k_a, rw_r_k, rw_ln_w, rw_ln_b, da_q_gain, da_k_gain, da_lq1,
           da_lk1, da_lq2, da_lk2, da_sub_gain, ft_w, pl_w, pl_scale, w_out, moe_router, moe_w_gate, moe_w_up,
           moe_w_down):
    depth, d = norm1_w.shape
    batch = x.shape[0]
    gw = d // N_MIXERS
    c_rows = jnp.zeros((SUBLANES, d), F32).at[:batch].set(c).at[batch].set(c_ctx)
    mod = modulation_vectors(c_rows, mod_w, mod_b)
    w_in_b = w_in.astype(BF16)
    w_out_b = w_out.astype(BF16)
    for l in range(depth):
        ctx_out = l < depth - 1
        lam_init = 0.8 - 0.6 * math.exp(-0.3 * l)
        mx = mod[l, :batch].reshape(batch, 6, 1, d)
        mc = jnp.broadcast_to(mod[l, batch].reshape(1, 6, 1, d), (batch, 6, 1, d))
        rw = (rw_mu_prev[l], rw_mu_next[l], rw_w0[l], rw_w_up[l], rw_a0[l], rw_a_up[l], rw_g_up[l],
              rw_k_k[l], rw_k_a[l], rw_r_k[l], rw_ln_w[l], rw_ln_b[l])
        da = (da_q_gain[l], da_k_gain[l], da_lq1[l], da_lk1[l], da_lq2[l], da_lk2[l], da_sub_gain[l])
        rwx, dax, ftx, plx = input_projection(x, norm1_w[l], mx[:, 1], mx[:, 0], w_in_b[l], gw)
        rwc, dac, ftc, plc = input_projection(ctx, norm1_w[l], mc[:, 1], mc[:, 0], w_in_b[l], gw)
        ax, ac = rwkv_mixer(rwx, rwc, rw, gw, ctx_out)
        bx, bc = diff_attention(dax, dac, da, gw, lam_init, ctx_out)
        fx = fourier_mix(ftx, ft_w[l])
        px = pool_mix(plx, pl_w[l], pl_scale[l])
        x = output_projection(x, mx[:, 2], (ax, bx, fx, px), w_out_b[l])
        streams = [(x, mx[:, 4], mx[:, 3], mx[:, 5])]
        if ctx_out:
            fc = fourier_mix(ftc, ft_w[l])
            pc = pool_mix(plc, pl_w[l], pl_scale[l])
            ctx = output_projection(ctx, mc[:, 2], (ac, bc, fc, pc), w_out_b[l])
            streams.append((ctx, mc[:, 4], mc[:, 3], mc[:, 5]))
        outs = moe_residual(streams, norm2_w[l], l, moe_router[l], moe_w_gate, moe_w_up, moe_w_down)
        x = outs[0]
        if ctx_out:
            ctx = outs[1]
    return x
```

```python
import functools
import math

import numpy as np
import jax
import jax.numpy as jnp
from jax import lax
from jax.experimental import pallas as pl
from jax.experimental.pallas import tpu as pltpu

F32 = jnp.float32
BF16 = jnp.bfloat16
HIGHEST = lax.Precision.HIGHEST

N_MIXERS = 4
HEAD = 64
NORM_EPS = 1e-6
RW_LN_EPS = 64e-5
GRID_W = 64
DA_QK = HEAD // 2
ROPE_BASE = 10000.0
POOL_WINDOWS = (2, 4, 8, 16)
N_EXPERTS = 16
CAPACITY_FACTOR = 2

LANES = 128
SUBLANES = 8
VMEM_LIMIT_BYTES = 56 * 1024 * 1024

RW_CHUNK = 64
RW_CHUNKS_PER_STEP = 8


def _cparams(*sem):
    return pltpu.CompilerParams(dimension_semantics=sem, vmem_limit_bytes=VMEM_LIMIT_BYTES)


def _dot(a, b):
    return jnp.dot(a.astype(BF16), b.astype(BF16), preferred_element_type=F32)


def _dot_f32(a, b):
    return jnp.dot(a, b, precision=HIGHEST, preferred_element_type=F32)


def _dot_tri(a, b):
    return _dot(a, b)


def _dot_nt(a, b, exact=False):
    dn = (((1,), (1,)), ((), ()))
    if exact:
        return lax.dot_general(a, b, dn, precision=HIGHEST, preferred_element_type=F32)
    return lax.dot_general(a.astype(BF16), b.astype(BF16), dn, preferred_element_type=F32)


def _dot_tn(a, b, exact=False):
    dn = (((0,), (0,)), ((), ()))
    if exact:
        return lax.dot_general(a, b, dn, precision=HIGHEST, preferred_element_type=F32)
    return lax.dot_general(a.astype(BF16), b.astype(BF16), dn, preferred_element_type=F32)


def _sigmoid(x):
    return 1.0 / (1.0 + jnp.exp(-x))


def _block_ones(n, blk, value=1.0):
    i = np.arange(n) // blk
    return jnp.asarray((i[:, None] == i[None, :]).astype(np.float32) * value)


def _const_spec(shape):
    nd = len(shape)
    return pl.BlockSpec(shape, lambda *_: (0,) * nd)


def _mod_kernel(c_ref, w_ref, b_ref, o_ref):
    c = c_ref[...]
    o_ref[...] = _dot_f32(c * _sigmoid(c), w_ref[...]) + b_ref[...]


def modulation_vectors(c_rows, mod_w, mod_b):
    depth, d, n = mod_w.shape
    tn = 1536
    return pl.pallas_call(
        _mod_kernel,
        grid=(depth, n // tn),
        in_specs=[
            pl.BlockSpec((SUBLANES, d), lambda l, j: (0, 0)),
            pl.BlockSpec((None, d, tn), lambda l, j: (l, 0, j)),
            pl.BlockSpec((None, 1, tn), lambda l, j: (l, 0, j)),
        ],
        out_specs=pl.BlockSpec((None, SUBLANES, tn), lambda l, j: (l, 0, j)),
        out_shape=jax.ShapeDtypeStruct((depth, SUBLANES, n), F32),
        compiler_params=_cparams("parallel", "parallel"),
    )(c_rows, mod_w, mod_b.reshape(depth, 1, n))


def _modulated_norm(x, gain, scale, shift):
    ms = jnp.mean(x * x, axis=-1, keepdims=True)
    return (x * lax.rsqrt(ms + NORM_EPS) * gain) * (1.0 + scale) + shift


def _inproj_kernel(splits, x_ref, gain_ref, sc_ref, sh_ref, w_ref, *o_refs):
    h = _modulated_norm(x_ref[...], gain_ref[...], sc_ref[...], sh_ref[...]).astype(BF16)
    for (lo, hi), o_ref in zip(splits, o_refs):
        o_ref[...] = jnp.dot(h, w_ref[:, lo:hi], preferred_element_type=F32)


def input_projection(x, gain, scale, shift, w_in_bf16, group_w):
    b, t, d = x.shape
    rw_cols = w_in_bf16.shape[1] - 3 * group_w - 2 * group_w
    cuts = [0, rw_cols, rw_cols + 3 * group_w, rw_cols + 4 * group_w, rw_cols + 5 * group_w]
    splits = tuple((cuts[i], cuts[i + 1]) for i in range(4))
    tm = min(512, t)
    row = pl.BlockSpec((None, 1, d), lambda bi, i: (bi, 0, 0))
    return pl.pallas_call(
        functools.partial(_inproj_kernel, splits),
        grid=(b, t // tm),
        in_specs=[
            pl.BlockSpec((None, tm, d), lambda bi, i: (bi, i, 0)),
            _const_spec((1, d)),
            row, row,
            _const_spec(w_in_bf16.shape),
        ],
        out_specs=[pl.BlockSpec((None, tm, hi - lo), lambda bi, i: (bi, i, 0)) for lo, hi in splits],
        out_shape=[jax.ShapeDtypeStruct((b, t, hi - lo), F32) for lo, hi in splits],
        compiler_params=_cparams("parallel", "parallel"),
    )(x, gain.reshape(1, d), scale, shift, w_in_bf16)


def _halo_specs(tm, t, width):
    nb8 = t // SUBLANES
    r8 = tm // SUBLANES
    return [
        pl.BlockSpec((None, tm, width), lambda b, i: (b, i, 0)),
        pl.BlockSpec((None, SUBLANES, width), lambda b, i: (b, jnp.maximum(i * r8 - 1, 0), 0)),
        pl.BlockSpec((None, SUBLANES, width), lambda b, i: (b, jnp.minimum((i + 1) * r8, nb8 - 1), 0)),
    ]


def _stage_with_halo(buf_ref, main_ref, prev_ref, next_ref):
    tm = main_ref.shape[0]
    i = pl.program_id(1)
    n = pl.num_programs(1)
    buf_ref[SUBLANES:SUBLANES + tm, :] = main_ref[...]
    buf_ref[0:SUBLANES, :] = jnp.where(i > 0, prev_ref[...], 0.0)
    buf_ref[SUBLANES + tm:2 * SUBLANES + tm, :] = jnp.where(i < n - 1, next_ref[...], 0.0)


def _rwkv_prep_kernel(gw, rw_ref, prev_ref, next_ref, mup_ref, mun_ref, kk_ref_, ka_ref, w0_ref, wup_ref,
                      a0_ref, aup_ref, gup_ref, hsum_ref,
                      r_o, k_o, v_o, kk_o, gate_o, lwf_o, kaf_o, kdf_o, lwb_o, kab_o, kdb_o, buf_ref):
    tm = rw_ref.shape[0]
    _stage_with_halo(buf_ref, rw_ref, prev_ref, next_ref)
    p = buf_ref[SUBLANES:SUBLANES + tm, :]
    prev = buf_ref[SUBLANES - 1:SUBLANES - 1 + tm, :]
    nxt = buf_ref[SUBLANES + 1:SUBLANES + 1 + tm, :]
    u = p + mup_ref[...] * (prev - p) + mun_ref[...] * (nxt - p)
    r = u[:, 0:gw]
    k = u[:, gw:2 * gw]
    v = u[:, 2 * gw:3 * gw]
    lora_w = u[:, 3 * gw:3 * gw + LANES]
    lora_a = u[:, 3 * gw + LANES:3 * gw + 2 * LANES]
    g = u[:, 3 * gw + 2 * LANES:3 * gw + 3 * LANES]
    kk = k * kk_ref_[...]
    ss = _dot_f32(kk * kk, hsum_ref[...])
    kk = kk * lax.rsqrt(jnp.maximum(ss, 1e-24))
    zw = _dot(jnp.tanh(lora_w), wup_ref[...]) + w0_ref[...]
    za = _dot(lora_a, aup_ref[...]) + a0_ref[...]
    logw = -_sigmoid(zw) * math.exp(-0.5)
    a = _sigmoid(za)
    r_o[...] = r
    k_o[...] = k
    v_o[...] = v
    kk_o[...] = kk
    gate_o[...] = _dot(_sigmoid(g), gup_ref[...])
    ka = ka_ref[...]
    for d, (lw_o, kka_o, kd_o) in enumerate(((lwf_o, kaf_o, kdf_o), (lwb_o, kab_o, kdb_o))):
        a_d = a[:, d * gw:(d + 1) * gw]
        lw_o[...] = logw[:, d * gw:(d + 1) * gw]
        kka_o[...] = kk * a_d
        kd_o[...] = k * (1.0 + (a_d - 1.0) * ka)


def _blockdiag2(m):
    r, c = m.shape[1:]
    z = jnp.zeros((r, c), m.dtype)
    return jnp.concatenate([jnp.concatenate([m[0], z], 1), jnp.concatenate([z, m[1]], 1)], 0)


def rwkv_prepare(rw, params, gw):
    (mu_prev, mu_next, w0, w_up, a0, a_up, g_up, k_k, k_a, r_k, ln_w, ln_b) = params
    b, t, cols = rw.shape
    tm = min(512, t)
    row = lambda v: v.reshape(1, -1)
    small = [row(mu_prev), row(mu_next), row(k_k), row(k_a), row(w0), _blockdiag2(w_up), row(a0),
             _blockdiag2(a_up), g_up, _block_ones(gw, HEAD)]
    out = jax.ShapeDtypeStruct((b, t, gw), F32)
    return pl.pallas_call(
        functools.partial(_rwkv_prep_kernel, gw),
        grid=(b, t // tm),
        in_specs=_halo_specs(tm, t, cols) + [_const_spec(s.shape) for s in small],
        out_specs=[pl.BlockSpec((None, tm, gw), lambda bi, i: (bi, i, 0))] * 11,
        out_shape=[out] * 11,
        scratch_shapes=[pltpu.VMEM((tm + 2 * SUBLANES, cols), F32)],
        compiler_params=_cparams("parallel", "parallel"),
    )(rw, rw, rw, *small)


RW_BUILD_CHUNKS_PER_STEP = 4


def _rwkv_chunk_kernel(ncs, r_ref, v_ref, kk_ref, lwf_ref, kaf_ref, kdf_ref, lwb_ref, kab_ref, kdb_ref,
                       y0f_o, qf_o, mf_o, g0f_o, y0b_o, qb_o, mb_o, g0b_o):
    L = RW_CHUNK
    rows = lax.broadcasted_iota(jnp.int32, (L, L), 0)
    cols = lax.broadcasted_iota(jnp.int32, (L, L), 1)
    eye = jnp.where(rows == cols, 1.0, 0.0)
    rows2 = lax.broadcasted_iota(jnp.int32, (L, 2 * L), 0)
    cols2 = lax.broadcasted_iota(jnp.int32, (L, 2 * L), 1) % L
    lane = lax.broadcasted_iota(jnp.int32, (1, LANES), 1)
    head_masks = [(lane >= h * HEAD) & (lane < (h + 1) * HEAD) for h in range(2)]
    r2 = lax.broadcasted_iota(jnp.int32, (LANES, LANES), 0)
    c2 = lax.broadcasted_iota(jnp.int32, (LANES, LANES), 1)
    same_head = (r2 // HEAD) == (c2 // HEAD)
    diag = r2 == c2
    directions = ((False, lwf_ref, kaf_ref, kdf_ref, y0f_o, qf_o, mf_o, g0f_o),
                  (True, lwb_ref, kab_ref, kdb_ref, y0b_o, qb_o, mb_o, g0b_o))

    probs = []
    for c in range(ncs):
        rs = slice(c * L, (c + 1) * L)
        for pair in range(2):
            sl = slice(pair * LANES, (pair + 1) * LANES)
            r = r_ref[rs, sl]
            v = v_ref[rs, sl]
            kk = kk_ref[rs, sl]
            for reverse, lw_ref, ka_ref, kd_ref, y0_o, q_o, m_o, g0_o in directions:
                incl = (cols >= rows) if reverse else (cols <= rows)
                strict = (cols > rows) if reverse else (cols < rows)
                incl2 = (cols2 >= rows2) if reverse else (cols2 <= rows2)
                logw = lw_ref[rs, sl]
                kd = kd_ref[rs, sl]
                a = -ka_ref[rs, sl]
                cum = _dot_f32(jnp.where(incl, 1.0, 0.0), logw)
                total = jnp.sum(logw, axis=0, keepdims=True)
                g_inv = jnp.exp(-cum)
                g_tail = jnp.exp(total - cum)
                bd = kk * jnp.exp(cum - logw)
                rd = r * jnp.exp(cum)
                probs.append(dict(
                    incl2=incl2, strict=strict, v=v, bd=bd, rd=rd, total=total,
                    lhs=jnp.concatenate([bd, rd], axis=0), rhs=jnp.concatenate([a * g_inv, kd * g_inv], axis=0),
                    tails=jnp.concatenate([kd * g_tail, a * g_tail], axis=0),
                    outs=(y0_o, q_o, m_o, g0_o), rs=rs, sl=sl, c=c, pair=pair))

    heads = []
    for p in probs:
        for h in range(2):
            gram = _dot_nt(jnp.where(head_masks[h], p["lhs"], 0.0), p["rhs"])
            heads.append(dict(
                p=p, h=h,
                nil=jnp.where(p["strict"], gram[:L, :L], 0.0),
                aak=jnp.where(p["strict"], gram[:L, L:], 0.0),
                ara_ark=jnp.where(p["incl2"], gram[L:, :], 0.0)))

    for hd in heads:
        hd["acc"] = eye + hd["nil"]
        hd["pow"] = hd["nil"]
    span = 2
    while span < L:
        for hd in heads:
            hd["pow"] = _dot_tri(hd["pow"], hd["pow"])
        for hd in heads:
            hd["acc"] = hd["acc"] + _dot_tri(hd["acc"], hd["pow"])
        span *= 2

    for hd in heads:
        hd["aakv"] = _dot(hd["aak"], hd["p"]["v"])
    for hd in heads:
        wp = _dot_tri(hd["acc"], jnp.concatenate([hd["aakv"], hd["p"]["bd"]], axis=1))
        hd["w1"], hd["pm"] = wp[:, :LANES], wp[:, LANES:]
    for hd in heads:
        rhs = jnp.concatenate([jnp.concatenate([hd["w1"], hd["pm"]], axis=1),
                               jnp.concatenate([hd["p"]["v"], jnp.zeros((L, LANES), F32)], axis=1)], axis=0)
        yq = _dot(hd["ara_ark"], rhs)
        hd["y0"], hd["q"] = yq[:, :LANES], yq[:, LANES:] + hd["p"]["rd"]

    for i, p in enumerate(probs):
        h0, h1 = heads[2 * i], heads[2 * i + 1]
        pick = lambda key: jnp.where(head_masks[0], h0[key], h1[key])
        y0_o, q_o, m_o, g0_o = p["outs"]
        y0_o[p["rs"], p["sl"]] = pick("y0")
        q_o[p["rs"], p["sl"]] = pick("q")
        vw = jnp.concatenate([p["v"], pick("w1")], axis=0)
        g0_o[p["c"], p["pair"]] = jnp.where(same_head, _dot_tn(p["tails"], vw), 0.0)
        m_o[p["c"], p["pair"]] = (jnp.where(same_head, _dot_tn(p["tails"][L:], pick("pm")), 0.0)
                                  + jnp.where(diag, jnp.exp(p["total"]), 0.0))


def rwkv_chunks(r, v, kk, lwf, kaf, kdf, lwb, kab, kdb):
    b, t, gw = r.shape
    L = RW_CHUNK
    nc = t // L
    ncs = RW_BUILD_CHUNKS_PER_STEP if nc % RW_BUILD_CHUNKS_PER_STEP == 0 else 1
    tok = pl.BlockSpec((None, ncs * L, gw), lambda bi, i: (bi, i, 0))
    mat = pl.BlockSpec((None, ncs, 2, LANES, LANES), lambda bi, i: (bi, i, 0, 0, 0))
    tok_s = jax.ShapeDtypeStruct((b, t, gw), F32)
    mat_s = jax.ShapeDtypeStruct((b, nc, 2, LANES, LANES), F32)
    return pl.pallas_call(
        functools.partial(_rwkv_chunk_kernel, ncs),
        grid=(b, nc // ncs),
        in_specs=[tok] * 9,
        out_specs=[tok, tok, mat, mat] * 2,
        out_shape=[tok_s, tok_s, mat_s, mat_s] * 2,
        compiler_params=_cparams("parallel", "parallel"),
    )(r, v, kk, lwf, kaf, kdf, lwb, kab, kdb)


def _rwkv_scan_kernel(cps, h0_ref, y0f_ref, qf_ref, mf_ref, g0f_ref, y0b_ref, qb_ref, mb_ref, g0b_ref,
                      yf_o, yb_o, hfin_o, h_ref):
    L = RW_CHUNK
    j = pl.program_id(1)

    @pl.when(j == 0)
    def _():
        h_ref[...] = h0_ref[...]

    for step in range(cps):
        for d, (y0_ref, q_ref, m_ref, g0_ref, y_o) in enumerate(
                ((y0f_ref, qf_ref, mf_ref, g0f_ref, yf_o), (y0b_ref, qb_ref, mb_ref, g0b_ref, yb_o))):
            c = step if d == 0 else cps - 1 - step
            rows = slice(c * L, (c + 1) * L)
            for pair in range(2):
                sl = slice(pair * LANES, (pair + 1) * LANES)
                h = h_ref[d, pair]
                y_o[rows, sl] = y0_ref[rows, sl] + _dot(q_ref[rows, sl], h)
                h_ref[d, pair] = _dot(m_ref[c, pair], h) + g0_ref[c, pair]

    @pl.when(j == pl.num_programs(1) - 1)
    def _():
        hfin_o[...] = h_ref[...]


def rwkv_scan(h0, y0f, qf, mf, g0f, y0b, qb, mb, g0b):
    b, t, gw = y0f.shape
    L = RW_CHUNK
    nc = t // L
    cps = min(RW_CHUNKS_PER_STEP, nc)
    nb = nc // cps
    tm = cps * L
    tok_f = pl.BlockSpec((None, tm, gw), lambda bi, i: (bi, i, 0))
    tok_b = pl.BlockSpec((None, tm, gw), lambda bi, i: (bi, nb - 1 - i, 0))
    mat_f = pl.BlockSpec((None, cps, 2, LANES, LANES), lambda bi, i: (bi, i, 0, 0, 0))
    mat_b = pl.BlockSpec((None, cps, 2, LANES, LANES), lambda bi, i: (bi, nb - 1 - i, 0, 0, 0))
    st = pl.BlockSpec((None, 2, 2, LANES, LANES), lambda bi, i: (bi, 0, 0, 0, 0))
    tok_s = jax.ShapeDtypeStruct((b, t, gw), F32)
    return pl.pallas_call(
        functools.partial(_rwkv_scan_kernel, cps),
        grid=(b, nb),
        in_specs=[st, tok_f, tok_f, mat_f, mat_f, tok_b, tok_b, mat_b, mat_b],
        out_specs=[tok_f, tok_b, st],
        out_shape=[tok_s, tok_s, jax.ShapeDtypeStruct((b, 2, 2, LANES, LANES), F32)],
        scratch_shapes=[pltpu.VMEM((2, 2, LANES, LANES), F32)],
        compiler_params=_cparams("parallel", "arbitrary"),
    )(h0, y0f, qf, mf, g0f, y0b, qb, mb, g0b)


def _rwkv_out_kernel(yf_ref, yb_ref, r_ref, k_ref, v_ref, gate_ref, rk_ref, lnw_ref, lnb_ref, hmean_ref, o_ref):
    y = yf_ref[...] + yb_ref[...]
    hmean = hmean_ref[...]
    mu = _dot_f32(y, hmean)
    yc = y - mu
    var = _dot_f32(yc * yc, hmean)
    yn = yc * lax.rsqrt(var + RW_LN_EPS) * lnw_ref[...] + lnb_ref[...]
    bonus = _dot_f32(r_ref[...] * k_ref[...] * rk_ref[...], hmean) * float(HEAD) * v_ref[...]
    o_ref[...] = (yn + bonus) * gate_ref[...]


def rwkv_output(yf, yb, r, k, v, gate, r_k, ln_w, ln_b):
    b, t, gw = yf.shape
    tm = min(512, t)
    tok = pl.BlockSpec((None, tm, gw), lambda bi, i: (bi, i, 0))
    small = [r_k.reshape(1, gw), ln_w.reshape(1, gw), ln_b.reshape(1, gw), _block_ones(gw, HEAD, 1.0 / HEAD)]
    return pl.pallas_call(
        _rwkv_out_kernel,
        grid=(b, t // tm),
        in_specs=[tok] * 6 + [_const_spec(s.shape) for s in small],
        out_specs=tok,
        out_shape=jax.ShapeDtypeStruct((b, t, gw), F32),
        compiler_params=_cparams("parallel", "parallel"),
    )(yf, yb, r, k, v, gate, *small)


def rwkv_mixer(rwx, rwc, params, gw, ctx_out):
    r_k, ln_w, ln_b = params[9], params[10], params[11]
    sx = rwkv_prepare(rwx, params, gw)
    sc = rwkv_prepare(rwc, params, gw)
    (rx, kx, vx, kkx, gx), dx = sx[:5], sx[5:]
    (rc, kc, vc, kkc, gc), dc = sc[:5], sc[5:]
    cx = rwkv_chunks(rx, vx, kkx, *dx)
    cc = rwkv_chunks(rc, vc, kkc, *dc)
    b = rwx.shape[0]
    h0 = jnp.zeros((b, 2, 2, LANES, LANES), F32)
    ycf, ycb, h_ctx = rwkv_scan(h0, *cc)
    yxf, yxb, _ = rwkv_scan(h_ctx, *cx)
    out_x = rwkv_output(yxf, yxb, rx, kx, vx, gx, r_k, ln_w, ln_b)
    out_c = rwkv_output(ycf, ycb, rc, kc, vc, gc, r_k, ln_w, ln_b) if ctx_out else None
    return out_x, out_c


def _rope_tables(n_tokens, reps):
    rows = n_tokens // GRID_W
    row = np.repeat(np.arange(rows), GRID_W).astype(np.float64)
    col = np.tile(np.arange(GRID_W), rows).astype(np.float64)
    n_freq = DA_QK // 4
    inv = ROPE_BASE ** (-np.arange(n_freq, dtype=np.float64) / n_freq)
    ar = row[:, None] * inv
    ac = col[:, None] * inv
    ang = np.concatenate([ar, ar, ac, ac], axis=-1)
    cos = np.tile(np.cos(ang), (1, reps)).astype(np.float32)
    sin = np.tile(np.sin(ang), (1, reps)).astype(np.float32)
    return jnp.asarray(cos), jnp.asarray(sin)


def _attn_prep_kernel(gw, rope, da_ref, qg_ref, kg_ref, gmean_ref, *rest):
    if rope:
        cos_ref, sin_ref, q_o, k_o, v_o = rest
    else:
        q_o, k_o, v_o = rest
    da = da_ref[...]
    gmean = gmean_ref[...]
    lane = lax.broadcasted_iota(jnp.int32, (1, gw), 1)
    first_half = (lane % (DA_QK // 2)) < (DA_QK // 4)

    def norm_rope(x, gain):
        ms = _dot_f32(x * x, gmean)
        y = x * lax.rsqrt(ms + NORM_EPS) * gain
        if rope:
            quarter = DA_QK // 4
            rot = jnp.where(first_half, -pltpu.roll(y, gw - quarter, 1), pltpu.roll(y, quarter, 1))
            y = y * cos_ref[...] + rot * sin_ref[...]
        return y

    tm = da.shape[0]
    q = norm_rope(da[:, 0:gw], qg_ref[...]) * (DA_QK ** -0.5 * LOG2_E)
    q_t = jnp.transpose(q)
    row_map = lax.broadcasted_iota(jnp.int32, (gw, 1), 0) // DA_QK
    for c in range(gw // DA_QK):
        q_o[c] = jnp.where(row_map == c, q_t, 0.0).astype(q_o.dtype)
    k_o[...] = norm_rope(da[:, gw:2 * gw], kg_ref[...]).astype(k_o.dtype)
    v_t = jnp.transpose(da[:, 2 * gw:3 * gw])
    pad = jnp.where(lax.broadcasted_iota(jnp.int32, (LANES - HEAD, tm), 0) == 0, 1.0, 0.0)
    for h in range(gw // HEAD):
        v_o[h] = jnp.concatenate([v_t[h * HEAD:(h + 1) * HEAD], pad], axis=0).astype(v_o.dtype)


def attention_prepare(da, q_gain, k_gain, gw, rope):
    b, t, cols = da.shape
    tm = min(512, t)
    reps = gw // DA_QK
    heads = gw // HEAD
    small = [jnp.tile(q_gain, reps).reshape(1, gw), jnp.tile(k_gain, reps).reshape(1, gw),
             _block_ones(gw, DA_QK, 1.0 / DA_QK)]
    args = [da] + small
    in_specs = [pl.BlockSpec((None, tm, cols), lambda bi, i: (bi, i, 0))] + [_const_spec(s.shape) for s in small]
    if rope:
        cos, sin = _rope_tables(t, reps)
        args += [cos, sin]
        in_specs += [pl.BlockSpec((tm, gw), lambda bi, i: (i, 0))] * 2
    return pl.pallas_call(
        functools.partial(_attn_prep_kernel, gw, rope),
        grid=(b, t // tm),
        in_specs=in_specs,
        out_specs=[pl.BlockSpec((None, reps, gw, tm), lambda bi, i: (bi, 0, 0, i)),
                   pl.BlockSpec((None, tm, gw), lambda bi, i: (bi, i, 0)),
                   pl.BlockSpec((None, heads, LANES, tm), lambda bi, i: (bi, 0, 0, i))],
        out_shape=[jax.ShapeDtypeStruct((b, reps, gw, t), BF16),
                   jax.ShapeDtypeStruct((b, t, gw), BF16),
                   jax.ShapeDtypeStruct((b, heads, LANES, t), BF16)],
        compiler_params=_cparams("parallel", "parallel"),
    )(*args)


LOG2_E = 1.4426950408889634
FLASH_MAX_KV_BLOCK = 2816


def _flash_kernel(lam_init, bounded_ref, bound_ref, qt_ref, k_ref, vt_ref, lq1_ref, lk1_ref, lq2_ref, lk2_ref,
                  sg_ref, o_ref, m_ref, acc_ref):
    j = pl.program_id(2)
    n_maps = qt_ref.shape[0]

    @pl.when(j == 0)
    def _():
        m_ref[...] = jnp.full(m_ref.shape, -1e30, F32)
        acc_ref[...] = jnp.zeros(acc_ref.shape, F32)

    k = k_ref[...]
    bounded = bounded_ref[0] == 1

    @pl.when(bounded)
    def _():
        shift = bound_ref[0]
        for c in range(n_maps):
            s = jnp.dot(k, qt_ref[c], preferred_element_type=F32)
            p = jnp.exp2(s - shift).astype(BF16)
            acc_ref[c] += jnp.dot(vt_ref[c // 2], p, preferred_element_type=F32)

    @pl.when(jnp.logical_not(bounded))
    def _():
        for c in range(n_maps):
            s = jnp.dot(k, qt_ref[c], preferred_element_type=F32)
            m_old = m_ref[c]
            m_new = jnp.maximum(m_old, jnp.max(s, axis=0, keepdims=True))
            p = jnp.exp2(s - m_new).astype(BF16)
            acc_ref[c] = (jnp.exp2(m_old - m_new) * acc_ref[c]
                          + jnp.dot(vt_ref[c // 2], p, preferred_element_type=F32))
            m_ref[c] = m_new

    @pl.when(j == pl.num_programs(2) - 1)
    def _():
        lam = (jnp.exp(jnp.sum(lq1_ref[...] * lk1_ref[...], axis=-1, keepdims=True))
               - jnp.exp(jnp.sum(lq2_ref[...] * lk2_ref[...], axis=-1, keepdims=True)) + lam_init)
        for h in range(n_maps // 2):
            a0 = acc_ref[2 * h]
            a1 = acc_ref[2 * h + 1]
            o = a0[:HEAD] / a0[HEAD:HEAD + 1] - lam * (a1[:HEAD] / a1[HEAD:HEAD + 1])
            ms = jnp.mean(o * o, axis=0, keepdims=True)
            o = o * lax.rsqrt(ms + NORM_EPS) * sg_ref[...] * (1.0 - lam_init)
            o_ref[:, h * HEAD:(h + 1) * HEAD] = jnp.transpose(o)


def _largest_divisor(n, cap, multiple):
    best = None
    for d in range(multiple, cap + 1, multiple):
        if n % d == 0:
            best = d
    return best if best is not None else n


FLASH_SAFE_SCORE_BOUND = 60.0


def _score_bound(q_gain, k_gain):
    bound = (1.02 * DA_QK * DA_QK ** -0.5 * LOG2_E) * jnp.max(jnp.abs(q_gain)) * jnp.max(jnp.abs(k_gain))
    return (bound <= FLASH_SAFE_SCORE_BOUND).astype(jnp.int32).reshape(1), bound.astype(F32).reshape(1)


def diff_attention_core(qt, k, vt, gains, lam_params, sub_gain, lam_init):
    b, n_maps, gw, t = qt.shape
    s = k.shape[1]
    heads = vt.shape[1]
    tq = min(512, t)
    tk = _largest_divisor(s, FLASH_MAX_KV_BLOCK, 2 * LANES)
    small = [p.reshape(1, -1) for p in lam_params] + [sub_gain.reshape(-1, 1)]
    bounded, bound = _score_bound(*gains)
    return pl.pallas_call(
        functools.partial(_flash_kernel, lam_init),
        grid_spec=pltpu.PrefetchScalarGridSpec(
            num_scalar_prefetch=2,
            grid=(b, t // tq, s // tk),
            in_specs=[
                pl.BlockSpec((None, n_maps, gw, tq), lambda bi, i, j, *_: (bi, 0, 0, i)),
                pl.BlockSpec((None, tk, gw), lambda bi, i, j, *_: (bi, j, 0)),
                pl.BlockSpec((None, heads, LANES, tk), lambda bi, i, j, *_: (bi, 0, 0, j)),
            ] + [pl.BlockSpec(x.shape, lambda bi, i, j, *_: (0, 0)) for x in small],
            out_specs=pl.BlockSpec((None, tq, gw), lambda bi, i, j, *_: (bi, i, 0)),
            scratch_shapes=[pltpu.VMEM((n_maps, 1, tq), F32), pltpu.VMEM((n_maps, LANES, tq), F32)],
        ),
        out_shape=jax.ShapeDtypeStruct((b, t, gw), F32),
        compiler_params=_cparams("parallel", "parallel", "arbitrary"),
    )(bounded, bound, qt, k, vt, *small)


def diff_attention(dax, dac, params, gw, lam_init, ctx_out):
    q_gain, k_gain, lq1, lk1, lq2, lk2, sub_gain = params
    qtx, kx, vtx = attention_prepare(dax, q_gain, k_gain, gw, rope=True)
    qtc, kc, vtc = attention_prepare(dac, q_gain, k_gain, gw, rope=False)
    k = jnp.concatenate([kx, kc], axis=1)
    vt = jnp.concatenate([vtx, vtc], axis=3)
    lam_params = (lq1, lk1, lq2, lk2)
    gains = (q_gain, k_gain)
    out_x = diff_attention_core(qtx, k, vt, gains, lam_params, sub_gain, lam_init)
    out_c = diff_attention_core(qtc, kc, vtc, gains, lam_params, sub_gain, lam_init) if ctx_out else None
    return out_x, out_c


FT_RADIX = 64


def _dft_cos_sin(n, scale=1.0):
    i = np.arange(n)
    ang = 2.0 * np.pi * ((i[:, None] * i[None, :]) % n) / n
    return np.cos(ang) * scale, np.sin(ang) * scale


def _channel_dft(gw, scale):
    c, s = _dft_cos_sin(HEAD, scale)
    eye = np.eye(gw // HEAD)
    return jnp.asarray(np.concatenate([np.kron(eye, c), np.kron(eye, s)], axis=0).astype(np.float32))


def _fnet_stage1_kernel(z_ref, gr_ref, gi_ref, or_ref, oi_ref):
    for j in range(z_ref.shape[1]):
        x = z_ref[:, j, :]
        or_ref[j] = _dot_f32(gr_ref[j], x)
        oi_ref[j] = _dot_f32(gi_ref[j], x)


def _matmul_f32_kernel(a_ref, b_ref, o_ref):
    o_ref[...] = _dot_f32(a_ref[...], b_ref[...])


def _channel_dft_times(chan, w_f):
    return pl.pallas_call(
        _matmul_f32_kernel,
        out_shape=jax.ShapeDtypeStruct((chan.shape[0], w_f.shape[1]), F32),
    )(chan, w_f)


def _fnet_stage2_kernel(gw, br_ref, bi_ref, rot_ref, chanw_ref, o_ref):
    n1 = br_ref.shape[0]
    p = _dot_f32(rot_ref[...], jnp.concatenate([br_ref[...], bi_ref[...]], axis=0))
    groups = br_ref.shape[1] // gw
    rows = jnp.concatenate(
        [jnp.concatenate([p[:n1, g * gw:(g + 1) * gw], p[n1:, g * gw:(g + 1) * gw]], axis=1) for g in range(groups)],
        axis=0)
    out = _dot_f32(rows, chanw_ref[...])
    for g in range(groups):
        o_ref[:, g * gw:(g + 1) * gw] = out[g * n1:(g + 1) * n1]


def fourier_mix_long(z, w_f):
    b, t, gw = z.shape
    n1 = FT_RADIX
    n2 = t // n1
    k2 = np.arange(n2)[None, :, None]
    n = np.arange(n1)[:, None, None] + n1 * np.arange(n2)[None, None, :]
    ang = 2.0 * np.pi * ((k2 * n) % t) / t
    g_r = jnp.asarray(np.cos(ang).astype(np.float32))
    g_i = jnp.asarray((-np.sin(ang)).astype(np.float32))
    j8 = SUBLANES
    br, bi = pl.pallas_call(
        _fnet_stage1_kernel,
        grid=(b, n1 // j8),
        in_specs=[
            pl.BlockSpec((None, n2, j8, gw), lambda bi_, i: (bi_, 0, i, 0)),
            pl.BlockSpec((j8, n2, n2), lambda bi_, i: (i, 0, 0)),
            pl.BlockSpec((j8, n2, n2), lambda bi_, i: (i, 0, 0)),
        ],
        out_specs=[pl.BlockSpec((None, j8, n2, gw), lambda bi_, i: (bi_, i, 0, 0))] * 2,
        out_shape=[jax.ShapeDtypeStruct((b, n1, n2, gw), F32)] * 2,
        compiler_params=_cparams("parallel", "parallel"),
    )(z.reshape(b, n2, n1, gw), g_r, g_i)
    c64, s64 = _dft_cos_sin(n1)
    rot = jnp.asarray(np.block([[c64, s64], [-s64, c64]]).astype(np.float32))
    chanw = _channel_dft_times(_channel_dft(gw, 1.0 / math.sqrt(t * HEAD)), w_f)
    cols = n2 * gw
    tc = min(2048, cols)
    out = pl.pallas_call(
        functools.partial(_fnet_stage2_kernel, gw),
        grid=(b, cols // tc),
        in_specs=[
            pl.BlockSpec((None, n1, tc), lambda bi_, i: (bi_, 0, i)),
            pl.BlockSpec((None, n1, tc), lambda bi_, i: (bi_, 0, i)),
            _const_spec(rot.shape), _const_spec(chanw.shape),
        ],
        out_specs=pl.BlockSpec((None, n1, tc), lambda bi_, i: (bi_, 0, i)),
        out_shape=jax.ShapeDtypeStruct((b, n1, cols), F32),
        compiler_params=_cparams("parallel", "parallel"),
    )(br.reshape(b, n1, cols), bi.reshape(b, n1, cols), rot, chanw)
    return out.reshape(b, t, gw)


def _fnet_dense_kernel(z_ref, ct_ref, st_ref, chanw_ref, o_ref):
    z = z_ref[...]
    pr = _dot_f32(ct_ref[...], z)
    pi = -_dot_f32(st_ref[...], z)
    o_ref[...] = _dot_f32(jnp.concatenate([pr, pi], axis=1), chanw_ref[...])


def fourier_mix_short(z, w_f):
    b, t, gw = z.shape
    ct, st = _dft_cos_sin(t)
    ct = jnp.asarray(ct.astype(np.float32))
    st = jnp.asarray(st.astype(np.float32))
    chanw = _channel_dft_times(_channel_dft(gw, 1.0 / math.sqrt(t * HEAD)), w_f)
    tok = pl.BlockSpec((None, t, gw), lambda bi: (bi, 0, 0))
    return pl.pallas_call(
        _fnet_dense_kernel,
        grid=(b,),
        in_specs=[tok, _const_spec(ct.shape), _const_spec(st.shape), _const_spec(chanw.shape)],
        out_specs=tok,
        out_shape=jax.ShapeDtypeStruct((b, t, gw), F32),
        compiler_params=_cparams("parallel"),
    )(z, ct, st, chanw)


def fourier_mix(z, w_f):
    t = z.shape[1]
    if t % (FT_RADIX * SUBLANES) == 0 and t // FT_RADIX >= LANES:
        return fourier_mix_long(z, w_f)
    return fourier_mix_short(z, w_f)


def _pool_kernel(t_total, u_ref, prev_ref, next_ref, w_ref, s_ref, o_ref, buf_ref):
    tm, gw = u_ref.shape
    _stage_with_halo(buf_ref, u_ref, prev_ref, next_ref)
    at = lambda off: buf_ref[SUBLANES + off:SUBLANES + off + tm, :]
    u = at(0)
    t = pl.program_id(1) * tm + lax.broadcasted_iota(jnp.int32, (tm, 1), 0)
    lane = lax.broadcasted_iota(jnp.int32, (1, gw), 1)
    group = lane // (gw // len(POOL_WINDOWS))
    mean = jnp.zeros((tm, gw), F32)
    run = jnp.zeros((tm, gw), F32)
    half_prev = 0
    for i, w in enumerate(POOL_WINDOWS):
        half = w // 2
        for off in range(half_prev, half):
            run = run + at(-off - 1) + at(off)
        half_prev = half
        cnt = (jnp.minimum(t + half, t_total) - jnp.maximum(t - half, 0)).astype(F32)
        mean = jnp.where(group == i, run / cnt, mean)
    o_ref[...] = _dot_f32(mean - u, w_ref[...]) * s_ref[...]


def pool_mix(u, w_p, s_p):
    b, t, gw = u.shape
    tm = min(512, t)
    nw, ch = w_p.shape[0], w_p.shape[1]
    w_bd = jnp.zeros((gw, gw), F32)
    for i in range(nw):
        w_bd = w_bd.at[i * ch:(i + 1) * ch, i * ch:(i + 1) * ch].set(w_p[i])
    return pl.pallas_call(
        functools.partial(_pool_kernel, t),
        grid=(b, t // tm),
        in_specs=_halo_specs(tm, t, gw) + [_const_spec((gw, gw)), _const_spec((1, gw))],
        out_specs=pl.BlockSpec((None, tm, gw), lambda bi, i: (bi, i, 0)),
        out_shape=jax.ShapeDtypeStruct((b, t, gw), F32),
        scratch_shapes=[pltpu.VMEM((tm + 2 * SUBLANES, gw), F32)],
        compiler_params=_cparams("parallel", "parallel"),
    )(u, u, u, w_bd, s_p.reshape(1, gw))


def _outproj_kernel(gw, x_ref, g_ref, a_ref, b_ref, f_ref, p_ref, w_ref, o_ref):
    acc = None
    for i, m_ref in enumerate((a_ref, b_ref, f_ref, p_ref)):
        part = jnp.dot(m_ref[...].astype(BF16), w_ref[i * gw:(i + 1) * gw, :], preferred_element_type=F32)
        acc = part if acc is None else acc + part
    o_ref[...] = x_ref[...] + g_ref[...] * acc


def output_projection(x, gate, mixers, w_out_bf16):
    b, t, d = x.shape
    gw = mixers[0].shape[-1]
    tm = min(512, t)
    tok = pl.BlockSpec((None, tm, gw), lambda bi, i: (bi, i, 0))
    xs = pl.BlockSpec((None, tm, d), lambda bi, i: (bi, i, 0))
    return pl.pallas_call(
        functools.partial(_outproj_kernel, gw),
        grid=(b, t // tm),
        in_specs=[xs, pl.BlockSpec((None, 1, d), lambda bi, i: (bi, 0, 0))] + [tok] * 4 + [_const_spec(w_out_bf16.shape)],
        out_specs=xs,
        out_shape=jax.ShapeDtypeStruct((b, t, d), F32),
        compiler_params=_cparams("parallel", "parallel"),
    )(x, gate, *mixers, w_out_bf16)


def _router_kernel(n_exp, x_ref, gain_ref, sc_ref, sh_ref, wr_ref, h_o, aff_o, afft_o):
    h = _modulated_norm(x_ref[...], gain_ref[...], sc_ref[...], sh_ref[...])
    h_o[...] = h
    logits = _dot_f32(h, wr_ref[...])
    lane = lax.broadcasted_iota(jnp.int32, logits.shape, 1)
    logits = jnp.where(lane < n_exp, logits, -1e30)
    e = jnp.exp(logits - jnp.max(logits, axis=-1, keepdims=True))
    aff = e / jnp.sum(e, axis=-1, keepdims=True)
    aff_o[...] = aff
    afft_o[...] = jnp.transpose(aff)[:n_exp, :]


def router(x, gain, scale, shift, w_router):
    b, t, d = x.shape
    n_exp = w_router.shape[1]
    tm = min(512, t)
    wr = jnp.zeros((d, LANES), F32).at[:, :n_exp].set(w_router)
    row = pl.BlockSpec((None, 1, d), lambda bi, i: (bi, 0, 0))
    return pl.pallas_call(
        functools.partial(_router_kernel, n_exp),
        grid=(b, t // tm),
        in_specs=[pl.BlockSpec((None, tm, d), lambda bi, i: (bi, i, 0)), _const_spec((1, d)), row, row,
                  _const_spec(wr.shape)],
        out_specs=[pl.BlockSpec((None, tm, d), lambda bi, i: (bi, i, 0)),
                   pl.BlockSpec((None, tm, LANES), lambda bi, i: (bi, i, 0)),
                   pl.BlockSpec((None, n_exp, tm), lambda bi, i: (bi, 0, i))],
        out_shape=[jax.ShapeDtypeStruct((b, t, d), F32), jax.ShapeDtypeStruct((b, t, LANES), F32),
                   jax.ShapeDtypeStruct((b, n_exp, t), F32)],
        compiler_params=_cparams("parallel", "parallel"),
    )(x, gain.reshape(1, d), scale, shift, wr)


TOPK_EXPONENT_STEPS = 7
TOPK_MANTISSA_STEPS = 44


def _row_cumsum(x_ref, o_ref, upper_ref):
    rows, t = x_ref.shape
    carry = jnp.zeros((rows, 1), F32)
    for g in range(t // LANES):
        sl = slice(g * LANES, (g + 1) * LANES)
        local = jnp.dot(x_ref[:, sl].astype(BF16), upper_ref[...], preferred_element_type=F32) + carry
        o_ref[:, sl] = local
        carry = local[:, LANES - 1:LANES]


def _topk_kernel(cap, aff_ref, upper_ref, idx_o, sel_ref, cs_ref, local_ref, begin_ref, end_ref):
    aff = aff_ref[...]
    rows, t = aff.shape
    capf = float(cap)
    count_ge = lambda thr: jnp.sum(jnp.where(aff >= thr, 1.0, 0.0), axis=-1, keepdims=True)
    hi = jnp.full((rows, 1), 2.0, F32)
    for step in reversed(range(TOPK_EXPONENT_STEPS)):
        cand = hi * (2.0 ** -(2 ** step))
        hi = jnp.where(count_ge(cand) < capf, cand, hi)
    lo = hi * 0.5
    lo = jnp.where(count_ge(lo) >= capf, lo, 0.0)

    def bisect(_, carry):
        lo, hi = carry
        mid = 0.5 * (lo + hi)
        enough = count_ge(mid) >= capf
        return jnp.where(enough, mid, lo), jnp.where(enough, hi, mid)

    lo, hi = lax.fori_loop(0, TOPK_MANTISSA_STEPS, bisect, (lo, hi))
    above = aff >= hi
    need = capf - count_ge(hi)
    sel_ref[...] = jnp.where((aff >= lo) & jnp.logical_not(above), 1.0, 0.0)
    _row_cumsum(sel_ref, cs_ref, upper_ref)
    tied_in = (sel_ref[...] > 0.5) & (cs_ref[...] <= need)
    sel_ref[...] = jnp.where(above | tied_in, 1.0, 0.0)

    groups = t // LANES
    carry = jnp.zeros((rows, 1), F32)
    for g in range(groups):
        local = jnp.dot(sel_ref[:, g * LANES:(g + 1) * LANES].astype(BF16), upper_ref[...],
                        preferred_element_type=F32)
        local_ref[:, g, :] = local
        begin_ref[:, g:g + 1] = carry
        carry = carry + local[:, LANES - 1:LANES]
        end_ref[:, g:g + 1] = carry

    ones_g = jnp.ones((SUBLANES, groups), BF16)
    ones_l = jnp.ones((SUBLANES, LANES), BF16)

    def compact(r, _):
        begin = begin_ref[pl.ds(r, 1), :]
        end = end_ref[pl.ds(r, 1), :]
        local = local_ref[r].astype(BF16)
        for cb in range(0, cap, LANES):
            n = min(LANES, cap - cb)
            slot = (lax.broadcasted_iota(jnp.int32, (n, 1), 0) + cb).astype(F32)
            before = jnp.where(end <= slot, 1.0, 0.0)
            mine = jnp.where((begin <= slot) & (slot < end), 1.0, 0.0)
            rank = slot - jnp.sum(mine * begin, axis=-1, keepdims=True)
            counts = jnp.dot(mine.astype(BF16), local, preferred_element_type=F32)
            reached = jnp.where(counts <= rank, 1.0, 0.0).astype(BF16)
            idx_row = LANES * _dot_nt(ones_g, before) + _dot_nt(ones_l, reached)
            idx_o[r, :, cb:cb + n] = idx_row[0:1].astype(jnp.int32)
        return 0

    lax.fori_loop(0, rows, compact, 0)


def expert_choice_topk(aff_t, cap):
    b, n_exp, t = aff_t.shape
    rows = b * n_exp
    upper = jnp.asarray(np.triu(np.ones((LANES, LANES), np.float32))).astype(BF16)
    idx = pl.pallas_call(
        functools.partial(_topk_kernel, cap),
        grid=(1,),
        in_specs=[_const_spec((rows, t)), _const_spec(upper.shape)],
        out_specs=_const_spec((rows, 1, cap)),
        out_shape=jax.ShapeDtypeStruct((rows, 1, cap), jnp.int32),
        scratch_shapes=[pltpu.VMEM((rows, t), F32), pltpu.VMEM((rows, t), F32),
                        pltpu.VMEM((rows, t // LANES, LANES), F32),
                        pltpu.VMEM((rows, t // LANES), F32), pltpu.VMEM((rows, t // LANES), F32)],
        compiler_params=_cparams("arbitrary"),
    )(aff_t.reshape(rows, t), upper)
    return idx.reshape(rows * cap)


GATHER_UNROLL = 8
SCATTER_BATCH = 4


def _gather_kernel(cap, idx_ref, h_hbm, aff_ref, xs_o, g_o, h_ref, buf_ref, sem):
    b = pl.program_id(0)
    e = pl.program_id(1)
    base = (b * pl.num_programs(1) + e) * cap

    @pl.when(e == 0)
    def _():
        load = pltpu.make_async_copy(h_hbm.at[b], h_ref, sem)
        load.start()
        load.wait()

    def body(i, _):
        for u in range(GATHER_UNROLL):
            c = i * GATHER_UNROLL + u
            row = idx_ref[base + c]
            buf_ref[pl.ds(c, 1), :] = h_ref[pl.ds(row, 1), :]
            g_o[pl.ds(c, 1), :] = aff_ref[pl.ds(row, 1), :]
        return 0

    lax.fori_loop(0, cap // GATHER_UNROLL, body, 0)
    xs_o[...] = buf_ref[...].astype(BF16)


def gather_tokens(idx, h, aff, n_exp, cap):
    b, t, d = h.shape
    return pl.pallas_call(
        functools.partial(_gather_kernel, cap),
        grid_spec=pltpu.PrefetchScalarGridSpec(
            num_scalar_prefetch=1,
            grid=(b, n_exp),
            in_specs=[pl.BlockSpec(memory_space=pl.ANY),
                      pl.BlockSpec((None, t, LANES), lambda bi, e, idx_: (bi, 0, 0))],
            out_specs=[pl.BlockSpec((None, None, cap, d), lambda bi, e, idx_: (bi, e, 0, 0)),
                       pl.BlockSpec((None, None, cap, LANES), lambda bi, e, idx_: (bi, e, 0, 0))],
            scratch_shapes=[pltpu.VMEM((t, d), F32), pltpu.VMEM((cap, d), F32), pltpu.SemaphoreType.DMA(())],
        ),
        out_shape=[jax.ShapeDtypeStruct((b, n_exp, cap, d), BF16),
                   jax.ShapeDtypeStruct((b, n_exp, cap, LANES), F32)],
        compiler_params=_cparams("arbitrary", "arbitrary"),
    )(idx, h, aff)


def _expert_ffn_kernel(n_groups, *refs):
    xs_refs = refs[0:3 * n_groups:3]
    g_refs = refs[1:3 * n_groups:3]
    gate_refs = refs[2:3 * n_groups:3]
    wg_ref, wu_ref, wd_ref = refs[3 * n_groups:3 * n_groups + 3]
    o_refs = refs[3 * n_groups + 3:4 * n_groups + 3]
    acc_refs = refs[4 * n_groups + 3:]
    f = pl.program_id(1)
    wg = wg_ref[...].astype(BF16)
    wu = wu_ref[...].astype(BF16)
    wd = wd_ref[...].astype(BF16)
    for xs_ref, acc_ref in zip(xs_refs, acc_refs):
        for bi in range(xs_ref.shape[0]):
            xb = xs_ref[bi]
            gate = jnp.dot(xb, wg, preferred_element_type=F32)
            up = jnp.dot(xb, wu, preferred_element_type=F32)
            hid = (gate * _sigmoid(gate) * up).astype(BF16)
            part = jnp.dot(hid, wd, preferred_element_type=F32)

            @pl.when(f == 0)
            def _():
                acc_ref[bi] = part

            @pl.when(f > 0)
            def _():
                acc_ref[bi] += part

    @pl.when(f == pl.num_programs(1) - 1)
    def _():
        e = pl.program_id(0)
        for g_ref, gate_ref, acc_ref, o_ref in zip(g_refs, gate_refs, acc_refs, o_refs):
            lane = lax.broadcasted_iota(jnp.int32, g_ref.shape, 2)
            g = jnp.sum(jnp.where(lane == e, g_ref[...], 0.0), axis=-1, keepdims=True)
            o_ref[...] = acc_ref[...] * g * gate_ref[...]


def expert_ffn(groups, layer, w_gate, w_up, w_down):
    _, n_exp, d, f_dim = w_gate.shape
    tf = _largest_divisor(f_dim, 256, LANES)
    args, in_specs, out_specs, out_shapes, scratch = [], [], [], [], []
    for xs, g_rows, gate in groups:
        b, _, cap, _ = xs.shape
        args += [xs, g_rows, gate]
        in_specs += [pl.BlockSpec((b, None, cap, d), lambda e, f: (0, e, 0, 0)),
                     pl.BlockSpec((b, None, cap, LANES), lambda e, f: (0, e, 0, 0)),
                     pl.BlockSpec((b, 1, d), lambda e, f: (0, 0, 0))]
        out_specs.append(pl.BlockSpec((None, b, cap, d), lambda e, f: (e, 0, 0, 0)))
        out_shapes.append(jax.ShapeDtypeStruct((n_exp, b, cap, d), F32))
        scratch.append(pltpu.VMEM((b, cap, d), F32))
    in_specs += [pl.BlockSpec((None, None, d, tf), lambda e, f: (layer, e, 0, f)),
                 pl.BlockSpec((None, None, d, tf), lambda e, f: (layer, e, 0, f)),
                 pl.BlockSpec((None, None, tf, d), lambda e, f: (layer, e, f, 0))]
    return pl.pallas_call(
        functools.partial(_expert_ffn_kernel, len(groups)),
        grid=(n_exp, f_dim // tf),
        in_specs=in_specs,
        out_specs=out_specs,
        out_shape=out_shapes,
        scratch_shapes=scratch,
        compiler_params=_cparams("parallel", "arbitrary"),
    )(*args, w_gate, w_up, w_down)


def _scatter_kernel(cap, idx_ref, eo_ref, x_hbm, o_hbm, acc_ref, sem):
    b = pl.program_id(0)
    e = pl.program_id(1)
    base = (b * pl.num_programs(1) + e) * cap

    @pl.when(e == 0)
    def _():
        load = pltpu.make_async_copy(x_hbm.at[b], acc_ref, sem)
        load.start()
        load.wait()

    def body(i, _):
        c0 = i * SCATTER_BATCH
        rows = [idx_ref[base + c0 + u] for u in range(SCATTER_BATCH)]
        sums = [acc_ref[pl.ds(rows[u], 1), :] + eo_ref[pl.ds(c0 + u, 1), :] for u in range(SCATTER_BATCH)]
        for u in range(SCATTER_BATCH):
            acc_ref[pl.ds(rows[u], 1), :] = sums[u]
        return 0

    lax.fori_loop(0, cap // SCATTER_BATCH, body, 0)

    @pl.when(e == pl.num_programs(1) - 1)
    def _():
        store = pltpu.make_async_copy(acc_ref, o_hbm.at[b], sem)
        store.start()
        store.wait()


def scatter_residual(idx, expert_out, x, cap):
    n_exp, b, _, d = expert_out.shape
    t = x.shape[1]
    return pl.pallas_call(
        functools.partial(_scatter_kernel, cap),
        grid_spec=pltpu.PrefetchScalarGridSpec(
            num_scalar_prefetch=1,
            grid=(b, n_exp),
            in_specs=[pl.BlockSpec((None, None, cap, d), lambda bi, e, idx_: (e, bi, 0, 0)),
                      pl.BlockSpec(memory_space=pl.ANY)],
            out_specs=pl.BlockSpec(memory_space=pl.ANY),
            scratch_shapes=[pltpu.VMEM((t, d), F32), pltpu.SemaphoreType.DMA(())],
        ),
        out_shape=jax.ShapeDtypeStruct((b, t, d), F32),
        compiler_params=_cparams("arbitrary", "arbitrary"),
    )(idx, expert_out, x)


def moe_residual(streams, gain, layer, w_router, w_gate, w_up, w_down):
    n_exp = w_router.shape[1]
    routed = []
    for x, scale, shift, gate in streams:
        cap = CAPACITY_FACTOR * x.shape[1] // n_exp
        h, aff, aff_t = router(x, gain, scale, shift, w_router)
        idx = expert_choice_topk(aff_t, cap)
        xs, g_rows = gather_tokens(idx, h, aff, n_exp, cap)
        routed.append((idx, cap, (xs, g_rows, gate)))
    outs = expert_ffn([g for _, _, g in routed], layer, w_gate, w_up, w_down)
    return [scatter_residual(idx, eo, x, cap) for (idx, cap, _), eo, (x, _, _, _) in zip(routed, outs, streams)]


def kernel(x, c, ctx, c_ctx, mod_w, mod_b, norm1_w, norm2_w, w_in, rw_mu_prev, rw_mu_next, rw_w0, rw_w_up,
           rw_a0, rw_a_up, rw_g_up, rw_k_k, rw_k_a, rw_r_k, rw_ln_w, rw_ln_b, da_q_gain, da_k_gain, da_lq1,
           da_lk1, da_lq2, da_lk2, da_sub_gain, ft_w, pl_w, pl_scale, w_out, moe_router, moe_w_gate, moe_w_up,
           moe_w_down):
    depth, d = norm1_w.shape
    batch = x.shape[0]
    gw = d // N_MIXERS
    c_rows = jnp.zeros((SUBLANES, d), F32).at[:batch].set(c).at[batch].set(c_ctx)
    mod = modulation_vectors(c_rows, mod_w, mod_b)
    w_in_b = w_in.astype(BF16)
    w_out_b = w_out.astype(BF16)
    for l in range(depth):
        ctx_out = l < depth - 1
        lam_init = 0.8 - 0.6 * math.exp(-0.3 * l)
        mx = mod[l, :batch].reshape(batch, 6, 1, d)
        mc = jnp.broadcast_to(mod[l, batch].reshape(1, 6, 1, d), (batch, 6, 1, d))
        rw = (rw_mu_prev[l], rw_mu_next[l], rw_w0[l], rw_w_up[l], rw_a0[l], rw_a_up[l], rw_g_up[l],
              rw_k_k[l], rw_k_a[l], rw_r_k[l], rw_ln_w[l], rw_ln_b[l])
        da = (da_q_gain[l], da_k_gain[l], da_lq1[l], da_lk1[l], da_lq2[l], da_lk2[l], da_sub_gain[l])
        rwx, dax, ftx, plx = input_projection(x, norm1_w[l], mx[:, 1], mx[:, 0], w_in_b[l], gw)
        rwc, dac, ftc, plc = input_projection(ctx, norm1_w[l], mc[:, 1], mc[:, 0], w_in_b[l], gw)
        ax, ac = rwkv_mixer(rwx, rwc, rw, gw, ctx_out)
        bx, bc = diff_attention(dax, dac, da, gw, lam_init, ctx_out)
        fx = fourier_mix(ftx, ft_w[l])
        px = pool_mix(plx, pl_w[l], pl_scale[l])
        x = output_projection(x, mx[:, 2], (ax, bx, fx, px), w_out_b[l])
        streams = [(x, mx[:, 4], mx[:, 3], mx[:, 5])]
        if ctx_out:
            fc = fourier_mix(ftc, ft_w[l])
            pc = pool_mix(plc, pl_w[l], pl_scale[l])
            ctx = output_projection(ctx, mc[:, 2], (ac, bc, fc, pc), w_out_b[l])
            streams.append((ctx, mc[:, 4], mc[:, 3], mc[:, 5]))
        outs = moe_residual(streams, norm2_w[l], l, moe_router[l], moe_w_gate, moe_w_up, moe_w_down)
        x = outs[0]
        if ctx_out:
            ctx = outs[1]
    return x
```

```python
import functools
import math

import numpy as np
import jax
import jax.numpy as jnp
from jax import lax
from jax.experimental import pallas as pl
from jax.experimental.pallas import tpu as pltpu

F32 = jnp.float32
BF16 = jnp.bfloat16
HIGHEST = lax.Precision.HIGHEST

N_MIXERS = 4
HEAD = 64
NORM_EPS = 1e-6
RW_LN_EPS = 64e-5
GRID_W = 64
DA_QK = HEAD // 2
ROPE_BASE = 10000.0
POOL_WINDOWS = (2, 4, 8, 16)
N_EXPERTS = 16
CAPACITY_FACTOR = 2

LANES = 128
SUBLANES = 8
VMEM_LIMIT_BYTES = 56 * 1024 * 1024

RW_CHUNK = 64
RW_CHUNKS_PER_STEP = 8


def _cparams(*sem):
    return pltpu.CompilerParams(dimension_semantics=sem, vmem_limit_bytes=VMEM_LIMIT_BYTES)


def _dot(a, b):
    return jnp.dot(a.astype(BF16), b.astype(BF16), preferred_element_type=F32)


def _dot_f32(a, b):
    return jnp.dot(a, b, precision=HIGHEST, preferred_element_type=F32)


def _dot_tri(a, b):
    return _dot(a, b)


def _dot_nt(a, b, exact=False):
    dn = (((1,), (1,)), ((), ()))
    if exact:
        return lax.dot_general(a, b, dn, precision=HIGHEST, preferred_element_type=F32)
    return lax.dot_general(a.astype(BF16), b.astype(BF16), dn, preferred_element_type=F32)


def _dot_tn(a, b, exact=False):
    dn = (((0,), (0,)), ((), ()))
    if exact:
        return lax.dot_general(a, b, dn, precision=HIGHEST, preferred_element_type=F32)
    return lax.dot_general(a.astype(BF16), b.astype(BF16), dn, preferred_element_type=F32)


def _sigmoid(x):
    return 1.0 / (1.0 + jnp.exp(-x))


def _block_ones(n, blk, value=1.0):
    i = np.arange(n) // blk
    return jnp.asarray((i[:, None] == i[None, :]).astype(np.float32) * value)


def _const_spec(shape):
    nd = len(shape)
    return pl.BlockSpec(shape, lambda *_: (0,) * nd)


def _mod_kernel(c_ref, w_ref, b_ref, o_ref):
    c = c_ref[...]
    o_ref[...] = _dot_f32(c * _sigmoid(c), w_ref[...]) + b_ref[...]


def modulation_vectors(c_rows, mod_w, mod_b):
    depth, d, n = mod_w.shape
    tn = 1536
    return pl.pallas_call(
        _mod_kernel,
        grid=(depth, n // tn),
        in_specs=[
            pl.BlockSpec((SUBLANES, d), lambda l, j: (0, 0)),
            pl.BlockSpec((None, d, tn), lambda l, j: (l, 0, j)),
            pl.BlockSpec((None, 1, tn), lambda l, j: (l, 0, j)),
        ],
        out_specs=pl.BlockSpec((None, SUBLANES, tn), lambda l, j: (l, 0, j)),
        out_shape=jax.ShapeDtypeStruct((depth, SUBLANES, n), F32),
        compiler_params=_cparams("parallel", "parallel"),
    )(c_rows, mod_w, mod_b.reshape(depth, 1, n))


def _modulated_norm(x, gain, scale, shift):
    ms = jnp.mean(x * x, axis=-1, keepdims=True)
    return (x * lax.rsqrt(ms + NORM_EPS) * gain) * (1.0 + scale) + shift


def _inproj_kernel(splits, x_ref, gain_ref, sc_ref, sh_ref, w_ref, *o_refs):
    h = _modulated_norm(x_ref[...], gain_ref[...], sc_ref[...], sh_ref[...]).astype(BF16)
    for (lo, hi), o_ref in zip(splits, o_refs):
        o_ref[...] = jnp.dot(h, w_ref[:, lo:hi], preferred_element_type=F32)


def input_projection(x, gain, scale, shift, w_in_bf16, group_w):
    b, t, d = x.shape
    rw_cols = w_in_bf16.shape[1] - 3 * group_w - 2 * group_w
    cuts = [0, rw_cols, rw_cols + 3 * group_w, rw_cols + 4 * group_w, rw_cols + 5 * group_w]
    splits = tuple((cuts[i], cuts[i + 1]) for i in range(4))
    tm = min(512, t)
    row = pl.BlockSpec((None, 1, d), lambda bi, i: (bi, 0, 0))
    return pl.pallas_call(
        functools.partial(_inproj_kernel, splits),
        grid=(b, t // tm),
        in_specs=[
            pl.BlockSpec((None, tm, d), lambda bi, i: (bi, i, 0)),
            _const_spec((1, d)),
            row, row,
            _const_spec(w_in_bf16.shape),
        ],
        out_specs=[pl.BlockSpec((None, tm, hi - lo), lambda bi, i: (bi, i, 0)) for lo, hi in splits],
        out_shape=[jax.ShapeDtypeStruct((b, t, hi - lo), F32) for lo, hi in splits],
        compiler_params=_cparams("parallel", "parallel"),
    )(x, gain.reshape(1, d), scale, shift, w_in_bf16)


def _halo_specs(tm, t, width):
    nb8 = t // SUBLANES
    r8 = tm // SUBLANES
    return [
        pl.BlockSpec((None, tm, width), lambda b, i: (b, i, 0)),
        pl.BlockSpec((None, SUBLANES, width), lambda b, i: (b, jnp.maximum(i * r8 - 1, 0), 0)),
        pl.BlockSpec((None, SUBLANES, width), lambda b, i: (b, jnp.minimum((i + 1) * r8, nb8 - 1), 0)),
    ]


def _stage_with_halo(buf_ref, main_ref, prev_ref, next_ref):
    tm = main_ref.shape[0]
    i = pl.program_id(1)
    n = pl.num_programs(1)
    buf_ref[SUBLANES:SUBLANES + tm, :] = main_ref[...]
    buf_ref[0:SUBLANES, :] = jnp.where(i > 0, prev_ref[...], 0.0)
    buf_ref[SUBLANES + tm:2 * SUBLANES + tm, :] = jnp.where(i < n - 1, next_ref[...], 0.0)


def _rwkv_prep_kernel(gw, rw_ref, prev_ref, next_ref, mup_ref, mun_ref, kk_ref_, ka_ref, w0_ref, wup_ref,
                      a0_ref, aup_ref, gup_ref, hsum_ref,
                      r_o, k_o, v_o, kk_o, gate_o, lwf_o, kaf_o, kdf_o, lwb_o, kab_o, kdb_o, buf_ref):
    tm = rw_ref.shape[0]
    _stage_with_halo(buf_ref, rw_ref, prev_ref, next_ref)
    p = buf_ref[SUBLANES:SUBLANES + tm, :]
    prev = buf_ref[SUBLANES - 1:SUBLANES - 1 + tm, :]
    nxt = buf_ref[SUBLANES + 1:SUBLANES + 1 + tm, :]
    u = p + mup_ref[...] * (prev - p) + mun_ref[...] * (nxt - p)
    r = u[:, 0:gw]
    k = u[:, gw:2 * gw]
    v = u[:, 2 * gw:3 * gw]
    lora_w = u[:, 3 * gw:3 * gw + LANES]
    lora_a = u[:, 3 * gw + LANES:3 * gw + 2 * LANES]
    g = u[:, 3 * gw + 2 * LANES:3 * gw + 3 * LANES]
    kk = k * kk_ref_[...]
    ss = _dot_f32(kk * kk, hsum_ref[...])
    kk = kk * lax.rsqrt(jnp.maximum(ss, 1e-24))
    zw = _dot(jnp.tanh(lora_w), wup_ref[...]) + w0_ref[...]
    za = _dot(lora_a, aup_ref[...]) + a0_ref[...]
    logw = -_sigmoid(zw) * math.exp(-0.5)
    a = _sigmoid(za)
    r_o[...] = r
    k_o[...] = k
    v_o[...] = v
    kk_o[...] = kk
    gate_o[...] = _dot(_sigmoid(g), gup_ref[...])
    ka = ka_ref[...]
    for d, (lw_o, kka_o, kd_o) in enumerate(((lwf_o, kaf_o, kdf_o), (lwb_o, kab_o, kdb_o))):
        a_d = a[:, d * gw:(d + 1) * gw]
        lw_o[...] = logw[:, d * gw:(d + 1) * gw]
        kka_o[...] = kk * a_d
        kd_o[...] = k * (1.0 + (a_d - 1.0) * ka)


def _blockdiag2(m):
    r, c = m.shape[1:]
    z = jnp.zeros((r, c), m.dtype)
    return jnp.concatenate([jnp.concatenate([m[0], z], 1), jnp.concatenate([z, m[1]], 1)], 0)


def rwkv_prepare(rw, params, gw):
    (mu_prev, mu_next, w0, w_up, a0, a_up, g_up, k_k, k_a, r_k, ln_w, ln_b) = params
    b, t, cols = rw.shape
    tm = min(512, t)
    row = lambda v: v.reshape(1, -1)
    small = [row(mu_prev), row(mu_next), row(k_k), row(k_a), row(w0), _blockdiag2(w_up), row(a0),
             _blockdiag2(a_up), g_up, _block_ones(gw, HEAD)]
    out = jax.ShapeDtypeStruct((b, t, gw), F32)
    return pl.pallas_call(
        functools.partial(_rwkv_prep_kernel, gw),
        grid=(b, t // tm),
        in_specs=_halo_specs(tm, t, cols) + [_const_spec(s.shape) for s in small],
        out_specs=[pl.BlockSpec((None, tm, gw), lambda bi, i: (bi, i, 0))] * 11,
        out_shape=[out] * 11,
        scratch_shapes=[pltpu.VMEM((tm + 2 * SUBLANES, cols), F32)],
        compiler_params=_cparams("parallel", "parallel"),
    )(rw, rw, rw, *small)


RW_BUILD_CHUNKS_PER_STEP = 4


def _rwkv_chunk_kernel(ncs, r_ref, v_ref, kk_ref, lwf_ref, kaf_ref, kdf_ref, lwb_ref, kab_ref, kdb_ref,
                       y0f_o, qf_o, mf_o, g0f_o, y0b_o, qb_o, mb_o, g0b_o):
    L = RW_CHUNK
    rows = lax.broadcasted_iota(jnp.int32, (L, L), 0)
    cols = lax.broadcasted_iota(jnp.int32, (L, L), 1)
    eye = jnp.where(rows == cols, 1.0, 0.0)
    rows2 = lax.broadcasted_iota(jnp.int32, (L, 2 * L), 0)
    cols2 = lax.broadcasted_iota(jnp.int32, (L, 2 * L), 1) % L
    lane = lax.broadcasted_iota(jnp.int32, (1, LANES), 1)
    head_masks = [(lane >= h * HEAD) & (lane < (h + 1) * HEAD) for h in range(2)]
    r2 = lax.broadcasted_iota(jnp.int32, (LANES, LANES), 0)
    c2 = lax.broadcasted_iota(jnp.int32, (LANES, LANES), 1)
    same_head = (r2 // HEAD) == (c2 // HEAD)
    diag = r2 == c2
    directions = ((False, lwf_ref, kaf_ref, kdf_ref, y0f_o, qf_o, mf_o, g0f_o),
                  (True, lwb_ref, kab_ref, kdb_ref, y0b_o, qb_o, mb_o, g0b_o))

    probs = []
    for c in range(ncs):
        rs = slice(c * L, (c + 1) * L)
        for pair in range(2):
            sl = slice(pair * LANES, (pair + 1) * LANES)
            r = r_ref[rs, sl]
            v = v_ref[rs, sl]
            kk = kk_ref[rs, sl]
            for reverse, lw_ref, ka_ref, kd_ref, y0_o, q_o, m_o, g0_o in directions:
                incl = (cols >= rows) if reverse else (cols <= rows)
                strict = (cols > rows) if reverse else (cols < rows)
                incl2 = (cols2 >= rows2) if reverse else (cols2 <= rows2)
                logw = lw_ref[rs, sl]
                kd = kd_ref[rs, sl]
                a = -ka_ref[rs, sl]
                cum = _dot_f32(jnp.where(incl, 1.0, 0.0), logw)
                total = jnp.sum(logw, axis=0, keepdims=True)
                g_inv = jnp.exp(-cum)
                g_tail = jnp.exp(total - cum)
                bd = kk * jnp.exp(cum - logw)
                rd = r * jnp.exp(cum)
                probs.append(dict(
                    incl2=incl2, strict=strict, v=v, bd=bd, rd=rd, total=total,
                    lhs=jnp.concatenate([bd, rd], axis=0), rhs=jnp.concatenate([a * g_inv, kd * g_inv], axis=0),
                    tails=jnp.concatenate([kd * g_tail, a * g_tail], axis=0),
                    outs=(y0_o, q_o, m_o, g0_o), rs=rs, sl=sl, c=c, pair=pair))

    heads = []
    for p in probs:
        for h in range(2):
            gram = _dot_nt(jnp.where(head_masks[h], p["lhs"], 0.0), p["rhs"])
            heads.append(dict(
                p=p, h=h,
                nil=jnp.where(p["strict"], gram[:L, :L], 0.0),
                aak=jnp.where(p["strict"], gram[:L, L:], 0.0),
                ara_ark=jnp.where(p["incl2"], gram[L:, :], 0.0)))

    for hd in heads:
        hd["acc"] = eye + hd["nil"]
        hd["pow"] = hd["nil"]
    span = 2
    while span < L:
        for hd in heads:
            hd["pow"] = _dot_tri(hd["pow"], hd["pow"])
        for hd in heads:
            hd["acc"] = hd["acc"] + _dot_tri(hd["acc"], hd["pow"])
        span *= 2

    for hd in heads:
        hd["aakv"] = _dot(hd["aak"], hd["p"]["v"])
    for hd in heads:
        wp = _dot_tri(hd["acc"], jnp.concatenate([hd["aakv"], hd["p"]["bd"]], axis=1))
        hd["w1"], hd["pm"] = wp[:, :LANES], wp[:, LANES:]
    for hd in heads:
        rhs = jnp.concatenate([jnp.concatenate([hd["w1"], hd["pm"]], axis=1),
                               jnp.concatenate([hd["p"]["v"], jnp.zeros((L, LANES), F32)], axis=1)], axis=0)
        yq = _dot(hd["ara_ark"], rhs)
        hd["y0"], hd["q"] = yq[:, :LANES], yq[:, LANES:] + hd["p"]["rd"]

    for i, p in enumerate(probs):
        h0, h1 = heads[2 * i], heads[2 * i + 1]
        pick = lambda key: jnp.where(head_masks[0], h0[key], h1[key])
        y0_o, q_o, m_o, g0_o = p["outs"]
        y0_o[p["rs"], p["sl"]] = pick("y0")
        q_o[p["rs"], p["sl"]] = pick("q")
        vw = jnp.concatenate([p["v"], pick("w1")], axis=0)
        g0_o[p["c"], p["pair"]] = jnp.where(same_head, _dot_tn(p["tails"], vw), 0.0)
        m_o[p["c"], p["pair"]] = (jnp.where(same_head, _dot_tn(p["tails"][L:], pick("pm")), 0.0)
                                  + jnp.where(diag, jnp.exp(p["total"]), 0.0))


def rwkv_chunks(r, v, kk, lwf, kaf, kdf, lwb, kab, kdb):
    b, t, gw = r.shape
    L = RW_CHUNK
    nc = t // L
    ncs = RW_BUILD_CHUNKS_PER_STEP if nc % RW_BUILD_CHUNKS_PER_STEP == 0 else 1
    tok = pl.BlockSpec((None, ncs * L, gw), lambda bi, i: (bi, i, 0))
    mat = pl.BlockSpec((None, ncs, 2, LANES, LANES), lambda bi, i: (bi, i, 0, 0, 0))
    tok_s = jax.ShapeDtypeStruct((b, t, gw), F32)
    mat_s = jax.ShapeDtypeStruct((b, nc, 2, LANES, LANES), F32)
    return pl.pallas_call(
        functools.partial(_rwkv_chunk_kernel, ncs),
        grid=(b, nc // ncs),
        in_specs=[tok] * 9,
        out_specs=[tok, tok, mat, mat] * 2,
        out_shape=[tok_s, tok_s, mat_s, mat_s] * 2,
        compiler_params=_cparams("parallel", "parallel"),
    )(r, v, kk, lwf, kaf, kdf, lwb, kab, kdb)


def _rwkv_scan_kernel(cps, h0_ref, y0f_ref, qf_ref, mf_ref, g0f_ref, y0b_ref, qb_ref, mb_ref, g0b_ref,
                      yf_o, yb_o, hfin_o, h_ref):
    L = RW_CHUNK
    j = pl.program_id(1)

    @pl.when(j == 0)
    def _():
        h_ref[...] = h0_ref[...]

    for step in range(cps):
        for d, (y0_ref, q_ref, m_ref, g0_ref, y_o) in enumerate(
                ((y0f_ref, qf_ref, mf_ref, g0f_ref, yf_o), (y0b_ref, qb_ref, mb_ref, g0b_ref, yb_o))):
            c = step if d == 0 else cps - 1 - step
            rows = slice(c * L, (c + 1) * L)
            for pair in range(2):
                sl = slice(pair * LANES, (pair + 1) * LANES)
                h = h_ref[d, pair]
                y_o[rows, sl] = y0_ref[rows, sl] + _dot(q_ref[rows, sl], h)
                h_ref[d, pair] = _dot(m_ref[c, pair], h) + g0_ref[c, pair]

    @pl.when(j == pl.num_programs(1) - 1)
    def _():
        hfin_o[...] = h_ref[...]


def rwkv_scan(h0, y0f, qf, mf, g0f, y0b, qb, mb, g0b):
    b, t, gw = y0f.shape
    L = RW_CHUNK
    nc = t // L
    cps = min(RW_CHUNKS_PER_STEP, nc)
    nb = nc // cps
    tm = cps * L
    tok_f = pl.BlockSpec((None, tm, gw), lambda bi, i: (bi, i, 0))
    tok_b = pl.BlockSpec((None, tm, gw), lambda bi, i: (bi, nb - 1 - i, 0))
    mat_f = pl.BlockSpec((None, cps, 2, LANES, LANES), lambda bi, i: (bi, i, 0, 0, 0))
    mat_b = pl.BlockSpec((None, cps, 2, LANES, LANES), lambda bi, i: (bi, nb - 1 - i, 0, 0, 0))
    st = pl.BlockSpec((None, 2, 2, LANES, LANES), lambda bi, i: (bi, 0, 0, 0, 0))
    tok_s = jax.ShapeDtypeStruct((b, t, gw), F32)
    return pl.pallas_call(
        functools.partial(_rwkv_scan_kernel, cps),
        grid=(b, nb),
        in_specs=[st, tok_f, tok_f, mat_f, mat_f, tok_b, tok_b, mat_b, mat_b],
        out_specs=[tok_f, tok_b, st],
        out_shape=[tok_s, tok_s, jax.ShapeDtypeStruct((b, 2, 2, LANES, LANES), F32)],
        scratch_shapes=[pltpu.VMEM((2, 2, LANES, LANES), F32)],
        compiler_params=_cparams("parallel", "arbitrary"),
    )(h0, y0f, qf, mf, g0f, y0b, qb, mb, g0b)


def _rwkv_head_output(yf, yb, r, k, v, gate, r_k, ln_w, ln_b, hmean):
    y = yf + yb
    mu = _dot_f32(y, hmean)
    yc = y - mu
    var = _dot_f32(yc * yc, hmean)
    yn = yc * lax.rsqrt(var + RW_LN_EPS) * ln_w + ln_b
    bonus = _dot_f32(r * k * r_k, hmean) * float(HEAD) * v
    return (yn + bonus) * gate


def rwkv_mixer(rwx, rwc, params, gw, ctx_out):
    sx = rwkv_prepare(rwx, params, gw)
    sc = rwkv_prepare(rwc, params, gw)
    (rx, kx, vx, kkx, gx), dx = sx[:5], sx[5:]
    (rc, kc, vc, kkc, gc), dc = sc[:5], sc[5:]
    cx = rwkv_chunks(rx, vx, kkx, *dx)
    cc = rwkv_chunks(rc, vc, kkc, *dc)
    b = rwx.shape[0]
    h0 = jnp.zeros((b, 2, 2, LANES, LANES), F32)
    ycf, ycb, h_ctx = rwkv_scan(h0, *cc)
    yxf, yxb, _ = rwkv_scan(h_ctx, *cx)
    out_x = (yxf, yxb, rx, kx, vx, gx)
    out_c = (ycf, ycb, rc, kc, vc, gc) if ctx_out else None
    return out_x, out_c


def _rope_tables(n_tokens, reps):
    rows = n_tokens // GRID_W
    row = np.repeat(np.arange(rows), GRID_W).astype(np.float64)
    col = np.tile(np.arange(GRID_W), rows).astype(np.float64)
    n_freq = DA_QK // 4
    inv = ROPE_BASE ** (-np.arange(n_freq, dtype=np.float64) / n_freq)
    ar = row[:, None] * inv
    ac = col[:, None] * inv
    ang = np.concatenate([ar, ar, ac, ac], axis=-1)
    cos = np.tile(np.cos(ang), (1, reps)).astype(np.float32)
    sin = np.tile(np.sin(ang), (1, reps)).astype(np.float32)
    return jnp.asarray(cos), jnp.asarray(sin)


def _attn_prep_kernel(gw, rope, da_ref, qg_ref, kg_ref, gmean_ref, *rest):
    if rope:
        cos_ref, sin_ref, q_o, k_o, v_o = rest
    else:
        q_o, k_o, v_o = rest
    da = da_ref[...]
    gmean = gmean_ref[...]
    lane = lax.broadcasted_iota(jnp.int32, (1, gw), 1)
    first_half = (lane % (DA_QK // 2)) < (DA_QK // 4)

    def norm_rope(x, gain):
        ms = _dot_f32(x * x, gmean)
        y = x * lax.rsqrt(ms + NORM_EPS) * gain
        if rope:
            quarter = DA_QK // 4
            rot = jnp.where(first_half, -pltpu.roll(y, gw - quarter, 1), pltpu.roll(y, quarter, 1))
            y = y * cos_ref[...] + rot * sin_ref[...]
        return y

    tm = da.shape[0]
    q = norm_rope(da[:, 0:gw], qg_ref[...]) * (DA_QK ** -0.5 * LOG2_E)
    q_t = jnp.transpose(q)
    row_map = lax.broadcasted_iota(jnp.int32, (gw, 1), 0) // DA_QK
    for c in range(gw // DA_QK):
        q_o[c] = jnp.where(row_map == c, q_t, 0.0).astype(q_o.dtype)
    k_o[...] = norm_rope(da[:, gw:2 * gw], kg_ref[...]).astype(k_o.dtype)
    v_t = jnp.transpose(da[:, 2 * gw:3 * gw])
    pad = jnp.where(lax.broadcasted_iota(jnp.int32, (LANES - HEAD, tm), 0) == 0, 1.0, 0.0)
    for h in range(gw // HEAD):
        v_o[h] = jnp.concatenate([v_t[h * HEAD:(h + 1) * HEAD], pad], axis=0).astype(v_o.dtype)


def attention_prepare(da, q_gain, k_gain, gw, rope):
    b, t, cols = da.shape
    tm = min(512, t)
    reps = gw // DA_QK
    heads = gw // HEAD
    small = [jnp.tile(q_gain, reps).reshape(1, gw), jnp.tile(k_gain, reps).reshape(1, gw),
             _block_ones(gw, DA_QK, 1.0 / DA_QK)]
    args = [da] + small
    in_specs = [pl.BlockSpec((None, tm, cols), lambda bi, i: (bi, i, 0))] + [_const_spec(s.shape) for s in small]
    if rope:
        cos, sin = _rope_tables(t, reps)
        args += [cos, sin]
        in_specs += [pl.BlockSpec((tm, gw), lambda bi, i: (i, 0))] * 2
    return pl.pallas_call(
        functools.partial(_attn_prep_kernel, gw, rope),
        grid=(b, t // tm),
        in_specs=in_specs,
        out_specs=[pl.BlockSpec((None, reps, gw, tm), lambda bi, i: (bi, 0, 0, i)),
                   pl.BlockSpec((None, tm, gw), lambda bi, i: (bi, i, 0)),
                   pl.BlockSpec((None, heads, LANES, tm), lambda bi, i: (bi, 0, 0, i))],
        out_shape=[jax.ShapeDtypeStruct((b, reps, gw, t), BF16),
                   jax.ShapeDtypeStruct((b, t, gw), BF16),
                   jax.ShapeDtypeStruct((b, heads, LANES, t), BF16)],
        compiler_params=_cparams("parallel", "parallel"),
    )(*args)


LOG2_E = 1.4426950408889634
FLASH_MAX_KV_BLOCK = 2816


def _flash_kernel(lam_init, bounded_ref, bound_ref, qt_ref, k_ref, vt_ref, lq1_ref, lk1_ref, lq2_ref, lk2_ref,
                  sg_ref, o_ref, m_ref, acc_ref):
    j = pl.program_id(2)
    n_maps = qt_ref.shape[0]

    @pl.when(j == 0)
    def _():
        m_ref[...] = jnp.full(m_ref.shape, -1e30, F32)
        acc_ref[...] = jnp.zeros(acc_ref.shape, F32)

    k = k_ref[...]
    bounded = bounded_ref[0] == 1

    @pl.when(bounded)
    def _():
        shift = bound_ref[0]
        for c in range(n_maps):
            s = jnp.dot(k, qt_ref[c], preferred_element_type=F32)
            p = jnp.exp2(s - shift).astype(BF16)
            acc_ref[c] += jnp.dot(vt_ref[c // 2], p, preferred_element_type=F32)

    @pl.when(jnp.logical_not(bounded))
    def _():
        for c in range(n_maps):
            s = jnp.dot(k, qt_ref[c], preferred_element_type=F32)
            m_old = m_ref[c]
            m_new = jnp.maximum(m_old, jnp.max(s, axis=0, keepdims=True))
            p = jnp.exp2(s - m_new).astype(BF16)
            acc_ref[c] = (jnp.exp2(m_old - m_new) * acc_ref[c]
                          + jnp.dot(vt_ref[c // 2], p, preferred_element_type=F32))
            m_ref[c] = m_new

    @pl.when(j == pl.num_programs(2) - 1)
    def _():
        lam = (jnp.exp(jnp.sum(lq1_ref[...] * lk1_ref[...], axis=-1, keepdims=True))
               - jnp.exp(jnp.sum(lq2_ref[...] * lk2_ref[...], axis=-1, keepdims=True)) + lam_init)
        for h in range(n_maps // 2):
            a0 = acc_ref[2 * h]
            a1 = acc_ref[2 * h + 1]
            o = a0[:HEAD] / a0[HEAD:HEAD + 1] - lam * (a1[:HEAD] / a1[HEAD:HEAD + 1])
            ms = jnp.mean(o * o, axis=0, keepdims=True)
            o = o * lax.rsqrt(ms + NORM_EPS) * sg_ref[...] * (1.0 - lam_init)
            o_ref[:, h * HEAD:(h + 1) * HEAD] = jnp.transpose(o)


def _largest_divisor(n, cap, multiple):
    best = None
    for d in range(multiple, cap + 1, multiple):
        if n % d == 0:
            best = d
    return best if best is not None else n


FLASH_SAFE_SCORE_BOUND = 60.0


def _score_bound(q_gain, k_gain):
    bound = (1.02 * DA_QK * DA_QK ** -0.5 * LOG2_E) * jnp.max(jnp.abs(q_gain)) * jnp.max(jnp.abs(k_gain))
    return (bound <= FLASH_SAFE_SCORE_BOUND).astype(jnp.int32).reshape(1), bound.astype(F32).reshape(1)


def diff_attention_core(qt, k, vt, gains, lam_params, sub_gain, lam_init):
    b, n_maps, gw, t = qt.shape
    s = k.shape[1]
    heads = vt.shape[1]
    tq = min(512, t)
    tk = _largest_divisor(s, FLASH_MAX_KV_BLOCK, 2 * LANES)
    small = [p.reshape(1, -1) for p in lam_params] + [sub_gain.reshape(-1, 1)]
    bounded, bound = _score_bound(*gains)
    return pl.pallas_call(
        functools.partial(_flash_kernel, lam_init),
        grid_spec=pltpu.PrefetchScalarGridSpec(
            num_scalar_prefetch=2,
            grid=(b, t // tq, s // tk),
            in_specs=[
                pl.BlockSpec((None, n_maps, gw, tq), lambda bi, i, j, *_: (bi, 0, 0, i)),
                pl.BlockSpec((None, tk, gw), lambda bi, i, j, *_: (bi, j, 0)),
                pl.BlockSpec((None, heads, LANES, tk), lambda bi, i, j, *_: (bi, 0, 0, j)),
            ] + [pl.BlockSpec(x.shape, lambda bi, i, j, *_: (0, 0)) for x in small],
            out_specs=pl.BlockSpec((None, tq, gw), lambda bi, i, j, *_: (bi, i, 0)),
            scratch_shapes=[pltpu.VMEM((n_maps, 1, tq), F32), pltpu.VMEM((n_maps, LANES, tq), F32)],
        ),
        out_shape=jax.ShapeDtypeStruct((b, t, gw), F32),
        compiler_params=_cparams("parallel", "parallel", "arbitrary"),
    )(bounded, bound, qt, k, vt, *small)


def diff_attention(dax, dac, params, gw, lam_init, ctx_out):
    q_gain, k_gain, lq1, lk1, lq2, lk2, sub_gain = params
    qtx, kx, vtx = attention_prepare(dax, q_gain, k_gain, gw, rope=True)
    qtc, kc, vtc = attention_prepare(dac, q_gain, k_gain, gw, rope=False)
    k = jnp.concatenate([kx, kc], axis=1)
    vt = jnp.concatenate([vtx, vtc], axis=3)
    lam_params = (lq1, lk1, lq2, lk2)
    gains = (q_gain, k_gain)
    out_x = diff_attention_core(qtx, k, vt, gains, lam_params, sub_gain, lam_init)
    out_c = diff_attention_core(qtc, kc, vtc, gains, lam_params, sub_gain, lam_init) if ctx_out else None
    return out_x, out_c


FT_RADIX = 64


def _dft_cos_sin(n, scale=1.0):
    i = np.arange(n)
    ang = 2.0 * np.pi * ((i[:, None] * i[None, :]) % n) / n
    return np.cos(ang) * scale, np.sin(ang) * scale


def _channel_dft(gw, scale):
    c, s = _dft_cos_sin(HEAD, scale)
    eye = np.eye(gw // HEAD)
    return jnp.asarray(np.concatenate([np.kron(eye, c), np.kron(eye, s)], axis=0).astype(np.float32))


def _fnet_stage1_kernel(z_ref, gr_ref, gi_ref, or_ref, oi_ref):
    for j in range(z_ref.shape[1]):
        x = z_ref[:, j, :]
        or_ref[j] = _dot_f32(gr_ref[j], x)
        oi_ref[j] = _dot_f32(gi_ref[j], x)


def _matmul_f32_kernel(a_ref, b_ref, o_ref):
    o_ref[...] = _dot_f32(a_ref[...], b_ref[...])


def _channel_dft_times(chan, w_f):
    return pl.pallas_call(
        _matmul_f32_kernel,
        out_shape=jax.ShapeDtypeStruct((chan.shape[0], w_f.shape[1]), F32),
    )(chan, w_f)


def _fnet_stage2_kernel(gw, br_ref, bi_ref, rot_ref, chanw_ref, o_ref):
    n1, groups, _ = br_ref.shape
    x = jnp.concatenate([jnp.concatenate([br_ref[:, g, :] for g in range(groups)], axis=1),
                         jnp.concatenate([bi_ref[:, g, :] for g in range(groups)], axis=1)], axis=0)
    p = _dot_f32(rot_ref[...], x)
    rows = jnp.concatenate(
        [jnp.concatenate([p[:n1, g * gw:(g + 1) * gw], p[n1:, g * gw:(g + 1) * gw]], axis=1) for g in range(groups)],
        axis=0)
    out = _dot_f32(rows, chanw_ref[...])
    for g in range(groups):
        o_ref[:, g, :] = out[g * n1:(g + 1) * n1]


def fourier_mix_long(z, w_f):
    b, t, gw = z.shape
    n1 = FT_RADIX
    n2 = t // n1
    k2 = np.arange(n2)[None, :, None]
    n = np.arange(n1)[:, None, None] + n1 * np.arange(n2)[None, None, :]
    ang = 2.0 * np.pi * ((k2 * n) % t) / t
    g_r = jnp.asarray(np.cos(ang).astype(np.float32))
    g_i = jnp.asarray((-np.sin(ang)).astype(np.float32))
    j8 = SUBLANES
    br, bi = pl.pallas_call(
        _fnet_stage1_kernel,
        grid=(b, n1 // j8),
        in_specs=[
            pl.BlockSpec((None, n2, j8, gw), lambda bi_, i: (bi_, 0, i, 0)),
            pl.BlockSpec((j8, n2, n2), lambda bi_, i: (i, 0, 0)),
            pl.BlockSpec((j8, n2, n2), lambda bi_, i: (i, 0, 0)),
        ],
        out_specs=[pl.BlockSpec((None, j8, n2, gw), lambda bi_, i: (bi_, i, 0, 0))] * 2,
        out_shape=[jax.ShapeDtypeStruct((b, n1, n2, gw), F32)] * 2,
        compiler_params=_cparams("parallel", "parallel"),
    )(z.reshape(b, n2, n1, gw), g_r, g_i)
    c64, s64 = _dft_cos_sin(n1)
    rot = jnp.asarray(np.block([[c64, s64], [-s64, c64]]).astype(np.float32))
    chanw = _channel_dft_times(_channel_dft(gw, 1.0 / math.sqrt(t * HEAD)), w_f)
    blk = pl.BlockSpec((None, n1, j8, gw), lambda bi_, i: (bi_, 0, i, 0))
    out = pl.pallas_call(
        functools.partial(_fnet_stage2_kernel, gw),
        grid=(b, n2 // j8),
        in_specs=[blk, blk, _const_spec(rot.shape), _const_spec(chanw.shape)],
        out_specs=blk,
        out_shape=jax.ShapeDtypeStruct((b, n1, n2, gw), F32),
        compiler_params=_cparams("parallel", "parallel"),
    )(br, bi, rot, chanw)
    return out.reshape(b, t, gw)


def _fnet_dense_kernel(z_ref, ct_ref, st_ref, chanw_ref, o_ref):
    z = z_ref[...]
    pr = _dot_f32(ct_ref[...], z)
    pi = -_dot_f32(st_ref[...], z)
    o_ref[...] = _dot_f32(jnp.concatenate([pr, pi], axis=1), chanw_ref[...])


def fourier_mix_short(z, w_f):
    b, t, gw = z.shape
    ct, st = _dft_cos_sin(t)
    ct = jnp.asarray(ct.astype(np.float32))
    st = jnp.asarray(st.astype(np.float32))
    chanw = _channel_dft_times(_channel_dft(gw, 1.0 / math.sqrt(t * HEAD)), w_f)
    tok = pl.BlockSpec((None, t, gw), lambda bi: (bi, 0, 0))
    return pl.pallas_call(
        _fnet_dense_kernel,
        grid=(b,),
        in_specs=[tok, _const_spec(ct.shape), _const_spec(st.shape), _const_spec(chanw.shape)],
        out_specs=tok,
        out_shape=jax.ShapeDtypeStruct((b, t, gw), F32),
        compiler_params=_cparams("parallel"),
    )(z, ct, st, chanw)


def fourier_mix(z, w_f):
    t = z.shape[1]
    if t % (FT_RADIX * SUBLANES) == 0 and t // FT_RADIX >= LANES:
        return fourier_mix_long(z, w_f)
    return fourier_mix_short(z, w_f)


def _pool_kernel(t_total, u_ref, prev_ref, next_ref, w_ref, s_ref, o_ref, buf_ref):
    tm, gw = u_ref.shape
    _stage_with_halo(buf_ref, u_ref, prev_ref, next_ref)
    at = lambda off: buf_ref[SUBLANES + off:SUBLANES + off + tm, :]
    u = at(0)
    t = pl.program_id(1) * tm + lax.broadcasted_iota(jnp.int32, (tm, 1), 0)
    lane = lax.broadcasted_iota(jnp.int32, (1, gw), 1)
    group = lane // (gw // len(POOL_WINDOWS))
    mean = jnp.zeros((tm, gw), F32)
    run = jnp.zeros((tm, gw), F32)
    half_prev = 0
    for i, w in enumerate(POOL_WINDOWS):
        half = w // 2
        for off in range(half_prev, half):
            run = run + at(-off - 1) + at(off)
        half_prev = half
        cnt = (jnp.minimum(t + half, t_total) - jnp.maximum(t - half, 0)).astype(F32)
        mean = jnp.where(group == i, run / cnt, mean)
    o_ref[...] = _dot_f32(mean - u, w_ref[...]) * s_ref[...]


def pool_mix(u, w_p, s_p):
    b, t, gw = u.shape
    tm = min(512, t)
    nw, ch = w_p.shape[0], w_p.shape[1]
    w_bd = jnp.zeros((gw, gw), F32)
    for i in range(nw):
        w_bd = w_bd.at[i * ch:(i + 1) * ch, i * ch:(i + 1) * ch].set(w_p[i])
    return pl.pallas_call(
        functools.partial(_pool_kernel, t),
        grid=(b, t // tm),
        in_specs=_halo_specs(tm, t, gw) + [_const_spec((gw, gw)), _const_spec((1, gw))],
        out_specs=pl.BlockSpec((None, tm, gw), lambda bi, i: (bi, i, 0)),
        out_shape=jax.ShapeDtypeStruct((b, t, gw), F32),
        scratch_shapes=[pltpu.VMEM((tm + 2 * SUBLANES, gw), F32)],
        compiler_params=_cparams("parallel", "parallel"),
    )(u, u, u, w_bd, s_p.reshape(1, gw))


def _outproj_kernel(gw, x_ref, g_ref, yf_ref, yb_ref, r_ref, k_ref, v_ref, rg_ref, b_ref, f_ref, p_ref,
                    rk_ref, lnw_ref, lnb_ref, hmean_ref, w_ref, o_ref):
    a = _rwkv_head_output(yf_ref[...], yb_ref[...], r_ref[...], k_ref[...], v_ref[...], rg_ref[...],
                          rk_ref[...], lnw_ref[...], lnb_ref[...], hmean_ref[...])
    acc = jnp.dot(a.astype(BF16), w_ref[0:gw, :], preferred_element_type=F32)
    for i, m_ref in enumerate((b_ref, f_ref, p_ref), start=1):
        acc = acc + jnp.dot(m_ref[...].astype(BF16), w_ref[i * gw:(i + 1) * gw, :], preferred_element_type=F32)
    o_ref[...] = x_ref[...] + g_ref[...] * acc


def output_projection(x, gate, rwkv_parts, rwkv_params, mixers, w_out_bf16):
    b, t, d = x.shape
    gw = mixers[0].shape[-1]
    tm = min(512, t)
    tok = pl.BlockSpec((None, tm, gw), lambda bi, i: (bi, i, 0))
    xs = pl.BlockSpec((None, tm, d), lambda bi, i: (bi, i, 0))
    r_k, ln_w, ln_b = rwkv_params
    small = [r_k.reshape(1, gw), ln_w.reshape(1, gw), ln_b.reshape(1, gw), _block_ones(gw, HEAD, 1.0 / HEAD)]
    return pl.pallas_call(
        functools.partial(_outproj_kernel, gw),
        grid=(b, t // tm),
        in_specs=([xs, pl.BlockSpec((None, 1, d), lambda bi, i: (bi, 0, 0))] + [tok] * 9
                  + [_const_spec(s.shape) for s in small] + [_const_spec(w_out_bf16.shape)]),
        out_specs=xs,
        out_shape=jax.ShapeDtypeStruct((b, t, d), F32),
        compiler_params=_cparams("parallel", "parallel"),
    )(x, gate, *rwkv_parts, *mixers, *small, w_out_bf16)


def _router_kernel(n_exp, x_ref, gain_ref, sc_ref, sh_ref, wr_ref, h_o, aff_o, afft_o):
    h = _modulated_norm(x_ref[...], gain_ref[...], sc_ref[...], sh_ref[...])
    h_o[...] = h
    logits = _dot_f32(h, wr_ref[...])
    lane = lax.broadcasted_iota(jnp.int32, logits.shape, 1)
    logits = jnp.where(lane < n_exp, logits, -1e30)
    e = jnp.exp(logits - jnp.max(logits, axis=-1, keepdims=True))
    aff = e / jnp.sum(e, axis=-1, keepdims=True)
    aff_o[...] = aff
    afft_o[...] = jnp.transpose(aff)[:n_exp, :]


def router(x, gain, scale, shift, w_router):
    b, t, d = x.shape
    n_exp = w_router.shape[1]
    tm = min(512, t)
    wr = jnp.zeros((d, LANES), F32).at[:, :n_exp].set(w_router)
    row = pl.BlockSpec((None, 1, d), lambda bi, i: (bi, 0, 0))
    return pl.pallas_call(
        functools.partial(_router_kernel, n_exp),
        grid=(b, t // tm),
        in_specs=[pl.BlockSpec((None, tm, d), lambda bi, i: (bi, i, 0)), _const_spec((1, d)), row, row,
                  _const_spec(wr.shape)],
        out_specs=[pl.BlockSpec((None, tm, d), lambda bi, i: (bi, i, 0)),
                   pl.BlockSpec((None, tm, LANES), lambda bi, i: (bi, i, 0)),
                   pl.BlockSpec((None, n_exp, tm), lambda bi, i: (bi, 0, i))],
        out_shape=[jax.ShapeDtypeStruct((b, t, d), F32), jax.ShapeDtypeStruct((b, t, LANES), F32),
                   jax.ShapeDtypeStruct((b, n_exp, t), F32)],
        compiler_params=_cparams("parallel", "parallel"),
    )(x, gain.reshape(1, d), scale, shift, wr)


TOPK_EXPONENT_STEPS = 7
TOPK_MANTISSA_STEPS = 44


def _row_cumsum(x_ref, o_ref, upper_ref):
    rows, t = x_ref.shape
    carry = jnp.zeros((rows, 1), F32)
    for g in range(t // LANES):
        sl = slice(g * LANES, (g + 1) * LANES)
        local = jnp.dot(x_ref[:, sl].astype(BF16), upper_ref[...], preferred_element_type=F32) + carry
        o_ref[:, sl] = local
        carry = local[:, LANES - 1:LANES]


def _topk_kernel(cap, aff_ref, upper_ref, idx_o, sel_ref, cs_ref, local_ref, begin_ref, end_ref):
    aff = aff_ref[...]
    rows, t = aff.shape
    capf = float(cap)
    count_ge = lambda thr: jnp.sum(jnp.where(aff >= thr, 1.0, 0.0), axis=-1, keepdims=True)
    hi = jnp.full((rows, 1), 2.0, F32)
    for step in reversed(range(TOPK_EXPONENT_STEPS)):
        cand = hi * (2.0 ** -(2 ** step))
        hi = jnp.where(count_ge(cand) < capf, cand, hi)
    lo = hi * 0.5
    lo = jnp.where(count_ge(lo) >= capf, lo, 0.0)

    def bisect(_, carry):
        lo, hi = carry
        mid = 0.5 * (lo + hi)
        enough = count_ge(mid) >= capf
        return jnp.where(enough, mid, lo), jnp.where(enough, hi, mid)

    lo, hi = lax.fori_loop(0, TOPK_MANTISSA_STEPS, bisect, (lo, hi))
    above = aff >= hi
    need = capf - count_ge(hi)
    sel_ref[...] = jnp.where((aff >= lo) & jnp.logical_not(above), 1.0, 0.0)
    _row_cumsum(sel_ref, cs_ref, upper_ref)
    tied_in = (sel_ref[...] > 0.5) & (cs_ref[...] <= need)
    sel_ref[...] = jnp.where(above | tied_in, 1.0, 0.0)

    groups = t // LANES
    carry = jnp.zeros((rows, 1), F32)
    for g in range(groups):
        local = jnp.dot(sel_ref[:, g * LANES:(g + 1) * LANES].astype(BF16), upper_ref[...],
                        preferred_element_type=F32)
        local_ref[:, g, :] = local
        begin_ref[:, g:g + 1] = carry
        carry = carry + local[:, LANES - 1:LANES]
        end_ref[:, g:g + 1] = carry

    ones_g = jnp.ones((SUBLANES, groups), BF16)
    ones_l = jnp.ones((SUBLANES, LANES), BF16)

    def compact(r, _):
        begin = begin_ref[pl.ds(r, 1), :]
        end = end_ref[pl.ds(r, 1), :]
        local = local_ref[r].astype(BF16)
        for cb in range(0, cap, LANES):
            n = min(LANES, cap - cb)
            slot = (lax.broadcasted_iota(jnp.int32, (n, 1), 0) + cb).astype(F32)
            before = jnp.where(end <= slot, 1.0, 0.0)
            mine = jnp.where((begin <= slot) & (slot < end), 1.0, 0.0)
            rank = slot - jnp.sum(mine * begin, axis=-1, keepdims=True)
            counts = jnp.dot(mine.astype(BF16), local, preferred_element_type=F32)
            reached = jnp.where(counts <= rank, 1.0, 0.0).astype(BF16)
            idx_row = LANES * _dot_nt(ones_g, before) + _dot_nt(ones_l, reached)
            idx_o[r, :, cb:cb + n] = idx_row[0:1].astype(jnp.int32)
        return 0

    lax.fori_loop(0, rows, compact, 0)


def expert_choice_topk(aff_t, cap):
    b, n_exp, t = aff_t.shape
    rows = b * n_exp
    upper = jnp.asarray(np.triu(np.ones((LANES, LANES), np.float32))).astype(BF16)
    idx = pl.pallas_call(
        functools.partial(_topk_kernel, cap),
        grid=(1,),
        in_specs=[_const_spec((rows, t)), _const_spec(upper.shape)],
        out_specs=_const_spec((rows, 1, cap)),
        out_shape=jax.ShapeDtypeStruct((rows, 1, cap), jnp.int32),
        scratch_shapes=[pltpu.VMEM((rows, t), F32), pltpu.VMEM((rows, t), F32),
                        pltpu.VMEM((rows, t // LANES, LANES), F32),
                        pltpu.VMEM((rows, t // LANES), F32), pltpu.VMEM((rows, t // LANES), F32)],
        compiler_params=_cparams("arbitrary"),
    )(aff_t.reshape(rows, t), upper)
    return idx.reshape(rows * cap)


GATHER_UNROLL = 8
SCATTER_BATCH = 4


def _gather_kernel(cap, idx_ref, h_hbm, aff_ref, xs_o, g_o, h_ref, buf_ref, sem):
    b = pl.program_id(0)
    e = pl.program_id(1)
    base = (b * pl.num_programs(1) + e) * cap

    @pl.when(e == 0)
    def _():
        load = pltpu.make_async_copy(h_hbm.at[b], h_ref, sem)
        load.start()
        load.wait()

    def body(i, _):
        for u in range(GATHER_UNROLL):
            c = i * GATHER_UNROLL + u
            row = idx_ref[base + c]
            buf_ref[pl.ds(c, 1), :] = h_ref[pl.ds(row, 1), :]
            g_o[pl.ds(c, 1), :] = aff_ref[pl.ds(row, 1), :]
        return 0

    lax.fori_loop(0, cap // GATHER_UNROLL, body, 0)
    xs_o[...] = buf_ref[...].astype(BF16)


def gather_tokens(idx, h, aff, n_exp, cap):
    b, t, d = h.shape
    return pl.pallas_call(
        functools.partial(_gather_kernel, cap),
        grid_spec=pltpu.PrefetchScalarGridSpec(
            num_scalar_prefetch=1,
            grid=(b, n_exp),
            in_specs=[pl.BlockSpec(memory_space=pl.ANY),
                      pl.BlockSpec((None, t, LANES), lambda bi, e, idx_: (bi, 0, 0))],
            out_specs=[pl.BlockSpec((None, None, cap, d), lambda bi, e, idx_: (bi, e, 0, 0)),
                       pl.BlockSpec((None, None, cap, LANES), lambda bi, e, idx_: (bi, e, 0, 0))],
            scratch_shapes=[pltpu.VMEM((t, d), F32), pltpu.VMEM((cap, d), F32), pltpu.SemaphoreType.DMA(())],
        ),
        out_shape=[jax.ShapeDtypeStruct((b, n_exp, cap, d), BF16),
                   jax.ShapeDtypeStruct((b, n_exp, cap, LANES), F32)],
        compiler_params=_cparams("arbitrary", "arbitrary"),
    )(idx, h, aff)


def _expert_ffn_kernel(n_groups, *refs):
    xs_refs = refs[0:3 * n_groups:3]
    g_refs = refs[1:3 * n_groups:3]
    gate_refs = refs[2:3 * n_groups:3]
    wg_ref, wu_ref, wd_ref = refs[3 * n_groups:3 * n_groups + 3]
    o_refs = refs[3 * n_groups + 3:4 * n_groups + 3]
    acc_refs = refs[4 * n_groups + 3:]
    f = pl.program_id(1)
    wg = wg_ref[...].astype(BF16)
    wu = wu_ref[...].astype(BF16)
    wd = wd_ref[...].astype(BF16)
    for xs_ref, acc_ref in zip(xs_refs, acc_refs):
        for bi in range(xs_ref.shape[0]):
            xb = xs_ref[bi]
            gate = jnp.dot(xb, wg, preferred_element_type=F32)
            up = jnp.dot(xb, wu, preferred_element_type=F32)
            hid = (gate * _sigmoid(gate) * up).astype(BF16)
            part = jnp.dot(hid, wd, preferred_element_type=F32)

            @pl.when(f == 0)
            def _():
                acc_ref[bi] = part

            @pl.when(f > 0)
            def _():
                acc_ref[bi] += part

    @pl.when(f == pl.num_programs(1) - 1)
    def _():
        e = pl.program_id(0)
        for g_ref, gate_ref, acc_ref, o_ref in zip(g_refs, gate_refs, acc_refs, o_refs):
            lane = lax.broadcasted_iota(jnp.int32, g_ref.shape, 2)
            g = jnp.sum(jnp.where(lane == e, g_ref[...], 0.0), axis=-1, keepdims=True)
            o_ref[...] = acc_ref[...] * g * gate_ref[...]


def expert_ffn(groups, layer, w_gate, w_up, w_down):
    _, n_exp, d, f_dim = w_gate.shape
    tf = _largest_divisor(f_dim, 256, LANES)
    args, in_specs, out_specs, out_shapes, scratch = [], [], [], [], []
    for xs, g_rows, gate in groups:
        b, _, cap, _ = xs.shape
        args += [xs, g_rows, gate]
        in_specs += [pl.BlockSpec((b, None, cap, d), lambda e, f: (0, e, 0, 0)),
                     pl.BlockSpec((b, None, cap, LANES), lambda e, f: (0, e, 0, 0)),
                     pl.BlockSpec((b, 1, d), lambda e, f: (0, 0, 0))]
        out_specs.append(pl.BlockSpec((None, b, cap, d), lambda e, f: (e, 0, 0, 0)))
        out_shapes.append(jax.ShapeDtypeStruct((n_exp, b, cap, d), F32))
        scratch.append(pltpu.VMEM((b, cap, d), F32))
    in_specs += [pl.BlockSpec((None, None, d, tf), lambda e, f: (layer, e, 0, f)),
                 pl.BlockSpec((None, None, d, tf), lambda e, f: (layer, e, 0, f)),
                 pl.BlockSpec((None, None, tf, d), lambda e, f: (layer, e, f, 0))]
    return pl.pallas_call(
        functools.partial(_expert_ffn_kernel, len(groups)),
        grid=(n_exp, f_dim // tf),
        in_specs=in_specs,
        out_specs=out_specs,
        out_shape=out_shapes,
        scratch_shapes=scratch,
        compiler_params=_cparams("parallel", "arbitrary"),
    )(*args, w_gate, w_up, w_down)


def _scatter_kernel(cap, idx_ref, eo_ref, x_hbm, o_hbm, acc_ref, sem):
    b = pl.program_id(0)
    e = pl.program_id(1)
    base = (b * pl.num_programs(1) + e) * cap

    @pl.when(e == 0)
    def _():
        load = pltpu.make_async_copy(x_hbm.at[b], acc_ref, sem)
        load.start()
        load.wait()

    def body(i, _):
        c0 = i * SCATTER_BATCH
        rows = [idx_ref[base + c0 + u] for u in range(SCATTER_BATCH)]
        sums = [acc_ref[pl.ds(rows[u], 1), :] + eo_ref[pl.ds(c0 + u, 1), :] for u in range(SCATTER_BATCH)]
        for u in range(SCATTER_BATCH):
            acc_ref[pl.ds(rows[u], 1), :] = sums[u]
        return 0

    lax.fori_loop(0, cap // SCATTER_BATCH, body, 0)

    @pl.when(e == pl.num_programs(1) - 1)
    def _():
        store = pltpu.make_async_copy(acc_ref, o_hbm.at[b], sem)
        store.start()
        store.wait()


def scatter_residual(idx, expert_out, x, cap):
    n_exp, b, _, d = expert_out.shape
    t = x.shape[1]
    return pl.pallas_call(
        functools.partial(_scatter_kernel, cap),
        grid_spec=pltpu.PrefetchScalarGridSpec(
            num_scalar_prefetch=1,
            grid=(b, n_exp),
            in_specs=[pl.BlockSpec((None, None, cap, d), lambda bi, e, idx_: (e, bi, 0, 0)),
                      pl.BlockSpec(memory_space=pl.ANY)],
            out_specs=pl.BlockSpec(memory_space=pl.ANY),
            scratch_shapes=[pltpu.VMEM((t, d), F32), pltpu.SemaphoreType.DMA(())],
        ),
        out_shape=jax.ShapeDtypeStruct((b, t, d), F32),
        compiler_params=_cparams("arbitrary", "arbitrary"),
    )(idx, expert_out, x)


def moe_residual(streams, gain, layer, w_router, w_gate, w_up, w_down):
    n_exp = w_router.shape[1]
    routed = []
    for x, scale, shift, gate in streams:
        cap = CAPACITY_FACTOR * x.shape[1] // n_exp
        h, aff, aff_t = router(x, gain, scale, shift, w_router)
        idx = expert_choice_topk(aff_t, cap)
        xs, g_rows = gather_tokens(idx, h, aff, n_exp, cap)
        routed.append((idx, cap, (xs, g_rows, gate)))
    outs = expert_ffn([g for _, _, g in routed], layer, w_gate, w_up, w_down)
    return [scatter_residual(idx, eo, x, cap) for (idx, cap, _), eo, (x, _, _, _) in zip(routed, outs, streams)]


def kernel(x, c, ctx, c_ctx, mod_w, mod_b, norm1_w, norm2_w, w_in, rw_mu_prev, rw_mu_next, rw_w0, rw_w_up,
           rw_a0, rw_a_up, rw_g_up, rw_k_k, rw_k_a, rw_r_k, rw_ln_w, rw_ln_b, da_q_gain, da_k_gain, da_lq1,
           da_lk1, da_lq2, da_lk2, da_sub_gain, ft_w, pl_w, pl_scale, w_out, moe_router, moe_w_gate, moe_w_up,
           moe_w_down):
    depth, d = norm1_w.shape
    batch = x.shape[0]
    gw = d // N_MIXERS
    c_rows = jnp.zeros((SUBLANES, d), F32).at[:batch].set(c).at[batch].set(c_ctx)
    mod = modulation_vectors(c_rows, mod_w, mod_b)
    w_in_b = w_in.astype(BF16)
    w_out_b = w_out.astype(BF16)
    for l in range(depth):
        ctx_out = l < depth - 1
        lam_init = 0.8 - 0.6 * math.exp(-0.3 * l)
        mx = mod[l, :batch].reshape(batch, 6, 1, d)
        mc = jnp.broadcast_to(mod[l, batch].reshape(1, 6, 1, d), (batch, 6, 1, d))
        rw = (rw_mu_prev[l], rw_mu_next[l], rw_w0[l], rw_w_up[l], rw_a0[l], rw_a_up[l], rw_g_up[l],
              rw_k_k[l], rw_k_a[l], rw_r_k[l], rw_ln_w[l], rw_ln_b[l])
        da = (da_q_gain[l], da_k_gain[l], da_lq1[l], da_lk1[l], da_lq2[l], da_lk2[l], da_sub_gain[l])
        rwx, dax, ftx, plx = input_projection(x, norm1_w[l], mx[:, 1], mx[:, 0], w_in_b[l], gw)
        rwc, dac, ftc, plc = input_projection(ctx, norm1_w[l], mc[:, 1], mc[:, 0], w_in_b[l], gw)
        ax, ac = rwkv_mixer(rwx, rwc, rw, gw, ctx_out)
        bx, bc = diff_attention(dax, dac, da, gw, lam_init, ctx_out)
        fx = fourier_mix(ftx, ft_w[l])
        px = pool_mix(plx, pl_w[l], pl_scale[l])
        rw_out = (rw_r_k[l], rw_ln_w[l], rw_ln_b[l])
        x = output_projection(x, mx[:, 2], ax, rw_out, (bx, fx, px), w_out_b[l])
        streams = [(x, mx[:, 4], mx[:, 3], mx[:, 5])]
        if ctx_out:
            fc = fourier_mix(ftc, ft_w[l])
            pc = pool_mix(plc, pl_w[l], pl_scale[l])
            ctx = output_projection(ctx, mc[:, 2], ac, rw_out, (bc, fc, pc), w_out_b[l])
            streams.append((ctx, mc[:, 4], mc[:, 3], mc[:, 5]))
        outs = moe_residual(streams, norm2_w[l], l, moe_router[l], moe_w_gate, moe_w_up, moe_w_down)
        x = outs[0]
        if ctx_out:
            ctx = outs[1]
    return x
```

```python
import functools
import math

import numpy as np
import jax
import jax.numpy as jnp
from jax import lax
from jax.experimental import pallas as pl
from jax.experimental.pallas import tpu as pltpu

F32 = jnp.float32
BF16 = jnp.bfloat16
HIGHEST = lax.Precision.HIGHEST

N_MIXERS = 4
HEAD = 64
NORM_EPS = 1e-6
RW_LN_EPS = 64e-5
GRID_W = 64
DA_QK = HEAD // 2
ROPE_BASE = 10000.0
POOL_WINDOWS = (2, 4, 8, 16)
N_EXPERTS = 16
CAPACITY_FACTOR = 2

LANES = 128
SUBLANES = 8
VMEM_LIMIT_BYTES = 56 * 1024 * 1024

RW_CHUNK = 64
RW_CHUNKS_PER_STEP = 8


def _cparams(*sem):
    return pltpu.CompilerParams(dimension_semantics=sem, vmem_limit_bytes=VMEM_LIMIT_BYTES)


def _dot(a, b):
    return jnp.dot(a.astype(BF16), b.astype(BF16), preferred_element_type=F32)


def _split_bf16(x):
    hi = x.astype(BF16)
    return hi, (x - hi.astype(F32)).astype(BF16)


def _dot_f32(a, b, a_exact=False, b_exact=False):
    dot = lambda x, y: jnp.dot(x, y, preferred_element_type=F32)
    a_hi, a_lo = (a.astype(BF16), None) if a_exact else _split_bf16(a)
    b_hi, b_lo = (b.astype(BF16), None) if b_exact else _split_bf16(b)
    out = dot(a_hi, b_hi)
    if a_lo is not None:
        out = out + dot(a_lo, b_hi)
    if b_lo is not None:
        out = out + dot(a_hi, b_lo)
    return out


def _dot_tri(a, b):
    return _dot(a, b)


def _dot_nt(a, b, exact=False):
    dn = (((1,), (1,)), ((), ()))
    if exact:
        return lax.dot_general(a, b, dn, precision=HIGHEST, preferred_element_type=F32)
    return lax.dot_general(a.astype(BF16), b.astype(BF16), dn, preferred_element_type=F32)


def _dot_tn(a, b, exact=False):
    dn = (((0,), (0,)), ((), ()))
    if exact:
        return lax.dot_general(a, b, dn, precision=HIGHEST, preferred_element_type=F32)
    return lax.dot_general(a.astype(BF16), b.astype(BF16), dn, preferred_element_type=F32)


def _sigmoid(x):
    return 1.0 / (1.0 + jnp.exp(-x))


def _block_ones(n, blk, value=1.0):
    i = np.arange(n) // blk
    return jnp.asarray((i[:, None] == i[None, :]).astype(np.float32) * value)


def _const_spec(shape):
    nd = len(shape)
    return pl.BlockSpec(shape, lambda *_: (0,) * nd)


def _mod_kernel(c_ref, w_ref, b_ref, o_ref):
    c = c_ref[...]
    o_ref[...] = _dot_f32(c * _sigmoid(c), w_ref[...]) + b_ref[...]


def modulation_vectors(c_rows, mod_w, mod_b):
    depth, d, n = mod_w.shape
    tn = 1536
    return pl.pallas_call(
        _mod_kernel,
        grid=(depth, n // tn),
        in_specs=[
            pl.BlockSpec((SUBLANES, d), lambda l, j: (0, 0)),
            pl.BlockSpec((None, d, tn), lambda l, j: (l, 0, j)),
            pl.BlockSpec((None, 1, tn), lambda l, j: (l, 0, j)),
        ],
        out_specs=pl.BlockSpec((None, SUBLANES, tn), lambda l, j: (l, 0, j)),
        out_shape=jax.ShapeDtypeStruct((depth, SUBLANES, n), F32),
        compiler_params=_cparams("parallel", "parallel"),
    )(c_rows, mod_w, mod_b.reshape(depth, 1, n))


def _modulated_norm(x, gain, scale, shift):
    ms = jnp.mean(x * x, axis=-1, keepdims=True)
    return (x * lax.rsqrt(ms + NORM_EPS) * gain) * (1.0 + scale) + shift


def _inproj_kernel(splits, x_ref, gain_ref, sc_ref, sh_ref, w_ref, *o_refs):
    h = _modulated_norm(x_ref[...], gain_ref[...], sc_ref[...], sh_ref[...]).astype(BF16)
    for (lo, hi), o_ref in zip(splits, o_refs):
        o_ref[...] = jnp.dot(h, w_ref[:, lo:hi], preferred_element_type=F32)


def input_projection(x, gain, scale, shift, w_in_bf16, group_w):
    b, t, d = x.shape
    rw_cols = w_in_bf16.shape[1] - 3 * group_w - 2 * group_w
    cuts = [0, rw_cols, rw_cols + 3 * group_w, rw_cols + 4 * group_w, rw_cols + 5 * group_w]
    splits = tuple((cuts[i], cuts[i + 1]) for i in range(4))
    tm = min(512, t)
    row = pl.BlockSpec((None, 1, d), lambda bi, i: (bi, 0, 0))
    return pl.pallas_call(
        functools.partial(_inproj_kernel, splits),
        grid=(b, t // tm),
        in_specs=[
            pl.BlockSpec((None, tm, d), lambda bi, i: (bi, i, 0)),
            _const_spec((1, d)),
            row, row,
            _const_spec(w_in_bf16.shape),
        ],
        out_specs=[pl.BlockSpec((None, tm, hi - lo), lambda bi, i: (bi, i, 0)) for lo, hi in splits],
        out_shape=[jax.ShapeDtypeStruct((b, t, hi - lo), F32) for lo, hi in splits],
        compiler_params=_cparams("parallel", "parallel"),
    )(x, gain.reshape(1, d), scale, shift, w_in_bf16)


def _halo_specs(tm, t, width):
    nb8 = t // SUBLANES
    r8 = tm // SUBLANES
    return [
        pl.BlockSpec((None, tm, width), lambda b, i: (b, i, 0)),
        pl.BlockSpec((None, SUBLANES, width), lambda b, i: (b, jnp.maximum(i * r8 - 1, 0), 0)),
        pl.BlockSpec((None, SUBLANES, width), lambda b, i: (b, jnp.minimum((i + 1) * r8, nb8 - 1), 0)),
    ]


def _stage_with_halo(buf_ref, main_ref, prev_ref, next_ref):
    tm = main_ref.shape[0]
    i = pl.program_id(1)
    n = pl.num_programs(1)
    buf_ref[SUBLANES:SUBLANES + tm, :] = main_ref[...]
    buf_ref[0:SUBLANES, :] = jnp.where(i > 0, prev_ref[...], 0.0)
    buf_ref[SUBLANES + tm:2 * SUBLANES + tm, :] = jnp.where(i < n - 1, next_ref[...], 0.0)


def _rwkv_prep_kernel(gw, rw_ref, prev_ref, next_ref, mup_ref, mun_ref, kk_ref_, ka_ref, w0_ref, wup_ref,
                      a0_ref, aup_ref, gup_ref, hsum_ref,
                      r_o, k_o, v_o, kk_o, gate_o, lwf_o, kaf_o, kdf_o, lwb_o, kab_o, kdb_o, buf_ref):
    tm = rw_ref.shape[0]
    _stage_with_halo(buf_ref, rw_ref, prev_ref, next_ref)
    p = buf_ref[SUBLANES:SUBLANES + tm, :]
    prev = buf_ref[SUBLANES - 1:SUBLANES - 1 + tm, :]
    nxt = buf_ref[SUBLANES + 1:SUBLANES + 1 + tm, :]
    u = p + mup_ref[...] * (prev - p) + mun_ref[...] * (nxt - p)
    r = u[:, 0:gw]
    k = u[:, gw:2 * gw]
    v = u[:, 2 * gw:3 * gw]
    lora_w = u[:, 3 * gw:3 * gw + LANES]
    lora_a = u[:, 3 * gw + LANES:3 * gw + 2 * LANES]
    g = u[:, 3 * gw + 2 * LANES:3 * gw + 3 * LANES]
    kk = k * kk_ref_[...]
    ss = _dot_f32(kk * kk, hsum_ref[...], b_exact=True)
    kk = kk * lax.rsqrt(jnp.maximum(ss, 1e-24))
    zw = _dot(jnp.tanh(lora_w), wup_ref[...]) + w0_ref[...]
    za = _dot(lora_a, aup_ref[...]) + a0_ref[...]
    logw = -_sigmoid(zw) * math.exp(-0.5)
    a = _sigmoid(za)
    r_o[...] = r
    k_o[...] = k
    v_o[...] = v
    kk_o[...] = kk
    gate_o[...] = _dot(_sigmoid(g), gup_ref[...])
    ka = ka_ref[...]
    for d, (lw_o, kka_o, kd_o) in enumerate(((lwf_o, kaf_o, kdf_o), (lwb_o, kab_o, kdb_o))):
        a_d = a[:, d * gw:(d + 1) * gw]
        lw_o[...] = logw[:, d * gw:(d + 1) * gw]
        kka_o[...] = kk * a_d
        kd_o[...] = k * (1.0 + (a_d - 1.0) * ka)


def _blockdiag2(m):
    r, c = m.shape[1:]
    z = jnp.zeros((r, c), m.dtype)
    return jnp.concatenate([jnp.concatenate([m[0], z], 1), jnp.concatenate([z, m[1]], 1)], 0)


def rwkv_prepare(rw, params, gw):
    (mu_prev, mu_next, w0, w_up, a0, a_up, g_up, k_k, k_a, r_k, ln_w, ln_b) = params
    b, t, cols = rw.shape
    tm = min(512, t)
    row = lambda v: v.reshape(1, -1)
    small = [row(mu_prev), row(mu_next), row(k_k), row(k_a), row(w0), _blockdiag2(w_up), row(a0),
             _blockdiag2(a_up), g_up, _block_ones(gw, HEAD)]
    out = jax.ShapeDtypeStruct((b, t, gw), F32)
    return pl.pallas_call(
        functools.partial(_rwkv_prep_kernel, gw),
        grid=(b, t // tm),
        in_specs=_halo_specs(tm, t, cols) + [_const_spec(s.shape) for s in small],
        out_specs=[pl.BlockSpec((None, tm, gw), lambda bi, i: (bi, i, 0))] * 11,
        out_shape=[out] * 11,
        scratch_shapes=[pltpu.VMEM((tm + 2 * SUBLANES, cols), F32)],
        compiler_params=_cparams("parallel", "parallel"),
    )(rw, rw, rw, *small)


RW_BUILD_CHUNKS_PER_STEP = 4


def _rwkv_chunk_kernel(ncs, r_ref, v_ref, kk_ref, lwf_ref, kaf_ref, kdf_ref, lwb_ref, kab_ref, kdb_ref,
                       y0f_o, qf_o, mf_o, g0f_o, y0b_o, qb_o, mb_o, g0b_o):
    L = RW_CHUNK
    rows = lax.broadcasted_iota(jnp.int32, (L, L), 0)
    cols = lax.broadcasted_iota(jnp.int32, (L, L), 1)
    eye = jnp.where(rows == cols, 1.0, 0.0)
    rows2 = lax.broadcasted_iota(jnp.int32, (L, 2 * L), 0)
    cols2 = lax.broadcasted_iota(jnp.int32, (L, 2 * L), 1) % L
    lane = lax.broadcasted_iota(jnp.int32, (1, LANES), 1)
    head_masks = [(lane >= h * HEAD) & (lane < (h + 1) * HEAD) for h in range(2)]
    r2 = lax.broadcasted_iota(jnp.int32, (LANES, LANES), 0)
    c2 = lax.broadcasted_iota(jnp.int32, (LANES, LANES), 1)
    same_head = (r2 // HEAD) == (c2 // HEAD)
    diag = r2 == c2
    directions = ((False, lwf_ref, kaf_ref, kdf_ref, y0f_o, qf_o, mf_o, g0f_o),
                  (True, lwb_ref, kab_ref, kdb_ref, y0b_o, qb_o, mb_o, g0b_o))

    probs = []
    for c in range(ncs):
        rs = slice(c * L, (c + 1) * L)
        for pair in range(2):
            sl = slice(pair * LANES, (pair + 1) * LANES)
            r = r_ref[rs, sl]
            v = v_ref[rs, sl]
            kk = kk_ref[rs, sl]
            for reverse, lw_ref, ka_ref, kd_ref, y0_o, q_o, m_o, g0_o in directions:
                incl = (cols >= rows) if reverse else (cols <= rows)
                strict = (cols > rows) if reverse else (cols < rows)
                incl2 = (cols2 >= rows2) if reverse else (cols2 <= rows2)
                logw = lw_ref[rs, sl]
                kd = kd_ref[rs, sl]
                a = -ka_ref[rs, sl]
                cum = _dot_f32(jnp.where(incl, 1.0, 0.0), logw, a_exact=True)
                total = jnp.sum(logw, axis=0, keepdims=True)
                g_inv = jnp.exp(-cum)
                g_tail = jnp.exp(total - cum)
                bd = kk * jnp.exp(cum - logw)
                rd = r * jnp.exp(cum)
                probs.append(dict(
                    incl2=incl2, strict=strict, v=v, bd=bd, rd=rd, total=total,
                    lhs=jnp.concatenate([bd, rd], axis=0), rhs=jnp.concatenate([a * g_inv, kd * g_inv], axis=0),
                    tails=jnp.concatenate([kd * g_tail, a * g_tail], axis=0),
                    outs=(y0_o, q_o, m_o, g0_o), rs=rs, sl=sl, c=c, pair=pair))

    heads = []
    for p in probs:
        for h in range(2):
            gram = _dot_nt(jnp.where(head_masks[h], p["lhs"], 0.0), p["rhs"])
            heads.append(dict(
                p=p, h=h,
                nil=jnp.where(p["strict"], gram[:L, :L], 0.0),
                aak=jnp.where(p["strict"], gram[:L, L:], 0.0),
                ara_ark=jnp.where(p["incl2"], gram[L:, :], 0.0)))

    for hd in heads:
        hd["acc"] = eye + hd["nil"]
        hd["pow"] = hd["nil"]
    span = 2
    while span < L:
        for hd in heads:
            hd["pow"] = _dot_tri(hd["pow"], hd["pow"])
        for hd in heads:
            hd["acc"] = hd["acc"] + _dot_tri(hd["acc"], hd["pow"])
        span *= 2

    for hd in heads:
        hd["aakv"] = _dot(hd["aak"], hd["p"]["v"])
    for hd in heads:
        wp = _dot_tri(hd["acc"], jnp.concatenate([hd["aakv"], hd["p"]["bd"]], axis=1))
        hd["w1"], hd["pm"] = wp[:, :LANES], wp[:, LANES:]
    for hd in heads:
        rhs = jnp.concatenate([jnp.concatenate([hd["w1"], hd["pm"]], axis=1),
                               jnp.concatenate([hd["p"]["v"], jnp.zeros((L, LANES), F32)], axis=1)], axis=0)
        yq = _dot(hd["ara_ark"], rhs)
        hd["y0"], hd["q"] = yq[:, :LANES], yq[:, LANES:] + hd["p"]["rd"]

    for i, p in enumerate(probs):
        h0, h1 = heads[2 * i], heads[2 * i + 1]
        pick = lambda key: jnp.where(head_masks[0], h0[key], h1[key])
        y0_o, q_o, m_o, g0_o = p["outs"]
        y0_o[p["rs"], p["sl"]] = pick("y0")
        q_o[p["rs"], p["sl"]] = pick("q")
        vw = jnp.concatenate([p["v"], pick("w1")], axis=0)
        g0_o[p["c"], p["pair"]] = jnp.where(same_head, _dot_tn(p["tails"], vw), 0.0)
        m_o[p["c"], p["pair"]] = (jnp.where(same_head, _dot_tn(p["tails"][L:], pick("pm")), 0.0)
                                  + jnp.where(diag, jnp.exp(p["total"]), 0.0))


def rwkv_chunks(r, v, kk, lwf, kaf, kdf, lwb, kab, kdb):
    b, t, gw = r.shape
    L = RW_CHUNK
    nc = t // L
    ncs = RW_BUILD_CHUNKS_PER_STEP if nc % RW_BUILD_CHUNKS_PER_STEP == 0 else 1
    tok = pl.BlockSpec((None, ncs * L, gw), lambda bi, i: (bi, i, 0))
    mat = pl.BlockSpec((None, ncs, 2, LANES, LANES), lambda bi, i: (bi, i, 0, 0, 0))
    tok_s = jax.ShapeDtypeStruct((b, t, gw), F32)
    mat_s = jax.ShapeDtypeStruct((b, nc, 2, LANES, LANES), F32)
    return pl.pallas_call(
        functools.partial(_rwkv_chunk_kernel, ncs),
        grid=(b, nc // ncs),
        in_specs=[tok] * 9,
        out_specs=[tok, tok, mat, mat] * 2,
        out_shape=[tok_s, tok_s, mat_s, mat_s] * 2,
        compiler_params=_cparams("parallel", "parallel"),
    )(r, v, kk, lwf, kaf, kdf, lwb, kab, kdb)


def _rwkv_scan_kernel(cps, h0_ref, y0f_ref, qf_ref, mf_ref, g0f_ref, y0b_ref, qb_ref, mb_ref, g0b_ref,
                      yf_o, yb_o, hfin_o, h_ref):
    L = RW_CHUNK
    j = pl.program_id(1)

    @pl.when(j == 0)
    def _():
        h_ref[...] = h0_ref[...]

    for step in range(cps):
        for d, (y0_ref, q_ref, m_ref, g0_ref, y_o) in enumerate(
                ((y0f_ref, qf_ref, mf_ref, g0f_ref, yf_o), (y0b_ref, qb_ref, mb_ref, g0b_ref, yb_o))):
            c = step if d == 0 else cps - 1 - step
            rows = slice(c * L, (c + 1) * L)
            for pair in range(2):
                sl = slice(pair * LANES, (pair + 1) * LANES)
                h = h_ref[d, pair]
                y_o[rows, sl] = y0_ref[rows, sl] + _dot(q_ref[rows, sl], h)
                h_ref[d, pair] = _dot(m_ref[c, pair], h) + g0_ref[c, pair]

    @pl.when(j == pl.num_programs(1) - 1)
    def _():
        hfin_o[...] = h_ref[...]


def rwkv_scan(h0, y0f, qf, mf, g0f, y0b, qb, mb, g0b):
    b, t, gw = y0f.shape
    L = RW_CHUNK
    nc = t // L
    cps = min(RW_CHUNKS_PER_STEP, nc)
    nb = nc // cps
    tm = cps * L
    tok_f = pl.BlockSpec((None, tm, gw), lambda bi, i: (bi, i, 0))
    tok_b = pl.BlockSpec((None, tm, gw), lambda bi, i: (bi, nb - 1 - i, 0))
    mat_f = pl.BlockSpec((None, cps, 2, LANES, LANES), lambda bi, i: (bi, i, 0, 0, 0))
    mat_b = pl.BlockSpec((None, cps, 2, LANES, LANES), lambda bi, i: (bi, nb - 1 - i, 0, 0, 0))
    st = pl.BlockSpec((None, 2, 2, LANES, LANES), lambda bi, i: (bi, 0, 0, 0, 0))
    tok_s = jax.ShapeDtypeStruct((b, t, gw), F32)
    return pl.pallas_call(
        functools.partial(_rwkv_scan_kernel, cps),
        grid=(b, nb),
        in_specs=[st, tok_f, tok_f, mat_f, mat_f, tok_b, tok_b, mat_b, mat_b],
        out_specs=[tok_f, tok_b, st],
        out_shape=[tok_s, tok_s, jax.ShapeDtypeStruct((b, 2, 2, LANES, LANES), F32)],
        scratch_shapes=[pltpu.VMEM((2, 2, LANES, LANES), F32)],
        compiler_params=_cparams("parallel", "arbitrary"),
    )(h0, y0f, qf, mf, g0f, y0b, qb, mb, g0b)


def _rwkv_head_output(yf, yb, r, k, v, gate, r_k, ln_w, ln_b, hmean):
    y = yf + yb
    mu = _dot_f32(y, hmean, b_exact=True)
    yc = y - mu
    var = _dot_f32(yc * yc, hmean, b_exact=True)
    yn = yc * lax.rsqrt(var + RW_LN_EPS) * ln_w + ln_b
    bonus = _dot_f32(r * k * r_k, hmean, b_exact=True) * float(HEAD) * v
    return (yn + bonus) * gate


def rwkv_mixer(rwx, rwc, params, gw, ctx_out):
    sx = rwkv_prepare(rwx, params, gw)
    sc = rwkv_prepare(rwc, params, gw)
    (rx, kx, vx, kkx, gx), dx = sx[:5], sx[5:]
    (rc, kc, vc, kkc, gc), dc = sc[:5], sc[5:]
    cx = rwkv_chunks(rx, vx, kkx, *dx)
    cc = rwkv_chunks(rc, vc, kkc, *dc)
    b = rwx.shape[0]
    h0 = jnp.zeros((b, 2, 2, LANES, LANES), F32)
    ycf, ycb, h_ctx = rwkv_scan(h0, *cc)
    yxf, yxb, _ = rwkv_scan(h_ctx, *cx)
    out_x = (yxf, yxb, rx, kx, vx, gx)
    out_c = (ycf, ycb, rc, kc, vc, gc) if ctx_out else None
    return out_x, out_c


def _rope_tables(n_tokens, reps):
    rows = n_tokens // GRID_W
    row = np.repeat(np.arange(rows), GRID_W).astype(np.float64)
    col = np.tile(np.arange(GRID_W), rows).astype(np.float64)
    n_freq = DA_QK // 4
    inv = ROPE_BASE ** (-np.arange(n_freq, dtype=np.float64) / n_freq)
    ar = row[:, None] * inv
    ac = col[:, None] * inv
    ang = np.concatenate([ar, ar, ac, ac], axis=-1)
    cos = np.tile(np.cos(ang), (1, reps)).astype(np.float32)
    sin = np.tile(np.sin(ang), (1, reps)).astype(np.float32)
    return jnp.asarray(cos), jnp.asarray(sin)


def _attn_prep_kernel(gw, rope, da_ref, qg_ref, kg_ref, gmean_ref, *rest):
    if rope:
        cos_ref, sin_ref, q_o, k_o, v_o = rest
    else:
        q_o, k_o, v_o = rest
    da = da_ref[...]
    gmean = gmean_ref[...]
    lane = lax.broadcasted_iota(jnp.int32, (1, gw), 1)
    first_half = (lane % (DA_QK // 2)) < (DA_QK // 4)

    def norm_rope(x, gain):
        ms = _dot_f32(x * x, gmean, b_exact=True)
        y = x * lax.rsqrt(ms + NORM_EPS) * gain
        if rope:
            quarter = DA_QK // 4
            rot = jnp.where(first_half, -pltpu.roll(y, gw - quarter, 1), pltpu.roll(y, quarter, 1))
            y = y * cos_ref[...] + rot * sin_ref[...]
        return y

    tm = da.shape[0]
    q = norm_rope(da[:, 0:gw], qg_ref[...]) * (DA_QK ** -0.5 * LOG2_E)
    q_t = jnp.transpose(q)
    row_map = lax.broadcasted_iota(jnp.int32, (gw, 1), 0) // DA_QK
    for c in range(gw // DA_QK):
        q_o[c] = jnp.where(row_map == c, q_t, 0.0).astype(q_o.dtype)
    k_o[...] = norm_rope(da[:, gw:2 * gw], kg_ref[...]).astype(k_o.dtype)
    v_t = jnp.transpose(da[:, 2 * gw:3 * gw])
    pad = jnp.where(lax.broadcasted_iota(jnp.int32, (LANES - HEAD, tm), 0) == 0, 1.0, 0.0)
    for h in range(gw // HEAD):
        v_o[h] = jnp.concatenate([v_t[h * HEAD:(h + 1) * HEAD], pad], axis=0).astype(v_o.dtype)


def attention_prepare(da, q_gain, k_gain, gw, rope):
    b, t, cols = da.shape
    tm = min(512, t)
    reps = gw // DA_QK
    heads = gw // HEAD
    small = [jnp.tile(q_gain, reps).reshape(1, gw), jnp.tile(k_gain, reps).reshape(1, gw),
             _block_ones(gw, DA_QK, 1.0 / DA_QK)]
    args = [da] + small
    in_specs = [pl.BlockSpec((None, tm, cols), lambda bi, i: (bi, i, 0))] + [_const_spec(s.shape) for s in small]
    if rope:
        cos, sin = _rope_tables(t, reps)
        args += [cos, sin]
        in_specs += [pl.BlockSpec((tm, gw), lambda bi, i: (i, 0))] * 2
    return pl.pallas_call(
        functools.partial(_attn_prep_kernel, gw, rope),
        grid=(b, t // tm),
        in_specs=in_specs,
        out_specs=[pl.BlockSpec((None, reps, gw, tm), lambda bi, i: (bi, 0, 0, i)),
                   pl.BlockSpec((None, tm, gw), lambda bi, i: (bi, i, 0)),
                   pl.BlockSpec((None, heads, LANES, tm), lambda bi, i: (bi, 0, 0, i))],
        out_shape=[jax.ShapeDtypeStruct((b, reps, gw, t), BF16),
                   jax.ShapeDtypeStruct((b, t, gw), BF16),
                   jax.ShapeDtypeStruct((b, heads, LANES, t), BF16)],
        compiler_params=_cparams("parallel", "parallel"),
    )(*args)


LOG2_E = 1.4426950408889634
FLASH_MAX_KV_BLOCK = 2816


def _flash_kernel(lam_init, bounded_ref, bound_ref, qt_ref, k_ref, vt_ref, lq1_ref, lk1_ref, lq2_ref, lk2_ref,
                  sg_ref, o_ref, m_ref, acc_ref):
    j = pl.program_id(2)
    n_maps = qt_ref.shape[0]

    @pl.when(j == 0)
    def _():
        m_ref[...] = jnp.full(m_ref.shape, -1e30, F32)
        acc_ref[...] = jnp.zeros(acc_ref.shape, F32)

    k = k_ref[...]
    bounded = bounded_ref[0] == 1

    @pl.when(bounded)
    def _():
        shift = bound_ref[0]
        for c in range(n_maps):
            s = jnp.dot(k, qt_ref[c], preferred_element_type=F32)
            p = jnp.exp2(s - shift).astype(BF16)
            acc_ref[c] += jnp.dot(vt_ref[c // 2], p, preferred_element_type=F32)

    @pl.when(jnp.logical_not(bounded))
    def _():
        for c in range(n_maps):
            s = jnp.dot(k, qt_ref[c], preferred_element_type=F32)
            m_old = m_ref[c]
            m_new = jnp.maximum(m_old, jnp.max(s, axis=0, keepdims=True))
            p = jnp.exp2(s - m_new).astype(BF16)
            acc_ref[c] = (jnp.exp2(m_old - m_new) * acc_ref[c]
                          + jnp.dot(vt_ref[c // 2], p, preferred_element_type=F32))
            m_ref[c] = m_new

    @pl.when(j == pl.num_programs(2) - 1)
    def _():
        lam = (jnp.exp(jnp.sum(lq1_ref[...] * lk1_ref[...], axis=-1, keepdims=True))
               - jnp.exp(jnp.sum(lq2_ref[...] * lk2_ref[...], axis=-1, keepdims=True)) + lam_init)
        for h in range(n_maps // 2):
            a0 = acc_ref[2 * h]
            a1 = acc_ref[2 * h + 1]
            o = a0[:HEAD] / a0[HEAD:HEAD + 1] - lam * (a1[:HEAD] / a1[HEAD:HEAD + 1])
            ms = jnp.mean(o * o, axis=0, keepdims=True)
            o = o * lax.rsqrt(ms + NORM_EPS) * sg_ref[...] * (1.0 - lam_init)
            o_ref[:, h * HEAD:(h + 1) * HEAD] = jnp.transpose(o)


def _largest_divisor(n, cap, multiple):
    best = None
    for d in range(multiple, cap + 1, multiple):
        if n % d == 0:
            best = d
    return best if best is not None else n


FLASH_SAFE_SCORE_BOUND = 60.0


def _score_bound(q_gain, k_gain):
    bound = (1.02 * DA_QK * DA_QK ** -0.5 * LOG2_E) * jnp.max(jnp.abs(q_gain)) * jnp.max(jnp.abs(k_gain))
    return (bound <= FLASH_SAFE_SCORE_BOUND).astype(jnp.int32).reshape(1), bound.astype(F32).reshape(1)


def diff_attention_core(qt, k, vt, gains, lam_params, sub_gain, lam_init):
    b, n_maps, gw, t = qt.shape
    s = k.shape[1]
    heads = vt.shape[1]
    tq = min(512, t)
    tk = _largest_divisor(s, FLASH_MAX_KV_BLOCK, 2 * LANES)
    small = [p.reshape(1, -1) for p in lam_params] + [sub_gain.reshape(-1, 1)]
    bounded, bound = _score_bound(*gains)
    return pl.pallas_call(
        functools.partial(_flash_kernel, lam_init),
        grid_spec=pltpu.PrefetchScalarGridSpec(
            num_scalar_prefetch=2,
            grid=(b, t // tq, s // tk),
            in_specs=[
                pl.BlockSpec((None, n_maps, gw, tq), lambda bi, i, j, *_: (bi, 0, 0, i)),
                pl.BlockSpec((None, tk, gw), lambda bi, i, j, *_: (bi, j, 0)),
                pl.BlockSpec((None, heads, LANES, tk), lambda bi, i, j, *_: (bi, 0, 0, j)),
            ] + [pl.BlockSpec(x.shape, lambda bi, i, j, *_: (0, 0)) for x in small],
            out_specs=pl.BlockSpec((None, tq, gw), lambda bi, i, j, *_: (bi, i, 0)),
            scratch_shapes=[pltpu.VMEM((n_maps, 1, tq), F32), pltpu.VMEM((n_maps, LANES, tq), F32)],
        ),
        out_shape=jax.ShapeDtypeStruct((b, t, gw), F32),
        compiler_params=_cparams("parallel", "parallel", "arbitrary"),
    )(bounded, bound, qt, k, vt, *small)


def diff_attention(dax, dac, params, gw, lam_init, ctx_out):
    q_gain, k_gain, lq1, lk1, lq2, lk2, sub_gain = params
    qtx, kx, vtx = attention_prepare(dax, q_gain, k_gain, gw, rope=True)
    qtc, kc, vtc = attention_prepare(dac, q_gain, k_gain, gw, rope=False)
    k = jnp.concatenate([kx, kc], axis=1)
    vt = jnp.concatenate([vtx, vtc], axis=3)
    lam_params = (lq1, lk1, lq2, lk2)
    gains = (q_gain, k_gain)
    out_x = diff_attention_core(qtx, k, vt, gains, lam_params, sub_gain, lam_init)
    out_c = diff_attention_core(qtc, kc, vtc, gains, lam_params, sub_gain, lam_init) if ctx_out else None
    return out_x, out_c


FT_RADIX = 64


def _dft_cos_sin(n, scale=1.0):
    i = np.arange(n)
    ang = 2.0 * np.pi * ((i[:, None] * i[None, :]) % n) / n
    return np.cos(ang) * scale, np.sin(ang) * scale


def _channel_dft(gw, scale):
    c, s = _dft_cos_sin(HEAD, scale)
    eye = np.eye(gw // HEAD)
    return jnp.asarray(np.concatenate([np.kron(eye, c), np.kron(eye, s)], axis=0).astype(np.float32))


def _fnet_stage1_kernel(z_ref, gr_ref, gi_ref, or_ref, oi_ref):
    for j in range(z_ref.shape[1]):
        x = z_ref[:, j, :]
        or_ref[j] = _dot_f32(gr_ref[j], x)
        oi_ref[j] = _dot_f32(gi_ref[j], x)


def _matmul_f32_kernel(a_ref, b_ref, o_ref):
    o_ref[...] = _dot_f32(a_ref[...], b_ref[...])


def _channel_dft_times(chan, w_f):
    return pl.pallas_call(
        _matmul_f32_kernel,
        out_shape=jax.ShapeDtypeStruct((chan.shape[0], w_f.shape[1]), F32),
    )(chan, w_f)


def _fnet_stage2_kernel(gw, br_ref, bi_ref, rot_ref, chanw_ref, o_ref):
    n1, groups, _ = br_ref.shape
    x = jnp.concatenate([jnp.concatenate([br_ref[:, g, :] for g in range(groups)], axis=1),
                         jnp.concatenate([bi_ref[:, g, :] for g in range(groups)], axis=1)], axis=0)
    p = _dot_f32(rot_ref[...], x)
    rows = jnp.concatenate(
        [jnp.concatenate([p[:n1, g * gw:(g + 1) * gw], p[n1:, g * gw:(g + 1) * gw]], axis=1) for g in range(groups)],
        axis=0)
    out = _dot_f32(rows, chanw_ref[...])
    for g in range(groups):
        o_ref[:, g, :] = out[g * n1:(g + 1) * n1]


def fourier_mix_long(z, w_f):
    b, t, gw = z.shape
    n1 = FT_RADIX
    n2 = t // n1
    k2 = np.arange(n2)[None, :, None]
    n = np.arange(n1)[:, None, None] + n1 * np.arange(n2)[None, None, :]
    ang = 2.0 * np.pi * ((k2 * n) % t) / t
    g_r = jnp.asarray(np.cos(ang).astype(np.float32))
    g_i = jnp.asarray((-np.sin(ang)).astype(np.float32))
    j8 = SUBLANES
    br, bi = pl.pallas_call(
        _fnet_stage1_kernel,
        grid=(b, n1 // j8),
        in_specs=[
            pl.BlockSpec((None, n2, j8, gw), lambda bi_, i: (bi_, 0, i, 0)),
            pl.BlockSpec((j8, n2, n2), lambda bi_, i: (i, 0, 0)),
            pl.BlockSpec((j8, n2, n2), lambda bi_, i: (i, 0, 0)),
        ],
        out_specs=[pl.BlockSpec((None, j8, n2, gw), lambda bi_, i: (bi_, i, 0, 0))] * 2,
        out_shape=[jax.ShapeDtypeStruct((b, n1, n2, gw), F32)] * 2,
        compiler_params=_cparams("parallel", "parallel"),
    )(z.reshape(b, n2, n1, gw), g_r, g_i)
    c64, s64 = _dft_cos_sin(n1)
    rot = jnp.asarray(np.block([[c64, s64], [-s64, c64]]).astype(np.float32))
    chanw = _channel_dft_times(_channel_dft(gw, 1.0 / math.sqrt(t * HEAD)), w_f)
    blk = pl.BlockSpec((None, n1, j8, gw), lambda bi_, i: (bi_, 0, i, 0))
    out = pl.pallas_call(
        functools.partial(_fnet_stage2_kernel, gw),
        grid=(b, n2 // j8),
        in_specs=[blk, blk, _const_spec(rot.shape), _const_spec(chanw.shape)],
        out_specs=blk,
        out_shape=jax.ShapeDtypeStruct((b, n1, n2, gw), F32),
        compiler_params=_cparams("parallel", "parallel"),
    )(br, bi, rot, chanw)
    return out.reshape(b, t, gw)


def _fnet_dense_kernel(z_ref, ct_ref, st_ref, chanw_ref, o_ref):
    z = z_ref[...]
    pr = _dot_f32(ct_ref[...], z)
    pi = -_dot_f32(st_ref[...], z)
    o_ref[...] = _dot_f32(jnp.concatenate([pr, pi], axis=1), chanw_ref[...])


def fourier_mix_short(z, w_f):
    b, t, gw = z.shape
    ct, st = _dft_cos_sin(t)
    ct = jnp.asarray(ct.astype(np.float32))
    st = jnp.asarray(st.astype(np.float32))
    chanw = _channel_dft_times(_channel_dft(gw, 1.0 / math.sqrt(t * HEAD)), w_f)
    tok = pl.BlockSpec((None, t, gw), lambda bi: (bi, 0, 0))
    return pl.pallas_call(
        _fnet_dense_kernel,
        grid=(b,),
        in_specs=[tok, _const_spec(ct.shape), _const_spec(st.shape), _const_spec(chanw.shape)],
        out_specs=tok,
        out_shape=jax.ShapeDtypeStruct((b, t, gw), F32),
        compiler_params=_cparams("parallel"),
    )(z, ct, st, chanw)


def fourier_mix(z, w_f):
    t = z.shape[1]
    if t % (FT_RADIX * SUBLANES) == 0 and t // FT_RADIX >= LANES:
        return fourier_mix_long(z, w_f)
    return fourier_mix_short(z, w_f)


def _pool_kernel(t_total, u_ref, prev_ref, next_ref, w_ref, s_ref, o_ref, buf_ref):
    tm, gw = u_ref.shape
    _stage_with_halo(buf_ref, u_ref, prev_ref, next_ref)
    at = lambda off: buf_ref[SUBLANES + off:SUBLANES + off + tm, :]
    u = at(0)
    t = pl.program_id(1) * tm + lax.broadcasted_iota(jnp.int32, (tm, 1), 0)
    lane = lax.broadcasted_iota(jnp.int32, (1, gw), 1)
    group = lane // (gw // len(POOL_WINDOWS))
    mean = jnp.zeros((tm, gw), F32)
    run = jnp.zeros((tm, gw), F32)
    half_prev = 0
    for i, w in enumerate(POOL_WINDOWS):
        half = w // 2
        for off in range(half_prev, half):
            run = run + at(-off - 1) + at(off)
        half_prev = half
        cnt = (jnp.minimum(t + half, t_total) - jnp.maximum(t - half, 0)).astype(F32)
        mean = jnp.where(group == i, run / cnt, mean)
    o_ref[...] = _dot_f32(mean - u, w_ref[...]) * s_ref[...]


def pool_mix(u, w_p, s_p):
    b, t, gw = u.shape
    tm = min(512, t)
    nw, ch = w_p.shape[0], w_p.shape[1]
    w_bd = jnp.zeros((gw, gw), F32)
    for i in range(nw):
        w_bd = w_bd.at[i * ch:(i + 1) * ch, i * ch:(i + 1) * ch].set(w_p[i])
    return pl.pallas_call(
        functools.partial(_pool_kernel, t),
        grid=(b, t // tm),
        in_specs=_halo_specs(tm, t, gw) + [_const_spec((gw, gw)), _const_spec((1, gw))],
        out_specs=pl.BlockSpec((None, tm, gw), lambda bi, i: (bi, i, 0)),
        out_shape=jax.ShapeDtypeStruct((b, t, gw), F32),
        scratch_shapes=[pltpu.VMEM((tm + 2 * SUBLANES, gw), F32)],
        compiler_params=_cparams("parallel", "parallel"),
    )(u, u, u, w_bd, s_p.reshape(1, gw))


def _outproj_kernel(gw, x_ref, g_ref, yf_ref, yb_ref, r_ref, k_ref, v_ref, rg_ref, b_ref, f_ref, p_ref,
                    rk_ref, lnw_ref, lnb_ref, hmean_ref, w_ref, o_ref):
    a = _rwkv_head_output(yf_ref[...], yb_ref[...], r_ref[...], k_ref[...], v_ref[...], rg_ref[...],
                          rk_ref[...], lnw_ref[...], lnb_ref[...], hmean_ref[...])
    acc = jnp.dot(a.astype(BF16), w_ref[0:gw, :], preferred_element_type=F32)
    for i, m_ref in enumerate((b_ref, f_ref, p_ref), start=1):
        acc = acc + jnp.dot(m_ref[...].astype(BF16), w_ref[i * gw:(i + 1) * gw, :], preferred_element_type=F32)
    o_ref[...] = x_ref[...] + g_ref[...] * acc


def output_projection(x, gate, rwkv_parts, rwkv_params, mixers, w_out_bf16):
    b, t, d = x.shape
    gw = mixers[0].shape[-1]
    tm = min(512, t)
    tok = pl.BlockSpec((None, tm, gw), lambda bi, i: (bi, i, 0))
    xs = pl.BlockSpec((None, tm, d), lambda bi, i: (bi, i, 0))
    r_k, ln_w, ln_b = rwkv_params
    small = [r_k.reshape(1, gw), ln_w.reshape(1, gw), ln_b.reshape(1, gw), _block_ones(gw, HEAD, 1.0 / HEAD)]
    return pl.pallas_call(
        functools.partial(_outproj_kernel, gw),
        grid=(b, t // tm),
        in_specs=([xs, pl.BlockSpec((None, 1, d), lambda bi, i: (bi, 0, 0))] + [tok] * 9
                  + [_const_spec(s.shape) for s in small] + [_const_spec(w_out_bf16.shape)]),
        out_specs=xs,
        out_shape=jax.ShapeDtypeStruct((b, t, d), F32),
        compiler_params=_cparams("parallel", "parallel"),
    )(x, gate, *rwkv_parts, *mixers, *small, w_out_bf16)


def _router_kernel(n_exp, x_ref, gain_ref, sc_ref, sh_ref, wr_ref, h_o, aff_o, afft_o):
    h = _modulated_norm(x_ref[...], gain_ref[...], sc_ref[...], sh_ref[...])
    h_o[...] = h
    logits = _dot_f32(h, wr_ref[...])
    lane = lax.broadcasted_iota(jnp.int32, logits.shape, 1)
    logits = jnp.where(lane < n_exp, logits, -1e30)
    e = jnp.exp(logits - jnp.max(logits, axis=-1, keepdims=True))
    aff = e / jnp.sum(e, axis=-1, keepdims=True)
    aff_o[...] = aff
    afft_o[...] = jnp.transpose(aff)[:n_exp, :]


def router(x, gain, scale, shift, w_router):
    b, t, d = x.shape
    n_exp = w_router.shape[1]
    tm = min(512, t)
    wr = jnp.zeros((d, LANES), F32).at[:, :n_exp].set(w_router)
    row = pl.BlockSpec((None, 1, d), lambda bi, i: (bi, 0, 0))
    return pl.pallas_call(
        functools.partial(_router_kernel, n_exp),
        grid=(b, t // tm),
        in_specs=[pl.BlockSpec((None, tm, d), lambda bi, i: (bi, i, 0)), _const_spec((1, d)), row, row,
                  _const_spec(wr.shape)],
        out_specs=[pl.BlockSpec((None, tm, d), lambda bi, i: (bi, i, 0)),
                   pl.BlockSpec((None, tm, LANES), lambda bi, i: (bi, i, 0)),
                   pl.BlockSpec((None, n_exp, tm), lambda bi, i: (bi, 0, i))],
        out_shape=[jax.ShapeDtypeStruct((b, t, d), F32), jax.ShapeDtypeStruct((b, t, LANES), F32),
                   jax.ShapeDtypeStruct((b, n_exp, t), F32)],
        compiler_params=_cparams("parallel", "parallel"),
    )(x, gain.reshape(1, d), scale, shift, wr)


TOPK_EXPONENT_STEPS = 7
TOPK_MANTISSA_STEPS = 44


def _row_cumsum(x_ref, o_ref, upper_ref):
    rows, t = x_ref.shape
    carry = jnp.zeros((rows, 1), F32)
    for g in range(t // LANES):
        sl = slice(g * LANES, (g + 1) * LANES)
        local = jnp.dot(x_ref[:, sl].astype(BF16), upper_ref[...], preferred_element_type=F32) + carry
        o_ref[:, sl] = local
        carry = local[:, LANES - 1:LANES]


def _topk_kernel(cap, aff_ref, upper_ref, idx_o, sel_ref, cs_ref, local_ref, begin_ref, end_ref):
    aff = aff_ref[...]
    rows, t = aff.shape
    capf = float(cap)
    count_ge = lambda thr: jnp.sum(jnp.where(aff >= thr, 1.0, 0.0), axis=-1, keepdims=True)
    hi = jnp.full((rows, 1), 2.0, F32)
    for step in reversed(range(TOPK_EXPONENT_STEPS)):
        cand = hi * (2.0 ** -(2 ** step))
        hi = jnp.where(count_ge(cand) < capf, cand, hi)
    lo = hi * 0.5
    lo = jnp.where(count_ge(lo) >= capf, lo, 0.0)

    def bisect(_, carry):
        lo, hi = carry
        mid = 0.5 * (lo + hi)
        enough = count_ge(mid) >= capf
        return jnp.where(enough, mid, lo), jnp.where(enough, hi, mid)

    lo, hi = lax.fori_loop(0, TOPK_MANTISSA_STEPS, bisect, (lo, hi))
    above = aff >= hi
    need = capf - count_ge(hi)
    sel_ref[...] = jnp.where((aff >= lo) & jnp.logical_not(above), 1.0, 0.0)
    _row_cumsum(sel_ref, cs_ref, upper_ref)
    tied_in = (sel_ref[...] > 0.5) & (cs_ref[...] <= need)
    sel_ref[...] = jnp.where(above | tied_in, 1.0, 0.0)

    groups = t // LANES
    carry = jnp.zeros((rows, 1), F32)
    for g in range(groups):
        local = jnp.dot(sel_ref[:, g * LANES:(g + 1) * LANES].astype(BF16), upper_ref[...],
                        preferred_element_type=F32)
        local_ref[:, g, :] = local
        begin_ref[:, g:g + 1] = carry
        carry = carry + local[:, LANES - 1:LANES]
        end_ref[:, g:g + 1] = carry

    ones_g = jnp.ones((SUBLANES, groups), BF16)
    ones_l = jnp.ones((SUBLANES, LANES), BF16)

    def compact(r, _):
        begin = begin_ref[pl.ds(r, 1), :]
        end = end_ref[pl.ds(r, 1), :]
        local = local_ref[r].astype(BF16)
        for cb in range(0, cap, LANES):
            n = min(LANES, cap - cb)
            slot = (lax.broadcasted_iota(jnp.int32, (n, 1), 0) + cb).astype(F32)
            before = jnp.where(end <= slot, 1.0, 0.0)
            mine = jnp.where((begin <= slot) & (slot < end), 1.0, 0.0)
            rank = slot - jnp.sum(mine * begin, axis=-1, keepdims=True)
            counts = jnp.dot(mine.astype(BF16), local, preferred_element_type=F32)
            reached = jnp.where(counts <= rank, 1.0, 0.0).astype(BF16)
            idx_row = LANES * _dot_nt(ones_g, before) + _dot_nt(ones_l, reached)
            idx_o[r, :, cb:cb + n] = idx_row[0:1].astype(jnp.int32)
        return 0

    lax.fori_loop(0, rows, compact, 0)


def expert_choice_topk(aff_t, cap):
    b, n_exp, t = aff_t.shape
    rows = b * n_exp
    upper = jnp.asarray(np.triu(np.ones((LANES, LANES), np.float32))).astype(BF16)
    idx = pl.pallas_call(
        functools.partial(_topk_kernel, cap),
        grid=(1,),
        in_specs=[_const_spec((rows, t)), _const_spec(upper.shape)],
        out_specs=_const_spec((rows, 1, cap)),
        out_shape=jax.ShapeDtypeStruct((rows, 1, cap), jnp.int32),
        scratch_shapes=[pltpu.VMEM((rows, t), F32), pltpu.VMEM((rows, t), F32),
                        pltpu.VMEM((rows, t // LANES, LANES), F32),
                        pltpu.VMEM((rows, t // LANES), F32), pltpu.VMEM((rows, t // LANES), F32)],
        compiler_params=_cparams("arbitrary"),
    )(aff_t.reshape(rows, t), upper)
    return idx.reshape(rows * cap)


GATHER_UNROLL = 8
SCATTER_BATCH = 4


def _gather_kernel(cap, idx_ref, h_hbm, aff_ref, xs_o, g_o, h_ref, buf_ref, sem):
    b = pl.program_id(0)
    e = pl.program_id(1)
    base = (b * pl.num_programs(1) + e) * cap

    @pl.when(e == 0)
    def _():
        load = pltpu.make_async_copy(h_hbm.at[b], h_ref, sem)
        load.start()
        load.wait()

    def body(i, _):
        for u in range(GATHER_UNROLL):
            c = i * GATHER_UNROLL + u
            row = idx_ref[base + c]
            buf_ref[pl.ds(c, 1), :] = h_ref[pl.ds(row, 1), :]
            g_o[pl.ds(c, 1), :] = aff_ref[pl.ds(row, 1), :]
        return 0

    lax.fori_loop(0, cap // GATHER_UNROLL, body, 0)
    xs_o[...] = buf_ref[...].astype(BF16)


def gather_tokens(idx, h, aff, n_exp, cap):
    b, t, d = h.shape
    return pl.pallas_call(
        functools.partial(_gather_kernel, cap),
        grid_spec=pltpu.PrefetchScalarGridSpec(
            num_scalar_prefetch=1,
            grid=(b, n_exp),
            in_specs=[pl.BlockSpec(memory_space=pl.ANY),
                      pl.BlockSpec((None, t, LANES), lambda bi, e, idx_: (bi, 0, 0))],
            out_specs=[pl.BlockSpec((None, None, cap, d), lambda bi, e, idx_: (bi, e, 0, 0)),
                       pl.BlockSpec((None, None, cap, LANES), lambda bi, e, idx_: (bi, e, 0, 0))],
            scratch_shapes=[pltpu.VMEM((t, d), F32), pltpu.VMEM((cap, d), F32), pltpu.SemaphoreType.DMA(())],
        ),
        out_shape=[jax.ShapeDtypeStruct((b, n_exp, cap, d), BF16),
                   jax.ShapeDtypeStruct((b, n_exp, cap, LANES), F32)],
        compiler_params=_cparams("arbitrary", "arbitrary"),
    )(idx, h, aff)


def _expert_ffn_kernel(n_groups, *refs):
    xs_refs = refs[0:3 * n_groups:3]
    g_refs = refs[1:3 * n_groups:3]
    gate_refs = refs[2:3 * n_groups:3]
    wg_ref, wu_ref, wd_ref = refs[3 * n_groups:3 * n_groups + 3]
    o_refs = refs[3 * n_groups + 3:4 * n_groups + 3]
    acc_refs = refs[4 * n_groups + 3:]
    f = pl.program_id(1)

    @pl.when(f == 0)
    def _():
        for acc_ref in acc_refs:
            acc_ref[...] = jnp.zeros(acc_ref.shape, F32)

    wg = wg_ref[...].astype(BF16)
    wu = wu_ref[...].astype(BF16)
    wd = wd_ref[...].astype(BF16)
    for xs_ref, acc_ref in zip(xs_refs, acc_refs):
        for bi in range(xs_ref.shape[0]):
            xb = xs_ref[bi]
            gate = jnp.dot(xb, wg, preferred_element_type=F32)
            up = jnp.dot(xb, wu, preferred_element_type=F32)
            hid = (gate * _sigmoid(gate) * up).astype(BF16)
            acc_ref[bi] += jnp.dot(hid, wd, preferred_element_type=F32)

    @pl.when(f == pl.num_programs(1) - 1)
    def _():
        e = pl.program_id(0)
        for g_ref, gate_ref, acc_ref, o_ref in zip(g_refs, gate_refs, acc_refs, o_refs):
            lane = lax.broadcasted_iota(jnp.int32, g_ref.shape, 2)
            g = jnp.sum(jnp.where(lane == e, g_ref[...], 0.0), axis=-1, keepdims=True)
            o_ref[...] = acc_ref[...] * g * gate_ref[...]


def expert_ffn(groups, layer, w_gate, w_up, w_down):
    _, n_exp, d, f_dim = w_gate.shape
    tf = _largest_divisor(f_dim, 256, LANES)
    args, in_specs, out_specs, out_shapes, scratch = [], [], [], [], []
    for xs, g_rows, gate in groups:
        b, _, cap, _ = xs.shape
        args += [xs, g_rows, gate]
        in_specs += [pl.BlockSpec((b, None, cap, d), lambda e, f: (0, e, 0, 0)),
                     pl.BlockSpec((b, None, cap, LANES), lambda e, f: (0, e, 0, 0)),
                     pl.BlockSpec((b, 1, d), lambda e, f: (0, 0, 0))]
        out_specs.append(pl.BlockSpec((None, b, cap, d), lambda e, f: (e, 0, 0, 0)))
        out_shapes.append(jax.ShapeDtypeStruct((n_exp, b, cap, d), F32))
        scratch.append(pltpu.VMEM((b, cap, d), F32))
    in_specs += [pl.BlockSpec((None, None, d, tf), lambda e, f: (layer, e, 0, f)),
                 pl.BlockSpec((None, None, d, tf), lambda e, f: (layer, e, 0, f)),
                 pl.BlockSpec((None, None, tf, d), lambda e, f: (layer, e, f, 0))]
    return pl.pallas_call(
        functools.partial(_expert_ffn_kernel, len(groups)),
        grid=(n_exp, f_dim // tf),
        in_specs=in_specs,
        out_specs=out_specs,
        out_shape=out_shapes,
        scratch_shapes=scratch,
        compiler_params=_cparams("parallel", "arbitrary"),
    )(*args, w_gate, w_up, w_down)


def _scatter_kernel(cap, idx_ref, eo_ref, x_hbm, o_hbm, acc_ref, sem):
    b = pl.program_id(0)
    e = pl.program_id(1)
    base = (b * pl.num_programs(1) + e) * cap

    @pl.when(e == 0)
    def _():
        load = pltpu.make_async_copy(x_hbm.at[b], acc_ref, sem)
        load.start()
        load.wait()

    def body(i, _):
        c0 = i * SCATTER_BATCH
        rows = [idx_ref[base + c0 + u] for u in range(SCATTER_BATCH)]
        sums = [acc_ref[pl.ds(rows[u], 1), :] + eo_ref[pl.ds(c0 + u, 1), :] for u in range(SCATTER_BATCH)]
        for u in range(SCATTER_BATCH):
            acc_ref[pl.ds(rows[u], 1), :] = sums[u]
        return 0

    lax.fori_loop(0, cap // SCATTER_BATCH, body, 0)

    @pl.when(e == pl.num_programs(1) - 1)
    def _():
        store = pltpu.make_async_copy(acc_ref, o_hbm.at[b], sem)
        store.start()
        store.wait()


def scatter_residual(idx, expert_out, x, cap):
    n_exp, b, _, d = expert_out.shape
    t = x.shape[1]
    return pl.pallas_call(
        functools.partial(_scatter_kernel, cap),
        grid_spec=pltpu.PrefetchScalarGridSpec(
            num_scalar_prefetch=1,
            grid=(b, n_exp),
            in_specs=[pl.BlockSpec((None, None, cap, d), lambda bi, e, idx_: (e, bi, 0, 0)),
                      pl.BlockSpec(memory_space=pl.ANY)],
            out_specs=pl.BlockSpec(memory_space=pl.ANY),
            scratch_shapes=[pltpu.VMEM((t, d), F32), pltpu.SemaphoreType.DMA(())],
        ),
        out_shape=jax.ShapeDtypeStruct((b, t, d), F32),
        compiler_params=_cparams("arbitrary", "arbitrary"),
    )(idx, expert_out, x)


def moe_residual(streams, gain, layer, w_router, w_gate, w_up, w_down):
    n_exp = w_router.shape[1]
    routed = []
    for x, scale, shift, gate in streams:
        cap = CAPACITY_FACTOR * x.shape[1] // n_exp
        h, aff, aff_t = router(x, gain, scale, shift, w_router)
        idx = expert_choice_topk(aff_t, cap)
        xs, g_rows = gather_tokens(idx, h, aff, n_exp, cap)
        routed.append((idx, cap, (xs, g_rows, gate)))
    outs = expert_ffn([g for _, _, g in routed], layer, w_gate, w_up, w_down)
    return [scatter_residual(idx, eo, x, cap) for (idx, cap, _), eo, (x, _, _, _) in zip(routed, outs, streams)]


def kernel(x, c, ctx, c_ctx, mod_w, mod_b, norm1_w, norm2_w, w_in, rw_mu_prev, rw_mu_next, rw_w0, rw_w_up,
           rw_a0, rw_a_up, rw_g_up, rw_k_k, rw_k_a, rw_r_k, rw_ln_w, rw_ln_b, da_q_gain, da_k_gain, da_lq1,
           da_lk1, da_lq2, da_lk2, da_sub_gain, ft_w, pl_w, pl_scale, w_out, moe_router, moe_w_gate, moe_w_up,
           moe_w_down):
    depth, d = norm1_w.shape
    batch = x.shape[0]
    gw = d // N_MIXERS
    c_rows = jnp.zeros((SUBLANES, d), F32).at[:batch].set(c).at[batch].set(c_ctx)
    mod = modulation_vectors(c_rows, mod_w, mod_b)
    w_in_b = w_in.astype(BF16)
    w_out_b = w_out.astype(BF16)
    for l in range(depth):
        ctx_out = l < depth - 1
        lam_init = 0.8 - 0.6 * math.exp(-0.3 * l)
        mx = mod[l, :batch].reshape(batch, 6, 1, d)
        mc = jnp.broadcast_to(mod[l, batch].reshape(1, 6, 1, d), (batch, 6, 1, d))
        rw = (rw_mu_prev[l], rw_mu_next[l], rw_w0[l], rw_w_up[l], rw_a0[l], rw_a_up[l], rw_g_up[l],
              rw_k_k[l], rw_k_a[l], rw_r_k[l], rw_ln_w[l], rw_ln_b[l])
        da = (da_q_gain[l], da_k_gain[l], da_lq1[l], da_lk1[l], da_lq2[l], da_lk2[l], da_sub_gain[l])
        rwx, dax, ftx, plx = input_projection(x, norm1_w[l], mx[:, 1], mx[:, 0], w_in_b[l], gw)
        rwc, dac, ftc, plc = input_projection(ctx, norm1_w[l], mc[:, 1], mc[:, 0], w_in_b[l], gw)
        ax, ac = rwkv_mixer(rwx, rwc, rw, gw, ctx_out)
        bx, bc = diff_attention(dax, dac, da, gw, lam_init, ctx_out)
        fx = fourier_mix(ftx, ft_w[l])
        px = pool_mix(plx, pl_w[l], pl_scale[l])
        rw_out = (rw_r_k[l], rw_ln_w[l], rw_ln_b[l])
        x = output_projection(x, mx[:, 2], ax, rw_out, (bx, fx, px), w_out_b[l])
        streams = [(x, mx[:, 4], mx[:, 3], mx[:, 5])]
        if ctx_out:
            fc = fourier_mix(ftc, ft_w[l])
            pc = pool_mix(plc, pl_w[l], pl_scale[l])
            ctx = output_projection(ctx, mc[:, 2], ac, rw_out, (bc, fc, pc), w_out_b[l])
            streams.append((ctx, mc[:, 4], mc[:, 3], mc[:, 5]))
        outs = moe_residual(streams, norm2_w[l], l, moe_router[l], moe_w_gate, moe_w_up, moe_w_down)
        x = outs[0]
        if ctx_out:
            ctx = outs[1]
    return x
```

```python
import functools
import math

import numpy as np
import jax
import jax.numpy as jnp
from jax import lax
from jax.experimental import pallas as pl
from jax.experimental.pallas import tpu as pltpu

F32 = jnp.float32
BF16 = jnp.bfloat16
HIGHEST = lax.Precision.HIGHEST

N_MIXERS = 4
HEAD = 64
NORM_EPS = 1e-6
RW_LN_EPS = 64e-5
GRID_W = 64
DA_QK = HEAD // 2
ROPE_BASE = 10000.0
POOL_WINDOWS = (2, 4, 8, 16)
N_EXPERTS = 16
CAPACITY_FACTOR = 2

LANES = 128
SUBLANES = 8
VMEM_LIMIT_BYTES = 56 * 1024 * 1024

RW_CHUNK = 64
RW_CHUNKS_PER_STEP = 8


def _cparams(*sem):
    return pltpu.CompilerParams(dimension_semantics=sem, vmem_limit_bytes=VMEM_LIMIT_BYTES)


def _dot(a, b):
    return jnp.dot(a.astype(BF16), b.astype(BF16), preferred_element_type=F32)


def _split_bf16(x):
    hi = x.astype(BF16)
    return hi, (x - hi.astype(F32)).astype(BF16)


def _dot_f32(a, b, a_exact=False, b_exact=False):
    dot = lambda x, y: jnp.dot(x, y, preferred_element_type=F32)
    a_hi, a_lo = (a.astype(BF16), None) if a_exact else _split_bf16(a)
    b_hi, b_lo = (b.astype(BF16), None) if b_exact else _split_bf16(b)
    out = dot(a_hi, b_hi)
    if a_lo is not None:
        out = out + dot(a_lo, b_hi)
    if b_lo is not None:
        out = out + dot(a_hi, b_lo)
    return out


def _dot_tri(a, b):
    return _dot(a, b)


def _dot_nt(a, b, exact=False):
    dn = (((1,), (1,)), ((), ()))
    if exact:
        return lax.dot_general(a, b, dn, precision=HIGHEST, preferred_element_type=F32)
    return lax.dot_general(a.astype(BF16), b.astype(BF16), dn, preferred_element_type=F32)


def _dot_tn(a, b, exact=False):
    dn = (((0,), (0,)), ((), ()))
    if exact:
        return lax.dot_general(a, b, dn, precision=HIGHEST, preferred_element_type=F32)
    return lax.dot_general(a.astype(BF16), b.astype(BF16), dn, preferred_element_type=F32)


def _sigmoid(x):
    return 1.0 / (1.0 + jnp.exp(-x))


def _block_ones(n, blk, value=1.0):
    i = np.arange(n) // blk
    return jnp.asarray((i[:, None] == i[None, :]).astype(np.float32) * value)


def _const_spec(shape):
    nd = len(shape)
    return pl.BlockSpec(shape, lambda *_: (0,) * nd)


def _mod_kernel(c_ref, w_ref, b_ref, o_ref):
    c = c_ref[...]
    o_ref[...] = _dot_f32(c * _sigmoid(c), w_ref[...]) + b_ref[...]


def modulation_vectors(c_rows, mod_w, mod_b):
    depth, d, n = mod_w.shape
    tn = 1536
    return pl.pallas_call(
        _mod_kernel,
        grid=(depth, n // tn),
        in_specs=[
            pl.BlockSpec((SUBLANES, d), lambda l, j: (0, 0)),
            pl.BlockSpec((None, d, tn), lambda l, j: (l, 0, j)),
            pl.BlockSpec((None, 1, tn), lambda l, j: (l, 0, j)),
        ],
        out_specs=pl.BlockSpec((None, SUBLANES, tn), lambda l, j: (l, 0, j)),
        out_shape=jax.ShapeDtypeStruct((depth, SUBLANES, n), F32),
        compiler_params=_cparams("parallel", "parallel"),
    )(c_rows, mod_w, mod_b.reshape(depth, 1, n))


def _modulated_norm(x, gain, scale, shift):
    ms = jnp.mean(x * x, axis=-1, keepdims=True)
    return (x * lax.rsqrt(ms + NORM_EPS) * gain) * (1.0 + scale) + shift


def _inproj_kernel(splits, x_ref, gain_ref, sc_ref, sh_ref, w_ref, *o_refs):
    h = _modulated_norm(x_ref[...], gain_ref[...], sc_ref[...], sh_ref[...]).astype(BF16)
    for (lo, hi), o_ref in zip(splits, o_refs):
        o_ref[...] = jnp.dot(h, w_ref[:, lo:hi], preferred_element_type=F32)


def input_projection(x, gain, scale, shift, w_in_bf16, group_w):
    b, t, d = x.shape
    rw_cols = w_in_bf16.shape[1] - 3 * group_w - 2 * group_w
    cuts = [0, rw_cols, rw_cols + 3 * group_w, rw_cols + 4 * group_w, rw_cols + 5 * group_w]
    splits = tuple((cuts[i], cuts[i + 1]) for i in range(4))
    tm = min(512, t)
    row = pl.BlockSpec((None, 1, d), lambda bi, i: (bi, 0, 0))
    return pl.pallas_call(
        functools.partial(_inproj_kernel, splits),
        grid=(b, t // tm),
        in_specs=[
            pl.BlockSpec((None, tm, d), lambda bi, i: (bi, i, 0)),
            _const_spec((1, d)),
            row, row,
            _const_spec(w_in_bf16.shape),
        ],
        out_specs=[pl.BlockSpec((None, tm, hi - lo), lambda bi, i: (bi, i, 0)) for lo, hi in splits],
        out_shape=[jax.ShapeDtypeStruct((b, t, hi - lo), F32) for lo, hi in splits],
        compiler_params=_cparams("parallel", "parallel"),
    )(x, gain.reshape(1, d), scale, shift, w_in_bf16)


def _halo_specs(tm, t, width):
    nb8 = t // SUBLANES
    r8 = tm // SUBLANES
    return [
        pl.BlockSpec((None, tm, width), lambda b, i: (b, i, 0)),
        pl.BlockSpec((None, SUBLANES, width), lambda b, i: (b, jnp.maximum(i * r8 - 1, 0), 0)),
        pl.BlockSpec((None, SUBLANES, width), lambda b, i: (b, jnp.minimum((i + 1) * r8, nb8 - 1), 0)),
    ]


def _stage_with_halo(buf_ref, main_ref, prev_ref, next_ref):
    tm = main_ref.shape[0]
    i = pl.program_id(1)
    n = pl.num_programs(1)
    buf_ref[SUBLANES:SUBLANES + tm, :] = main_ref[...]
    buf_ref[0:SUBLANES, :] = jnp.where(i > 0, prev_ref[...], 0.0)
    buf_ref[SUBLANES + tm:2 * SUBLANES + tm, :] = jnp.where(i < n - 1, next_ref[...], 0.0)


def _rwkv_prep_kernel(gw, rw_ref, prev_ref, next_ref, mup_ref, mun_ref, kk_ref_, ka_ref, w0_ref, wup_ref,
                      a0_ref, aup_ref, gup_ref, hsum_ref,
                      r_o, k_o, v_o, kk_o, gate_o, lwf_o, kaf_o, kdf_o, lwb_o, kab_o, kdb_o, buf_ref):
    tm = rw_ref.shape[0]
    _stage_with_halo(buf_ref, rw_ref, prev_ref, next_ref)
    p = buf_ref[SUBLANES:SUBLANES + tm, :]
    prev = buf_ref[SUBLANES - 1:SUBLANES - 1 + tm, :]
    nxt = buf_ref[SUBLANES + 1:SUBLANES + 1 + tm, :]
    u = p + mup_ref[...] * (prev - p) + mun_ref[...] * (nxt - p)
    r = u[:, 0:gw]
    k = u[:, gw:2 * gw]
    v = u[:, 2 * gw:3 * gw]
    lora_w = u[:, 3 * gw:3 * gw + LANES]
    lora_a = u[:, 3 * gw + LANES:3 * gw + 2 * LANES]
    g = u[:, 3 * gw + 2 * LANES:3 * gw + 3 * LANES]
    kk = k * kk_ref_[...]
    ss = _dot_f32(kk * kk, hsum_ref[...], b_exact=True)
    kk = kk * lax.rsqrt(jnp.maximum(ss, 1e-24))
    zw = _dot(jnp.tanh(lora_w), wup_ref[...]) + w0_ref[...]
    za = _dot(lora_a, aup_ref[...]) + a0_ref[...]
    logw = -_sigmoid(zw) * math.exp(-0.5)
    a = _sigmoid(za)
    r_o[...] = r
    k_o[...] = k
    v_o[...] = v
    kk_o[...] = kk
    gate_o[...] = _dot(_sigmoid(g), gup_ref[...])
    ka = ka_ref[...]
    for d, (lw_o, kka_o, kd_o) in enumerate(((lwf_o, kaf_o, kdf_o), (lwb_o, kab_o, kdb_o))):
        a_d = a[:, d * gw:(d + 1) * gw]
        lw_o[...] = logw[:, d * gw:(d + 1) * gw]
        kka_o[...] = kk * a_d
        kd_o[...] = k * (1.0 + (a_d - 1.0) * ka)


def _blockdiag2(m):
    r, c = m.shape[1:]
    z = jnp.zeros((r, c), m.dtype)
    return jnp.concatenate([jnp.concatenate([m[0], z], 1), jnp.concatenate([z, m[1]], 1)], 0)


def rwkv_prepare(rw, params, gw):
    (mu_prev, mu_next, w0, w_up, a0, a_up, g_up, k_k, k_a, r_k, ln_w, ln_b) = params
    b, t, cols = rw.shape
    tm = min(512, t)
    row = lambda v: v.reshape(1, -1)
    small = [row(mu_prev), row(mu_next), row(k_k), row(k_a), row(w0), _blockdiag2(w_up), row(a0),
             _blockdiag2(a_up), g_up, _block_ones(gw, HEAD)]
    out = jax.ShapeDtypeStruct((b, t, gw), F32)
    return pl.pallas_call(
        functools.partial(_rwkv_prep_kernel, gw),
        grid=(b, t // tm),
        in_specs=_halo_specs(tm, t, cols) + [_const_spec(s.shape) for s in small],
        out_specs=[pl.BlockSpec((None, tm, gw), lambda bi, i: (bi, i, 0))] * 11,
        out_shape=[out] * 11,
        scratch_shapes=[pltpu.VMEM((tm + 2 * SUBLANES, cols), F32)],
        compiler_params=_cparams("parallel", "parallel"),
    )(rw, rw, rw, *small)


RW_BUILD_CHUNKS_PER_STEP = 4


def _rwkv_chunk_kernel(ncs, r_ref, v_ref, kk_ref, lwf_ref, kaf_ref, kdf_ref, lwb_ref, kab_ref, kdb_ref,
                       y0f_o, qf_o, mf_o, g0f_o, y0b_o, qb_o, mb_o, g0b_o):
    L = RW_CHUNK
    rows = lax.broadcasted_iota(jnp.int32, (L, L), 0)
    cols = lax.broadcasted_iota(jnp.int32, (L, L), 1)
    eye = jnp.where(rows == cols, 1.0, 0.0)
    rows2 = lax.broadcasted_iota(jnp.int32, (L, 2 * L), 0)
    cols2 = lax.broadcasted_iota(jnp.int32, (L, 2 * L), 1) % L
    lane = lax.broadcasted_iota(jnp.int32, (1, LANES), 1)
    head_masks = [(lane >= h * HEAD) & (lane < (h + 1) * HEAD) for h in range(2)]
    r2 = lax.broadcasted_iota(jnp.int32, (LANES, LANES), 0)
    c2 = lax.broadcasted_iota(jnp.int32, (LANES, LANES), 1)
    same_head = (r2 // HEAD) == (c2 // HEAD)
    diag = r2 == c2
    directions = ((False, lwf_ref, kaf_ref, kdf_ref, y0f_o, qf_o, mf_o, g0f_o),
                  (True, lwb_ref, kab_ref, kdb_ref, y0b_o, qb_o, mb_o, g0b_o))

    probs = []
    for c in range(ncs):
        rs = slice(c * L, (c + 1) * L)
        for pair in range(2):
            sl = slice(pair * LANES, (pair + 1) * LANES)
            r = r_ref[rs, sl]
            v = v_ref[rs, sl]
            kk = kk_ref[rs, sl]
            for reverse, lw_ref, ka_ref, kd_ref, y0_o, q_o, m_o, g0_o in directions:
                incl = (cols >= rows) if reverse else (cols <= rows)
                strict = (cols > rows) if reverse else (cols < rows)
                incl2 = (cols2 >= rows2) if reverse else (cols2 <= rows2)
                logw = lw_ref[rs, sl]
                kd = kd_ref[rs, sl]
                a = -ka_ref[rs, sl]
                cum = _dot_f32(jnp.where(incl, 1.0, 0.0), logw, a_exact=True)
                total = jnp.sum(logw, axis=0, keepdims=True)
                g_inv = jnp.exp(-cum)
                g_tail = jnp.exp(total - cum)
                bd = kk * jnp.exp(cum - logw)
                rd = r * jnp.exp(cum)
                probs.append(dict(
                    incl2=incl2, strict=strict, v=v, bd=bd, rd=rd, total=total,
                    lhs=jnp.concatenate([bd, rd], axis=0), rhs=jnp.concatenate([a * g_inv, kd * g_inv], axis=0),
                    tails=jnp.concatenate([kd * g_tail, a * g_tail], axis=0),
                    outs=(y0_o, q_o, m_o, g0_o), rs=rs, sl=sl, c=c, pair=pair))

    heads = []
    for p in probs:
        for h in range(2):
            gram = _dot_nt(jnp.where(head_masks[h], p["lhs"], 0.0), p["rhs"])
            heads.append(dict(
                p=p, h=h,
                nil=jnp.where(p["strict"], gram[:L, :L], 0.0),
                aak=jnp.where(p["strict"], gram[:L, L:], 0.0),
                ara_ark=jnp.where(p["incl2"], gram[L:, :], 0.0)))

    for hd in heads:
        hd["acc"] = eye + hd["nil"]
        hd["pow"] = hd["nil"]
    span = 2
    while span < L:
        for hd in heads:
            hd["pow"] = _dot_tri(hd["pow"], hd["pow"])
        for hd in heads:
            hd["acc"] = hd["acc"] + _dot_tri(hd["acc"], hd["pow"])
        span *= 2

    for hd in heads:
        hd["aakv"] = _dot(hd["aak"], hd["p"]["v"])
    for hd in heads:
        wp = _dot_tri(hd["acc"], jnp.concatenate([hd["aakv"], hd["p"]["bd"]], axis=1))
        hd["w1"], hd["pm"] = wp[:, :LANES], wp[:, LANES:]
    for hd in heads:
        rhs = jnp.concatenate([jnp.concatenate([hd["w1"], hd["pm"]], axis=1),
                               jnp.concatenate([hd["p"]["v"], jnp.zeros((L, LANES), F32)], axis=1)], axis=0)
        yq = _dot(hd["ara_ark"], rhs)
        hd["y0"], hd["q"] = yq[:, :LANES], yq[:, LANES:] + hd["p"]["rd"]

    for i, p in enumerate(probs):
        h0, h1 = heads[2 * i], heads[2 * i + 1]
        pick = lambda key: jnp.where(head_masks[0], h0[key], h1[key])
        y0_o, q_o, m_o, g0_o = p["outs"]
        y0_o[p["rs"], p["sl"]] = pick("y0")
        q_o[p["rs"], p["sl"]] = pick("q")
        vw = jnp.concatenate([p["v"], pick("w1")], axis=0)
        g0_o[p["c"], p["pair"]] = jnp.where(same_head, _dot_tn(p["tails"], vw), 0.0)
        m_o[p["c"], p["pair"]] = (jnp.where(same_head, _dot_tn(p["tails"][L:], pick("pm")), 0.0)
                                  + jnp.where(diag, jnp.exp(p["total"]), 0.0))


def rwkv_chunks(r, v, kk, lwf, kaf, kdf, lwb, kab, kdb):
    b, t, gw = r.shape
    L = RW_CHUNK
    nc = t // L
    ncs = RW_BUILD_CHUNKS_PER_STEP if nc % RW_BUILD_CHUNKS_PER_STEP == 0 else 1
    tok = pl.BlockSpec((None, ncs * L, gw), lambda bi, i: (bi, i, 0))
    mat = pl.BlockSpec((None, ncs, 2, LANES, LANES), lambda bi, i: (bi, i, 0, 0, 0))
    tok_s = jax.ShapeDtypeStruct((b, t, gw), F32)
    mat_s = jax.ShapeDtypeStruct((b, nc, 2, LANES, LANES), F32)
    return pl.pallas_call(
        functools.partial(_rwkv_chunk_kernel, ncs),
        grid=(b, nc // ncs),
        in_specs=[tok] * 9,
        out_specs=[tok, tok, mat, mat] * 2,
        out_shape=[tok_s, tok_s, mat_s, mat_s] * 2,
        compiler_params=_cparams("parallel", "parallel"),
    )(r, v, kk, lwf, kaf, kdf, lwb, kab, kdb)


def _rwkv_scan_kernel(cps, h0_ref, y0f_ref, qf_ref, mf_ref, g0f_ref, y0b_ref, qb_ref, mb_ref, g0b_ref,
                      yf_o, yb_o, hfin_o, h_ref):
    L = RW_CHUNK
    j = pl.program_id(1)

    @pl.when(j == 0)
    def _():
        h_ref[...] = h0_ref[...]

    for step in range(cps):
        for d, (y0_ref, q_ref, m_ref, g0_ref, y_o) in enumerate(
                ((y0f_ref, qf_ref, mf_ref, g0f_ref, yf_o), (y0b_ref, qb_ref, mb_ref, g0b_ref, yb_o))):
            c = step if d == 0 else cps - 1 - step
            rows = slice(c * L, (c + 1) * L)
            for pair in range(2):
                sl = slice(pair * LANES, (pair + 1) * LANES)
                h = h_ref[d, pair]
                y_o[rows, sl] = y0_ref[rows, sl] + _dot(q_ref[rows, sl], h)
                h_ref[d, pair] = _dot(m_ref[c, pair], h) + g0_ref[c, pair]

    @pl.when(j == pl.num_programs(1) - 1)
    def _():
        hfin_o[...] = h_ref[...]


def rwkv_scan(h0, y0f, qf, mf, g0f, y0b, qb, mb, g0b):
    b, t, gw = y0f.shape
    L = RW_CHUNK
    nc = t // L
    cps = min(RW_CHUNKS_PER_STEP, nc)
    nb = nc // cps
    tm = cps * L
    tok_f = pl.BlockSpec((None, tm, gw), lambda bi, i: (bi, i, 0))
    tok_b = pl.BlockSpec((None, tm, gw), lambda bi, i: (bi, nb - 1 - i, 0))
    mat_f = pl.BlockSpec((None, cps, 2, LANES, LANES), lambda bi, i: (bi, i, 0, 0, 0))
    mat_b = pl.BlockSpec((None, cps, 2, LANES, LANES), lambda bi, i: (bi, nb - 1 - i, 0, 0, 0))
    st = pl.BlockSpec((None, 2, 2, LANES, LANES), lambda bi, i: (bi, 0, 0, 0, 0))
    tok_s = jax.ShapeDtypeStruct((b, t, gw), F32)
    return pl.pallas_call(
        functools.partial(_rwkv_scan_kernel, cps),
        grid=(b, nb),
        in_specs=[st, tok_f, tok_f, mat_f, mat_f, tok_b, tok_b, mat_b, mat_b],
        out_specs=[tok_f, tok_b, st],
        out_shape=[tok_s, tok_s, jax.ShapeDtypeStruct((b, 2, 2, LANES, LANES), F32)],
        scratch_shapes=[pltpu.VMEM((2, 2, LANES, LANES), F32)],
        compiler_params=_cparams("parallel", "arbitrary"),
    )(h0, y0f, qf, mf, g0f, y0b, qb, mb, g0b)


def _rwkv_head_output(yf, yb, r, k, v, gate, r_k, ln_w, ln_b, hmean):
    y = yf + yb
    mu = _dot_f32(y, hmean, b_exact=True)
    yc = y - mu
    var = _dot_f32(yc * yc, hmean, b_exact=True)
    yn = yc * lax.rsqrt(var + RW_LN_EPS) * ln_w + ln_b
    bonus = _dot_f32(r * k * r_k, hmean, b_exact=True) * float(HEAD) * v
    return (yn + bonus) * gate


def rwkv_mixer(rwx, rwc, params, gw, ctx_out):
    sx = rwkv_prepare(rwx, params, gw)
    sc = rwkv_prepare(rwc, params, gw)
    (rx, kx, vx, kkx, gx), dx = sx[:5], sx[5:]
    (rc, kc, vc, kkc, gc), dc = sc[:5], sc[5:]
    cx = rwkv_chunks(rx, vx, kkx, *dx)
    cc = rwkv_chunks(rc, vc, kkc, *dc)
    b = rwx.shape[0]
    h0 = jnp.zeros((b, 2, 2, LANES, LANES), F32)
    ycf, ycb, h_ctx = rwkv_scan(h0, *cc)
    yxf, yxb, _ = rwkv_scan(h_ctx, *cx)
    out_x = (yxf, yxb, rx, kx, vx, gx)
    out_c = (ycf, ycb, rc, kc, vc, gc) if ctx_out else None
    return out_x, out_c


def _rope_tables(n_tokens, reps):
    rows = n_tokens // GRID_W
    row = np.repeat(np.arange(rows), GRID_W).astype(np.float64)
    col = np.tile(np.arange(GRID_W), rows).astype(np.float64)
    n_freq = DA_QK // 4
    inv = ROPE_BASE ** (-np.arange(n_freq, dtype=np.float64) / n_freq)
    ar = row[:, None] * inv
    ac = col[:, None] * inv
    ang = np.concatenate([ar, ar, ac, ac], axis=-1)
    cos = np.tile(np.cos(ang), (1, reps)).astype(np.float32)
    sin = np.tile(np.sin(ang), (1, reps)).astype(np.float32)
    return jnp.asarray(cos), jnp.asarray(sin)


def _attn_prep_kernel(gw, rope, da_ref, qg_ref, kg_ref, gmean_ref, *rest):
    if rope:
        cos_ref, sin_ref, q_o, k_o, v_o = rest
    else:
        q_o, k_o, v_o = rest
    da = da_ref[...]
    gmean = gmean_ref[...]
    lane = lax.broadcasted_iota(jnp.int32, (1, gw), 1)
    first_half = (lane % (DA_QK // 2)) < (DA_QK // 4)

    def norm_rope(x, gain):
        ms = _dot_f32(x * x, gmean, b_exact=True)
        y = x * lax.rsqrt(ms + NORM_EPS) * gain
        if rope:
            quarter = DA_QK // 4
            rot = jnp.where(first_half, -pltpu.roll(y, gw - quarter, 1), pltpu.roll(y, quarter, 1))
            y = y * cos_ref[...] + rot * sin_ref[...]
        return y

    tm = da.shape[0]
    q = norm_rope(da[:, 0:gw], qg_ref[...]) * (DA_QK ** -0.5 * LOG2_E)
    q_t = jnp.transpose(q)
    row_map = lax.broadcasted_iota(jnp.int32, (gw, 1), 0) // DA_QK
    for c in range(gw // DA_QK):
        q_o[c] = jnp.where(row_map == c, q_t, 0.0).astype(q_o.dtype)
    k_o[...] = norm_rope(da[:, gw:2 * gw], kg_ref[...]).astype(k_o.dtype)
    v_t = jnp.transpose(da[:, 2 * gw:3 * gw])
    pad = jnp.where(lax.broadcasted_iota(jnp.int32, (LANES - HEAD, tm), 0) == 0, 1.0, 0.0)
    for h in range(gw // HEAD):
        v_o[h] = jnp.concatenate([v_t[h * HEAD:(h + 1) * HEAD], pad], axis=0).astype(v_o.dtype)


def attention_prepare(da, q_gain, k_gain, gw, rope):
    b, t, cols = da.shape
    tm = min(512, t)
    reps = gw // DA_QK
    heads = gw // HEAD
    small = [jnp.tile(q_gain, reps).reshape(1, gw), jnp.tile(k_gain, reps).reshape(1, gw),
             _block_ones(gw, DA_QK, 1.0 / DA_QK)]
    args = [da] + small
    in_specs = [pl.BlockSpec((None, tm, cols), lambda bi, i: (bi, i, 0))] + [_const_spec(s.shape) for s in small]
    if rope:
        cos, sin = _rope_tables(t, reps)
        args += [cos, sin]
        in_specs += [pl.BlockSpec((tm, gw), lambda bi, i: (i, 0))] * 2
    return pl.pallas_call(
        functools.partial(_attn_prep_kernel, gw, rope),
        grid=(b, t // tm),
        in_specs=in_specs,
        out_specs=[pl.BlockSpec((None, reps, gw, tm), lambda bi, i: (bi, 0, 0, i)),
                   pl.BlockSpec((None, tm, gw), lambda bi, i: (bi, i, 0)),
                   pl.BlockSpec((None, heads, LANES, tm), lambda bi, i: (bi, 0, 0, i))],
        out_shape=[jax.ShapeDtypeStruct((b, reps, gw, t), BF16),
                   jax.ShapeDtypeStruct((b, t, gw), BF16),
                   jax.ShapeDtypeStruct((b, heads, LANES, t), BF16)],
        compiler_params=_cparams("parallel", "parallel"),
    )(*args)


LOG2_E = 1.4426950408889634
FLASH_MAX_KV_BLOCK = 2816


def _flash_kernel(lam_init, bounded_ref, bound_ref, qt_ref, k_ref, vt_ref, lq1_ref, lk1_ref, lq2_ref, lk2_ref,
                  sg_ref, o_ref, m_ref, acc_ref):
    j = pl.program_id(2)
    n_maps = qt_ref.shape[0]

    @pl.when(j == 0)
    def _():
        m_ref[...] = jnp.full(m_ref.shape, -1e30, F32)
        acc_ref[...] = jnp.zeros(acc_ref.shape, F32)

    k = k_ref[...]
    bounded = bounded_ref[0] == 1

    @pl.when(bounded)
    def _():
        shift = bound_ref[0]
        for c in range(n_maps):
            s = jnp.dot(k, qt_ref[c], preferred_element_type=F32)
            p = jnp.exp2(s - shift).astype(BF16)
            acc_ref[c] += jnp.dot(vt_ref[c // 2], p, preferred_element_type=F32)

    @pl.when(jnp.logical_not(bounded))
    def _():
        for c in range(n_maps):
            s = jnp.dot(k, qt_ref[c], preferred_element_type=F32)
            m_old = m_ref[c]
            m_new = jnp.maximum(m_old, jnp.max(s, axis=0, keepdims=True))
            p = jnp.exp2(s - m_new).astype(BF16)
            acc_ref[c] = (jnp.exp2(m_old - m_new) * acc_ref[c]
                          + jnp.dot(vt_ref[c // 2], p, preferred_element_type=F32))
            m_ref[c] = m_new

    @pl.when(j == pl.num_programs(2) - 1)
    def _():
        lam = (jnp.exp(jnp.sum(lq1_ref[...] * lk1_ref[...], axis=-1, keepdims=True))
               - jnp.exp(jnp.sum(lq2_ref[...] * lk2_ref[...], axis=-1, keepdims=True)) + lam_init)
        for h in range(n_maps // 2):
            a0 = acc_ref[2 * h]
            a1 = acc_ref[2 * h + 1]
            o = a0[:HEAD] / a0[HEAD:HEAD + 1] - lam * (a1[:HEAD] / a1[HEAD:HEAD + 1])
            ms = jnp.mean(o * o, axis=0, keepdims=True)
            o = o * lax.rsqrt(ms + NORM_EPS) * sg_ref[...] * (1.0 - lam_init)
            o_ref[:, h * HEAD:(h + 1) * HEAD] = jnp.transpose(o)


def _largest_divisor(n, cap, multiple):
    best = None
    for d in range(multiple, cap + 1, multiple):
        if n % d == 0:
            best = d
    return best if best is not None else n


FLASH_SAFE_SCORE_BOUND = 60.0


def _score_bound(q_gain, k_gain):
    bound = (1.02 * DA_QK * DA_QK ** -0.5 * LOG2_E) * jnp.max(jnp.abs(q_gain)) * jnp.max(jnp.abs(k_gain))
    return (bound <= FLASH_SAFE_SCORE_BOUND).astype(jnp.int32).reshape(1), bound.astype(F32).reshape(1)


def diff_attention_core(qt, k, vt, gains, lam_params, sub_gain, lam_init):
    b, n_maps, gw, t = qt.shape
    s = k.shape[1]
    heads = vt.shape[1]
    tq = min(512, t)
    tk = _largest_divisor(s, FLASH_MAX_KV_BLOCK, 2 * LANES)
    small = [p.reshape(1, -1) for p in lam_params] + [sub_gain.reshape(-1, 1)]
    bounded, bound = _score_bound(*gains)
    return pl.pallas_call(
        functools.partial(_flash_kernel, lam_init),
        grid_spec=pltpu.PrefetchScalarGridSpec(
            num_scalar_prefetch=2,
            grid=(b, t // tq, s // tk),
            in_specs=[
                pl.BlockSpec((None, n_maps, gw, tq), lambda bi, i, j, *_: (bi, 0, 0, i)),
                pl.BlockSpec((None, tk, gw), lambda bi, i, j, *_: (bi, j, 0)),
                pl.BlockSpec((None, heads, LANES, tk), lambda bi, i, j, *_: (bi, 0, 0, j)),
            ] + [pl.BlockSpec(x.shape, lambda bi, i, j, *_: (0, 0)) for x in small],
            out_specs=pl.BlockSpec((None, tq, gw), lambda bi, i, j, *_: (bi, i, 0)),
            scratch_shapes=[pltpu.VMEM((n_maps, 1, tq), F32), pltpu.VMEM((n_maps, LANES, tq), F32)],
        ),
        out_shape=jax.ShapeDtypeStruct((b, t, gw), F32),
        compiler_params=_cparams("parallel", "parallel", "arbitrary"),
    )(bounded, bound, qt, k, vt, *small)


def diff_attention(dax, dac, params, gw, lam_init, ctx_out):
    q_gain, k_gain, lq1, lk1, lq2, lk2, sub_gain = params
    qtx, kx, vtx = attention_prepare(dax, q_gain, k_gain, gw, rope=True)
    qtc, kc, vtc = attention_prepare(dac, q_gain, k_gain, gw, rope=False)
    k = jnp.concatenate([kx, kc], axis=1)
    vt = jnp.concatenate([vtx, vtc], axis=3)
    lam_params = (lq1, lk1, lq2, lk2)
    gains = (q_gain, k_gain)
    out_x = diff_attention_core(qtx, k, vt, gains, lam_params, sub_gain, lam_init)
    out_c = diff_attention_core(qtc, kc, vtc, gains, lam_params, sub_gain, lam_init) if ctx_out else None
    return out_x, out_c


FT_RADIX = 64


def _dft_cos_sin(n, scale=1.0):
    i = np.arange(n)
    ang = 2.0 * np.pi * ((i[:, None] * i[None, :]) % n) / n
    return np.cos(ang) * scale, np.sin(ang) * scale


def _channel_dft(gw, scale):
    c, s = _dft_cos_sin(HEAD, scale)
    eye = np.eye(gw // HEAD)
    return jnp.asarray(np.concatenate([np.kron(eye, c), np.kron(eye, s)], axis=0).astype(np.float32))


def _fnet_stage1_kernel(z_ref, grh_ref, grl_ref, gih_ref, gil_ref, or_ref, oi_ref):
    dot = lambda a, b: jnp.dot(a, b, preferred_element_type=F32)
    for j in range(z_ref.shape[1]):
        x_hi, x_lo = _split_bf16(z_ref[:, j, :])
        or_ref[j] = dot(grh_ref[j], x_hi) + (dot(grl_ref[j], x_hi) + dot(grh_ref[j], x_lo))
        oi_ref[j] = dot(gih_ref[j], x_hi) + (dot(gil_ref[j], x_hi) + dot(gih_ref[j], x_lo))


def _matmul_f32_kernel(a_ref, b_ref, o_ref):
    o_ref[...] = _dot_f32(a_ref[...], b_ref[...])


def _channel_dft_times(chan, w_f):
    return pl.pallas_call(
        _matmul_f32_kernel,
        out_shape=jax.ShapeDtypeStruct((chan.shape[0], w_f.shape[1]), F32),
    )(chan, w_f)


def _fnet_stage2_kernel(gw, br_ref, bi_ref, rot_ref, chanw_ref, o_ref):
    n1, groups, _ = br_ref.shape
    x = jnp.concatenate([jnp.concatenate([br_ref[:, g, :] for g in range(groups)], axis=1),
                         jnp.concatenate([bi_ref[:, g, :] for g in range(groups)], axis=1)], axis=0)
    p = _dot_f32(rot_ref[...], x)
    rows = jnp.concatenate(
        [jnp.concatenate([p[:n1, g * gw:(g + 1) * gw], p[n1:, g * gw:(g + 1) * gw]], axis=1) for g in range(groups)],
        axis=0)
    out = _dot_f32(rows, chanw_ref[...])
    for g in range(groups):
        o_ref[:, g, :] = out[g * n1:(g + 1) * n1]


def fourier_mix_long(z, w_f):
    b, t, gw = z.shape
    n1 = FT_RADIX
    n2 = t // n1
    k2 = np.arange(n2)[None, :, None]
    n = np.arange(n1)[:, None, None] + n1 * np.arange(n2)[None, None, :]
    ang = 2.0 * np.pi * ((k2 * n) % t) / t
    tables = []
    for g in (np.cos(ang), -np.sin(ang)):
        g_hi = jnp.asarray(g.astype(np.float32)).astype(BF16)
        g_lo = (jnp.asarray(g.astype(np.float32)) - g_hi.astype(F32)).astype(BF16)
        tables += [g_hi, g_lo]
    j8 = SUBLANES
    table_spec = pl.BlockSpec((j8, n2, n2), lambda bi_, i: (i, 0, 0))
    br, bi = pl.pallas_call(
        _fnet_stage1_kernel,
        grid=(b, n1 // j8),
        in_specs=[pl.BlockSpec((None, n2, j8, gw), lambda bi_, i: (bi_, 0, i, 0))] + [table_spec] * 4,
        out_specs=[pl.BlockSpec((None, j8, n2, gw), lambda bi_, i: (bi_, i, 0, 0))] * 2,
        out_shape=[jax.ShapeDtypeStruct((b, n1, n2, gw), F32)] * 2,
        compiler_params=_cparams("parallel", "parallel"),
    )(z.reshape(b, n2, n1, gw), *tables)
    c64, s64 = _dft_cos_sin(n1)
    rot = jnp.asarray(np.block([[c64, s64], [-s64, c64]]).astype(np.float32))
    chanw = _channel_dft_times(_channel_dft(gw, 1.0 / math.sqrt(t * HEAD)), w_f)
    blk = pl.BlockSpec((None, n1, j8, gw), lambda bi_, i: (bi_, 0, i, 0))
    out = pl.pallas_call(
        functools.partial(_fnet_stage2_kernel, gw),
        grid=(b, n2 // j8),
        in_specs=[blk, blk, _const_spec(rot.shape), _const_spec(chanw.shape)],
        out_specs=blk,
        out_shape=jax.ShapeDtypeStruct((b, n1, n2, gw), F32),
        compiler_params=_cparams("parallel", "parallel"),
    )(br, bi, rot, chanw)
    return out.reshape(b, t, gw)


def _fnet_dense_kernel(z_ref, ct_ref, st_ref, chanw_ref, o_ref):
    z = z_ref[...]
    pr = _dot_f32(ct_ref[...], z)
    pi = -_dot_f32(st_ref[...], z)
    o_ref[...] = _dot_f32(jnp.concatenate([pr, pi], axis=1), chanw_ref[...])


def fourier_mix_short(z, w_f):
    b, t, gw = z.shape
    ct, st = _dft_cos_sin(t)
    ct = jnp.asarray(ct.astype(np.float32))
    st = jnp.asarray(st.astype(np.float32))
    chanw = _channel_dft_times(_channel_dft(gw, 1.0 / math.sqrt(t * HEAD)), w_f)
    tok = pl.BlockSpec((None, t, gw), lambda bi: (bi, 0, 0))
    return pl.pallas_call(
        _fnet_dense_kernel,
        grid=(b,),
        in_specs=[tok, _const_spec(ct.shape), _const_spec(st.shape), _const_spec(chanw.shape)],
        out_specs=tok,
        out_shape=jax.ShapeDtypeStruct((b, t, gw), F32),
        compiler_params=_cparams("parallel"),
    )(z, ct, st, chanw)


def fourier_mix(z, w_f):
    t = z.shape[1]
    if t % (FT_RADIX * SUBLANES) == 0 and t // FT_RADIX >= LANES:
        return fourier_mix_long(z, w_f)
    return fourier_mix_short(z, w_f)


def _pool_kernel(t_total, u_ref, prev_ref, next_ref, w_ref, s_ref, o_ref, buf_ref):
    tm, gw = u_ref.shape
    _stage_with_halo(buf_ref, u_ref, prev_ref, next_ref)
    at = lambda off: buf_ref[SUBLANES + off:SUBLANES + off + tm, :]
    u = at(0)
    t = pl.program_id(1) * tm + lax.broadcasted_iota(jnp.int32, (tm, 1), 0)
    lane = lax.broadcasted_iota(jnp.int32, (1, gw), 1)
    group = lane // (gw // len(POOL_WINDOWS))
    mean = jnp.zeros((tm, gw), F32)
    run = jnp.zeros((tm, gw), F32)
    half_prev = 0
    for i, w in enumerate(POOL_WINDOWS):
        half = w // 2
        for off in range(half_prev, half):
            run = run + at(-off - 1) + at(off)
        half_prev = half
        cnt = (jnp.minimum(t + half, t_total) - jnp.maximum(t - half, 0)).astype(F32)
        mean = jnp.where(group == i, run / cnt, mean)
    o_ref[...] = _dot_f32(mean - u, w_ref[...]) * s_ref[...]


def pool_mix(u, w_p, s_p):
    b, t, gw = u.shape
    tm = min(512, t)
    nw, ch = w_p.shape[0], w_p.shape[1]
    w_bd = jnp.zeros((gw, gw), F32)
    for i in range(nw):
        w_bd = w_bd.at[i * ch:(i + 1) * ch, i * ch:(i + 1) * ch].set(w_p[i])
    return pl.pallas_call(
        functools.partial(_pool_kernel, t),
        grid=(b, t // tm),
        in_specs=_halo_specs(tm, t, gw) + [_const_spec((gw, gw)), _const_spec((1, gw))],
        out_specs=pl.BlockSpec((None, tm, gw), lambda bi, i: (bi, i, 0)),
        out_shape=jax.ShapeDtypeStruct((b, t, gw), F32),
        scratch_shapes=[pltpu.VMEM((tm + 2 * SUBLANES, gw), F32)],
        compiler_params=_cparams("parallel", "parallel"),
    )(u, u, u, w_bd, s_p.reshape(1, gw))


def _outproj_kernel(gw, x_ref, g_ref, yf_ref, yb_ref, r_ref, k_ref, v_ref, rg_ref, b_ref, f_ref, p_ref,
                    rk_ref, lnw_ref, lnb_ref, hmean_ref, w_ref, o_ref):
    a = _rwkv_head_output(yf_ref[...], yb_ref[...], r_ref[...], k_ref[...], v_ref[...], rg_ref[...],
                          rk_ref[...], lnw_ref[...], lnb_ref[...], hmean_ref[...])
    acc = jnp.dot(a.astype(BF16), w_ref[0:gw, :], preferred_element_type=F32)
    for i, m_ref in enumerate((b_ref, f_ref, p_ref), start=1):
        acc = acc + jnp.dot(m_ref[...].astype(BF16), w_ref[i * gw:(i + 1) * gw, :], preferred_element_type=F32)
    o_ref[...] = x_ref[...] + g_ref[...] * acc


def output_projection(x, gate, rwkv_parts, rwkv_params, mixers, w_out_bf16):
    b, t, d = x.shape
    gw = mixers[0].shape[-1]
    tm = min(512, t)
    tok = pl.BlockSpec((None, tm, gw), lambda bi, i: (bi, i, 0))
    xs = pl.BlockSpec((None, tm, d), lambda bi, i: (bi, i, 0))
    r_k, ln_w, ln_b = rwkv_params
    small = [r_k.reshape(1, gw), ln_w.reshape(1, gw), ln_b.reshape(1, gw), _block_ones(gw, HEAD, 1.0 / HEAD)]
    return pl.pallas_call(
        functools.partial(_outproj_kernel, gw),
        grid=(b, t // tm),
        in_specs=([xs, pl.BlockSpec((None, 1, d), lambda bi, i: (bi, 0, 0))] + [tok] * 9
                  + [_const_spec(s.shape) for s in small] + [_const_spec(w_out_bf16.shape)]),
        out_specs=xs,
        out_shape=jax.ShapeDtypeStruct((b, t, d), F32),
        compiler_params=_cparams("parallel", "parallel"),
    )(x, gate, *rwkv_parts, *mixers, *small, w_out_bf16)


def _router_kernel(n_exp, x_ref, gain_ref, sc_ref, sh_ref, wr_ref, h_o, aff_o, afft_o):
    h = _modulated_norm(x_ref[...], gain_ref[...], sc_ref[...], sh_ref[...])
    h_o[...] = h
    logits = _dot_f32(h, wr_ref[...])
    lane = lax.broadcasted_iota(jnp.int32, logits.shape, 1)
    logits = jnp.where(lane < n_exp, logits, -1e30)
    e = jnp.exp(logits - jnp.max(logits, axis=-1, keepdims=True))
    aff = e / jnp.sum(e, axis=-1, keepdims=True)
    aff_o[...] = aff
    afft_o[...] = jnp.transpose(aff)[:n_exp, :]


def router(x, gain, scale, shift, w_router):
    b, t, d = x.shape
    n_exp = w_router.shape[1]
    tm = min(512, t)
    wr = jnp.zeros((d, LANES), F32).at[:, :n_exp].set(w_router)
    row = pl.BlockSpec((None, 1, d), lambda bi, i: (bi, 0, 0))
    return pl.pallas_call(
        functools.partial(_router_kernel, n_exp),
        grid=(b, t // tm),
        in_specs=[pl.BlockSpec((None, tm, d), lambda bi, i: (bi, i, 0)), _const_spec((1, d)), row, row,
                  _const_spec(wr.shape)],
        out_specs=[pl.BlockSpec((None, tm, d), lambda bi, i: (bi, i, 0)),
                   pl.BlockSpec((None, tm, LANES), lambda bi, i: (bi, i, 0)),
                   pl.BlockSpec((None, n_exp, tm), lambda bi, i: (bi, 0, i))],
        out_shape=[jax.ShapeDtypeStruct((b, t, d), F32), jax.ShapeDtypeStruct((b, t, LANES), F32),
                   jax.ShapeDtypeStruct((b, n_exp, t), F32)],
        compiler_params=_cparams("parallel", "parallel"),
    )(x, gain.reshape(1, d), scale, shift, wr)


TOPK_EXPONENT_STEPS = 7
TOPK_MANTISSA_STEPS = 44


def _row_cumsum(x_ref, o_ref, upper_ref):
    rows, t = x_ref.shape
    carry = jnp.zeros((rows, 1), F32)
    for g in range(t // LANES):
        sl = slice(g * LANES, (g + 1) * LANES)
        local = jnp.dot(x_ref[:, sl].astype(BF16), upper_ref[...], preferred_element_type=F32) + carry
        o_ref[:, sl] = local
        carry = local[:, LANES - 1:LANES]


def _topk_kernel(cap, aff_ref, upper_ref, idx_o, sel_ref, cs_ref, local_ref, begin_ref, end_ref):
    aff = aff_ref[...]
    rows, t = aff.shape
    capf = float(cap)
    count_ge = lambda thr: jnp.sum(jnp.where(aff >= thr, 1.0, 0.0), axis=-1, keepdims=True)
    hi = jnp.full((rows, 1), 2.0, F32)
    for step in reversed(range(TOPK_EXPONENT_STEPS)):
        cand = hi * (2.0 ** -(2 ** step))
        hi = jnp.where(count_ge(cand) < capf, cand, hi)
    lo = hi * 0.5
    lo = jnp.where(count_ge(lo) >= capf, lo, 0.0)

    def bisect(_, carry):
        lo, hi = carry
        mid = 0.5 * (lo + hi)
        enough = count_ge(mid) >= capf
        return jnp.where(enough, mid, lo), jnp.where(enough, hi, mid)

    lo, hi = lax.fori_loop(0, TOPK_MANTISSA_STEPS, bisect, (lo, hi))
    above = aff >= hi
    need = capf - count_ge(hi)
    sel_ref[...] = jnp.where((aff >= lo) & jnp.logical_not(above), 1.0, 0.0)
    _row_cumsum(sel_ref, cs_ref, upper_ref)
    tied_in = (sel_ref[...] > 0.5) & (cs_ref[...] <= need)
    sel_ref[...] = jnp.where(above | tied_in, 1.0, 0.0)

    groups = t // LANES
    carry = jnp.zeros((rows, 1), F32)
    for g in range(groups):
        local = jnp.dot(sel_ref[:, g * LANES:(g + 1) * LANES].astype(BF16), upper_ref[...],
                        preferred_element_type=F32)
        local_ref[:, g, :] = local
        begin_ref[:, g:g + 1] = carry
        carry = carry + local[:, LANES - 1:LANES]
        end_ref[:, g:g + 1] = carry

    ones_g = jnp.ones((SUBLANES, groups), BF16)
    ones_l = jnp.ones((SUBLANES, LANES), BF16)

    def compact(r, _):
        begin = begin_ref[pl.ds(r, 1), :]
        end = end_ref[pl.ds(r, 1), :]
        local = local_ref[r].astype(BF16)
        for cb in range(0, cap, LANES):
            n = min(LANES, cap - cb)
            slot = (lax.broadcasted_iota(jnp.int32, (n, 1), 0) + cb).astype(F32)
            before = jnp.where(end <= slot, 1.0, 0.0)
            mine = jnp.where((begin <= slot) & (slot < end), 1.0, 0.0)
            rank = slot - jnp.sum(mine * begin, axis=-1, keepdims=True)
            counts = jnp.dot(mine.astype(BF16), local, preferred_element_type=F32)
            reached = jnp.where(counts <= rank, 1.0, 0.0).astype(BF16)
            idx_row = LANES * _dot_nt(ones_g, before) + _dot_nt(ones_l, reached)
            idx_o[r, :, cb:cb + n] = idx_row[0:1].astype(jnp.int32)
        return 0

    lax.fori_loop(0, rows, compact, 0)


def expert_choice_topk(aff_t, cap):
    b, n_exp, t = aff_t.shape
    rows = b * n_exp
    upper = jnp.asarray(np.triu(np.ones((LANES, LANES), np.float32))).astype(BF16)
    idx = pl.pallas_call(
        functools.partial(_topk_kernel, cap),
        grid=(1,),
        in_specs=[_const_spec((rows, t)), _const_spec(upper.shape)],
        out_specs=_const_spec((rows, 1, cap)),
        out_shape=jax.ShapeDtypeStruct((rows, 1, cap), jnp.int32),
        scratch_shapes=[pltpu.VMEM((rows, t), F32), pltpu.VMEM((rows, t), F32),
                        pltpu.VMEM((rows, t // LANES, LANES), F32),
                        pltpu.VMEM((rows, t // LANES), F32), pltpu.VMEM((rows, t // LANES), F32)],
        compiler_params=_cparams("arbitrary"),
    )(aff_t.reshape(rows, t), upper)
    return idx.reshape(rows * cap)


GATHER_UNROLL = 8
SCATTER_BATCH = 4


def _gather_kernel(cap, idx_ref, h_hbm, aff_ref, xs_o, g_o, h_ref, buf_ref, sem):
    b = pl.program_id(0)
    e = pl.program_id(1)
    base = (b * pl.num_programs(1) + e) * cap

    @pl.when(e == 0)
    def _():
        load = pltpu.make_async_copy(h_hbm.at[b], h_ref, sem)
        load.start()
        load.wait()

    def body(i, _):
        for u in range(GATHER_UNROLL):
            c = i * GATHER_UNROLL + u
            row = idx_ref[base + c]
            buf_ref[pl.ds(c, 1), :] = h_ref[pl.ds(row, 1), :]
            g_o[pl.ds(c, 1), :] = aff_ref[pl.ds(row, 1), :]
        return 0

    lax.fori_loop(0, cap // GATHER_UNROLL, body, 0)
    xs_o[...] = buf_ref[...].astype(BF16)


def gather_tokens(idx, h, aff, n_exp, cap):
    b, t, d = h.shape
    return pl.pallas_call(
        functools.partial(_gather_kernel, cap),
        grid_spec=pltpu.PrefetchScalarGridSpec(
            num_scalar_prefetch=1,
            grid=(b, n_exp),
            in_specs=[pl.BlockSpec(memory_space=pl.ANY),
                      pl.BlockSpec((None, t, LANES), lambda bi, e, idx_: (bi, 0, 0))],
            out_specs=[pl.BlockSpec((None, None, cap, d), lambda bi, e, idx_: (bi, e, 0, 0)),
                       pl.BlockSpec((None, None, cap, LANES), lambda bi, e, idx_: (bi, e, 0, 0))],
            scratch_shapes=[pltpu.VMEM((t, d), F32), pltpu.VMEM((cap, d), F32), pltpu.SemaphoreType.DMA(())],
        ),
        out_shape=[jax.ShapeDtypeStruct((b, n_exp, cap, d), BF16),
                   jax.ShapeDtypeStruct((b, n_exp, cap, LANES), F32)],
        compiler_params=_cparams("arbitrary", "arbitrary"),
    )(idx, h, aff)


def _expert_ffn_kernel(n_groups, *refs):
    xs_refs = refs[0:3 * n_groups:3]
    g_refs = refs[1:3 * n_groups:3]
    gate_refs = refs[2:3 * n_groups:3]
    wg_ref, wu_ref, wd_ref = refs[3 * n_groups:3 * n_groups + 3]
    o_refs = refs[3 * n_groups + 3:4 * n_groups + 3]
    acc_refs = refs[4 * n_groups + 3:]
    f = pl.program_id(1)

    @pl.when(f == 0)
    def _():
        for acc_ref in acc_refs:
            acc_ref[...] = jnp.zeros(acc_ref.shape, F32)

    wg = wg_ref[...].astype(BF16)
    wu = wu_ref[...].astype(BF16)
    wd = wd_ref[...].astype(BF16)
    for xs_ref, acc_ref in zip(xs_refs, acc_refs):
        for bi in range(xs_ref.shape[0]):
            xb = xs_ref[bi]
            gate = jnp.dot(xb, wg, preferred_element_type=F32)
            up = jnp.dot(xb, wu, preferred_element_type=F32)
            hid = (gate * _sigmoid(gate) * up).astype(BF16)
            acc_ref[bi] += jnp.dot(hid, wd, preferred_element_type=F32)

    @pl.when(f == pl.num_programs(1) - 1)
    def _():
        e = pl.program_id(0)
        for g_ref, gate_ref, acc_ref, o_ref in zip(g_refs, gate_refs, acc_refs, o_refs):
            lane = lax.broadcasted_iota(jnp.int32, g_ref.shape, 2)
            g = jnp.sum(jnp.where(lane == e, g_ref[...], 0.0), axis=-1, keepdims=True)
            o_ref[...] = acc_ref[...] * g * gate_ref[...]


def expert_ffn(groups, layer, w_gate, w_up, w_down):
    _, n_exp, d, f_dim = w_gate.shape
    tf = _largest_divisor(f_dim, 256, LANES)
    args, in_specs, out_specs, out_shapes, scratch = [], [], [], [], []
    for xs, g_rows, gate in groups:
        b, _, cap, _ = xs.shape
        args += [xs, g_rows, gate]
        in_specs += [pl.BlockSpec((b, None, cap, d), lambda e, f: (0, e, 0, 0)),
                     pl.BlockSpec((b, None, cap, LANES), lambda e, f: (0, e, 0, 0)),
                     pl.BlockSpec((b, 1, d), lambda e, f: (0, 0, 0))]
        out_specs.append(pl.BlockSpec((None, b, cap, d), lambda e, f: (e, 0, 0, 0)))
        out_shapes.append(jax.ShapeDtypeStruct((n_exp, b, cap, d), F32))
        scratch.append(pltpu.VMEM((b, cap, d), F32))
    in_specs += [pl.BlockSpec((None, None, d, tf), lambda e, f: (layer, e, 0, f)),
                 pl.BlockSpec((None, None, d, tf), lambda e, f: (layer, e, 0, f)),
                 pl.BlockSpec((None, None, tf, d), lambda e, f: (layer, e, f, 0))]
    return pl.pallas_call(
        functools.partial(_expert_ffn_kernel, len(groups)),
        grid=(n_exp, f_dim // tf),
        in_specs=in_specs,
        out_specs=out_specs,
        out_shape=out_shapes,
        scratch_shapes=scratch,
        compiler_params=_cparams("parallel", "arbitrary"),
    )(*args, w_gate, w_up, w_down)


def _scatter_kernel(cap, idx_ref, eo_ref, x_hbm, o_hbm, acc_ref, sem):
    b = pl.program_id(0)
    e = pl.program_id(1)
    base = (b * pl.num_programs(1) + e) * cap

    @pl.when(e == 0)
    def _():
        load = pltpu.make_async_copy(x_hbm.at[b], acc_ref, sem)
        load.start()
        load.wait()

    def body(i, _):
        c0 = i * SCATTER_BATCH
        rows = [idx_ref[base + c0 + u] for u in range(SCATTER_BATCH)]
        sums = [acc_ref[pl.ds(rows[u], 1), :] + eo_ref[pl.ds(c0 + u, 1), :] for u in range(SCATTER_BATCH)]
        for u in range(SCATTER_BATCH):
            acc_ref[pl.ds(rows[u], 1), :] = sums[u]
        return 0

    lax.fori_loop(0, cap // SCATTER_BATCH, body, 0)

    @pl.when(e == pl.num_programs(1) - 1)
    def _():
        store = pltpu.make_async_copy(acc_ref, o_hbm.at[b], sem)
        store.start()
        store.wait()


def scatter_residual(idx, expert_out, x, cap):
    n_exp, b, _, d = expert_out.shape
    t = x.shape[1]
    return pl.pallas_call(
        functools.partial(_scatter_kernel, cap),
        grid_spec=pltpu.PrefetchScalarGridSpec(
            num_scalar_prefetch=1,
            grid=(b, n_exp),
            in_specs=[pl.BlockSpec((None, None, cap, d), lambda bi, e, idx_: (e, bi, 0, 0)),
                      pl.BlockSpec(memory_space=pl.ANY)],
            out_specs=pl.BlockSpec(memory_space=pl.ANY),
            scratch_shapes=[pltpu.VMEM((t, d), F32), pltpu.SemaphoreType.DMA(())],
        ),
        out_shape=jax.ShapeDtypeStruct((b, t, d), F32),
        compiler_params=_cparams("arbitrary", "arbitrary"),
    )(idx, expert_out, x)


def moe_residual(streams, gain, layer, w_router, w_gate, w_up, w_down):
    n_exp = w_router.shape[1]
    routed = []
    for x, scale, shift, gate in streams:
        cap = CAPACITY_FACTOR * x.shape[1] // n_exp
        h, aff, aff_t = router(x, gain, scale, shift, w_router)
        idx = expert_choice_topk(aff_t, cap)
        xs, g_rows = gather_tokens(idx, h, aff, n_exp, cap)
        routed.append((idx, cap, (xs, g_rows, gate)))
    outs = expert_ffn([g for _, _, g in routed], layer, w_gate, w_up, w_down)
    return [scatter_residual(idx, eo, x, cap) for (idx, cap, _), eo, (x, _, _, _) in zip(routed, outs, streams)]


def kernel(x, c, ctx, c_ctx, mod_w, mod_b, norm1_w, norm2_w, w_in, rw_mu_prev, rw_mu_next, rw_w0, rw_w_up,
           rw_a0, rw_a_up, rw_g_up, rw_k_k, rw_k_a, rw_r_k, rw_ln_w, rw_ln_b, da_q_gain, da_k_gain, da_lq1,
           da_lk1, da_lq2, da_lk2, da_sub_gain, ft_w, pl_w, pl_scale, w_out, moe_router, moe_w_gate, moe_w_up,
           moe_w_down):
    depth, d = norm1_w.shape
    batch = x.shape[0]
    gw = d // N_MIXERS
    c_rows = jnp.zeros((SUBLANES, d), F32).at[:batch].set(c).at[batch].set(c_ctx)
    mod = modulation_vectors(c_rows, mod_w, mod_b)
    w_in_b = w_in.astype(BF16)
    w_out_b = w_out.astype(BF16)
    for l in range(depth):
        ctx_out = l < depth - 1
        lam_init = 0.8 - 0.6 * math.exp(-0.3 * l)
        mx = mod[l, :batch].reshape(batch, 6, 1, d)
        mc = jnp.broadcast_to(mod[l, batch].reshape(1, 6, 1, d), (batch, 6, 1, d))
        rw = (rw_mu_prev[l], rw_mu_next[l], rw_w0[l], rw_w_up[l], rw_a0[l], rw_a_up[l], rw_g_up[l],
              rw_k_k[l], rw_k_a[l], rw_r_k[l], rw_ln_w[l], rw_ln_b[l])
        da = (da_q_gain[l], da_k_gain[l], da_lq1[l], da_lk1[l], da_lq2[l], da_lk2[l], da_sub_gain[l])
        rwx, dax, ftx, plx = input_projection(x, norm1_w[l], mx[:, 1], mx[:, 0], w_in_b[l], gw)
        rwc, dac, ftc, plc = input_projection(ctx, norm1_w[l], mc[:, 1], mc[:, 0], w_in_b[l], gw)
        ax, ac = rwkv_mixer(rwx, rwc, rw, gw, ctx_out)
        bx, bc = diff_attention(dax, dac, da, gw, lam_init, ctx_out)
        fx = fourier_mix(ftx, ft_w[l])
        px = pool_mix(plx, pl_w[l], pl_scale[l])
        rw_out = (rw_r_k[l], rw_ln_w[l], rw_ln_b[l])
        x = output_projection(x, mx[:, 2], ax, rw_out, (bx, fx, px), w_out_b[l])
        streams = [(x, mx[:, 4], mx[:, 3], mx[:, 5])]
        if ctx_out:
            fc = fourier_mix(ftc, ft_w[l])
            pc = pool_mix(plc, pl_w[l], pl_scale[l])
            ctx = output_projection(ctx, mc[:, 2], ac, rw_out, (bc, fc, pc), w_out_b[l])
            streams.append((ctx, mc[:, 4], mc[:, 3], mc[:, 5]))
        outs = moe_residual(streams, norm2_w[l], l, moe_router[l], moe_w_gate, moe_w_up, moe_w_down)
        x = outs[0]
        if ctx_out:
            ctx = outs[1]
    return x
```

```python
import functools
import math

import numpy as np
import jax
import jax.numpy as jnp
from jax import lax
from jax.experimental import pallas as pl
from jax.experimental.pallas import tpu as pltpu

F32 = jnp.float32
BF16 = jnp.bfloat16

N_MIXERS = 4
HEAD = 64
NORM_EPS = 1e-6
RW_LN_EPS = 64e-5
GRID_W = 64
DA_QK = HEAD // 2
ROPE_BASE = 10000.0
POOL_WINDOWS = (2, 4, 8, 16)
CAPACITY_FACTOR = 2

LANES = 128
SUBLANES = 8
VMEM_LIMIT_BYTES = 56 * 1024 * 1024

ROW_BLOCK = 512
MATMUL_ROW_BLOCK = 1024

RW_CHUNK = 64
RW_CHUNKS_PER_STEP = 16


def _cparams(*sem):
    return pltpu.CompilerParams(dimension_semantics=sem, vmem_limit_bytes=VMEM_LIMIT_BYTES)


def _row_block(t, target):
    return min(target, t)


def _dot(a, b):
    return jnp.dot(a.astype(BF16), b.astype(BF16), preferred_element_type=F32)


def _split_bf16(x):
    hi = x.astype(BF16)
    return hi, (x - hi.astype(F32)).astype(BF16)


def _dot_f32(a, b, a_exact=False, b_exact=False):
    dot = lambda x, y: jnp.dot(x, y, preferred_element_type=F32)
    a_hi, a_lo = (a.astype(BF16), None) if a_exact else _split_bf16(a)
    b_hi, b_lo = (b.astype(BF16), None) if b_exact else _split_bf16(b)
    out = dot(a_hi, b_hi)
    if a_lo is not None:
        out = out + dot(a_lo, b_hi)
    if b_lo is not None:
        out = out + dot(a_hi, b_lo)
    return out


def _dot_tri(a, b):
    return _dot(a, b)


def _dot_nt(a, b):
    dn = (((1,), (1,)), ((), ()))
    return lax.dot_general(a.astype(BF16), b.astype(BF16), dn, preferred_element_type=F32)


def _dot_tn(a, b):
    dn = (((0,), (0,)), ((), ()))
    return lax.dot_general(a.astype(BF16), b.astype(BF16), dn, preferred_element_type=F32)


def _sigmoid(x):
    return 1.0 / (1.0 + jnp.exp(-x))


def _block_ones(n, blk, value=1.0):
    i = np.arange(n) // blk
    return jnp.asarray((i[:, None] == i[None, :]).astype(np.float32) * value)


def _const_spec(shape):
    nd = len(shape)
    return pl.BlockSpec(shape, lambda *_: (0,) * nd)


def _mod_kernel(c_ref, w_ref, b_ref, o_ref):
    c = c_ref[...]
    o_ref[...] = _dot_f32(c * _sigmoid(c), w_ref[...]) + b_ref[...]


def modulation_vectors(c_rows, mod_w, mod_b):
    depth, d, n = mod_w.shape
    tn = 1536
    return pl.pallas_call(
        _mod_kernel,
        grid=(depth, n // tn),
        in_specs=[
            pl.BlockSpec((SUBLANES, d), lambda l, j: (0, 0)),
            pl.BlockSpec((None, d, tn), lambda l, j: (l, 0, j)),
            pl.BlockSpec((None, 1, tn), lambda l, j: (l, 0, j)),
        ],
        out_specs=pl.BlockSpec((None, SUBLANES, tn), lambda l, j: (l, 0, j)),
        out_shape=jax.ShapeDtypeStruct((depth, SUBLANES, n), F32),
        compiler_params=_cparams("parallel", "parallel"),
    )(c_rows, mod_w, mod_b.reshape(depth, 1, n))


def _modulated_norm(x, gain, scale, shift):
    ms = jnp.mean(x * x, axis=-1, keepdims=True)
    return (x * lax.rsqrt(ms + NORM_EPS) * gain) * (1.0 + scale) + shift


def _inproj_kernel(splits, x_ref, gain_ref, sc_ref, sh_ref, w_ref, *o_refs):
    h = _modulated_norm(x_ref[...], gain_ref[...], sc_ref[...], sh_ref[...]).astype(BF16)
    for (lo, hi), o_ref in zip(splits, o_refs):
        o_ref[...] = jnp.dot(h, w_ref[:, lo:hi], preferred_element_type=F32)


def input_projection(x, gain, scale, shift, w_in_bf16, group_w):
    b, t, d = x.shape
    rw_cols = w_in_bf16.shape[1] - 3 * group_w - 2 * group_w
    cuts = [0, rw_cols, rw_cols + 3 * group_w, rw_cols + 4 * group_w, rw_cols + 5 * group_w]
    splits = tuple((cuts[i], cuts[i + 1]) for i in range(4))
    tm = _row_block(t, MATMUL_ROW_BLOCK)
    row = pl.BlockSpec((None, 1, d), lambda bi, i: (bi, 0, 0))
    return pl.pallas_call(
        functools.partial(_inproj_kernel, splits),
        grid=(b, t // tm),
        in_specs=[
            pl.BlockSpec((None, tm, d), lambda bi, i: (bi, i, 0)),
            _const_spec((1, d)),
            row, row,
            _const_spec(w_in_bf16.shape),
        ],
        out_specs=[pl.BlockSpec((None, tm, hi - lo), lambda bi, i: (bi, i, 0)) for lo, hi in splits],
        out_shape=[jax.ShapeDtypeStruct((b, t, hi - lo), F32) for lo, hi in splits],
        compiler_params=_cparams("parallel", "parallel"),
    )(x, gain.reshape(1, d), scale, shift, w_in_bf16)


def _halo_specs(tm, t, width):
    nb8 = t // SUBLANES
    r8 = tm // SUBLANES
    return [
        pl.BlockSpec((None, tm, width), lambda b, i: (b, i, 0)),
        pl.BlockSpec((None, SUBLANES, width), lambda b, i: (b, jnp.maximum(i * r8 - 1, 0), 0)),
        pl.BlockSpec((None, SUBLANES, width), lambda b, i: (b, jnp.minimum((i + 1) * r8, nb8 - 1), 0)),
    ]


def _stage_with_halo(buf_ref, main_ref, prev_ref, next_ref):
    tm = main_ref.shape[0]
    i = pl.program_id(1)
    n = pl.num_programs(1)
    buf_ref[SUBLANES:SUBLANES + tm, :] = main_ref[...]
    buf_ref[0:SUBLANES, :] = jnp.where(i > 0, prev_ref[...], 0.0)
    buf_ref[SUBLANES + tm:2 * SUBLANES + tm, :] = jnp.where(i < n - 1, next_ref[...], 0.0)


def _rwkv_prep_kernel(gw, rw_ref, prev_ref, next_ref, mup_ref, mun_ref, kk_ref_, ka_ref, w0_ref, wup_ref,
                      a0_ref, aup_ref, gup_ref, hsum_ref,
                      r_o, k_o, v_o, kk_o, gate_o, lwf_o, kaf_o, kdf_o, lwb_o, kab_o, kdb_o, buf_ref):
    tm = rw_ref.shape[0]
    _stage_with_halo(buf_ref, rw_ref, prev_ref, next_ref)
    p = buf_ref[SUBLANES:SUBLANES + tm, :]
    prev = buf_ref[SUBLANES - 1:SUBLANES - 1 + tm, :]
    nxt = buf_ref[SUBLANES + 1:SUBLANES + 1 + tm, :]
    u = p + mup_ref[...] * (prev - p) + mun_ref[...] * (nxt - p)
    r = u[:, 0:gw]
    k = u[:, gw:2 * gw]
    v = u[:, 2 * gw:3 * gw]
    lora_w = u[:, 3 * gw:3 * gw + LANES]
    lora_a = u[:, 3 * gw + LANES:3 * gw + 2 * LANES]
    g = u[:, 3 * gw + 2 * LANES:3 * gw + 3 * LANES]
    kk = k * kk_ref_[...]
    ss = _dot_f32(kk * kk, hsum_ref[...], b_exact=True)
    kk = kk * lax.rsqrt(jnp.maximum(ss, 1e-24))
    zw = _dot(jnp.tanh(lora_w), wup_ref[...]) + w0_ref[...]
    za = _dot(lora_a, aup_ref[...]) + a0_ref[...]
    logw = -_sigmoid(zw) * math.exp(-0.5)
    a = _sigmoid(za)
    r_o[...] = r
    k_o[...] = k
    v_o[...] = v
    kk_o[...] = kk
    gate_o[...] = _dot(_sigmoid(g), gup_ref[...])
    ka = ka_ref[...]
    for d, (lw_o, kka_o, kd_o) in enumerate(((lwf_o, kaf_o, kdf_o), (lwb_o, kab_o, kdb_o))):
        a_d = a[:, d * gw:(d + 1) * gw]
        lw_o[...] = logw[:, d * gw:(d + 1) * gw]
        kka_o[...] = kk * a_d
        kd_o[...] = k * (1.0 + (a_d - 1.0) * ka)


def _blockdiag2(m):
    r, c = m.shape[1:]
    z = jnp.zeros((r, c), m.dtype)
    return jnp.concatenate([jnp.concatenate([m[0], z], 1), jnp.concatenate([z, m[1]], 1)], 0)


def rwkv_prepare(rw, params, gw):
    (mu_prev, mu_next, w0, w_up, a0, a_up, g_up, k_k, k_a, r_k, ln_w, ln_b) = params
    b, t, cols = rw.shape
    tm = _row_block(t, ROW_BLOCK)
    row = lambda v: v.reshape(1, -1)
    small = [row(mu_prev), row(mu_next), row(k_k), row(k_a), row(w0), _blockdiag2(w_up), row(a0),
             _blockdiag2(a_up), g_up, _block_ones(gw, HEAD)]
    out = jax.ShapeDtypeStruct((b, t, gw), F32)
    return pl.pallas_call(
        functools.partial(_rwkv_prep_kernel, gw),
        grid=(b, t // tm),
        in_specs=_halo_specs(tm, t, cols) + [_const_spec(s.shape) for s in small],
        out_specs=[pl.BlockSpec((None, tm, gw), lambda bi, i: (bi, i, 0))] * 11,
        out_shape=[out] * 11,
        scratch_shapes=[pltpu.VMEM((tm + 2 * SUBLANES, cols), F32)],
        compiler_params=_cparams("parallel", "parallel"),
    )(rw, rw, rw, *small)


RW_BUILD_CHUNKS_PER_STEP = 4


def _rwkv_chunk_kernel(ncs, r_ref, v_ref, kk_ref, lwf_ref, kaf_ref, kdf_ref, lwb_ref, kab_ref, kdb_ref,
                       y0f_o, qf_o, mf_o, g0f_o, y0b_o, qb_o, mb_o, g0b_o):
    L = RW_CHUNK
    rows = lax.broadcasted_iota(jnp.int32, (L, L), 0)
    cols = lax.broadcasted_iota(jnp.int32, (L, L), 1)
    eye = jnp.where(rows == cols, 1.0, 0.0)
    rows2 = lax.broadcasted_iota(jnp.int32, (L, 2 * L), 0)
    cols2 = lax.broadcasted_iota(jnp.int32, (L, 2 * L), 1) % L
    lane = lax.broadcasted_iota(jnp.int32, (1, LANES), 1)
    head_masks = [(lane >= h * HEAD) & (lane < (h + 1) * HEAD) for h in range(2)]
    r2 = lax.broadcasted_iota(jnp.int32, (LANES, LANES), 0)
    c2 = lax.broadcasted_iota(jnp.int32, (LANES, LANES), 1)
    same_head = (r2 // HEAD) == (c2 // HEAD)
    diag = r2 == c2
    directions = ((False, lwf_ref, kaf_ref, kdf_ref, y0f_o, qf_o, mf_o, g0f_o),
                  (True, lwb_ref, kab_ref, kdb_ref, y0b_o, qb_o, mb_o, g0b_o))

    probs = []
    for c in range(ncs):
        rs = slice(c * L, (c + 1) * L)
        for pair in range(2):
            sl = slice(pair * LANES, (pair + 1) * LANES)
            r = r_ref[rs, sl]
            v = v_ref[rs, sl]
            kk = kk_ref[rs, sl]
            for reverse, lw_ref, ka_ref, kd_ref, y0_o, q_o, m_o, g0_o in directions:
                incl = (cols >= rows) if reverse else (cols <= rows)
                strict = (cols > rows) if reverse else (cols < rows)
                incl2 = (cols2 >= rows2) if reverse else (cols2 <= rows2)
                logw = lw_ref[rs, sl]
                kd = kd_ref[rs, sl]
                a = -ka_ref[rs, sl]
                cum = _dot_f32(jnp.where(incl, 1.0, 0.0), logw, a_exact=True)
                total = jnp.sum(logw, axis=0, keepdims=True)
                g_inv = jnp.exp(-cum)
                g_tail = jnp.exp(total - cum)
                bd = kk * jnp.exp(cum - logw)
                rd = r * jnp.exp(cum)
                probs.append(dict(
                    incl2=incl2, strict=strict, v=v, bd=bd, rd=rd, total=total,
                    lhs=jnp.concatenate([bd, rd], axis=0), rhs=jnp.concatenate([a * g_inv, kd * g_inv], axis=0),
                    tails=jnp.concatenate([kd * g_tail, a * g_tail], axis=0),
                    outs=(y0_o, q_o, m_o, g0_o), rs=rs, sl=sl, c=c, pair=pair))

    heads = []
    for p in probs:
        for h in range(2):
            gram = _dot_nt(jnp.where(head_masks[h], p["lhs"], 0.0), p["rhs"])
            heads.append(dict(
                p=p, h=h,
                nil=jnp.where(p["strict"], gram[:L, :L], 0.0),
                aak=jnp.where(p["strict"], gram[:L, L:], 0.0),
                ara_ark=jnp.where(p["incl2"], gram[L:, :], 0.0)))

    for hd in heads:
        hd["acc"] = eye + hd["nil"]
        hd["pow"] = hd["nil"]
    span = 2
    while span < L:
        for hd in heads:
            hd["pow"] = _dot_tri(hd["pow"], hd["pow"])
        for hd in heads:
            hd["acc"] = hd["acc"] + _dot_tri(hd["acc"], hd["pow"])
        span *= 2

    for hd in heads:
        hd["aakv"] = _dot(hd["aak"], hd["p"]["v"])
    for hd in heads:
        wp = _dot_tri(hd["acc"], jnp.concatenate([hd["aakv"], hd["p"]["bd"]], axis=1))
        hd["w1"], hd["pm"] = wp[:, :LANES], wp[:, LANES:]
    for hd in heads:
        rhs = jnp.concatenate([jnp.concatenate([hd["w1"], hd["pm"]], axis=1),
                               jnp.concatenate([hd["p"]["v"], jnp.zeros((L, LANES), F32)], axis=1)], axis=0)
        yq = _dot(hd["ara_ark"], rhs)
        hd["y0"], hd["q"] = yq[:, :LANES], yq[:, LANES:] + hd["p"]["rd"]

    for i, p in enumerate(probs):
        h0, h1 = heads[2 * i], heads[2 * i + 1]
        pick = lambda key: jnp.where(head_masks[0], h0[key], h1[key])
        y0_o, q_o, m_o, g0_o = p["outs"]
        y0_o[p["rs"], p["sl"]] = pick("y0")
        q_o[p["rs"], p["sl"]] = pick("q")
        vw = jnp.concatenate([p["v"], pick("w1")], axis=0)
        g0_o[p["c"], p["pair"]] = jnp.where(same_head, _dot_tn(p["tails"], vw), 0.0)
        m_o[p["c"], p["pair"]] = (jnp.where(same_head, _dot_tn(p["tails"][L:], pick("pm")), 0.0)
                                  + jnp.where(diag, jnp.exp(p["total"]), 0.0))


def rwkv_chunks(r, v, kk, lwf, kaf, kdf, lwb, kab, kdb):
    b, t, gw = r.shape
    L = RW_CHUNK
    nc = t // L
    ncs = RW_BUILD_CHUNKS_PER_STEP if nc % RW_BUILD_CHUNKS_PER_STEP == 0 else 1
    tok = pl.BlockSpec((None, ncs * L, gw), lambda bi, i: (bi, i, 0))
    mat = pl.BlockSpec((None, ncs, 2, LANES, LANES), lambda bi, i: (bi, i, 0, 0, 0))
    tok_s = jax.ShapeDtypeStruct((b, t, gw), F32)
    mat_s = jax.ShapeDtypeStruct((b, nc, 2, LANES, LANES), F32)
    return pl.pallas_call(
        functools.partial(_rwkv_chunk_kernel, ncs),
        grid=(b, nc // ncs),
        in_specs=[tok] * 9,
        out_specs=[tok, tok, mat, mat] * 2,
        out_shape=[tok_s, tok_s, mat_s, mat_s] * 2,
        compiler_params=_cparams("parallel", "parallel"),
    )(r, v, kk, lwf, kaf, kdf, lwb, kab, kdb)


def _rwkv_scan_kernel(cps, h0_ref, y0f_ref, qf_ref, mf_ref, g0f_ref, y0b_ref, qb_ref, mb_ref, g0b_ref,
                      yf_o, yb_o, hfin_o, h_ref):
    L = RW_CHUNK
    j = pl.program_id(1)

    @pl.when(j == 0)
    def _():
        h_ref[...] = h0_ref[...]

    for step in range(cps):
        for d, (y0_ref, q_ref, m_ref, g0_ref, y_o) in enumerate(
                ((y0f_ref, qf_ref, mf_ref, g0f_ref, yf_o), (y0b_ref, qb_ref, mb_ref, g0b_ref, yb_o))):
            c = step if d == 0 else cps - 1 - step
            rows = slice(c * L, (c + 1) * L)
            for pair in range(2):
                sl = slice(pair * LANES, (pair + 1) * LANES)
                h = h_ref[d, pair]
                y_o[rows, sl] = y0_ref[rows, sl] + _dot(q_ref[rows, sl], h)
                h_ref[d, pair] = _dot(m_ref[c, pair], h) + g0_ref[c, pair]

    @pl.when(j == pl.num_programs(1) - 1)
    def _():
        hfin_o[...] = h_ref[...]


def rwkv_scan(h0, y0f, qf, mf, g0f, y0b, qb, mb, g0b):
    b, t, gw = y0f.shape
    L = RW_CHUNK
    nc = t // L
    cps = min(RW_CHUNKS_PER_STEP, nc)
    nb = nc // cps
    tm = cps * L
    tok_f = pl.BlockSpec((None, tm, gw), lambda bi, i: (bi, i, 0))
    tok_b = pl.BlockSpec((None, tm, gw), lambda bi, i: (bi, nb - 1 - i, 0))
    mat_f = pl.BlockSpec((None, cps, 2, LANES, LANES), lambda bi, i: (bi, i, 0, 0, 0))
    mat_b = pl.BlockSpec((None, cps, 2, LANES, LANES), lambda bi, i: (bi, nb - 1 - i, 0, 0, 0))
    st = pl.BlockSpec((None, 2, 2, LANES, LANES), lambda bi, i: (bi, 0, 0, 0, 0))
    tok_s = jax.ShapeDtypeStruct((b, t, gw), F32)
    return pl.pallas_call(
        functools.partial(_rwkv_scan_kernel, cps),
        grid=(b, nb),
        in_specs=[st, tok_f, tok_f, mat_f, mat_f, tok_b, tok_b, mat_b, mat_b],
        out_specs=[tok_f, tok_b, st],
        out_shape=[tok_s, tok_s, jax.ShapeDtypeStruct((b, 2, 2, LANES, LANES), F32)],
        scratch_shapes=[pltpu.VMEM((2, 2, LANES, LANES), F32)],
        compiler_params=_cparams("parallel", "arbitrary"),
    )(h0, y0f, qf, mf, g0f, y0b, qb, mb, g0b)


def _rwkv_head_output(yf, yb, r, k, v, gate, r_k, ln_w, ln_b, hmean):
    y = yf + yb
    mu = _dot_f32(y, hmean, b_exact=True)
    yc = y - mu
    var = _dot_f32(yc * yc, hmean, b_exact=True)
    yn = yc * lax.rsqrt(var + RW_LN_EPS) * ln_w + ln_b
    bonus = _dot_f32(r * k * r_k, hmean, b_exact=True) * float(HEAD) * v
    return (yn + bonus) * gate


def rwkv_mixer(rwx, rwc, params, gw, ctx_out):
    sx = rwkv_prepare(rwx, params, gw)
    sc = rwkv_prepare(rwc, params, gw)
    (rx, kx, vx, kkx, gx), dx = sx[:5], sx[5:]
    (rc, kc, vc, kkc, gc), dc = sc[:5], sc[5:]
    cx = rwkv_chunks(rx, vx, kkx, *dx)
    cc = rwkv_chunks(rc, vc, kkc, *dc)
    b = rwx.shape[0]
    h0 = jnp.zeros((b, 2, 2, LANES, LANES), F32)
    ycf, ycb, h_ctx = rwkv_scan(h0, *cc)
    yxf, yxb, _ = rwkv_scan(h_ctx, *cx)
    out_x = (yxf, yxb, rx, kx, vx, gx)
    out_c = (ycf, ycb, rc, kc, vc, gc) if ctx_out else None
    return out_x, out_c


def _rope_tables(n_tokens, reps):
    rows = n_tokens // GRID_W
    row = np.repeat(np.arange(rows), GRID_W).astype(np.float64)
    col = np.tile(np.arange(GRID_W), rows).astype(np.float64)
    n_freq = DA_QK // 4
    inv = ROPE_BASE ** (-np.arange(n_freq, dtype=np.float64) / n_freq)
    ar = row[:, None] * inv
    ac = col[:, None] * inv
    ang = np.concatenate([ar, ar, ac, ac], axis=-1)
    cos = np.tile(np.cos(ang), (1, reps)).astype(np.float32)
    sin = np.tile(np.sin(ang), (1, reps)).astype(np.float32)
    return jnp.asarray(cos), jnp.asarray(sin)


def _attn_prep_kernel(gw, rope, da_ref, qg_ref, kg_ref, gmean_ref, *rest):
    if rope:
        cos_ref, sin_ref, q_o, k_o, v_o = rest
    else:
        q_o, k_o, v_o = rest
    da = da_ref[...]
    gmean = gmean_ref[...]
    lane = lax.broadcasted_iota(jnp.int32, (1, gw), 1)
    first_half = (lane % (DA_QK // 2)) < (DA_QK // 4)

    def norm_rope(x, gain):
        ms = _dot_f32(x * x, gmean, b_exact=True)
        y = x * lax.rsqrt(ms + NORM_EPS) * gain
        if rope:
            quarter = DA_QK // 4
            rot = jnp.where(first_half, -pltpu.roll(y, gw - quarter, 1), pltpu.roll(y, quarter, 1))
            y = y * cos_ref[...] + rot * sin_ref[...]
        return y

    tm = da.shape[0]
    q = norm_rope(da[:, 0:gw], qg_ref[...]) * (DA_QK ** -0.5 * LOG2_E)
    q_t = jnp.transpose(q)
    row_map = lax.broadcasted_iota(jnp.int32, (gw, 1), 0) // DA_QK
    for c in range(gw // DA_QK):
        q_o[c] = jnp.where(row_map == c, q_t, 0.0).astype(q_o.dtype)
    k_o[...] = norm_rope(da[:, gw:2 * gw], kg_ref[...]).astype(k_o.dtype)
    v_t = jnp.transpose(da[:, 2 * gw:3 * gw])
    pad = jnp.where(lax.broadcasted_iota(jnp.int32, (LANES - HEAD, tm), 0) == 0, 1.0, 0.0)
    for h in range(gw // HEAD):
        v_o[h] = jnp.concatenate([v_t[h * HEAD:(h + 1) * HEAD], pad], axis=0).astype(v_o.dtype)


def attention_prepare(da, q_gain, k_gain, gw, rope):
    b, t, cols = da.shape
    tm = _row_block(t, ROW_BLOCK)
    reps = gw // DA_QK
    heads = gw // HEAD
    small = [jnp.tile(q_gain, reps).reshape(1, gw), jnp.tile(k_gain, reps).reshape(1, gw),
             _block_ones(gw, DA_QK, 1.0 / DA_QK)]
    args = [da] + small
    in_specs = [pl.BlockSpec((None, tm, cols), lambda bi, i: (bi, i, 0))] + [_const_spec(s.shape) for s in small]
    if rope:
        cos, sin = _rope_tables(t, reps)
        args += [cos, sin]
        in_specs += [pl.BlockSpec((tm, gw), lambda bi, i: (i, 0))] * 2
    return pl.pallas_call(
        functools.partial(_attn_prep_kernel, gw, rope),
        grid=(b, t // tm),
        in_specs=in_specs,
        out_specs=[pl.BlockSpec((None, reps, gw, tm), lambda bi, i: (bi, 0, 0, i)),
                   pl.BlockSpec((None, tm, gw), lambda bi, i: (bi, i, 0)),
                   pl.BlockSpec((None, heads, LANES, tm), lambda bi, i: (bi, 0, 0, i))],
        out_shape=[jax.ShapeDtypeStruct((b, reps, gw, t), BF16),
                   jax.ShapeDtypeStruct((b, t, gw), BF16),
                   jax.ShapeDtypeStruct((b, heads, LANES, t), BF16)],
        compiler_params=_cparams("parallel", "parallel"),
    )(*args)


LOG2_E = 1.4426950408889634
FLASH_MAX_KV_BLOCK = 2816


def _flash_kernel(lam_init, bounded_ref, bound_ref, qt_ref, k_ref, vt_ref, lq1_ref, lk1_ref, lq2_ref, lk2_ref,
                  sg_ref, o_ref, m_ref, acc_ref):
    j = pl.program_id(2)
    n_maps = qt_ref.shape[0]

    @pl.when(j == 0)
    def _():
        m_ref[...] = jnp.full(m_ref.shape, -1e30, F32)
        acc_ref[...] = jnp.zeros(acc_ref.shape, F32)

    k = k_ref[...]
    bounded = bounded_ref[0] == 1

    @pl.when(bounded)
    def _():
        shift = bound_ref[0]
        for c in range(n_maps):
            s = jnp.dot(k, qt_ref[c], preferred_element_type=F32)
            p = jnp.exp2(s - shift).astype(BF16)
            acc_ref[c] += jnp.dot(vt_ref[c // 2], p, preferred_element_type=F32)

    @pl.when(jnp.logical_not(bounded))
    def _():
        for c in range(n_maps):
            s = jnp.dot(k, qt_ref[c], preferred_element_type=F32)
            m_old = m_ref[c]
            m_new = jnp.maximum(m_old, jnp.max(s, axis=0, keepdims=True))
            p = jnp.exp2(s - m_new).astype(BF16)
            acc_ref[c] = (jnp.exp2(m_old - m_new) * acc_ref[c]
                          + jnp.dot(vt_ref[c // 2], p, preferred_element_type=F32))
            m_ref[c] = m_new

    @pl.when(j == pl.num_programs(2) - 1)
    def _():
        lam = (jnp.exp(jnp.sum(lq1_ref[...] * lk1_ref[...], axis=-1, keepdims=True))
               - jnp.exp(jnp.sum(lq2_ref[...] * lk2_ref[...], axis=-1, keepdims=True)) + lam_init)
        for h in range(n_maps // 2):
            a0 = acc_ref[2 * h]
            a1 = acc_ref[2 * h + 1]
            o = a0[:HEAD] / a0[HEAD:HEAD + 1] - lam * (a1[:HEAD] / a1[HEAD:HEAD + 1])
            ms = jnp.mean(o * o, axis=0, keepdims=True)
            o = o * lax.rsqrt(ms + NORM_EPS) * sg_ref[...] * (1.0 - lam_init)
            o_ref[:, h * HEAD:(h + 1) * HEAD] = jnp.transpose(o)


def _largest_divisor(n, cap, multiple):
    best = None
    for d in range(multiple, cap + 1, multiple):
        if n % d == 0:
            best = d
    return best if best is not None else n


FLASH_SAFE_SCORE_BOUND = 60.0


def _score_bound(q_gain, k_gain):
    bound = (1.02 * DA_QK * DA_QK ** -0.5 * LOG2_E) * jnp.max(jnp.abs(q_gain)) * jnp.max(jnp.abs(k_gain))
    return (bound <= FLASH_SAFE_SCORE_BOUND).astype(jnp.int32).reshape(1), bound.astype(F32).reshape(1)


def diff_attention_core(qt, k, vt, gains, lam_params, sub_gain, lam_init):
    b, n_maps, gw, t = qt.shape
    s = k.shape[1]
    heads = vt.shape[1]
    tq = min(512, t)
    tk = _largest_divisor(s, FLASH_MAX_KV_BLOCK, 2 * LANES)
    small = [p.reshape(1, -1) for p in lam_params] + [sub_gain.reshape(-1, 1)]
    bounded, bound = _score_bound(*gains)
    return pl.pallas_call(
        functools.partial(_flash_kernel, lam_init),
        grid_spec=pltpu.PrefetchScalarGridSpec(
            num_scalar_prefetch=2,
            grid=(b, t // tq, s // tk),
            in_specs=[
                pl.BlockSpec((None, n_maps, gw, tq), lambda bi, i, j, *_: (bi, 0, 0, i)),
                pl.BlockSpec((None, tk, gw), lambda bi, i, j, *_: (bi, j, 0)),
                pl.BlockSpec((None, heads, LANES, tk), lambda bi, i, j, *_: (bi, 0, 0, j)),
            ] + [pl.BlockSpec(x.shape, lambda bi, i, j, *_: (0, 0)) for x in small],
            out_specs=pl.BlockSpec((None, tq, gw), lambda bi, i, j, *_: (bi, i, 0)),
            scratch_shapes=[pltpu.VMEM((n_maps, 1, tq), F32), pltpu.VMEM((n_maps, LANES, tq), F32)],
        ),
        out_shape=jax.ShapeDtypeStruct((b, t, gw), F32),
        compiler_params=_cparams("parallel", "parallel", "arbitrary"),
    )(bounded, bound, qt, k, vt, *small)


def diff_attention(dax, dac, params, gw, lam_init, ctx_out):
    q_gain, k_gain, lq1, lk1, lq2, lk2, sub_gain = params
    qtx, kx, vtx = attention_prepare(dax, q_gain, k_gain, gw, rope=True)
    qtc, kc, vtc = attention_prepare(dac, q_gain, k_gain, gw, rope=False)
    k = jnp.concatenate([kx, kc], axis=1)
    vt = jnp.concatenate([vtx, vtc], axis=3)
    lam_params = (lq1, lk1, lq2, lk2)
    gains = (q_gain, k_gain)
    out_x = diff_attention_core(qtx, k, vt, gains, lam_params, sub_gain, lam_init)
    out_c = diff_attention_core(qtc, kc, vtc, gains, lam_params, sub_gain, lam_init) if ctx_out else None
    return out_x, out_c


FT_RADIX = 64


def _dft_cos_sin(n, scale=1.0):
    i = np.arange(n)
    ang = 2.0 * np.pi * ((i[:, None] * i[None, :]) % n) / n
    return np.cos(ang) * scale, np.sin(ang) * scale


def _channel_dft(gw, scale):
    c, s = _dft_cos_sin(HEAD, scale)
    eye = np.eye(gw // HEAD)
    return jnp.asarray(np.concatenate([np.kron(eye, c), np.kron(eye, s)], axis=0).astype(np.float32))


def _fnet_stage1_kernel(z_ref, grh_ref, grl_ref, gih_ref, gil_ref, or_ref, oi_ref):
    dot = lambda a, b: jnp.dot(a, b, preferred_element_type=F32)
    for j in range(z_ref.shape[1]):
        x_hi, x_lo = _split_bf16(z_ref[:, j, :])
        or_ref[j] = dot(grh_ref[j], x_hi) + (dot(grl_ref[j], x_hi) + dot(grh_ref[j], x_lo))
        oi_ref[j] = dot(gih_ref[j], x_hi) + (dot(gil_ref[j], x_hi) + dot(gih_ref[j], x_lo))


def _matmul_f32_kernel(a_ref, b_ref, o_ref):
    o_ref[...] = _dot_f32(a_ref[...], b_ref[...])


def _channel_dft_times(chan, w_f):
    return pl.pallas_call(
        _matmul_f32_kernel,
        out_shape=jax.ShapeDtypeStruct((chan.shape[0], w_f.shape[1]), F32),
    )(chan, w_f)


def _fnet_stage2_kernel(gw, br_ref, bi_ref, rot_ref, chanw_ref, o_ref):
    n1, groups, _ = br_ref.shape
    x = jnp.concatenate([jnp.concatenate([br_ref[:, g, :] for g in range(groups)], axis=1),
                         jnp.concatenate([bi_ref[:, g, :] for g in range(groups)], axis=1)], axis=0)
    p = _dot_f32(rot_ref[...], x)
    rows = jnp.concatenate(
        [jnp.concatenate([p[:n1, g * gw:(g + 1) * gw], p[n1:, g * gw:(g + 1) * gw]], axis=1) for g in range(groups)],
        axis=0)
    out = _dot_f32(rows, chanw_ref[...])
    for g in range(groups):
        o_ref[:, g, :] = out[g * n1:(g + 1) * n1]


def fourier_mix_long(z, w_f):
    b, t, gw = z.shape
    n1 = FT_RADIX
    n2 = t // n1
    k2 = np.arange(n2)[None, :, None]
    n = np.arange(n1)[:, None, None] + n1 * np.arange(n2)[None, None, :]
    ang = 2.0 * np.pi * ((k2 * n) % t) / t
    tables = []
    for g in (np.cos(ang), -np.sin(ang)):
        g_hi = jnp.asarray(g.astype(np.float32)).astype(BF16)
        g_lo = (jnp.asarray(g.astype(np.float32)) - g_hi.astype(F32)).astype(BF16)
        tables += [g_hi, g_lo]
    j8 = SUBLANES
    table_spec = pl.BlockSpec((j8, n2, n2), lambda bi_, i: (i, 0, 0))
    br, bi = pl.pallas_call(
        _fnet_stage1_kernel,
        grid=(b, n1 // j8),
        in_specs=[pl.BlockSpec((None, n2, j8, gw), lambda bi_, i: (bi_, 0, i, 0))] + [table_spec] * 4,
        out_specs=[pl.BlockSpec((None, j8, n2, gw), lambda bi_, i: (bi_, i, 0, 0))] * 2,
        out_shape=[jax.ShapeDtypeStruct((b, n1, n2, gw), F32)] * 2,
        compiler_params=_cparams("parallel", "parallel"),
    )(z.reshape(b, n2, n1, gw), *tables)
    c64, s64 = _dft_cos_sin(n1)
    rot = jnp.asarray(np.block([[c64, s64], [-s64, c64]]).astype(np.float32))
    chanw = _channel_dft_times(_channel_dft(gw, 1.0 / math.sqrt(t * HEAD)), w_f)
    blk = pl.BlockSpec((None, n1, j8, gw), lambda bi_, i: (bi_, 0, i, 0))
    out = pl.pallas_call(
        functools.partial(_fnet_stage2_kernel, gw),
        grid=(b, n2 // j8),
        in_specs=[blk, blk, _const_spec(rot.shape), _const_spec(chanw.shape)],
        out_specs=blk,
        out_shape=jax.ShapeDtypeStruct((b, n1, n2, gw), F32),
        compiler_params=_cparams("parallel", "parallel"),
    )(br, bi, rot, chanw)
    return out.reshape(b, t, gw)


def _fnet_dense_kernel(z_ref, ct_ref, st_ref, chanw_ref, o_ref):
    z = z_ref[...]
    pr = _dot_f32(ct_ref[...], z)
    pi = -_dot_f32(st_ref[...], z)
    o_ref[...] = _dot_f32(jnp.concatenate([pr, pi], axis=1), chanw_ref[...])


def fourier_mix_short(z, w_f):
    b, t, gw = z.shape
    ct, st = _dft_cos_sin(t)
    ct = jnp.asarray(ct.astype(np.float32))
    st = jnp.asarray(st.astype(np.float32))
    chanw = _channel_dft_times(_channel_dft(gw, 1.0 / math.sqrt(t * HEAD)), w_f)
    tok = pl.BlockSpec((None, t, gw), lambda bi: (bi, 0, 0))
    return pl.pallas_call(
        _fnet_dense_kernel,
        grid=(b,),
        in_specs=[tok, _const_spec(ct.shape), _const_spec(st.shape), _const_spec(chanw.shape)],
        out_specs=tok,
        out_shape=jax.ShapeDtypeStruct((b, t, gw), F32),
        compiler_params=_cparams("parallel"),
    )(z, ct, st, chanw)


def fourier_mix(z, w_f):
    t = z.shape[1]
    if t % (FT_RADIX * SUBLANES) == 0 and t // FT_RADIX >= LANES:
        return fourier_mix_long(z, w_f)
    return fourier_mix_short(z, w_f)


def _pool_kernel(t_total, u_ref, prev_ref, next_ref, w_ref, s_ref, o_ref, buf_ref):
    tm, gw = u_ref.shape
    _stage_with_halo(buf_ref, u_ref, prev_ref, next_ref)
    at = lambda off: buf_ref[SUBLANES + off:SUBLANES + off + tm, :]
    u = at(0)
    t = pl.program_id(1) * tm + lax.broadcasted_iota(jnp.int32, (tm, 1), 0)
    lane = lax.broadcasted_iota(jnp.int32, (1, gw), 1)
    group = lane // (gw // len(POOL_WINDOWS))
    mean = jnp.zeros((tm, gw), F32)
    run = jnp.zeros((tm, gw), F32)
    half_prev = 0
    for i, w in enumerate(POOL_WINDOWS):
        half = w // 2
        for off in range(half_prev, half):
            run = run + at(-off - 1) + at(off)
        half_prev = half
        cnt = (jnp.minimum(t + half, t_total) - jnp.maximum(t - half, 0)).astype(F32)
        mean = jnp.where(group == i, run / cnt, mean)
    o_ref[...] = _dot_f32(mean - u, w_ref[...]) * s_ref[...]


def pool_mix(u, w_p, s_p):
    b, t, gw = u.shape
    tm = _row_block(t, ROW_BLOCK)
    nw, ch = w_p.shape[0], w_p.shape[1]
    w_bd = jnp.zeros((gw, gw), F32)
    for i in range(nw):
        w_bd = w_bd.at[i * ch:(i + 1) * ch, i * ch:(i + 1) * ch].set(w_p[i])
    return pl.pallas_call(
        functools.partial(_pool_kernel, t),
        grid=(b, t // tm),
        in_specs=_halo_specs(tm, t, gw) + [_const_spec((gw, gw)), _const_spec((1, gw))],
        out_specs=pl.BlockSpec((None, tm, gw), lambda bi, i: (bi, i, 0)),
        out_shape=jax.ShapeDtypeStruct((b, t, gw), F32),
        scratch_shapes=[pltpu.VMEM((tm + 2 * SUBLANES, gw), F32)],
        compiler_params=_cparams("parallel", "parallel"),
    )(u, u, u, w_bd, s_p.reshape(1, gw))


def _outproj_kernel(gw, x_ref, g_ref, yf_ref, yb_ref, r_ref, k_ref, v_ref, rg_ref, b_ref, f_ref, p_ref,
                    rk_ref, lnw_ref, lnb_ref, hmean_ref, w_ref, o_ref):
    a = _rwkv_head_output(yf_ref[...], yb_ref[...], r_ref[...], k_ref[...], v_ref[...], rg_ref[...],
                          rk_ref[...], lnw_ref[...], lnb_ref[...], hmean_ref[...])
    acc = jnp.dot(a.astype(BF16), w_ref[0:gw, :], preferred_element_type=F32)
    for i, m_ref in enumerate((b_ref, f_ref, p_ref), start=1):
        acc = acc + jnp.dot(m_ref[...].astype(BF16), w_ref[i * gw:(i + 1) * gw, :], preferred_element_type=F32)
    o_ref[...] = x_ref[...] + g_ref[...] * acc


def output_projection(x, gate, rwkv_parts, rwkv_params, mixers, w_out_bf16):
    b, t, d = x.shape
    gw = mixers[0].shape[-1]
    tm = _row_block(t, MATMUL_ROW_BLOCK)
    tok = pl.BlockSpec((None, tm, gw), lambda bi, i: (bi, i, 0))
    xs = pl.BlockSpec((None, tm, d), lambda bi, i: (bi, i, 0))
    r_k, ln_w, ln_b = rwkv_params
    small = [r_k.reshape(1, gw), ln_w.reshape(1, gw), ln_b.reshape(1, gw), _block_ones(gw, HEAD, 1.0 / HEAD)]
    return pl.pallas_call(
        functools.partial(_outproj_kernel, gw),
        grid=(b, t // tm),
        in_specs=([xs, pl.BlockSpec((None, 1, d), lambda bi, i: (bi, 0, 0))] + [tok] * 9
                  + [_const_spec(s.shape) for s in small] + [_const_spec(w_out_bf16.shape)]),
        out_specs=xs,
        out_shape=jax.ShapeDtypeStruct((b, t, d), F32),
        compiler_params=_cparams("parallel", "parallel"),
    )(x, gate, *rwkv_parts, *mixers, *small, w_out_bf16)


def _router_kernel(n_exp, x_ref, gain_ref, sc_ref, sh_ref, wr_ref, h_o, aff_o, afft_o):
    h = _modulated_norm(x_ref[...], gain_ref[...], sc_ref[...], sh_ref[...])
    h_o[...] = h
    logits = _dot_f32(h, wr_ref[...])
    lane = lax.broadcasted_iota(jnp.int32, logits.shape, 1)
    logits = jnp.where(lane < n_exp, logits, -1e30)
    e = jnp.exp(logits - jnp.max(logits, axis=-1, keepdims=True))
    aff = e / jnp.sum(e, axis=-1, keepdims=True)
    aff_o[...] = aff
    afft_o[...] = jnp.transpose(aff)[:n_exp, :]


def router(x, gain, scale, shift, w_router):
    b, t, d = x.shape
    n_exp = w_router.shape[1]
    tm = _row_block(t, MATMUL_ROW_BLOCK)
    wr = jnp.zeros((d, LANES), F32).at[:, :n_exp].set(w_router)
    row = pl.BlockSpec((None, 1, d), lambda bi, i: (bi, 0, 0))
    return pl.pallas_call(
        functools.partial(_router_kernel, n_exp),
        grid=(b, t // tm),
        in_specs=[pl.BlockSpec((None, tm, d), lambda bi, i: (bi, i, 0)), _const_spec((1, d)), row, row,
                  _const_spec(wr.shape)],
        out_specs=[pl.BlockSpec((None, tm, d), lambda bi, i: (bi, i, 0)),
                   pl.BlockSpec((None, tm, LANES), lambda bi, i: (bi, i, 0)),
                   pl.BlockSpec((None, n_exp, tm), lambda bi, i: (bi, 0, i))],
        out_shape=[jax.ShapeDtypeStruct((b, t, d), F32), jax.ShapeDtypeStruct((b, t, LANES), F32),
                   jax.ShapeDtypeStruct((b, n_exp, t), F32)],
        compiler_params=_cparams("parallel", "parallel"),
    )(x, gain.reshape(1, d), scale, shift, wr)


TOPK_EXPONENT_STEPS = 7
TOPK_MANTISSA_STEPS = 44


def _row_cumsum(x_ref, o_ref, upper_ref):
    rows, t = x_ref.shape
    carry = jnp.zeros((rows, 1), F32)
    for g in range(t // LANES):
        sl = slice(g * LANES, (g + 1) * LANES)
        local = jnp.dot(x_ref[:, sl].astype(BF16), upper_ref[...], preferred_element_type=F32) + carry
        o_ref[:, sl] = local
        carry = local[:, LANES - 1:LANES]


def _topk_kernel(cap, aff_ref, upper_ref, idx_o, sel_ref, cs_ref, local_ref, begin_ref, end_ref):
    aff = aff_ref[...]
    rows, t = aff.shape
    capf = float(cap)
    count_ge = lambda thr: jnp.sum(jnp.where(aff >= thr, 1.0, 0.0), axis=-1, keepdims=True)
    hi = jnp.full((rows, 1), 2.0, F32)
    for step in reversed(range(TOPK_EXPONENT_STEPS)):
        cand = hi * (2.0 ** -(2 ** step))
        hi = jnp.where(count_ge(cand) < capf, cand, hi)
    lo = hi * 0.5
    lo = jnp.where(count_ge(lo) >= capf, lo, 0.0)

    def bisect(_, carry):
        lo, hi = carry
        mid = 0.5 * (lo + hi)
        enough = count_ge(mid) >= capf
        return jnp.where(enough, mid, lo), jnp.where(enough, hi, mid)

    lo, hi = lax.fori_loop(0, TOPK_MANTISSA_STEPS, bisect, (lo, hi))
    above = aff >= hi
    need = capf - count_ge(hi)
    sel_ref[...] = jnp.where((aff >= lo) & jnp.logical_not(above), 1.0, 0.0)
    _row_cumsum(sel_ref, cs_ref, upper_ref)
    tied_in = (sel_ref[...] > 0.5) & (cs_ref[...] <= need)
    sel_ref[...] = jnp.where(above | tied_in, 1.0, 0.0)

    groups = t // LANES
    carry = jnp.zeros((rows, 1), F32)
    for g in range(groups):
        local = jnp.dot(sel_ref[:, g * LANES:(g + 1) * LANES].astype(BF16), upper_ref[...],
                        preferred_element_type=F32)
        local_ref[:, g, :] = local
        begin_ref[:, g:g + 1] = carry
        carry = carry + local[:, LANES - 1:LANES]
        end_ref[:, g:g + 1] = carry

    ones_g = jnp.ones((SUBLANES, groups), BF16)
    ones_l = jnp.ones((SUBLANES, LANES), BF16)

    def compact(r, _):
        begin = begin_ref[pl.ds(r, 1), :]
        end = end_ref[pl.ds(r, 1), :]
        local = local_ref[r].astype(BF16)
        for cb in range(0, cap, LANES):
            n = min(LANES, cap - cb)
            slot = (lax.broadcasted_iota(jnp.int32, (n, 1), 0) + cb).astype(F32)
            before = jnp.where(end <= slot, 1.0, 0.0)
            mine = jnp.where((begin <= slot) & (slot < end), 1.0, 0.0)
            rank = slot - jnp.sum(mine * begin, axis=-1, keepdims=True)
            counts = jnp.dot(mine.astype(BF16), local, preferred_element_type=F32)
            reached = jnp.where(counts <= rank, 1.0, 0.0).astype(BF16)
            idx_row = LANES * _dot_nt(ones_g, before) + _dot_nt(ones_l, reached)
            idx_o[r, :, cb:cb + n] = idx_row[0:1].astype(jnp.int32)
        return 0

    lax.fori_loop(0, rows, compact, 0)


def expert_choice_topk(aff_t, cap):
    b, n_exp, t = aff_t.shape
    rows = b * n_exp
    upper = jnp.asarray(np.triu(np.ones((LANES, LANES), np.float32))).astype(BF16)
    idx = pl.pallas_call(
        functools.partial(_topk_kernel, cap),
        grid=(1,),
        in_specs=[_const_spec((rows, t)), _const_spec(upper.shape)],
        out_specs=_const_spec((rows, 1, cap)),
        out_shape=jax.ShapeDtypeStruct((rows, 1, cap), jnp.int32),
        scratch_shapes=[pltpu.VMEM((rows, t), F32), pltpu.VMEM((rows, t), F32),
                        pltpu.VMEM((rows, t // LANES, LANES), F32),
                        pltpu.VMEM((rows, t // LANES), F32), pltpu.VMEM((rows, t // LANES), F32)],
        compiler_params=_cparams("arbitrary"),
    )(aff_t.reshape(rows, t), upper)
    return idx.reshape(rows * cap)


GATHER_UNROLL = 8
SCATTER_BATCH = 4


def _gather_kernel(cap, idx_ref, h_hbm, aff_ref, xs_o, g_o, h_ref, buf_ref, sem):
    b = pl.program_id(0)
    e = pl.program_id(1)
    base = (b * pl.num_programs(1) + e) * cap

    @pl.when(e == 0)
    def _():
        load = pltpu.make_async_copy(h_hbm.at[b], h_ref, sem)
        load.start()
        load.wait()

    def body(i, _):
        for u in range(GATHER_UNROLL):
            c = i * GATHER_UNROLL + u
            row = idx_ref[base + c]
            buf_ref[pl.ds(c, 1), :] = h_ref[pl.ds(row, 1), :]
            g_o[pl.ds(c, 1), :] = aff_ref[pl.ds(row, 1), :]
        return 0

    lax.fori_loop(0, cap // GATHER_UNROLL, body, 0)
    xs_o[...] = buf_ref[...].astype(BF16)


def gather_tokens(idx, h, aff, n_exp, cap):
    b, t, d = h.shape
    return pl.pallas_call(
        functools.partial(_gather_kernel, cap),
        grid_spec=pltpu.PrefetchScalarGridSpec(
            num_scalar_prefetch=1,
            grid=(b, n_exp),
            in_specs=[pl.BlockSpec(memory_space=pl.ANY),
                      pl.BlockSpec((None, t, LANES), lambda bi, e, idx_: (bi, 0, 0))],
            out_specs=[pl.BlockSpec((None, None, cap, d), lambda bi, e, idx_: (bi, e, 0, 0)),
                       pl.BlockSpec((None, None, cap, LANES), lambda bi, e, idx_: (bi, e, 0, 0))],
            scratch_shapes=[pltpu.VMEM((t, d), F32), pltpu.VMEM((cap, d), F32), pltpu.SemaphoreType.DMA(())],
        ),
        out_shape=[jax.ShapeDtypeStruct((b, n_exp, cap, d), BF16),
                   jax.ShapeDtypeStruct((b, n_exp, cap, LANES), F32)],
        compiler_params=_cparams("arbitrary", "arbitrary"),
    )(idx, h, aff)


def _expert_ffn_kernel(n_groups, *refs):
    xs_refs = refs[0:3 * n_groups:3]
    g_refs = refs[1:3 * n_groups:3]
    gate_refs = refs[2:3 * n_groups:3]
    wg_ref, wu_ref, wd_ref = refs[3 * n_groups:3 * n_groups + 3]
    o_refs = refs[3 * n_groups + 3:4 * n_groups + 3]
    acc_refs = refs[4 * n_groups + 3:]
    f = pl.program_id(1)

    @pl.when(f == 0)
    def _():
        for acc_ref in acc_refs:
            acc_ref[...] = jnp.zeros(acc_ref.shape, F32)

    wg = wg_ref[...].astype(BF16)
    wu = wu_ref[...].astype(BF16)
    wd = wd_ref[...].astype(BF16)
    for xs_ref, acc_ref in zip(xs_refs, acc_refs):
        for bi in range(xs_ref.shape[0]):
            xb = xs_ref[bi]
            gate = jnp.dot(xb, wg, preferred_element_type=F32)
            up = jnp.dot(xb, wu, preferred_element_type=F32)
            hid = (gate * _sigmoid(gate) * up).astype(BF16)
            acc_ref[bi] += jnp.dot(hid, wd, preferred_element_type=F32)

    @pl.when(f == pl.num_programs(1) - 1)
    def _():
        e = pl.program_id(0)
        for g_ref, gate_ref, acc_ref, o_ref in zip(g_refs, gate_refs, acc_refs, o_refs):
            lane = lax.broadcasted_iota(jnp.int32, g_ref.shape, 2)
            g = jnp.sum(jnp.where(lane == e, g_ref[...], 0.0), axis=-1, keepdims=True)
            o_ref[...] = acc_ref[...] * g * gate_ref[...]


def expert_ffn(groups, layer, w_gate, w_up, w_down):
    _, n_exp, d, f_dim = w_gate.shape
    tf = _largest_divisor(f_dim, 256, LANES)
    args, in_specs, out_specs, out_shapes, scratch = [], [], [], [], []
    for xs, g_rows, gate in groups:
        b, _, cap, _ = xs.shape
        args += [xs, g_rows, gate]
        in_specs += [pl.BlockSpec((b, None, cap, d), lambda e, f: (0, e, 0, 0)),
                     pl.BlockSpec((b, None, cap, LANES), lambda e, f: (0, e, 0, 0)),
                     pl.BlockSpec((b, 1, d), lambda e, f: (0, 0, 0))]
        out_specs.append(pl.BlockSpec((None, b, cap, d), lambda e, f: (e, 0, 0, 0)))
        out_shapes.append(jax.ShapeDtypeStruct((n_exp, b, cap, d), F32))
        scratch.append(pltpu.VMEM((b, cap, d), F32))
    in_specs += [pl.BlockSpec((None, None, d, tf), lambda e, f: (layer, e, 0, f)),
                 pl.BlockSpec((None, None, d, tf), lambda e, f: (layer, e, 0, f)),
                 pl.BlockSpec((None, None, tf, d), lambda e, f: (layer, e, f, 0))]
    return pl.pallas_call(
        functools.partial(_expert_ffn_kernel, len(groups)),
        grid=(n_exp, f_dim // tf),
        in_specs=in_specs,
        out_specs=out_specs,
        out_shape=out_shapes,
        scratch_shapes=scratch,
        compiler_params=_cparams("parallel", "arbitrary"),
    )(*args, w_gate, w_up, w_down)


def _scatter_kernel(cap, idx_ref, eo_ref, x_hbm, o_hbm, acc_ref, sem):
    b = pl.program_id(0)
    e = pl.program_id(1)
    base = (b * pl.num_programs(1) + e) * cap

    @pl.when(e == 0)
    def _():
        load = pltpu.make_async_copy(x_hbm.at[b], acc_ref, sem)
        load.start()
        load.wait()

    def body(i, _):
        c0 = i * SCATTER_BATCH
        rows = [idx_ref[base + c0 + u] for u in range(SCATTER_BATCH)]
        sums = [acc_ref[pl.ds(rows[u], 1), :] + eo_ref[pl.ds(c0 + u, 1), :] for u in range(SCATTER_BATCH)]
        for u in range(SCATTER_BATCH):
            acc_ref[pl.ds(rows[u], 1), :] = sums[u]
        return 0

    lax.fori_loop(0, cap // SCATTER_BATCH, body, 0)

    @pl.when(e == pl.num_programs(1) - 1)
    def _():
        store = pltpu.make_async_copy(acc_ref, o_hbm.at[b], sem)
        store.start()
        store.wait()


def scatter_residual(idx, expert_out, x, cap):
    n_exp, b, _, d = expert_out.shape
    t = x.shape[1]
    return pl.pallas_call(
        functools.partial(_scatter_kernel, cap),
        grid_spec=pltpu.PrefetchScalarGridSpec(
            num_scalar_prefetch=1,
            grid=(b, n_exp),
            in_specs=[pl.BlockSpec((None, None, cap, d), lambda bi, e, idx_: (e, bi, 0, 0)),
                      pl.BlockSpec(memory_space=pl.ANY)],
            out_specs=pl.BlockSpec(memory_space=pl.ANY),
            scratch_shapes=[pltpu.VMEM((t, d), F32), pltpu.SemaphoreType.DMA(())],
        ),
        out_shape=jax.ShapeDtypeStruct((b, t, d), F32),
        compiler_params=_cparams("arbitrary", "arbitrary"),
    )(idx, expert_out, x)


def moe_residual(streams, gain, layer, w_router, w_gate, w_up, w_down):
    n_exp = w_router.shape[1]
    routed = []
    for x, scale, shift, gate in streams:
        cap = CAPACITY_FACTOR * x.shape[1] // n_exp
        h, aff, aff_t = router(x, gain, scale, shift, w_router)
        idx = expert_choice_topk(aff_t, cap)
        xs, g_rows = gather_tokens(idx, h, aff, n_exp, cap)
        routed.append((idx, cap, (xs, g_rows, gate)))
    outs = expert_ffn([g for _, _, g in routed], layer, w_gate, w_up, w_down)
    return [scatter_residual(idx, eo, x, cap) for (idx, cap, _), eo, (x, _, _, _) in zip(routed, outs, streams)]


def kernel(x, c, ctx, c_ctx, mod_w, mod_b, norm1_w, norm2_w, w_in, rw_mu_prev, rw_mu_next, rw_w0, rw_w_up,
           rw_a0, rw_a_up, rw_g_up, rw_k_k, rw_k_a, rw_r_k, rw_ln_w, rw_ln_b, da_q_gain, da_k_gain, da_lq1,
           da_lk1, da_lq2, da_lk2, da_sub_gain, ft_w, pl_w, pl_scale, w_out, moe_router, moe_w_gate, moe_w_up,
           moe_w_down):
    depth, d = norm1_w.shape
    batch = x.shape[0]
    gw = d // N_MIXERS
    c_rows = jnp.zeros((SUBLANES, d), F32).at[:batch].set(c).at[batch].set(c_ctx)
    mod = modulation_vectors(c_rows, mod_w, mod_b)
    w_in_b = w_in.astype(BF16)
    w_out_b = w_out.astype(BF16)
    for l in range(depth):
        ctx_out = l < depth - 1
        lam_init = 0.8 - 0.6 * math.exp(-0.3 * l)
        mx = mod[l, :batch].reshape(batch, 6, 1, d)
        mc = jnp.broadcast_to(mod[l, batch].reshape(1, 6, 1, d), (batch, 6, 1, d))
        rw = (rw_mu_prev[l], rw_mu_next[l], rw_w0[l], rw_w_up[l], rw_a0[l], rw_a_up[l], rw_g_up[l],
              rw_k_k[l], rw_k_a[l], rw_r_k[l], rw_ln_w[l], rw_ln_b[l])
        da = (da_q_gain[l], da_k_gain[l], da_lq1[l], da_lk1[l], da_lq2[l], da_lk2[l], da_sub_gain[l])
        rwx, dax, ftx, plx = input_projection(x, norm1_w[l], mx[:, 1], mx[:, 0], w_in_b[l], gw)
        rwc, dac, ftc, plc = input_projection(ctx, norm1_w[l], mc[:, 1], mc[:, 0], w_in_b[l], gw)
        ax, ac = rwkv_mixer(rwx, rwc, rw, gw, ctx_out)
        bx, bc = diff_attention(dax, dac, da, gw, lam_init, ctx_out)
        fx = fourier_mix(ftx, ft_w[l])
        px = pool_mix(plx, pl_w[l], pl_scale[l])
        rw_out = (rw_r_k[l], rw_ln_w[l], rw_ln_b[l])
        x = output_projection(x, mx[:, 2], ax, rw_out, (bx, fx, px), w_out_b[l])
        streams = [(x, mx[:, 4], mx[:, 3], mx[:, 5])]
        if ctx_out:
            fc = fourier_mix(ftc, ft_w[l])
            pc = pool_mix(plc, pl_w[l], pl_scale[l])
            ctx = output_projection(ctx, mc[:, 2], ac, rw_out, (bc, fc, pc), w_out_b[l])
            streams.append((ctx, mc[:, 4], mc[:, 3], mc[:, 5]))
        outs = moe_residual(streams, norm2_w[l], l, moe_router[l], moe_w_gate, moe_w_up, moe_w_down)
        x = outs[0]
        if ctx_out:
            ctx = outs[1]
    return x
```

```python
import functools
import math

import numpy as np
import jax
import jax.numpy as jnp
from jax import lax
from jax.experimental import pallas as pl
from jax.experimental.pallas import tpu as pltpu

F32 = jnp.float32
BF16 = jnp.bfloat16

N_MIXERS = 4
HEAD = 64
NORM_EPS = 1e-6
RW_LN_EPS = 64e-5
GRID_W = 64
DA_QK = HEAD // 2
ROPE_BASE = 10000.0
POOL_WINDOWS = (2, 4, 8, 16)
CAPACITY_FACTOR = 2

LANES = 128
SUBLANES = 8
VMEM_LIMIT_BYTES = 56 * 1024 * 1024

ROW_BLOCK = 1024
MATMUL_ROW_BLOCK = 1024

RW_CHUNK = 64
RW_CHUNKS_PER_STEP = 16


def _cparams(*sem):
    return pltpu.CompilerParams(dimension_semantics=sem, vmem_limit_bytes=VMEM_LIMIT_BYTES)


def _row_block(t, target):
    return min(target, t)


def _dot(a, b):
    return jnp.dot(a.astype(BF16), b.astype(BF16), preferred_element_type=F32)


def _split_bf16(x):
    hi = x.astype(BF16)
    return hi, (x - hi.astype(F32)).astype(BF16)


def _dot_f32(a, b, a_exact=False, b_exact=False):
    dot = lambda x, y: jnp.dot(x, y, preferred_element_type=F32)
    a_hi, a_lo = (a.astype(BF16), None) if a_exact else _split_bf16(a)
    b_hi, b_lo = (b.astype(BF16), None) if b_exact else _split_bf16(b)
    out = dot(a_hi, b_hi)
    if a_lo is not None:
        out = out + dot(a_lo, b_hi)
    if b_lo is not None:
        out = out + dot(a_hi, b_lo)
    return out


def _dot_tri(a, b):
    return _dot(a, b)


def _dot_nt(a, b):
    dn = (((1,), (1,)), ((), ()))
    return lax.dot_general(a.astype(BF16), b.astype(BF16), dn, preferred_element_type=F32)


def _dot_tn(a, b):
    dn = (((0,), (0,)), ((), ()))
    return lax.dot_general(a.astype(BF16), b.astype(BF16), dn, preferred_element_type=F32)


def _sigmoid(x):
    return 1.0 / (1.0 + jnp.exp(-x))


def _block_ones(n, blk, value=1.0):
    i = np.arange(n) // blk
    return jnp.asarray((i[:, None] == i[None, :]).astype(np.float32) * value)


def _const_spec(shape):
    nd = len(shape)
    return pl.BlockSpec(shape, lambda *_: (0,) * nd)


def _mod_kernel(c_ref, w_ref, b_ref, o_ref):
    c = c_ref[...]
    o_ref[...] = _dot_f32(c * _sigmoid(c), w_ref[...]) + b_ref[...]


def modulation_vectors(c_rows, mod_w, mod_b):
    depth, d, n = mod_w.shape
    tn = 1536
    return pl.pallas_call(
        _mod_kernel,
        grid=(depth, n // tn),
        in_specs=[
            pl.BlockSpec((SUBLANES, d), lambda l, j: (0, 0)),
            pl.BlockSpec((None, d, tn), lambda l, j: (l, 0, j)),
            pl.BlockSpec((None, 1, tn), lambda l, j: (l, 0, j)),
        ],
        out_specs=pl.BlockSpec((None, SUBLANES, tn), lambda l, j: (l, 0, j)),
        out_shape=jax.ShapeDtypeStruct((depth, SUBLANES, n), F32),
        compiler_params=_cparams("parallel", "parallel"),
    )(c_rows, mod_w, mod_b.reshape(depth, 1, n))


def _modulated_norm(x, gain, scale, shift):
    ms = jnp.mean(x * x, axis=-1, keepdims=True)
    return (x * lax.rsqrt(ms + NORM_EPS) * gain) * (1.0 + scale) + shift


def _inproj_kernel(splits, x_ref, gain_ref, sc_ref, sh_ref, w_ref, *o_refs):
    h = _modulated_norm(x_ref[...], gain_ref[...], sc_ref[...], sh_ref[...]).astype(BF16)
    for (lo, hi), o_ref in zip(splits, o_refs):
        o_ref[...] = jnp.dot(h, w_ref[:, lo:hi], preferred_element_type=F32)


def input_projection(x, gain, scale, shift, w_in_bf16, group_w):
    b, t, d = x.shape
    rw_cols = w_in_bf16.shape[1] - 3 * group_w - 2 * group_w
    cuts = [0, rw_cols, rw_cols + 3 * group_w, rw_cols + 4 * group_w, rw_cols + 5 * group_w]
    splits = tuple((cuts[i], cuts[i + 1]) for i in range(4))
    tm = _row_block(t, MATMUL_ROW_BLOCK)
    row = pl.BlockSpec((None, 1, d), lambda bi, i: (bi, 0, 0))
    return pl.pallas_call(
        functools.partial(_inproj_kernel, splits),
        grid=(b, t // tm),
        in_specs=[
            pl.BlockSpec((None, tm, d), lambda bi, i: (bi, i, 0)),
            _const_spec((1, d)),
            row, row,
            _const_spec(w_in_bf16.shape),
        ],
        out_specs=[pl.BlockSpec((None, tm, hi - lo), lambda bi, i: (bi, i, 0)) for lo, hi in splits],
        out_shape=[jax.ShapeDtypeStruct((b, t, hi - lo), F32) for lo, hi in splits],
        compiler_params=_cparams("parallel", "parallel"),
    )(x, gain.reshape(1, d), scale, shift, w_in_bf16)


def _halo_specs(tm, t, width):
    nb8 = t // SUBLANES
    r8 = tm // SUBLANES
    return [
        pl.BlockSpec((None, tm, width), lambda b, i: (b, i, 0)),
        pl.BlockSpec((None, SUBLANES, width), lambda b, i: (b, jnp.maximum(i * r8 - 1, 0), 0)),
        pl.BlockSpec((None, SUBLANES, width), lambda b, i: (b, jnp.minimum((i + 1) * r8, nb8 - 1), 0)),
    ]


def _stage_with_halo(buf_ref, main_ref, prev_ref, next_ref):
    tm = main_ref.shape[0]
    i = pl.program_id(1)
    n = pl.num_programs(1)
    buf_ref[SUBLANES:SUBLANES + tm, :] = main_ref[...]
    buf_ref[0:SUBLANES, :] = jnp.where(i > 0, prev_ref[...], 0.0)
    buf_ref[SUBLANES + tm:2 * SUBLANES + tm, :] = jnp.where(i < n - 1, next_ref[...], 0.0)


def _rwkv_prep_kernel(gw, rw_ref, prev_ref, next_ref, mup_ref, mun_ref, kk_ref_, ka_ref, w0_ref, wup_ref,
                      a0_ref, aup_ref, gup_ref, hsum_ref,
                      r_o, k_o, v_o, kk_o, gate_o, lwf_o, kaf_o, kdf_o, lwb_o, kab_o, kdb_o, buf_ref):
    tm = rw_ref.shape[0]
    _stage_with_halo(buf_ref, rw_ref, prev_ref, next_ref)
    p = buf_ref[SUBLANES:SUBLANES + tm, :]
    prev = buf_ref[SUBLANES - 1:SUBLANES - 1 + tm, :]
    nxt = buf_ref[SUBLANES + 1:SUBLANES + 1 + tm, :]
    u = p + mup_ref[...] * (prev - p) + mun_ref[...] * (nxt - p)
    r = u[:, 0:gw]
    k = u[:, gw:2 * gw]
    v = u[:, 2 * gw:3 * gw]
    lora_w = u[:, 3 * gw:3 * gw + LANES]
    lora_a = u[:, 3 * gw + LANES:3 * gw + 2 * LANES]
    g = u[:, 3 * gw + 2 * LANES:3 * gw + 3 * LANES]
    kk = k * kk_ref_[...]
    ss = _dot_f32(kk * kk, hsum_ref[...], b_exact=True)
    kk = kk * lax.rsqrt(jnp.maximum(ss, 1e-24))
    zw = _dot(jnp.tanh(lora_w), wup_ref[...]) + w0_ref[...]
    za = _dot(lora_a, aup_ref[...]) + a0_ref[...]
    logw = -_sigmoid(zw) * math.exp(-0.5)
    a = _sigmoid(za)
    r_o[...] = r
    k_o[...] = k
    v_o[...] = v
    kk_o[...] = kk
    gate_o[...] = _dot(_sigmoid(g), gup_ref[...])
    ka = ka_ref[...]
    for d, (lw_o, kka_o, kd_o) in enumerate(((lwf_o, kaf_o, kdf_o), (lwb_o, kab_o, kdb_o))):
        a_d = a[:, d * gw:(d + 1) * gw]
        lw_o[...] = logw[:, d * gw:(d + 1) * gw]
        kka_o[...] = kk * a_d
        kd_o[...] = k * (1.0 + (a_d - 1.0) * ka)


def _blockdiag2(m):
    r, c = m.shape[1:]
    z = jnp.zeros((r, c), m.dtype)
    return jnp.concatenate([jnp.concatenate([m[0], z], 1), jnp.concatenate([z, m[1]], 1)], 0)


def rwkv_prepare(rw, params, gw):
    (mu_prev, mu_next, w0, w_up, a0, a_up, g_up, k_k, k_a, r_k, ln_w, ln_b) = params
    b, t, cols = rw.shape
    tm = _row_block(t, ROW_BLOCK)
    row = lambda v: v.reshape(1, -1)
    small = [row(mu_prev), row(mu_next), row(k_k), row(k_a), row(w0), _blockdiag2(w_up), row(a0),
             _blockdiag2(a_up), g_up, _block_ones(gw, HEAD)]
    out = jax.ShapeDtypeStruct((b, t, gw), F32)
    return pl.pallas_call(
        functools.partial(_rwkv_prep_kernel, gw),
        grid=(b, t // tm),
        in_specs=_halo_specs(tm, t, cols) + [_const_spec(s.shape) for s in small],
        out_specs=[pl.BlockSpec((None, tm, gw), lambda bi, i: (bi, i, 0))] * 11,
        out_shape=[out] * 11,
        scratch_shapes=[pltpu.VMEM((tm + 2 * SUBLANES, cols), F32)],
        compiler_params=_cparams("parallel", "parallel"),
    )(rw, rw, rw, *small)


RW_BUILD_CHUNKS_PER_STEP = 4


def _rwkv_chunk_kernel(ncs, r_ref, v_ref, kk_ref, lwf_ref, kaf_ref, kdf_ref, lwb_ref, kab_ref, kdb_ref,
                       y0f_o, qf_o, mf_o, g0f_o, y0b_o, qb_o, mb_o, g0b_o):
    L = RW_CHUNK
    rows = lax.broadcasted_iota(jnp.int32, (L, L), 0)
    cols = lax.broadcasted_iota(jnp.int32, (L, L), 1)
    eye = jnp.where(rows == cols, 1.0, 0.0)
    rows2 = lax.broadcasted_iota(jnp.int32, (L, 2 * L), 0)
    cols2 = lax.broadcasted_iota(jnp.int32, (L, 2 * L), 1) % L
    lane = lax.broadcasted_iota(jnp.int32, (1, LANES), 1)
    head_masks = [(lane >= h * HEAD) & (lane < (h + 1) * HEAD) for h in range(2)]
    r2 = lax.broadcasted_iota(jnp.int32, (LANES, LANES), 0)
    c2 = lax.broadcasted_iota(jnp.int32, (LANES, LANES), 1)
    same_head = (r2 // HEAD) == (c2 // HEAD)
    diag = r2 == c2
    directions = ((False, lwf_ref, kaf_ref, kdf_ref, y0f_o, qf_o, mf_o, g0f_o),
                  (True, lwb_ref, kab_ref, kdb_ref, y0b_o, qb_o, mb_o, g0b_o))

    probs = []
    for c in range(ncs):
        rs = slice(c * L, (c + 1) * L)
        for pair in range(2):
            sl = slice(pair * LANES, (pair + 1) * LANES)
            r = r_ref[rs, sl]
            v = v_ref[rs, sl]
            kk = kk_ref[rs, sl]
            for reverse, lw_ref, ka_ref, kd_ref, y0_o, q_o, m_o, g0_o in directions:
                incl = (cols >= rows) if reverse else (cols <= rows)
                strict = (cols > rows) if reverse else (cols < rows)
                incl2 = (cols2 >= rows2) if reverse else (cols2 <= rows2)
                logw = lw_ref[rs, sl]
                kd = kd_ref[rs, sl]
                a = -ka_ref[rs, sl]
                cum = _dot_f32(jnp.where(incl, 1.0, 0.0), logw, a_exact=True)
                total = jnp.sum(logw, axis=0, keepdims=True)
                g_inv = jnp.exp(-cum)
                g_tail = jnp.exp(total - cum)
                bd = kk * jnp.exp(cum - logw)
                rd = r * jnp.exp(cum)
                probs.append(dict(
                    incl2=incl2, strict=strict, v=v, bd=bd, rd=rd, total=total,
                    lhs=jnp.concatenate([bd, rd], axis=0), rhs=jnp.concatenate([a * g_inv, kd * g_inv], axis=0),
                    tails=jnp.concatenate([kd * g_tail, a * g_tail], axis=0),
                    outs=(y0_o, q_o, m_o, g0_o), rs=rs, sl=sl, c=c, pair=pair))

    heads = []
    for p in probs:
        for h in range(2):
            gram = _dot_nt(jnp.where(head_masks[h], p["lhs"], 0.0), p["rhs"])
            heads.append(dict(
                p=p, h=h,
                nil=jnp.where(p["strict"], gram[:L, :L], 0.0),
                aak=jnp.where(p["strict"], gram[:L, L:], 0.0),
                ara_ark=jnp.where(p["incl2"], gram[L:, :], 0.0)))

    for hd in heads:
        hd["acc"] = eye + hd["nil"]
        hd["pow"] = hd["nil"]
    span = 2
    while span < L:
        for hd in heads:
            hd["pow"] = _dot_tri(hd["pow"], hd["pow"])
        for hd in heads:
            hd["acc"] = hd["acc"] + _dot_tri(hd["acc"], hd["pow"])
        span *= 2

    for hd in heads:
        hd["aakv"] = _dot(hd["aak"], hd["p"]["v"])
    for hd in heads:
        wp = _dot_tri(hd["acc"], jnp.concatenate([hd["aakv"], hd["p"]["bd"]], axis=1))
        hd["w1"], hd["pm"] = wp[:, :LANES], wp[:, LANES:]
    for hd in heads:
        rhs = jnp.concatenate([jnp.concatenate([hd["w1"], hd["pm"]], axis=1),
                               jnp.concatenate([hd["p"]["v"], jnp.zeros((L, LANES), F32)], axis=1)], axis=0)
        yq = _dot(hd["ara_ark"], rhs)
        hd["y0"], hd["q"] = yq[:, :LANES], yq[:, LANES:] + hd["p"]["rd"]

    for i, p in enumerate(probs):
        h0, h1 = heads[2 * i], heads[2 * i + 1]
        pick = lambda key: jnp.where(head_masks[0], h0[key], h1[key])
        y0_o, q_o, m_o, g0_o = p["outs"]
        y0_o[p["rs"], p["sl"]] = pick("y0")
        q_o[p["rs"], p["sl"]] = pick("q")
        vw = jnp.concatenate([p["v"], pick("w1")], axis=0)
        g0_o[p["c"], p["pair"]] = jnp.where(same_head, _dot_tn(p["tails"], vw), 0.0)
        m_o[p["c"], p["pair"]] = (jnp.where(same_head, _dot_tn(p["tails"][L:], pick("pm")), 0.0)
                                  + jnp.where(diag, jnp.exp(p["total"]), 0.0))


def rwkv_chunks(r, v, kk, lwf, kaf, kdf, lwb, kab, kdb):
    b, t, gw = r.shape
    L = RW_CHUNK
    nc = t // L
    ncs = RW_BUILD_CHUNKS_PER_STEP if nc % RW_BUILD_CHUNKS_PER_STEP == 0 else 1
    tok = pl.BlockSpec((None, ncs * L, gw), lambda bi, i: (bi, i, 0))
    mat = pl.BlockSpec((None, ncs, 2, LANES, LANES), lambda bi, i: (bi, i, 0, 0, 0))
    tok_s = jax.ShapeDtypeStruct((b, t, gw), F32)
    mat_s = jax.ShapeDtypeStruct((b, nc, 2, LANES, LANES), F32)
    return pl.pallas_call(
        functools.partial(_rwkv_chunk_kernel, ncs),
        grid=(b, nc // ncs),
        in_specs=[tok] * 9,
        out_specs=[tok, tok, mat, mat] * 2,
        out_shape=[tok_s, tok_s, mat_s, mat_s] * 2,
        compiler_params=_cparams("parallel", "parallel"),
    )(r, v, kk, lwf, kaf, kdf, lwb, kab, kdb)


def _rwkv_scan_kernel(cps, h0_ref, y0f_ref, qf_ref, mf_ref, g0f_ref, y0b_ref, qb_ref, mb_ref, g0b_ref,
                      yf_o, yb_o, hfin_o, h_ref):
    L = RW_CHUNK
    j = pl.program_id(1)

    @pl.when(j == 0)
    def _():
        h_ref[...] = h0_ref[...]

    for step in range(cps):
        for d, (y0_ref, q_ref, m_ref, g0_ref, y_o) in enumerate(
                ((y0f_ref, qf_ref, mf_ref, g0f_ref, yf_o), (y0b_ref, qb_ref, mb_ref, g0b_ref, yb_o))):
            c = step if d == 0 else cps - 1 - step
            rows = slice(c * L, (c + 1) * L)
            for pair in range(2):
                sl = slice(pair * LANES, (pair + 1) * LANES)
                h = h_ref[d, pair]
                y_o[rows, sl] = y0_ref[rows, sl] + _dot(q_ref[rows, sl], h)
                h_ref[d, pair] = _dot(m_ref[c, pair], h) + g0_ref[c, pair]

    @pl.when(j == pl.num_programs(1) - 1)
    def _():
        hfin_o[...] = h_ref[...]


def rwkv_scan(h0, y0f, qf, mf, g0f, y0b, qb, mb, g0b):
    b, t, gw = y0f.shape
    L = RW_CHUNK
    nc = t // L
    cps = min(RW_CHUNKS_PER_STEP, nc)
    nb = nc // cps
    tm = cps * L
    tok_f = pl.BlockSpec((None, tm, gw), lambda bi, i: (bi, i, 0))
    tok_b = pl.BlockSpec((None, tm, gw), lambda bi, i: (bi, nb - 1 - i, 0))
    mat_f = pl.BlockSpec((None, cps, 2, LANES, LANES), lambda bi, i: (bi, i, 0, 0, 0))
    mat_b = pl.BlockSpec((None, cps, 2, LANES, LANES), lambda bi, i: (bi, nb - 1 - i, 0, 0, 0))
    st = pl.BlockSpec((None, 2, 2, LANES, LANES), lambda bi, i: (bi, 0, 0, 0, 0))
    tok_s = jax.ShapeDtypeStruct((b, t, gw), F32)
    return pl.pallas_call(
        functools.partial(_rwkv_scan_kernel, cps),
        grid=(b, nb),
        in_specs=[st, tok_f, tok_f, mat_f, mat_f, tok_b, tok_b, mat_b, mat_b],
        out_specs=[tok_f, tok_b, st],
        out_shape=[tok_s, tok_s, jax.ShapeDtypeStruct((b, 2, 2, LANES, LANES), F32)],
        scratch_shapes=[pltpu.VMEM((2, 2, LANES, LANES), F32)],
        compiler_params=_cparams("parallel", "arbitrary"),
    )(h0, y0f, qf, mf, g0f, y0b, qb, mb, g0b)


def _rwkv_head_output(yf, yb, r, k, v, gate, r_k, ln_w, ln_b, hmean):
    y = yf + yb
    mu = _dot_f32(y, hmean, b_exact=True)
    yc = y - mu
    var = _dot_f32(yc * yc, hmean, b_exact=True)
    yn = yc * lax.rsqrt(var + RW_LN_EPS) * ln_w + ln_b
    bonus = _dot_f32(r * k * r_k, hmean, b_exact=True) * float(HEAD) * v
    return (yn + bonus) * gate


def rwkv_mixer(rwx, rwc, params, gw, ctx_out):
    sx = rwkv_prepare(rwx, params, gw)
    sc = rwkv_prepare(rwc, params, gw)
    (rx, kx, vx, kkx, gx), dx = sx[:5], sx[5:]
    (rc, kc, vc, kkc, gc), dc = sc[:5], sc[5:]
    cx = rwkv_chunks(rx, vx, kkx, *dx)
    cc = rwkv_chunks(rc, vc, kkc, *dc)
    b = rwx.shape[0]
    h0 = jnp.zeros((b, 2, 2, LANES, LANES), F32)
    ycf, ycb, h_ctx = rwkv_scan(h0, *cc)
    yxf, yxb, _ = rwkv_scan(h_ctx, *cx)
    out_x = (yxf, yxb, rx, kx, vx, gx)
    out_c = (ycf, ycb, rc, kc, vc, gc) if ctx_out else None
    return out_x, out_c


def _rope_tables(n_tokens, reps):
    rows = n_tokens // GRID_W
    row = np.repeat(np.arange(rows), GRID_W).astype(np.float64)
    col = np.tile(np.arange(GRID_W), rows).astype(np.float64)
    n_freq = DA_QK // 4
    inv = ROPE_BASE ** (-np.arange(n_freq, dtype=np.float64) / n_freq)
    ar = row[:, None] * inv
    ac = col[:, None] * inv
    ang = np.concatenate([ar, ar, ac, ac], axis=-1)
    cos = np.tile(np.cos(ang), (1, reps)).astype(np.float32)
    sin = np.tile(np.sin(ang), (1, reps)).astype(np.float32)
    return jnp.asarray(cos), jnp.asarray(sin)


def _attn_prep_kernel(gw, rope, da_ref, qg_ref, kg_ref, gmean_ref, *rest):
    if rope:
        cos_ref, sin_ref, q_o, k_o, v_o = rest
    else:
        q_o, k_o, v_o = rest
    da = da_ref[...]
    gmean = gmean_ref[...]
    lane = lax.broadcasted_iota(jnp.int32, (1, gw), 1)
    first_half = (lane % (DA_QK // 2)) < (DA_QK // 4)

    def norm_rope(x, gain):
        ms = _dot_f32(x * x, gmean, b_exact=True)
        y = x * lax.rsqrt(ms + NORM_EPS) * gain
        if rope:
            quarter = DA_QK // 4
            rot = jnp.where(first_half, -pltpu.roll(y, gw - quarter, 1), pltpu.roll(y, quarter, 1))
            y = y * cos_ref[...] + rot * sin_ref[...]
        return y

    tm = da.shape[0]
    q = norm_rope(da[:, 0:gw], qg_ref[...]) * (DA_QK ** -0.5 * LOG2_E)
    q_t = jnp.transpose(q)
    row_map = lax.broadcasted_iota(jnp.int32, (gw, 1), 0) // DA_QK
    for c in range(gw // DA_QK):
        q_o[c] = jnp.where(row_map == c, q_t, 0.0).astype(q_o.dtype)
    k_o[...] = norm_rope(da[:, gw:2 * gw], kg_ref[...]).astype(k_o.dtype)
    v_t = jnp.transpose(da[:, 2 * gw:3 * gw])
    pad = jnp.where(lax.broadcasted_iota(jnp.int32, (LANES - HEAD, tm), 0) == 0, 1.0, 0.0)
    for h in range(gw // HEAD):
        v_o[h] = jnp.concatenate([v_t[h * HEAD:(h + 1) * HEAD], pad], axis=0).astype(v_o.dtype)


def attention_prepare(da, q_gain, k_gain, gw, rope):
    b, t, cols = da.shape
    tm = _row_block(t, ROW_BLOCK)
    reps = gw // DA_QK
    heads = gw // HEAD
    small = [jnp.tile(q_gain, reps).reshape(1, gw), jnp.tile(k_gain, reps).reshape(1, gw),
             _block_ones(gw, DA_QK, 1.0 / DA_QK)]
    args = [da] + small
    in_specs = [pl.BlockSpec((None, tm, cols), lambda bi, i: (bi, i, 0))] + [_const_spec(s.shape) for s in small]
    if rope:
        cos, sin = _rope_tables(t, reps)
        args += [cos, sin]
        in_specs += [pl.BlockSpec((tm, gw), lambda bi, i: (i, 0))] * 2
    return pl.pallas_call(
        functools.partial(_attn_prep_kernel, gw, rope),
        grid=(b, t // tm),
        in_specs=in_specs,
        out_specs=[pl.BlockSpec((None, reps, gw, tm), lambda bi, i: (bi, 0, 0, i)),
                   pl.BlockSpec((None, tm, gw), lambda bi, i: (bi, i, 0)),
                   pl.BlockSpec((None, heads, LANES, tm), lambda bi, i: (bi, 0, 0, i))],
        out_shape=[jax.ShapeDtypeStruct((b, reps, gw, t), BF16),
                   jax.ShapeDtypeStruct((b, t, gw), BF16),
                   jax.ShapeDtypeStruct((b, heads, LANES, t), BF16)],
        compiler_params=_cparams("parallel", "parallel"),
    )(*args)


LOG2_E = 1.4426950408889634
FLASH_MAX_KV_BLOCK = 2816


def _flash_kernel(lam_init, bounded_ref, bound_ref, qt_ref, k_ref, vt_ref, lq1_ref, lk1_ref, lq2_ref, lk2_ref,
                  sg_ref, o_ref, m_ref, acc_ref):
    j = pl.program_id(2)
    n_maps = qt_ref.shape[0]

    @pl.when(j == 0)
    def _():
        m_ref[...] = jnp.full(m_ref.shape, -1e30, F32)
        acc_ref[...] = jnp.zeros(acc_ref.shape, F32)

    k = k_ref[...]
    bounded = bounded_ref[0] == 1

    @pl.when(bounded)
    def _():
        shift = bound_ref[0]
        for c in range(n_maps):
            s = jnp.dot(k, qt_ref[c], preferred_element_type=F32)
            p = jnp.exp2(s - shift).astype(BF16)
            acc_ref[c] += jnp.dot(vt_ref[c // 2], p, preferred_element_type=F32)

    @pl.when(jnp.logical_not(bounded))
    def _():
        for c in range(n_maps):
            s = jnp.dot(k, qt_ref[c], preferred_element_type=F32)
            m_old = m_ref[c]
            m_new = jnp.maximum(m_old, jnp.max(s, axis=0, keepdims=True))
            p = jnp.exp2(s - m_new).astype(BF16)
            acc_ref[c] = (jnp.exp2(m_old - m_new) * acc_ref[c]
                          + jnp.dot(vt_ref[c // 2], p, preferred_element_type=F32))
            m_ref[c] = m_new

    @pl.when(j == pl.num_programs(2) - 1)
    def _():
        lam = (jnp.exp(jnp.sum(lq1_ref[...] * lk1_ref[...], axis=-1, keepdims=True))
               - jnp.exp(jnp.sum(lq2_ref[...] * lk2_ref[...], axis=-1, keepdims=True)) + lam_init)
        for h in range(n_maps // 2):
            a0 = acc_ref[2 * h]
            a1 = acc_ref[2 * h + 1]
            o = a0[:HEAD] / a0[HEAD:HEAD + 1] - lam * (a1[:HEAD] / a1[HEAD:HEAD + 1])
            ms = jnp.mean(o * o, axis=0, keepdims=True)
            o = o * lax.rsqrt(ms + NORM_EPS) * sg_ref[...] * (1.0 - lam_init)
            o_ref[:, h * HEAD:(h + 1) * HEAD] = jnp.transpose(o)


def _largest_divisor(n, cap, multiple):
    best = None
    for d in range(multiple, cap + 1, multiple):
        if n % d == 0:
            best = d
    return best if best is not None else n


FLASH_SAFE_SCORE_BOUND = 60.0


def _score_bound(q_gain, k_gain):
    bound = (1.02 * DA_QK * DA_QK ** -0.5 * LOG2_E) * jnp.max(jnp.abs(q_gain)) * jnp.max(jnp.abs(k_gain))
    return (bound <= FLASH_SAFE_SCORE_BOUND).astype(jnp.int32).reshape(1), bound.astype(F32).reshape(1)


def diff_attention_core(qt, k, vt, gains, lam_params, sub_gain, lam_init):
    b, n_maps, gw, t = qt.shape
    s = k.shape[1]
    heads = vt.shape[1]
    tq = min(512, t)
    tk = _largest_divisor(s, FLASH_MAX_KV_BLOCK, 2 * LANES)
    small = [p.reshape(1, -1) for p in lam_params] + [sub_gain.reshape(-1, 1)]
    bounded, bound = _score_bound(*gains)
    return pl.pallas_call(
        functools.partial(_flash_kernel, lam_init),
        grid_spec=pltpu.PrefetchScalarGridSpec(
            num_scalar_prefetch=2,
            grid=(b, t // tq, s // tk),
            in_specs=[
                pl.BlockSpec((None, n_maps, gw, tq), lambda bi, i, j, *_: (bi, 0, 0, i)),
                pl.BlockSpec((None, tk, gw), lambda bi, i, j, *_: (bi, j, 0)),
                pl.BlockSpec((None, heads, LANES, tk), lambda bi, i, j, *_: (bi, 0, 0, j)),
            ] + [pl.BlockSpec(x.shape, lambda bi, i, j, *_: (0, 0)) for x in small],
            out_specs=pl.BlockSpec((None, tq, gw), lambda bi, i, j, *_: (bi, i, 0)),
            scratch_shapes=[pltpu.VMEM((n_maps, 1, tq), F32), pltpu.VMEM((n_maps, LANES, tq), F32)],
        ),
        out_shape=jax.ShapeDtypeStruct((b, t, gw), F32),
        compiler_params=_cparams("parallel", "parallel", "arbitrary"),
    )(bounded, bound, qt, k, vt, *small)


def diff_attention(dax, dac, params, gw, lam_init, ctx_out):
    q_gain, k_gain, lq1, lk1, lq2, lk2, sub_gain = params
    qtx, kx, vtx = attention_prepare(dax, q_gain, k_gain, gw, rope=True)
    qtc, kc, vtc = attention_prepare(dac, q_gain, k_gain, gw, rope=False)
    k = jnp.concatenate([kx, kc], axis=1)
    vt = jnp.concatenate([vtx, vtc], axis=3)
    lam_params = (lq1, lk1, lq2, lk2)
    gains = (q_gain, k_gain)
    out_x = diff_attention_core(qtx, k, vt, gains, lam_params, sub_gain, lam_init)
    out_c = diff_attention_core(qtc, kc, vtc, gains, lam_params, sub_gain, lam_init) if ctx_out else None
    return out_x, out_c


FT_RADIX = 64


def _dft_cos_sin(n, scale=1.0):
    i = np.arange(n)
    ang = 2.0 * np.pi * ((i[:, None] * i[None, :]) % n) / n
    return np.cos(ang) * scale, np.sin(ang) * scale


def _channel_dft(gw, scale):
    c, s = _dft_cos_sin(HEAD, scale)
    eye = np.eye(gw // HEAD)
    return jnp.asarray(np.concatenate([np.kron(eye, c), np.kron(eye, s)], axis=0).astype(np.float32))


def _fnet_stage1_kernel(z_ref, grh_ref, grl_ref, gih_ref, gil_ref, or_ref, oi_ref):
    dot = lambda a, b: jnp.dot(a, b, preferred_element_type=F32)
    for j in range(z_ref.shape[1]):
        x_hi, x_lo = _split_bf16(z_ref[:, j, :])
        or_ref[j] = dot(grh_ref[j], x_hi) + (dot(grl_ref[j], x_hi) + dot(grh_ref[j], x_lo))
        oi_ref[j] = dot(gih_ref[j], x_hi) + (dot(gil_ref[j], x_hi) + dot(gih_ref[j], x_lo))


def _matmul_f32_kernel(a_ref, b_ref, o_ref):
    o_ref[...] = _dot_f32(a_ref[...], b_ref[...])


def _channel_dft_times(chan, w_f):
    return pl.pallas_call(
        _matmul_f32_kernel,
        out_shape=jax.ShapeDtypeStruct((chan.shape[0], w_f.shape[1]), F32),
    )(chan, w_f)


def _fnet_stage2_kernel(gw, br_ref, bi_ref, rot_ref, chanw_ref, o_ref):
    n1, groups, _ = br_ref.shape
    x = jnp.concatenate([jnp.concatenate([br_ref[:, g, :] for g in range(groups)], axis=1),
                         jnp.concatenate([bi_ref[:, g, :] for g in range(groups)], axis=1)], axis=0)
    p = _dot_f32(rot_ref[...], x)
    rows = jnp.concatenate(
        [jnp.concatenate([p[:n1, g * gw:(g + 1) * gw], p[n1:, g * gw:(g + 1) * gw]], axis=1) for g in range(groups)],
        axis=0)
    out = _dot_f32(rows, chanw_ref[...])
    for g in range(groups):
        o_ref[:, g, :] = out[g * n1:(g + 1) * n1]


def fourier_mix_long(z, w_f):
    b, t, gw = z.shape
    n1 = FT_RADIX
    n2 = t // n1
    k2 = np.arange(n2)[None, :, None]
    n = np.arange(n1)[:, None, None] + n1 * np.arange(n2)[None, None, :]
    ang = 2.0 * np.pi * ((k2 * n) % t) / t
    tables = []
    for g in (np.cos(ang), -np.sin(ang)):
        g_hi = jnp.asarray(g.astype(np.float32)).astype(BF16)
        g_lo = (jnp.asarray(g.astype(np.float32)) - g_hi.astype(F32)).astype(BF16)
        tables += [g_hi, g_lo]
    j8 = SUBLANES
    table_spec = pl.BlockSpec((j8, n2, n2), lambda bi_, i: (i, 0, 0))
    br, bi = pl.pallas_call(
        _fnet_stage1_kernel,
        grid=(b, n1 // j8),
        in_specs=[pl.BlockSpec((None, n2, j8, gw), lambda bi_, i: (bi_, 0, i, 0))] + [table_spec] * 4,
        out_specs=[pl.BlockSpec((None, j8, n2, gw), lambda bi_, i: (bi_, i, 0, 0))] * 2,
        out_shape=[jax.ShapeDtypeStruct((b, n1, n2, gw), F32)] * 2,
        compiler_params=_cparams("parallel", "parallel"),
    )(z.reshape(b, n2, n1, gw), *tables)
    c64, s64 = _dft_cos_sin(n1)
    rot = jnp.asarray(np.block([[c64, s64], [-s64, c64]]).astype(np.float32))
    chanw = _channel_dft_times(_channel_dft(gw, 1.0 / math.sqrt(t * HEAD)), w_f)
    blk = pl.BlockSpec((None, n1, j8, gw), lambda bi_, i: (bi_, 0, i, 0))
    out = pl.pallas_call(
        functools.partial(_fnet_stage2_kernel, gw),
        grid=(b, n2 // j8),
        in_specs=[blk, blk, _const_spec(rot.shape), _const_spec(chanw.shape)],
        out_specs=blk,
        out_shape=jax.ShapeDtypeStruct((b, n1, n2, gw), F32),
        compiler_params=_cparams("parallel", "parallel"),
    )(br, bi, rot, chanw)
    return out.reshape(b, t, gw)


def _fnet_dense_kernel(z_ref, ct_ref, st_ref, chanw_ref, o_ref):
    z = z_ref[...]
    pr = _dot_f32(ct_ref[...], z)
    pi = -_dot_f32(st_ref[...], z)
    o_ref[...] = _dot_f32(jnp.concatenate([pr, pi], axis=1), chanw_ref[...])


def fourier_mix_short(z, w_f):
    b, t, gw = z.shape
    ct, st = _dft_cos_sin(t)
    ct = jnp.asarray(ct.astype(np.float32))
    st = jnp.asarray(st.astype(np.float32))
    chanw = _channel_dft_times(_channel_dft(gw, 1.0 / math.sqrt(t * HEAD)), w_f)
    tok = pl.BlockSpec((None, t, gw), lambda bi: (bi, 0, 0))
    return pl.pallas_call(
        _fnet_dense_kernel,
        grid=(b,),
        in_specs=[tok, _const_spec(ct.shape), _const_spec(st.shape), _const_spec(chanw.shape)],
        out_specs=tok,
        out_shape=jax.ShapeDtypeStruct((b, t, gw), F32),
        compiler_params=_cparams("parallel"),
    )(z, ct, st, chanw)


def fourier_mix(z, w_f):
    t = z.shape[1]
    if t % (FT_RADIX * SUBLANES) == 0 and t // FT_RADIX >= LANES:
        return fourier_mix_long(z, w_f)
    return fourier_mix_short(z, w_f)


def _pool_kernel(t_total, u_ref, prev_ref, next_ref, w_ref, s_ref, o_ref, buf_ref):
    tm, gw = u_ref.shape
    _stage_with_halo(buf_ref, u_ref, prev_ref, next_ref)
    at = lambda off: buf_ref[SUBLANES + off:SUBLANES + off + tm, :]
    u = at(0)
    t = pl.program_id(1) * tm + lax.broadcasted_iota(jnp.int32, (tm, 1), 0)
    lane = lax.broadcasted_iota(jnp.int32, (1, gw), 1)
    group = lane // (gw // len(POOL_WINDOWS))
    mean = jnp.zeros((tm, gw), F32)
    run = jnp.zeros((tm, gw), F32)
    half_prev = 0
    for i, w in enumerate(POOL_WINDOWS):
        half = w // 2
        for off in range(half_prev, half):
            run = run + at(-off - 1) + at(off)
        half_prev = half
        cnt = (jnp.minimum(t + half, t_total) - jnp.maximum(t - half, 0)).astype(F32)
        mean = jnp.where(group == i, run / cnt, mean)
    o_ref[...] = _dot_f32(mean - u, w_ref[...]) * s_ref[...]


def pool_mix(u, w_p, s_p):
    b, t, gw = u.shape
    tm = _row_block(t, ROW_BLOCK)
    nw, ch = w_p.shape[0], w_p.shape[1]
    w_bd = jnp.zeros((gw, gw), F32)
    for i in range(nw):
        w_bd = w_bd.at[i * ch:(i + 1) * ch, i * ch:(i + 1) * ch].set(w_p[i])
    return pl.pallas_call(
        functools.partial(_pool_kernel, t),
        grid=(b, t // tm),
        in_specs=_halo_specs(tm, t, gw) + [_const_spec((gw, gw)), _const_spec((1, gw))],
        out_specs=pl.BlockSpec((None, tm, gw), lambda bi, i: (bi, i, 0)),
        out_shape=jax.ShapeDtypeStruct((b, t, gw), F32),
        scratch_shapes=[pltpu.VMEM((tm + 2 * SUBLANES, gw), F32)],
        compiler_params=_cparams("parallel", "parallel"),
    )(u, u, u, w_bd, s_p.reshape(1, gw))


def _outproj_kernel(gw, x_ref, g_ref, yf_ref, yb_ref, r_ref, k_ref, v_ref, rg_ref, b_ref, f_ref, p_ref,
                    rk_ref, lnw_ref, lnb_ref, hmean_ref, w_ref, o_ref):
    a = _rwkv_head_output(yf_ref[...], yb_ref[...], r_ref[...], k_ref[...], v_ref[...], rg_ref[...],
                          rk_ref[...], lnw_ref[...], lnb_ref[...], hmean_ref[...])
    acc = jnp.dot(a.astype(BF16), w_ref[0:gw, :], preferred_element_type=F32)
    for i, m_ref in enumerate((b_ref, f_ref, p_ref), start=1):
        acc = acc + jnp.dot(m_ref[...].astype(BF16), w_ref[i * gw:(i + 1) * gw, :], preferred_element_type=F32)
    o_ref[...] = x_ref[...] + g_ref[...] * acc


def output_projection(x, gate, rwkv_parts, rwkv_params, mixers, w_out_bf16):
    b, t, d = x.shape
    gw = mixers[0].shape[-1]
    tm = _row_block(t, MATMUL_ROW_BLOCK)
    tok = pl.BlockSpec((None, tm, gw), lambda bi, i: (bi, i, 0))
    xs = pl.BlockSpec((None, tm, d), lambda bi, i: (bi, i, 0))
    r_k, ln_w, ln_b = rwkv_params
    small = [r_k.reshape(1, gw), ln_w.reshape(1, gw), ln_b.reshape(1, gw), _block_ones(gw, HEAD, 1.0 / HEAD)]
    return pl.pallas_call(
        functools.partial(_outproj_kernel, gw),
        grid=(b, t // tm),
        in_specs=([xs, pl.BlockSpec((None, 1, d), lambda bi, i: (bi, 0, 0))] + [tok] * 9
                  + [_const_spec(s.shape) for s in small] + [_const_spec(w_out_bf16.shape)]),
        out_specs=xs,
        out_shape=jax.ShapeDtypeStruct((b, t, d), F32),
        compiler_params=_cparams("parallel", "parallel"),
    )(x, gate, *rwkv_parts, *mixers, *small, w_out_bf16)


def _router_kernel(n_exp, x_ref, gain_ref, sc_ref, sh_ref, wr_ref, h_o, aff_o, afft_o):
    h = _modulated_norm(x_ref[...], gain_ref[...], sc_ref[...], sh_ref[...])
    h_o[...] = h
    logits = _dot_f32(h, wr_ref[...])
    lane = lax.broadcasted_iota(jnp.int32, logits.shape, 1)
    logits = jnp.where(lane < n_exp, logits, -1e30)
    e = jnp.exp(logits - jnp.max(logits, axis=-1, keepdims=True))
    aff = e / jnp.sum(e, axis=-1, keepdims=True)
    aff_o[...] = aff
    afft_o[...] = jnp.transpose(aff)[:n_exp, :]


def router(x, gain, scale, shift, w_router):
    b, t, d = x.shape
    n_exp = w_router.shape[1]
    tm = _row_block(t, MATMUL_ROW_BLOCK)
    wr = jnp.zeros((d, LANES), F32).at[:, :n_exp].set(w_router)
    row = pl.BlockSpec((None, 1, d), lambda bi, i: (bi, 0, 0))
    return pl.pallas_call(
        functools.partial(_router_kernel, n_exp),
        grid=(b, t // tm),
        in_specs=[pl.BlockSpec((None, tm, d), lambda bi, i: (bi, i, 0)), _const_spec((1, d)), row, row,
                  _const_spec(wr.shape)],
        out_specs=[pl.BlockSpec((None, tm, d), lambda bi, i: (bi, i, 0)),
                   pl.BlockSpec((None, tm, LANES), lambda bi, i: (bi, i, 0)),
                   pl.BlockSpec((None, n_exp, tm), lambda bi, i: (bi, 0, i))],
        out_shape=[jax.ShapeDtypeStruct((b, t, d), F32), jax.ShapeDtypeStruct((b, t, LANES), F32),
                   jax.ShapeDtypeStruct((b, n_exp, t), F32)],
        compiler_params=_cparams("parallel", "parallel"),
    )(x, gain.reshape(1, d), scale, shift, wr)


TOPK_EXPONENT_STEPS = 7
TOPK_MANTISSA_STEPS = 44


def _row_cumsum(x_ref, o_ref, upper_ref):
    rows, t = x_ref.shape
    carry = jnp.zeros((rows, 1), F32)
    for g in range(t // LANES):
        sl = slice(g * LANES, (g + 1) * LANES)
        local = jnp.dot(x_ref[:, sl].astype(BF16), upper_ref[...], preferred_element_type=F32) + carry
        o_ref[:, sl] = local
        carry = local[:, LANES - 1:LANES]


def _topk_kernel(cap, aff_ref, upper_ref, idx_o, sel_ref, cs_ref, local_ref, begin_ref, end_ref):
    aff = aff_ref[...]
    rows, t = aff.shape
    capf = float(cap)
    count_ge = lambda thr: jnp.sum(jnp.where(aff >= thr, 1.0, 0.0), axis=-1, keepdims=True)
    hi = jnp.full((rows, 1), 2.0, F32)
    for step in reversed(range(TOPK_EXPONENT_STEPS)):
        cand = hi * (2.0 ** -(2 ** step))
        hi = jnp.where(count_ge(cand) < capf, cand, hi)
    lo = hi * 0.5
    lo = jnp.where(count_ge(lo) >= capf, lo, 0.0)

    def bisect(_, carry):
        lo, hi = carry
        mid = 0.5 * (lo + hi)
        enough = count_ge(mid) >= capf
        return jnp.where(enough, mid, lo), jnp.where(enough, hi, mid)

    lo, hi = lax.fori_loop(0, TOPK_MANTISSA_STEPS, bisect, (lo, hi))
    above = aff >= hi
    need = capf - count_ge(hi)
    sel_ref[...] = jnp.where((aff >= lo) & jnp.logical_not(above), 1.0, 0.0)
    _row_cumsum(sel_ref, cs_ref, upper_ref)
    tied_in = (sel_ref[...] > 0.5) & (cs_ref[...] <= need)
    sel_ref[...] = jnp.where(above | tied_in, 1.0, 0.0)

    groups = t // LANES
    carry = jnp.zeros((rows, 1), F32)
    for g in range(groups):
        local = jnp.dot(sel_ref[:, g * LANES:(g + 1) * LANES].astype(BF16), upper_ref[...],
                        preferred_element_type=F32)
        local_ref[:, g, :] = local
        begin_ref[:, g:g + 1] = carry
        carry = carry + local[:, LANES - 1:LANES]
        end_ref[:, g:g + 1] = carry

    ones_g = jnp.ones((SUBLANES, groups), BF16)
    ones_l = jnp.ones((SUBLANES, LANES), BF16)

    def compact(r, _):
        begin = begin_ref[pl.ds(r, 1), :]
        end = end_ref[pl.ds(r, 1), :]
        local = local_ref[r].astype(BF16)
        for cb in range(0, cap, LANES):
            n = min(LANES, cap - cb)
            slot = (lax.broadcasted_iota(jnp.int32, (n, 1), 0) + cb).astype(F32)
            before = jnp.where(end <= slot, 1.0, 0.0)
            mine = jnp.where((begin <= slot) & (slot < end), 1.0, 0.0)
            rank = slot - jnp.sum(mine * begin, axis=-1, keepdims=True)
            counts = jnp.dot(mine.astype(BF16), local, preferred_element_type=F32)
            reached = jnp.where(counts <= rank, 1.0, 0.0).astype(BF16)
            idx_row = LANES * _dot_nt(ones_g, before) + _dot_nt(ones_l, reached)
            idx_o[r, :, cb:cb + n] = idx_row[0:1].astype(jnp.int32)
        return 0

    lax.fori_loop(0, rows, compact, 0)


def expert_choice_topk(aff_t, cap):
    b, n_exp, t = aff_t.shape
    rows = b * n_exp
    upper = jnp.asarray(np.triu(np.ones((LANES, LANES), np.float32))).astype(BF16)
    idx = pl.pallas_call(
        functools.partial(_topk_kernel, cap),
        grid=(1,),
        in_specs=[_const_spec((rows, t)), _const_spec(upper.shape)],
        out_specs=_const_spec((rows, 1, cap)),
        out_shape=jax.ShapeDtypeStruct((rows, 1, cap), jnp.int32),
        scratch_shapes=[pltpu.VMEM((rows, t), F32), pltpu.VMEM((rows, t), F32),
                        pltpu.VMEM((rows, t // LANES, LANES), F32),
                        pltpu.VMEM((rows, t // LANES), F32), pltpu.VMEM((rows, t // LANES), F32)],
        compiler_params=_cparams("arbitrary"),
    )(aff_t.reshape(rows, t), upper)
    return idx.reshape(rows * cap)


GATHER_UNROLL = 8
SCATTER_BATCH = 4


def _gather_kernel(cap, idx_ref, h_hbm, aff_ref, xs_o, g_o, h_ref, buf_ref, sem):
    b = pl.program_id(0)
    e = pl.program_id(1)
    base = (b * pl.num_programs(1) + e) * cap

    @pl.when(e == 0)
    def _():
        load = pltpu.make_async_copy(h_hbm.at[b], h_ref, sem)
        load.start()
        load.wait()

    def body(i, _):
        for u in range(GATHER_UNROLL):
            c = i * GATHER_UNROLL + u
            row = idx_ref[base + c]
            buf_ref[pl.ds(c, 1), :] = h_ref[pl.ds(row, 1), :]
            g_o[pl.ds(c, 1), :] = aff_ref[pl.ds(row, 1), :]
        return 0

    lax.fori_loop(0, cap // GATHER_UNROLL, body, 0)
    xs_o[...] = buf_ref[...].astype(BF16)


def gather_tokens(idx, h, aff, n_exp, cap):
    b, t, d = h.shape
    return pl.pallas_call(
        functools.partial(_gather_kernel, cap),
        grid_spec=pltpu.PrefetchScalarGridSpec(
            num_scalar_prefetch=1,
            grid=(b, n_exp),
            in_specs=[pl.BlockSpec(memory_space=pl.ANY),
                      pl.BlockSpec((None, t, LANES), lambda bi, e, idx_: (bi, 0, 0))],
            out_specs=[pl.BlockSpec((None, None, cap, d), lambda bi, e, idx_: (bi, e, 0, 0)),
                       pl.BlockSpec((None, None, cap, LANES), lambda bi, e, idx_: (bi, e, 0, 0))],
            scratch_shapes=[pltpu.VMEM((t, d), F32), pltpu.VMEM((cap, d), F32), pltpu.SemaphoreType.DMA(())],
        ),
        out_shape=[jax.ShapeDtypeStruct((b, n_exp, cap, d), BF16),
                   jax.ShapeDtypeStruct((b, n_exp, cap, LANES), F32)],
        compiler_params=_cparams("arbitrary", "arbitrary"),
    )(idx, h, aff)


def _expert_ffn_kernel(n_groups, *refs):
    xs_refs = refs[0:3 * n_groups:3]
    g_refs = refs[1:3 * n_groups:3]
    gate_refs = refs[2:3 * n_groups:3]
    wg_ref, wu_ref, wd_ref = refs[3 * n_groups:3 * n_groups + 3]
    o_refs = refs[3 * n_groups + 3:4 * n_groups + 3]
    acc_refs = refs[4 * n_groups + 3:]
    f = pl.program_id(1)

    @pl.when(f == 0)
    def _():
        for acc_ref in acc_refs:
            acc_ref[...] = jnp.zeros(acc_ref.shape, F32)

    wg = wg_ref[...].astype(BF16)
    wu = wu_ref[...].astype(BF16)
    wd = wd_ref[...].astype(BF16)
    for xs_ref, acc_ref in zip(xs_refs, acc_refs):
        for bi in range(xs_ref.shape[0]):
            xb = xs_ref[bi]
            gate = jnp.dot(xb, wg, preferred_element_type=F32)
            up = jnp.dot(xb, wu, preferred_element_type=F32)
            hid = (gate * _sigmoid(gate) * up).astype(BF16)
            acc_ref[bi] += jnp.dot(hid, wd, preferred_element_type=F32)

    @pl.when(f == pl.num_programs(1) - 1)
    def _():
        e = pl.program_id(0)
        for g_ref, gate_ref, acc_ref, o_ref in zip(g_refs, gate_refs, acc_refs, o_refs):
            lane = lax.broadcasted_iota(jnp.int32, g_ref.shape, 2)
            g = jnp.sum(jnp.where(lane == e, g_ref[...], 0.0), axis=-1, keepdims=True)
            o_ref[...] = acc_ref[...] * g * gate_ref[...]


def expert_ffn(groups, layer, w_gate, w_up, w_down):
    _, n_exp, d, f_dim = w_gate.shape
    tf = _largest_divisor(f_dim, 256, LANES)
    args, in_specs, out_specs, out_shapes, scratch = [], [], [], [], []
    for xs, g_rows, gate in groups:
        b, _, cap, _ = xs.shape
        args += [xs, g_rows, gate]
        in_specs += [pl.BlockSpec((b, None, cap, d), lambda e, f: (0, e, 0, 0)),
                     pl.BlockSpec((b, None, cap, LANES), lambda e, f: (0, e, 0, 0)),
                     pl.BlockSpec((b, 1, d), lambda e, f: (0, 0, 0))]
        out_specs.append(pl.BlockSpec((None, b, cap, d), lambda e, f: (e, 0, 0, 0)))
        out_shapes.append(jax.ShapeDtypeStruct((n_exp, b, cap, d), F32))
        scratch.append(pltpu.VMEM((b, cap, d), F32))
    in_specs += [pl.BlockSpec((None, None, d, tf), lambda e, f: (layer, e, 0, f)),
                 pl.BlockSpec((None, None, d, tf), lambda e, f: (layer, e, 0, f)),
                 pl.BlockSpec((None, None, tf, d), lambda e, f: (layer, e, f, 0))]
    return pl.pallas_call(
        functools.partial(_expert_ffn_kernel, len(groups)),
        grid=(n_exp, f_dim // tf),
        in_specs=in_specs,
        out_specs=out_specs,
        out_shape=out_shapes,
        scratch_shapes=scratch,
        compiler_params=_cparams("parallel", "arbitrary"),
    )(*args, w_gate, w_up, w_down)


def _scatter_kernel(cap, idx_ref, eo_ref, x_hbm, o_hbm, acc_ref, sem):
    b = pl.program_id(0)
    e = pl.program_id(1)
    base = (b * pl.num_programs(1) + e) * cap

    @pl.when(e == 0)
    def _():
        load = pltpu.make_async_copy(x_hbm.at[b], acc_ref, sem)
        load.start()
        load.wait()

    def body(i, _):
        c0 = i * SCATTER_BATCH
        rows = [idx_ref[base + c0 + u] for u in range(SCATTER_BATCH)]
        sums = [acc_ref[pl.ds(rows[u], 1), :] + eo_ref[pl.ds(c0 + u, 1), :] for u in range(SCATTER_BATCH)]
        for u in range(SCATTER_BATCH):
            acc_ref[pl.ds(rows[u], 1), :] = sums[u]
        return 0

    lax.fori_loop(0, cap // SCATTER_BATCH, body, 0)

    @pl.when(e == pl.num_programs(1) - 1)
    def _():
        store = pltpu.make_async_copy(acc_ref, o_hbm.at[b], sem)
        store.start()
        store.wait()


def scatter_residual(idx, expert_out, x, cap):
    n_exp, b, _, d = expert_out.shape
    t = x.shape[1]
    return pl.pallas_call(
        functools.partial(_scatter_kernel, cap),
        grid_spec=pltpu.PrefetchScalarGridSpec(
            num_scalar_prefetch=1,
            grid=(b, n_exp),
            in_specs=[pl.BlockSpec((None, None, cap, d), lambda bi, e, idx_: (e, bi, 0, 0)),
                      pl.BlockSpec(memory_space=pl.ANY)],
            out_specs=pl.BlockSpec(memory_space=pl.ANY),
            scratch_shapes=[pltpu.VMEM((t, d), F32), pltpu.SemaphoreType.DMA(())],
        ),
        out_shape=jax.ShapeDtypeStruct((b, t, d), F32),
        compiler_params=_cparams("arbitrary", "arbitrary"),
    )(idx, expert_out, x)


def moe_residual(streams, gain, layer, w_router, w_gate, w_up, w_down):
    n_exp = w_router.shape[1]
    routed = []
    for x, scale, shift, gate in streams:
        cap = CAPACITY_FACTOR * x.shape[1] // n_exp
        h, aff, aff_t = router(x, gain, scale, shift, w_router)
        idx = expert_choice_topk(aff_t, cap)
        xs, g_rows = gather_tokens(idx, h, aff, n_exp, cap)
        routed.append((idx, cap, (xs, g_rows, gate)))
    outs = expert_ffn([g for _, _, g in routed], layer, w_gate, w_up, w_down)
    return [scatter_residual(idx, eo, x, cap) for (idx, cap, _), eo, (x, _, _, _) in zip(routed, outs, streams)]


def kernel(x, c, ctx, c_ctx, mod_w, mod_b, norm1_w, norm2_w, w_in, rw_mu_prev, rw_mu_next, rw_w0, rw_w_up,
           rw_a0, rw_a_up, rw_g_up, rw_k_k, rw_k_a, rw_r_k, rw_ln_w, rw_ln_b, da_q_gain, da_k_gain, da_lq1,
           da_lk1, da_lq2, da_lk2, da_sub_gain, ft_w, pl_w, pl_scale, w_out, moe_router, moe_w_gate, moe_w_up,
           moe_w_down):
    depth, d = norm1_w.shape
    batch = x.shape[0]
    gw = d // N_MIXERS
    c_rows = jnp.zeros((SUBLANES, d), F32).at[:batch].set(c).at[batch].set(c_ctx)
    mod = modulation_vectors(c_rows, mod_w, mod_b)
    w_in_b = w_in.astype(BF16)
    w_out_b = w_out.astype(BF16)
    for l in range(depth):
        ctx_out = l < depth - 1
        lam_init = 0.8 - 0.6 * math.exp(-0.3 * l)
        mx = mod[l, :batch].reshape(batch, 6, 1, d)
        mc = jnp.broadcast_to(mod[l, batch].reshape(1, 6, 1, d), (batch, 6, 1, d))
        rw = (rw_mu_prev[l], rw_mu_next[l], rw_w0[l], rw_w_up[l], rw_a0[l], rw_a_up[l], rw_g_up[l],
              rw_k_k[l], rw_k_a[l], rw_r_k[l], rw_ln_w[l], rw_ln_b[l])
        da = (da_q_gain[l], da_k_gain[l], da_lq1[l], da_lk1[l], da_lq2[l], da_lk2[l], da_sub_gain[l])
        rwx, dax, ftx, plx = input_projection(x, norm1_w[l], mx[:, 1], mx[:, 0], w_in_b[l], gw)
        rwc, dac, ftc, plc = input_projection(ctx, norm1_w[l], mc[:, 1], mc[:, 0], w_in_b[l], gw)
        ax, ac = rwkv_mixer(rwx, rwc, rw, gw, ctx_out)
        bx, bc = diff_attention(dax, dac, da, gw, lam_init, ctx_out)
        fx = fourier_mix(ftx, ft_w[l])
        px = pool_mix(plx, pl_w[l], pl_scale[l])
        rw_out = (rw_r_k[l], rw_ln_w[l], rw_ln_b[l])
        x = output_projection(x, mx[:, 2], ax, rw_out, (bx, fx, px), w_out_b[l])
        streams = [(x, mx[:, 4], mx[:, 3], mx[:, 5])]
        if ctx_out:
            fc = fourier_mix(ftc, ft_w[l])
            pc = pool_mix(plc, pl_w[l], pl_scale[l])
            ctx = output_projection(ctx, mc[:, 2], ac, rw_out, (bc, fc, pc), w_out_b[l])
            streams.append((ctx, mc[:, 4], mc[:, 3], mc[:, 5]))
        outs = moe_residual(streams, norm2_w[l], l, moe_router[l], moe_w_gate, moe_w_up, moe_w_down)
        x = outs[0]
        if ctx_out:
            ctx = outs[1]
    return x
```
